```python
import math
import jax
import jax.numpy as jnp
from jax import lax
import numpy as np

D_MODEL = 1024
BATCH = 4
SEQ = 4096
DEPTH = 4
DEC_BATCH = 32
DEC_SEQ = 8
PAST_LEN = 8192
PAGE_SIZE = 128

N_GROUPS = 4
GROUP_WIDTH = D_MODEL // N_GROUPS
N_HEADS = 4
HEAD_DIM = GROUP_WIDTH // N_HEADS
ML_CHUNK = 64
DQK = HEAD_DIM // 2
DV = HEAD_DIM
ROPE_THETA = 10000.0
CMP_BLOCK = 32
CMP_HIDDEN = 128
SEL_BLOCK = 64
SEL_TOPK = 16
WINDOW = 512
Q_BLOCK = 128
CONV_W = 4
LRU_C = 8.0
N_MEM = 256
MEM_HEADS = 4
MEM_HEAD_DIM = D_MODEL // MEM_HEADS
D_FF = 4 * D_MODEL
EPS = 1e-6
NEG = -1e30
FORCE_SCORE = 1e9
TINY = 1e-30

PROJ_SPLITS = (
    ('ml_q', GROUP_WIDTH), ('ml_k', GROUP_WIDTH), ('ml_v', GROUP_WIDTH),
    ('ml_i', N_HEADS), ('ml_f', N_HEADS), ('ml_o', GROUP_WIDTH),
    ('df_q', 2 * N_HEADS * DQK), ('df_k', 2 * N_HEADS * DQK), ('df_v', N_HEADS * DV),
    ('ns_q', GROUP_WIDTH), ('ns_kc', HEAD_DIM), ('ns_vc', HEAD_DIM), ('ns_ks', HEAD_DIM),
    ('ns_vs', HEAD_DIM), ('ns_kw', HEAD_DIM), ('ns_vw', HEAD_DIM), ('ns_g', 3 * N_HEADS),
    ('lr_x', GROUP_WIDTH), ('lr_g', GROUP_WIDTH),
)
D_PROJ = sum(w for _, w in PROJ_SPLITS)

kernel_name = 'hymba_mlstm_diff_nsa_rglru_step'


def split_cols(z):
    out = {}
    off = 0
    for name, w in PROJ_SPLITS:
        out[name] = z[..., off:off + w]
        off += w
    return out


def rmsnorm(x, g):
    xf = x.astype(jnp.float32)
    y = xf * lax.rsqrt(jnp.mean(xf * xf, axis=-1, keepdims=True) + EPS)
    return (y * g.astype(jnp.float32)).astype(x.dtype)


def rope(x, pos):
    half = x.shape[-1] // 2
    inv = ROPE_THETA ** (-jnp.arange(half, dtype=jnp.float32) / half)
    ang = pos.astype(jnp.float32)[:, None] * inv[None, :]
    cos = jnp.cos(ang)[:, None, :]
    sin = jnp.sin(ang)[:, None, :]
    xf = x.astype(jnp.float32)
    x1, x2 = xf[..., :half], xf[..., half:]
    return jnp.concatenate([x1 * cos - x2 * sin, x2 * cos + x1 * sin], -1).astype(x.dtype)


def gather_pages(pool, page_table):
    rows = pool[page_table]
    return rows.reshape(rows.shape[0], rows.shape[1] * rows.shape[2], *rows.shape[3:])


def over_query_blocks(fn, qs, pos):
    T = pos.shape[0]
    if T % Q_BLOCK != 0 or T <= Q_BLOCK:
        return fn(*qs, pos)
    nb = T // Q_BLOCK
    B = qs[0].shape[0]
    blk = lambda a: jnp.moveaxis(a.reshape(B, nb, Q_BLOCK, *a.shape[2:]), 1, 0)
    out = lax.map(lambda args: fn(*args[0], args[1]),
                  (tuple(blk(a) for a in qs), pos.reshape(nb, Q_BLOCK)))
    return jnp.moveaxis(out, 0, 1).reshape(B, T, *out.shape[3:])


def mlstm_chunk(carry, inp):
    C, n, m = carry
    q, k, v, ig, lf = inp
    L = q.shape[1]
    b = jnp.cumsum(lf, axis=1)
    D = b[:, :, None, :] - b[:, None, :, :] + ig[:, None, :, :]
    causal = jnp.tril(jnp.ones((L, L), dtype=bool))
    D = jnp.where(causal[None, :, :, None], D, -jnp.inf)
    m_t = jnp.maximum(b + m[:, None, :], jnp.max(D, axis=2))
    inter = jnp.exp(b + m[:, None, :] - m_t)
    Sw = jnp.exp(D - m_t[:, :, None, :]) * jnp.einsum('bthd,bshd->btsh', q, k)
    num = jnp.einsum('btsh,bshd->bthd', Sw, v) + inter[..., None] * jnp.einsum('bthk,bhkv->bthv', q, C)
    den = jnp.sum(Sw, axis=2) + inter * jnp.einsum('bthk,bhk->bth', q, n)
    h = num / jnp.maximum(jnp.abs(den), jnp.exp(-m_t))[..., None]
    m_new = m_t[:, -1]
    wL = jnp.exp(b[:, -1:, :] - b + ig - m_new[:, None, :])
    decay = jnp.exp(b[:, -1] + m - m_new)
    C_new = decay[..., None, None] * C + jnp.einsum('bsh,bshk,bshv->bhkv', wL, k, v)
    n_new = decay[..., None] * n + jnp.einsum('bsh,bshk->bhk', wL, k)
    return (C_new, n_new, m_new), h


def mlstm_mixer(z, gate_b, norm_g, state):
    dt = z['ml_q'].dtype
    f32 = jnp.float32
    B, T, _ = z['ml_q'].shape
    hd = lambda a: a.astype(f32).reshape(B, T, N_HEADS, HEAD_DIM)
    q = hd(z['ml_q'])
    k = hd(z['ml_k']) * HEAD_DIM ** -0.5
    v = hd(z['ml_v'])
    ig = z['ml_i'].astype(f32) + gate_b[0].astype(f32)
    lf = jax.nn.log_sigmoid(z['ml_f'].astype(f32) + gate_b[1].astype(f32))
    L = ML_CHUNK if T % ML_CHUNK == 0 else T
    nc = T // L
    chunk = lambda a: jnp.moveaxis(a.reshape(B, nc, L, *a.shape[2:]), 1, 0)
    carry0 = tuple(s.astype(f32) for s in state)
    carry, h = lax.scan(mlstm_chunk, carry0, (chunk(q), chunk(k), chunk(v), chunk(ig), chunk(lf)))
    h = jnp.moveaxis(h, 0, 1).reshape(B, T, N_HEADS, HEAD_DIM)
    mu = jnp.mean(h, -1, keepdims=True)
    var = jnp.mean(jnp.square(h - mu), -1, keepdims=True)
    h = ((h - mu) * lax.rsqrt(var + EPS)).reshape(B, T, GROUP_WIDTH) * norm_g.astype(f32)
    h = h * jax.nn.sigmoid(z['ml_o'].astype(f32))
    return h.astype(dt), carry


def diff_mixer(z, pos, lam, lam_init, norm_g, past):
    B, T, _ = z['df_q'].shape
    f32 = jnp.float32
    q = rope(z['df_q'].reshape(B, T, 2 * N_HEADS, DQK), pos).reshape(B, T, N_HEADS, 2, DQK)
    k = rope(z['df_k'].reshape(B, T, 2 * N_HEADS, DQK), pos).reshape(B, T, N_HEADS, 2 * DQK)
    v = z['df_v'].reshape(B, T, N_HEADS, DV)
    if past is None:
        k_all, v_all, k_pos = k, v, pos
    else:
        k_all = jnp.concatenate([past[0].astype(k.dtype), k], 1)
        v_all = jnp.concatenate([past[1].astype(v.dtype), v], 1)
        k_pos = jnp.arange(k_all.shape[1])
    k1, k2 = k_all[..., :DQK], k_all[..., DQK:]
    scale = DQK ** -0.5

    def attend(q1, q2, qp):
        ok = k_pos[None, :] <= qp[:, None]
        s1 = jnp.einsum('bqhd,bkhd->bhqk', q1, k1).astype(f32) * scale
        s2 = jnp.einsum('bqhd,bkhd->bhqk', q2, k2).astype(f32) * scale
        p = jax.nn.softmax(jnp.where(ok, s1, NEG), -1) - lam * jax.nn.softmax(jnp.where(ok, s2, NEG), -1)
        return jnp.einsum('bhqk,bkhd->bqhd', p.astype(v_all.dtype), v_all)

    o = over_query_blocks(attend, (q[..., 0, :], q[..., 1, :]), pos)
    o = rmsnorm(o, norm_g) * (1.0 - lam_init)
    return o.reshape(B, T, GROUP_WIDTH), (k, v)


def nsa_attend(q_rope, q_raw, gates, pos, kcmp, vcmp, slc, win, win_pos):
    B, Tq = q_rope.shape[:2]
    f32 = jnp.float32
    scale = HEAD_DIM ** -0.5
    nc = kcmp.shape[1]
    c_end = (jnp.arange(nc) + 1) * CMP_BLOCK - 1
    c_ok = c_end[None, :] <= pos[:, None]
    s = jnp.where(c_ok, jnp.einsum('bqhd,bjd->bhqj', q_raw, kcmp).astype(f32) * scale, NEG)
    p = jnp.exp(s - jnp.max(s, -1, keepdims=True)) * c_ok
    p = p / jnp.maximum(jnp.sum(p, -1, keepdims=True), TINY)
    o_cmp = jnp.einsum('bhqj,bjd->bqhd', p.astype(vcmp.dtype), vcmp)
    ns = slc.shape[1] // SEL_BLOCK
    ratio = SEL_BLOCK // CMP_BLOCK
    imp = jnp.pad(jnp.sum(p, 1), ((0, 0), (0, 0), (0, ns * ratio - nc))).reshape(B, Tq, ns, ratio).sum(-1)
    cur = pos // SEL_BLOCK
    blk = jnp.arange(ns)
    imp = jnp.where((blk[None, :] == cur[:, None]) | (blk[None, :] == 0), FORCE_SCORE, imp)
    imp = jnp.where(blk[None, :] > cur[:, None], NEG, imp)
    _, sel = lax.top_k(imp, min(SEL_TOPK, ns))
    tok = (sel[..., None] * SEL_BLOCK + jnp.arange(SEL_BLOCK)).reshape(B, Tq, -1)
    g = jax.vmap(lambda rows, idx: rows[idx])(slc, tok)
    s = jnp.einsum('bqhd,bqnd->bhqn', q_rope, g[..., 0, :]).astype(f32) * scale
    s = jnp.where((tok <= pos[None, :, None])[:, None], s, NEG)
    p = jax.nn.softmax(s, -1)
    o_slc = jnp.einsum('bhqn,bqnd->bqhd', p.astype(g.dtype), g[..., 1, :])
    s = jnp.einsum('bqhd,bkd->bhqk', q_rope, win[..., 0, :]).astype(f32) * scale
    w_ok = ((win_pos[None, :] <= pos[:, None]) & (pos[:, None] - win_pos[None, :] < WINDOW)
            & (win_pos[None, :] >= 0))
    p = jax.nn.softmax(jnp.where(w_ok, s, NEG), -1)
    o_win = jnp.einsum('bhqk,bkd->bqhd', p.astype(win.dtype), win[..., 1, :])
    out = gates[..., 0:1] * o_cmp + gates[..., 1:2] * o_slc + gates[..., 2:3] * o_win
    return out.astype(q_rope.dtype)


def nsa_mixer(z, pos, pos_emb, w1, w2, past):
    B, T, _ = z['ns_q'].shape
    q_raw = z['ns_q'].reshape(B, T, N_HEADS, HEAD_DIM)
    q_rope = rope(q_raw, pos)
    gates = jax.nn.sigmoid(z['ns_g'].astype(jnp.float32)).reshape(B, T, N_HEADS, 3)
    rope1 = lambda a: rope(a[:, :, None, :], pos)[:, :, 0, :]
    new_cmp = jnp.stack([z['ns_kc'], z['ns_vc']], 2)
    new_slc = jnp.stack([rope1(z['ns_ks']), z['ns_vs']], 2)
    new_win = jnp.stack([rope1(z['ns_kw']), z['ns_vw']], 2)
    if past is None:
        cmp_all, slc_all = new_cmp, new_slc
        win_all = jnp.concatenate([jnp.zeros((B, WINDOW, 2, HEAD_DIM), new_win.dtype), new_win], 1)
        win_pos = jnp.arange(WINDOW + T) - WINDOW
        keep = min(WINDOW, T)
    else:
        past_cmp, past_slc, win_buf = past
        n_past = past_cmp.shape[1]
        cmp_all = jnp.concatenate([past_cmp.astype(new_cmp.dtype), new_cmp], 1)
        slc_all = jnp.concatenate([past_slc.astype(new_slc.dtype), new_slc], 1)
        win_all = jnp.concatenate([win_buf.astype(new_win.dtype), new_win], 1)
        win_pos = n_past - win_buf.shape[1] + jnp.arange(win_all.shape[1])
        keep = min(WINDOW, win_all.shape[1])
    L = cmp_all.shape[1]
    nc = L // CMP_BLOCK
    blocks = cmp_all[:, :nc * CMP_BLOCK].reshape(B, nc, CMP_BLOCK, 2, HEAD_DIM) + pos_emb
    flat = jnp.moveaxis(blocks, 3, 2).reshape(B, nc, 2, CMP_BLOCK * HEAD_DIM)
    hid = jax.nn.gelu(jnp.einsum('bnsi,sih->bnsh', flat, w1))
    comp = jnp.einsum('bnsh,shd->bnsd', hid, w2)
    kcmp, vcmp = comp[:, :, 0], comp[:, :, 1]
    ns = -(-L // SEL_BLOCK)
    slc_pad = jnp.pad(slc_all, ((0, 0), (0, ns * SEL_BLOCK - L), (0, 0), (0, 0)))

    def attend(qr, qo, g, qp):
        if past is None:
            wk = lax.dynamic_slice_in_dim(win_all, qp[0], WINDOW + qp.shape[0], axis=1)
            wp = lax.dynamic_slice_in_dim(win_pos, qp[0], WINDOW + qp.shape[0])
        else:
            wk, wp = win_all, win_pos
        return nsa_attend(qr, qo, g, qp, kcmp, vcmp, slc_pad, wk, wp)

    out = over_query_blocks(attend, (q_rope, q_raw, gates), pos)
    win_state = win_all[:, win_all.shape[1] - keep:]
    return out.reshape(B, T, GROUP_WIDTH), (new_cmp, new_slc, win_state)


def rglru_mixer(z, conv_w, conv_b, wa, ba, wx, bx, lam, state):
    h0, buf = state
    dt = z['lr_x'].dtype
    f32 = jnp.float32
    B, T, _ = z['lr_x'].shape
    xin = jnp.concatenate([buf.astype(f32), z['lr_x'].astype(f32)], 1)
    xc = conv_b.astype(f32) + conv_w[0].astype(f32) * xin[:, 0:T]
    for j in range(1, CONV_W):
        xc = xc + conv_w[j].astype(f32) * xin[:, j:j + T]
    xh = xc.reshape(B, T, N_HEADS, HEAD_DIM)
    r = jax.nn.sigmoid(jnp.einsum('bthi,hij->bthj', xh, wa.astype(f32)).reshape(B, T, GROUP_WIDTH) + ba.astype(f32))
    i = jax.nn.sigmoid(jnp.einsum('bthi,hij->bthj', xh, wx.astype(f32)).reshape(B, T, GROUP_WIDTH) + bx.astype(f32))
    log_a = -LRU_C * r * jax.nn.softplus(-lam.astype(f32))
    a = jnp.exp(log_a)
    u = jnp.sqrt(-jnp.expm1(2.0 * log_a)) * (i * xc)
    u = u.at[:, 0].add(a[:, 0] * h0.astype(f32))
    _, h = lax.associative_scan(lambda e1, e2: (e1[0] * e2[0], e2[0] * e1[1] + e2[1]), (a, u), axis=1)
    y = h * jax.nn.gelu(z['lr_g'].astype(f32))
    return y.astype(dt), (h[:, -1], xin[:, xin.shape[1] - (CONV_W - 1):])


def mem_kv(mem, g, w_k, w_v):
    B, M, _ = mem.shape
    mn = rmsnorm(mem, g)
    return ((mn @ w_k).reshape(B, M, MEM_HEADS, MEM_HEAD_DIM),
            (mn @ w_v).reshape(B, M, MEM_HEADS, MEM_HEAD_DIM))


def mem_attn(h, mem_k, mem_v, w_q, w_o):
    B, T, _ = h.shape
    q = (h @ w_q).reshape(B, T, MEM_HEADS, MEM_HEAD_DIM)
    s = jnp.einsum('bqhd,bkhd->bhqk', q, mem_k.astype(q.dtype)).astype(jnp.float32) * MEM_HEAD_DIM ** -0.5
    p = jax.nn.softmax(s, -1)
    o = jnp.einsum('bhqk,bkhd->bqhd', p.astype(q.dtype), mem_v.astype(q.dtype)).reshape(B, T, D_MODEL)
    return o @ w_o


def layer(x, pos, lw, lam_init, mem_k, mem_v, ml_state, lr_state, df_past, ns_past):
    h = rmsnorm(x, lw['g_mix'])
    z = split_cols(h @ lw['w_in'])
    o_ml, ml_new = mlstm_mixer(z, lw['ml_gate_b'], lw['ml_norm_g'], ml_state)
    lq = lw['df_lam'].astype(jnp.float32)
    lam = jnp.exp(jnp.sum(lq[0] * lq[1])) - jnp.exp(jnp.sum(lq[2] * lq[3])) + lam_init
    o_df, df_new = diff_mixer(z, pos, lam, lam_init, lw['df_norm_g'], df_past)
    o_ns, ns_new = nsa_mixer(z, pos, lw['nsa_pos'], lw['nsa_w1'], lw['nsa_w2'], ns_past)
    o_lr, lr_new = rglru_mixer(z, lw['lru_conv_w'], lw['lru_conv_b'], lw['lru_wa'], lw['lru_ba'],
                               lw['lru_wx'], lw['lru_bx'], lw['lru_lambda'], lr_state)
    mix = jnp.concatenate([o_ml, o_df, o_ns, o_lr], -1).astype(x.dtype)
    x = x + mix @ lw['w_out']
    x = x + mem_attn(rmsnorm(x, lw['g_mem_q']), mem_k, mem_v, lw['w_mq'], lw['w_mo'])
    u = jax.nn.relu(rmsnorm(x, lw['g_mlp']) @ lw['w_up'])
    x = x + (u * u) @ lw['w_down']
    return x, ml_new, lr_new, df_new, ns_new


def setup_inputs(seed: int = 0) -> dict:
    key = jax.random.key(seed)
    keys = iter(jax.random.split(key, 64))
    f32 = jnp.float32
    n_pages = PAST_LEN // PAGE_SIZE
    n_used = DEC_BATCH * n_pages
    n_pool = n_used + n_used // 4
    win_buf = min(WINDOW, PAST_LEN)

    def nrm(shape, scale=1.0):
        return jax.random.normal(next(keys), shape, f32) * scale

    def gain(shape):
        return 1.0 + 0.01 * nrm(shape)

    inp = {}
    inp['x_prompt'] = nrm((BATCH, SEQ, D_MODEL))
    inp['x_sample'] = nrm((DEC_BATCH, DEC_SEQ, D_MODEL))
    inp['mem_prompt'] = nrm((BATCH, N_MEM, D_MODEL))
    inp['cache_df_k'] = nrm((DEPTH, n_pool, PAGE_SIZE, N_HEADS, 2 * DQK))
    inp['cache_df_v'] = nrm((DEPTH, n_pool, PAGE_SIZE, N_HEADS, DV))
    inp['cache_nsa_cmp'] = nrm((DEPTH, n_pool, PAGE_SIZE, 2, HEAD_DIM))
    inp['cache_nsa_slc'] = nrm((DEPTH, n_pool, PAGE_SIZE, 2, HEAD_DIM))
    inp['cache_nsa_win'] = nrm((DEPTH, DEC_BATCH, win_buf, 2, HEAD_DIM))
    inp['state_ml_C'] = nrm((DEPTH, DEC_BATCH, N_HEADS, HEAD_DIM, HEAD_DIM), 0.1)
    inp['state_ml_n'] = nrm((DEPTH, DEC_BATCH, N_HEADS, HEAD_DIM), 0.5)
    inp['state_ml_m'] = nrm((DEPTH, DEC_BATCH, N_HEADS))
    inp['state_lru_h'] = nrm((DEPTH, DEC_BATCH, GROUP_WIDTH), 0.5)
    inp['state_lru_conv'] = nrm((DEPTH, DEC_BATCH, CONV_W - 1, GROUP_WIDTH))
    inp['cache_mem_k'] = nrm((DEPTH, DEC_BATCH, N_MEM, MEM_HEADS, MEM_HEAD_DIM))
    inp['cache_mem_v'] = nrm((DEPTH, DEC_BATCH, N_MEM, MEM_HEADS, MEM_HEAD_DIM))
    perm = jax.random.permutation(next(keys), n_pool)
    inp['page_table'] = perm[:n_used].reshape(DEC_BATCH, n_pages).astype(jnp.int32)
    inp['g_mix'] = gain((DEPTH, D_MODEL))
    inp['w_in'] = nrm((DEPTH, D_MODEL, D_PROJ), D_MODEL ** -0.5)
    inp['w_out'] = nrm((DEPTH, D_MODEL, D_MODEL), D_MODEL ** -0.5)
    i_bias = nrm((DEPTH, N_HEADS), 0.1)
    f_bias = 3.0 + 3.0 * jax.random.uniform(next(keys), (DEPTH, N_HEADS), f32)
    inp['ml_gate_b'] = jnp.stack([i_bias, f_bias], axis=1)
    inp['ml_norm_g'] = gain((DEPTH, GROUP_WIDTH))
    inp['df_lam'] = nrm((DEPTH, 4, DQK), 0.1)
    inp['df_norm_g'] = gain((DEPTH, DV))
    inp['nsa_pos'] = nrm((DEPTH, CMP_BLOCK, 2, HEAD_DIM), 0.1)
    inp['nsa_w1'] = nrm((DEPTH, 2, CMP_BLOCK * HEAD_DIM, CMP_HIDDEN), (CMP_BLOCK * HEAD_DIM) ** -0.5)
    inp['nsa_w2'] = nrm((DEPTH, 2, CMP_HIDDEN, HEAD_DIM), CMP_HIDDEN ** -0.5)
    inp['lru_conv_w'] = nrm((DEPTH, CONV_W, GROUP_WIDTH), CONV_W ** -0.5)
    inp['lru_conv_b'] = nrm((DEPTH, GROUP_WIDTH), 0.01)
    inp['lru_wa'] = nrm((DEPTH, N_HEADS, HEAD_DIM, HEAD_DIM), HEAD_DIM ** -0.5)
    inp['lru_ba'] = nrm((DEPTH, GROUP_WIDTH), 0.01)
    inp['lru_wx'] = nrm((DEPTH, N_HEADS, HEAD_DIM, HEAD_DIM), HEAD_DIM ** -0.5)
    inp['lru_bx'] = nrm((DEPTH, GROUP_WIDTH), 0.01)
    a0 = jax.random.uniform(next(keys), (DEPTH, GROUP_WIDTH), f32, 0.9, 0.999)
    s0 = a0 ** (1.0 / LRU_C)
    inp['lru_lambda'] = jnp.log(s0) - jnp.log1p(-s0)
    inp['g_mem_q'] = gain((DEPTH, D_MODEL))
    inp['g_mem_kv'] = gain((DEPTH, D_MODEL))
    inp['w_mq'] = nrm((DEPTH, D_MODEL, D_MODEL), D_MODEL ** -0.5)
    inp['w_mk'] = nrm((DEPTH, D_MODEL, D_MODEL), D_MODEL ** -0.5)
    inp['w_mv'] = nrm((DEPTH, D_MODEL, D_MODEL), D_MODEL ** -0.5)
    inp['w_mo'] = nrm((DEPTH, D_MODEL, D_MODEL), D_MODEL ** -0.5)
    inp['g_mlp'] = gain((DEPTH, D_MODEL))
    inp['w_up'] = nrm((DEPTH, D_MODEL, D_FF), D_MODEL ** -0.5)
    inp['w_down'] = nrm((DEPTH, D_FF, D_MODEL), D_FF ** -0.5)
    inp['g_final'] = gain((D_MODEL,))
    return inp


def reference(x_prompt, x_sample, mem_prompt, cache_df_k, cache_df_v, cache_nsa_cmp, cache_nsa_slc,
              cache_nsa_win, state_ml_C, state_ml_n, state_ml_m, state_lru_h, state_lru_conv,
              cache_mem_k, cache_mem_v, page_table, g_mix, w_in, w_out, ml_gate_b, ml_norm_g, df_lam,
              df_norm_g, nsa_pos, nsa_w1, nsa_w2, lru_conv_w, lru_conv_b, lru_wa, lru_ba, lru_wx, lru_bx,
              lru_lambda, g_mem_q, g_mem_kv, w_mq, w_mk, w_mv, w_mo, g_mlp, w_up, w_down, g_final):
    f32 = jnp.float32
    B, T, _ = x_prompt.shape
    Ts = x_sample.shape[1]
    n_past = page_table.shape[1] * cache_df_k.shape[2]
    pos_p = jnp.arange(T)
    pos_s = n_past + jnp.arange(Ts)
    names = ('df_k', 'df_v', 'nsa_cmp', 'nsa_slc', 'nsa_win', 'ml_C', 'ml_n', 'ml_m', 'lru_h', 'lru_conv')
    acc = {pre + n: [] for n in names for pre in ('p_', 's_')}
    acc['p_mem_k'] = []
    acc['p_mem_v'] = []
    xp, xs = x_prompt, x_sample
    for l in range(DEPTH):
        lw = dict(g_mix=g_mix[l], w_in=w_in[l], w_out=w_out[l], ml_gate_b=ml_gate_b[l],
                  ml_norm_g=ml_norm_g[l], df_lam=df_lam[l], df_norm_g=df_norm_g[l], nsa_pos=nsa_pos[l],
                  nsa_w1=nsa_w1[l], nsa_w2=nsa_w2[l], lru_conv_w=lru_conv_w[l], lru_conv_b=lru_conv_b[l],
                  lru_wa=lru_wa[l], lru_ba=lru_ba[l], lru_wx=lru_wx[l], lru_bx=lru_bx[l],
                  lru_lambda=lru_lambda[l], g_mem_q=g_mem_q[l], w_mq=w_mq[l], w_mo=w_mo[l],
                  g_mlp=g_mlp[l], w_up=w_up[l], w_down=w_down[l])
        lam_init = 0.8 - 0.6 * math.exp(-0.3 * l)
        mk_p, mv_p = mem_kv(mem_prompt, g_mem_kv[l], w_mk[l], w_mv[l])
        ml0 = (jnp.zeros((B, N_HEADS, HEAD_DIM, HEAD_DIM), f32), jnp.zeros((B, N_HEADS, HEAD_DIM), f32),
               jnp.zeros((B, N_HEADS), f32))
        lr0 = (jnp.zeros((B, GROUP_WIDTH), f32), jnp.zeros((B, CONV_W - 1, GROUP_WIDTH), f32))
        xp, ml_p, lr_p, df_p, ns_p = layer(xp, pos_p, lw, lam_init, mk_p, mv_p, ml0, lr0, None, None)
        df_past = (gather_pages(cache_df_k[l], page_table), gather_pages(cache_df_v[l], page_table))
        ns_past = (gather_pages(cache_nsa_cmp[l], page_table), gather_pages(cache_nsa_slc[l], page_table),
                   cache_nsa_win[l])
        xs, ml_s, lr_s, df_s, ns_s = layer(xs, pos_s, lw, lam_init, cache_mem_k[l], cache_mem_v[l],
                                           (state_ml_C[l], state_ml_n[l], state_ml_m[l]),
                                           (state_lru_h[l], state_lru_conv[l]), df_past, ns_past)
        for pre, df, ns, ml, lr in (('p_', df_p, ns_p, ml_p, lr_p), ('s_', df_s, ns_s, ml_s, lr_s)):
            acc[pre + 'df_k'].append(df[0])
            acc[pre + 'df_v'].append(df[1])
            acc[pre + 'nsa_cmp'].append(ns[0])
            acc[pre + 'nsa_slc'].append(ns[1])
            acc[pre + 'nsa_win'].append(ns[2])
            acc[pre + 'ml_C'].append(ml[0])
            acc[pre + 'ml_n'].append(ml[1])
            acc[pre + 'ml_m'].append(ml[2])
            acc[pre + 'lru_h'].append(lr[0])
            acc[pre + 'lru_conv'].append(lr[1])
        acc['p_mem_k'].append(mk_p)
        acc['p_mem_v'].append(mv_p)
    st = {k: jnp.stack(v) for k, v in acc.items()}
    y_prompt = rmsnorm(xp, g_final)
    y_sample = rmsnorm(xs, g_final)
    return (y_prompt, y_sample, st['p_df_k'], st['s_df_k'], st['p_df_v'], st['s_df_v'],
            st['p_nsa_cmp'], st['s_nsa_cmp'], st['p_nsa_slc'], st['s_nsa_slc'],
            st['p_nsa_win'], st['s_nsa_win'], st['p_ml_C'], st['s_ml_C'], st['p_ml_n'], st['s_ml_n'],
            st['p_ml_m'], st['s_ml_m'], st['p_lru_h'], st['s_lru_h'], st['p_lru_conv'], st['s_lru_conv'],
            st['p_mem_k'], st['p_mem_v'])
```

```python
import functools
import math

import jax
import jax.numpy as jnp
from jax import lax
from jax.experimental import pallas as pl
from jax.experimental.pallas import tpu as pltpu

f32 = jnp.float32
bf16 = jnp.bfloat16

D_MODEL = 1024
DEPTH = 4
GROUP_WIDTH = 256
N_HEADS = 4
HEAD_DIM = 64
DQK = 32
ROPE_THETA = 10000.0
CMP_BLOCK = 32
CMP_HIDDEN = 128
SEL_BLOCK = 64
SEL_TOPK = 16
WINDOW = 512
CONV_W = 4
LRU_C = 8.0
MEM_HEADS = 4
MEM_HEAD_DIM = 256
EPS = 1e-6
NEG = -1e30
FORCE_SCORE = 1e9
TINY = 1e-30

VMEM_LIMIT_BYTES = 48 * 1024 * 1024

_SRC_SPLITS = (
    ('ml_q', 256), ('ml_k', 256), ('ml_v', 256), ('ml_i', 4), ('ml_f', 4), ('ml_o', 256),
    ('df_q', 256), ('df_k', 256), ('df_v', 256),
    ('ns_q', 256), ('ns_kc', 64), ('ns_vc', 64), ('ns_ks', 64), ('ns_vs', 64),
    ('ns_kw', 64), ('ns_vw', 64), ('ns_g', 12), ('lr_x', 256), ('lr_g', 256),
)
_DST_ORDER = ('ml_q', 'ml_k', 'ml_v', 'ml_o', 'df_q', 'df_k', 'df_v', 'ns_q', 'lr_x', 'lr_g',
              'ns_kc', 'ns_vc', 'ns_ks', 'ns_vs', 'ns_kw', 'ns_vw', 'ml_i', 'ml_f', 'ns_g')
D_PROJ_PAD = 3072
CB_ML_Q, CB_ML_K, CB_ML_V, CB_ML_O, CB_DF_Q, CB_DF_K, CB_DF_V, CB_NS_Q, CB_LR_X, CB_LR_G = range(10)
CB128_CMP, CB128_SLC, CB128_WIN, CB128_SMALL = 20, 21, 22, 23


def _proj_perm():
    off = {}
    o = 0
    for name, w in _SRC_SPLITS:
        off[name] = (o, w)
        o += w
    idx = []
    for name in _DST_ORDER:
        s, w = off[name]
        idx.extend(range(s, s + w))
    n_real = len(idx)
    idx.extend([0] * (D_PROJ_PAD - n_real))
    return jnp.asarray(idx, jnp.int32), n_real


def _cparams(sem):
    return pltpu.CompilerParams(dimension_semantics=sem, vmem_limit_bytes=VMEM_LIMIT_BYTES)


def _mm_kernel(*refs, has_norm, has_res, act):
    it = iter(refs)
    x_ref = next(it)
    w_ref = next(it)
    g_ref = next(it) if has_norm else None
    r_ref = next(it) if has_res else None
    o_ref = next(it)
    h_ref = next(it) if has_norm else None
    if has_norm:
        @pl.when(pl.program_id(1) == 0)
        def _():
            x = x_ref[...].astype(f32)
            y = x * lax.rsqrt(jnp.mean(x * x, axis=-1, keepdims=True) + EPS)
            h_ref[...] = (y * g_ref[...]).astype(bf16)
        h = h_ref[...]
    else:
        h = x_ref[...].astype(bf16)
    acc = jnp.dot(h, w_ref[...], preferred_element_type=f32)
    if act == 'relu2':
        acc = jnp.maximum(acc, 0.0)
        acc = acc * acc
    if has_res:
        acc = acc + r_ref[...]
    o_ref[...] = acc.astype(o_ref.dtype)


def matmul(x, w, *, g=None, res=None, act=None, out_dtype=f32, tm=1024, tn=512):
    M, K = x.shape
    N = w.shape[1]
    tm = min(tm, M)
    tn = min(tn, N)
    assert M % tm == 0 and N % tn == 0
    has_norm = g is not None
    has_res = res is not None
    in_specs = [pl.BlockSpec((tm, K), lambda i, j: (i, 0)),
                pl.BlockSpec((K, tn), lambda i, j: (0, j))]
    args = [x, w]
    if has_norm:
        in_specs.append(pl.BlockSpec((1, K), lambda i, j: (0, 0)))
        args.append(g.reshape(1, K).astype(f32))
    if has_res:
        in_specs.append(pl.BlockSpec((tm, tn), lambda i, j: (i, j)))
        args.append(res)
    scratch = [pltpu.VMEM((tm, K), bf16)] if has_norm else []
    return pl.pallas_call(
        functools.partial(_mm_kernel, has_norm=has_norm, has_res=has_res, act=act),
        out_shape=jax.ShapeDtypeStruct((M, N), out_dtype),
        grid=(M // tm, N // tn),
        in_specs=in_specs,
        out_specs=pl.BlockSpec((tm, tn), lambda i, j: (i, j)),
        scratch_shapes=scratch,
        compiler_params=_cparams(("parallel", "arbitrary")),
        name="matmul",
    )(*args)


def _rmsnorm_kernel(x_ref, g_ref, o_ref):
    x = x_ref[...]
    y = x * lax.rsqrt(jnp.mean(x * x, axis=-1, keepdims=True) + EPS)
    o_ref[...] = y * g_ref[...]


def rmsnorm_rows(x, g, tm=1024):
    M, K = x.shape
    tm = min(tm, M)
    return pl.pallas_call(
        _rmsnorm_kernel,
        out_shape=jax.ShapeDtypeStruct((M, K), f32),
        grid=(M // tm,),
        in_specs=[pl.BlockSpec((tm, K), lambda i: (i, 0)), pl.BlockSpec((1, K), lambda i: (0, 0))],
        out_specs=pl.BlockSpec((tm, K), lambda i: (i, 0)),
        compiler_params=_cparams(("parallel",)),
        name="final_norm",
    )(x, g.reshape(1, K))


def _rotate(x, cos, sin, half):
    n = x.shape[-1]
    lane = lax.broadcasted_iota(jnp.int32, x.shape, 1)
    first = (lane & (2 * half - 1)) < half
    partner = jnp.where(first, pltpu.roll(x, n - half, 1), pltpu.roll(x, half, 1))
    return x * cos + partner * sin


def _rope_kernel(dq, dk, nq, sl, wn, ca, sa, cb, sb, cc, sc, odq, odk, onq, osl, own):
    odq[...] = _rotate(dq[...], ca[...], sa[...], DQK // 2)
    odk[...] = _rotate(dk[...], ca[...], sa[...], DQK // 2)
    onq[...] = _rotate(nq[...], cb[...], sb[...], HEAD_DIM // 2)
    osl[...] = _rotate(sl[...], cc[...], sc[...], HEAD_DIM // 2)
    own[...] = _rotate(wn[...], cc[...], sc[...], HEAD_DIM // 2)


def rope_tables(pos):
    posf = pos.astype(f32)[:, None]

    def tab(half, reps):
        inv = ROPE_THETA ** (-jnp.arange(half, dtype=f32) / half)
        ang = posf * inv[None, :]
        c = jnp.cos(ang)
        s = jnp.sin(ang)
        return jnp.tile(jnp.concatenate([c, c], 1), (1, reps)), jnp.tile(jnp.concatenate([-s, s], 1), (1, reps))

    ca, sa = tab(DQK // 2, GROUP_WIDTH // DQK)
    cb, sb = tab(HEAD_DIM // 2, N_HEADS)
    n = pos.shape[0]
    cc = jnp.concatenate([cb[:, :HEAD_DIM], jnp.ones((n, HEAD_DIM), f32)], 1)
    sc = jnp.concatenate([sb[:, :HEAD_DIM], jnp.zeros((n, HEAD_DIM), f32)], 1)
    return ca, sa, cb, sb, cc, sc


def rope_prep(z, tables, tm):
    N = z.shape[0]
    R = tables[0].shape[0]
    tm = min(tm, R)
    assert R % tm == 0 and N % tm == 0
    nr = R // tm
    zs = lambda cb: pl.BlockSpec((tm, 256), lambda i, cb=cb: (i, cb))
    zs128 = lambda cb: pl.BlockSpec((tm, 128), lambda i, cb=cb: (i, cb))
    t256 = pl.BlockSpec((tm, 256), lambda i: (i % nr, 0))
    t128 = pl.BlockSpec((tm, 128), lambda i: (i % nr, 0))
    o256 = pl.BlockSpec((tm, 256), lambda i: (i, 0))
    o128 = pl.BlockSpec((tm, 128), lambda i: (i, 0))
    return pl.pallas_call(
        _rope_kernel,
        out_shape=[jax.ShapeDtypeStruct((N, 256), f32)] * 3 + [jax.ShapeDtypeStruct((N, 128), f32)] * 2,
        grid=(N // tm,),
        in_specs=[zs(CB_DF_Q), zs(CB_DF_K), zs(CB_NS_Q), zs128(CB128_SLC), zs128(CB128_WIN),
                  t256, t256, t256, t256, t128, t128],
        out_specs=[o256, o256, o256, o128, o128],
        compiler_params=_cparams(("parallel",)),
        name="rope_prep",
    )(z, z, z, z, z, *tables)


def _diff_kernel(qT_ref, k_ref, vT_ref, lam_ref, g_ref, o_ref, qp_ref, m_ref, l_ref, acc_ref,
                 *, tq, tk, pos0, nk, lam_init, per_head):
    qi = pl.program_id(1)
    kj = pl.program_id(2)
    q_lo = pos0 + qi * tq
    needed = (q_lo + tq - 1) // tk + 1
    nmap = 2 * N_HEADS

    @pl.when(kj == 0)
    def _():
        qT = qT_ref[0] * (DQK ** -0.5)
        row = lax.broadcasted_iota(jnp.int32, qT.shape, 0)
        for c in range(nmap):
            blk = jnp.where((row >= DQK * c) & (row < DQK * (c + 1)), qT, 0.0)
            qp_ref[:, c * tq:(c + 1) * tq] = blk.astype(bf16)
        m_ref[...] = jnp.full(m_ref.shape, NEG, f32)
        l_ref[...] = jnp.zeros(l_ref.shape, f32)
        acc_ref[...] = jnp.zeros(acc_ref.shape, f32)

    def step(masked):
        k = k_ref[0].astype(bf16)
        s = jnp.dot(k, qp_ref[...], preferred_element_type=f32)
        if masked:
            kpos = kj * tk + lax.broadcasted_iota(jnp.int32, s.shape, 0)
            qpos = q_lo + (lax.broadcasted_iota(jnp.int32, s.shape, 1) & (tq - 1))
            ok = kpos <= qpos
            s = jnp.where(ok, s, NEG)
        m_old = m_ref[...]
        m_new = jnp.maximum(m_old, jnp.max(s, axis=0, keepdims=True))
        alpha = jnp.exp(m_old - m_new)
        p = jnp.exp(s - m_new)
        if masked:
            p = jnp.where(ok, p, 0.0)
        l_ref[...] = alpha * l_ref[...] + jnp.sum(p, axis=0, keepdims=True)
        m_ref[...] = m_new
        pb = p.astype(bf16)
        vT = vT_ref[0].astype(bf16)
        if per_head:
            for h in range(N_HEADS):
                rs = slice(HEAD_DIM * h, HEAD_DIM * (h + 1))
                cs = slice(2 * tq * h, 2 * tq * (h + 1))
                acc_ref[rs, :] = acc_ref[rs, :] * alpha[:, cs] + jnp.dot(
                    vT[rs, :], pb[:, cs], preferred_element_type=f32)
        else:
            acc_ref[...] = acc_ref[...] * alpha + jnp.dot(vT, pb, preferred_element_type=f32)

    active = kj < needed
    crosses = kj * tk + tk - 1 > q_lo

    @pl.when(active & crosses)
    def _():
        step(True)

    @pl.when(active & jnp.logical_not(crosses))
    def _():
        step(False)

    @pl.when(kj == nk - 1)
    def _():
        lq = lam_ref[...]
        lam = (jnp.exp(jnp.sum(lq[0:1] * lq[1:2], keepdims=True))
               - jnp.exp(jnp.sum(lq[2:3] * lq[3:4], keepdims=True)) + lam_init)
        l = l_ref[...]
        for h in range(N_HEADS):
            rs = slice(HEAD_DIM * h, HEAD_DIM * (h + 1))
            c1 = slice(2 * h * tq, (2 * h + 1) * tq)
            c2 = slice((2 * h + 1) * tq, (2 * h + 2) * tq)
            if per_head:
                a1 = acc_ref[rs, 0:tq]
                a2 = acc_ref[rs, tq:2 * tq]
            else:
                a1 = acc_ref[rs, c1]
                a2 = acc_ref[rs, c2]
            o = a1 / l[:, c1] - lam * (a2 / l[:, c2])
            y = o * lax.rsqrt(jnp.mean(o * o, axis=0, keepdims=True) + EPS)
            o_ref[0, rs, :] = (y * g_ref[...]) * (1.0 - lam_init)


def diff_attention(qT, k, vT, lam_q, norm_g, *, pos0, lam_init, tq, tk):
    B, _, Tq = qT.shape
    Tk = k.shape[1]
    assert Tq % tq == 0 and Tk % tk == 0 and (tq & (tq - 1)) == 0
    nq, nk = Tq // tq, Tk // tk
    per_head = (2 * tq) % 128 == 0
    last = lambda qi: (pos0 + qi * tq + tq - 1) // tk
    kern = functools.partial(_diff_kernel, tq=tq, tk=tk, pos0=pos0, nk=nk, lam_init=lam_init,
                             per_head=per_head)
    acc_shape = (GROUP_WIDTH, 2 * tq) if per_head else (GROUP_WIDTH, 8 * tq)
    return pl.pallas_call(
        kern,
        out_shape=jax.ShapeDtypeStruct((B, GROUP_WIDTH, Tq), f32),
        grid=(B, nq, nk),
        in_specs=[
            pl.BlockSpec((1, GROUP_WIDTH, tq), lambda b, qi, kj: (b, 0, qi)),
            pl.BlockSpec((1, tk, GROUP_WIDTH), lambda b, qi, kj: (b, jnp.minimum(kj, last(qi)), 0)),
            pl.BlockSpec((1, GROUP_WIDTH, tk), lambda b, qi, kj: (b, 0, jnp.minimum(kj, last(qi)))),
            pl.BlockSpec((4, DQK), lambda b, qi, kj: (0, 0)),
            pl.BlockSpec((HEAD_DIM, 1), lambda b, qi, kj: (0, 0)),
        ],
        out_specs=pl.BlockSpec((1, GROUP_WIDTH, tq), lambda b, qi, kj: (b, 0, qi)),
        scratch_shapes=[pltpu.VMEM((GROUP_WIDTH, 8 * tq), bf16), pltpu.VMEM((1, 8 * tq), f32),
                        pltpu.VMEM((1, 8 * tq), f32), pltpu.VMEM(acc_shape, f32)],
        compiler_params=_cparams(("parallel", "parallel", "arbitrary")),
        name="diff_attention",
    )(qT, k, vT, lam_q, norm_g.reshape(HEAD_DIM, 1))


def _compress_kernel(x_ref, pos_ref, w1_ref, w2_ref, o_ref):
    x = (x_ref[...] + pos_ref[...]).astype(bf16)
    hid = jax.nn.gelu(jnp.dot(x, w1_ref[...], preferred_element_type=f32))
    o_ref[...] = jnp.dot(hid.astype(bf16), w2_ref[...], preferred_element_type=f32)


def nsa_compress(blocks, pos_flat, w1c, w2c, tm=256):
    R, K = blocks.shape
    tm = min(tm, R)
    assert R % tm == 0
    return pl.pallas_call(
        _compress_kernel,
        out_shape=jax.ShapeDtypeStruct((R, 2 * HEAD_DIM), f32),
        grid=(R // tm,),
        in_specs=[pl.BlockSpec((tm, K), lambda i: (i, 0)), pl.BlockSpec((1, K), lambda i: (0, 0)),
                  pl.BlockSpec((K, 2 * CMP_HIDDEN), lambda i: (0, 0)),
                  pl.BlockSpec((2 * CMP_HIDDEN, 2 * HEAD_DIM), lambda i: (0, 0))],
        out_specs=pl.BlockSpec((tm, 2 * HEAD_DIM), lambda i: (i, 0)),
        compiler_params=_cparams(("parallel",)),
        name="nsa_compress",
    )(blocks, pos_flat, w1c, w2c)


def _nsa_kernel(qrT_ref, qwT_ref, gT_ref, cmp_ref, cmpT_ref, slc_ref, slcT_ref, win_ref, winT_ref,
                o_ref, qr_ref, v_ref, sel_ref, ocmp_ref, ms_ref, ls_ref, accs_ref, mw_ref, lw_ref, accw_ref,
                *, tq, tk, tkw, pos0, wpos0, nk, nc, ns, nsp, topk):
    qi = pl.program_id(1)
    kj = pl.program_id(2)
    q_lo = pos0 + qi * tq
    H = N_HEADS
    d = HEAD_DIM
    scale = d ** -0.5
    needed = (q_lo + tq - 1) // tk + 1
    w_lo = jnp.maximum(q_lo - (WINDOW - 1) - wpos0, 0) // tkw
    w_hi = (q_lo + tq - 1 - wpos0) // tkw
    half = nc // 2

    def heads_on_lanes(xT):
        return jnp.concatenate([xT[d * h:d * (h + 1), :] for h in range(H)], axis=1)

    def qpos_row(n):
        return q_lo + (lax.broadcasted_iota(jnp.int32, (1, n), 1) & (tq - 1))

    @pl.when(kj == 0)
    def _():
        zeros = jnp.zeros((d, H * tq), f32)
        qr_ref[...] = jnp.concatenate([heads_on_lanes(qrT_ref[0] * scale), zeros], 0).astype(bf16)
        qw = jnp.concatenate([heads_on_lanes(qwT_ref[0] * scale), zeros], 0).astype(bf16)
        s = jnp.dot(cmp_ref[0].astype(bf16), qw, preferred_element_type=f32)
        r = lax.broadcasted_iota(jnp.int32, (nc, 1), 0)
        blk = jnp.where(r < half, 2 * r, 2 * (r - half) + 1)
        c_end = (blk + 1) * CMP_BLOCK - 1
        c_ok = c_end <= qpos_row(H * tq)
        s = jnp.where(c_ok, s, NEG)
        p = jnp.where(c_ok, jnp.exp(s - jnp.max(s, axis=0, keepdims=True)), 0.0)
        p = p / jnp.maximum(jnp.sum(p, axis=0, keepdims=True), TINY)
        ocmp_ref[...] = jnp.dot(cmpT_ref[0, d:2 * d, :].astype(bf16), p.astype(bf16),
                                preferred_element_type=f32)
        imp = p[:, 0:tq]
        for h in range(1, H):
            imp = imp + p[:, h * tq:(h + 1) * tq]
        imp = imp[:half, :] + imp[half:, :]
        if nsp > half:
            imp = jnp.concatenate([imp, jnp.zeros((nsp - half, tq), f32)], 0)
        sb = lax.broadcasted_iota(jnp.int32, (nsp, tq), 0)
        cur = qpos_row(tq) // SEL_BLOCK
        v = jnp.where((sb == cur) | (sb == 0), FORCE_SCORE, imp)
        v = jnp.where(sb > cur, NEG, v)
        v_ref[...] = v

        def rank(i, cnt):
            vi = v_ref[pl.ds(i, 1), :]
            ahead = (vi > v) | ((vi == v) & (i < sb))
            return cnt + jnp.where(ahead, 1.0, 0.0)

        cnt = lax.fori_loop(0, nsp, rank, jnp.zeros((nsp, tq), f32))
        sel_ref[...] = jnp.where(cnt < topk, 1.0, 0.0)
        for m_r, l_r, a_r in ((ms_ref, ls_ref, accs_ref), (mw_ref, lw_ref, accw_ref)):
            m_r[...] = jnp.full(m_r.shape, NEG, f32)
            l_r[...] = jnp.zeros(l_r.shape, f32)
            a_r[...] = jnp.zeros(a_r.shape, f32)

    def flash(kv_ref, kvT_ref, ok, m_r, l_r, a_r):
        s = jnp.dot(kv_ref[0].astype(bf16), qr_ref[...], preferred_element_type=f32)
        ok4 = jnp.concatenate([jnp.where(ok, 1.0, 0.0)] * H, axis=1) > 0.5
        s = jnp.where(ok4, s, NEG)
        m_old = m_r[...]
        m_new = jnp.maximum(m_old, jnp.max(s, axis=0, keepdims=True))
        alpha = jnp.exp(m_old - m_new)
        p = jnp.where(ok4, jnp.exp(s - m_new), 0.0)
        l_r[...] = alpha * l_r[...] + jnp.sum(p, axis=0, keepdims=True)
        m_r[...] = m_new
        a_r[...] = a_r[...] * alpha + jnp.dot(kvT_ref[0, d:2 * d, :].astype(bf16), p.astype(bf16),
                                              preferred_element_type=f32)

    @pl.when(kj < needed)
    def _():
        kpos = kj * tk + lax.broadcasted_iota(jnp.int32, (tk, tq), 0)
        nb = tk // SEL_BLOCK
        rows = [jnp.broadcast_to(sel_ref[pl.ds(kj * nb + c, 1), :], (SEL_BLOCK, tq)) for c in range(nb)]
        chosen = jnp.concatenate(rows, axis=0) > 0.5
        ok = chosen & (kpos <= qpos_row(tq))
        flash(slc_ref, slcT_ref, ok, ms_ref, ls_ref, accs_ref)

    @pl.when(kj <= w_hi - w_lo)
    def _():
        wpos = wpos0 + (w_lo + kj) * tkw + lax.broadcasted_iota(jnp.int32, (tkw, tq), 0)
        qp = qpos_row(tq)
        ok = (wpos <= qp) & (qp - wpos < WINDOW) & (wpos >= 0)
        flash(win_ref, winT_ref, ok, mw_ref, lw_ref, accw_ref)

    @pl.when(kj == nk - 1)
    def _():
        g = jax.nn.sigmoid(gT_ref[0])
        o_slc = accs_ref[...] / ls_ref[...]
        o_win = accw_ref[...] / lw_ref[...]
        o_cmp = ocmp_ref[...]
        for h in range(H):
            cs = slice(h * tq, (h + 1) * tq)
            o_ref[0, d * h:d * (h + 1), :] = (g[3 * h:3 * h + 1] * o_cmp[:, cs]
                                              + g[3 * h + 1:3 * h + 2] * o_slc[:, cs]
                                              + g[3 * h + 2:3 * h + 3] * o_win[:, cs])


def nsa_attention(qrT, qwT, gT, cmp, slc, win, *, pos0, wpos0, n_sel, tq, tk, tkw):
    B, _, Tq = qrT.shape
    nc = cmp.shape[1]
    Tk, Lw = slc.shape[1], win.shape[1]
    assert Tq % tq == 0 and Tk % tk == 0 and Lw % tkw == 0 and tk % SEL_BLOCK == 0 and nc % 2 == 0
    nq, nk = Tq // tq, Tk // tk
    nsp = Tk // SEL_BLOCK
    assert nsp >= n_sel and nsp >= nc // 2 and nsp % 8 == 0
    last = lambda qi: (pos0 + qi * tq + tq - 1) // tk
    w_lo = lambda qi: jnp.maximum(pos0 + qi * tq - (WINDOW - 1) - wpos0, 0) // tkw
    w_hi = lambda qi: (pos0 + qi * tq + tq - 1 - wpos0) // tkw
    for qi in range(nq):
        lo = max(pos0 + qi * tq - (WINDOW - 1) - wpos0, 0) // tkw
        hi = (pos0 + qi * tq + tq - 1 - wpos0) // tkw
        assert hi - lo + 1 <= nk and hi < Lw // tkw
    w_idx = lambda qi, kj: jnp.minimum(w_lo(qi) + kj, w_hi(qi))
    cmpT = jnp.swapaxes(cmp, 1, 2)
    slcT = jnp.swapaxes(slc, 1, 2)
    winT = jnp.swapaxes(win, 1, 2)
    kern = functools.partial(_nsa_kernel, tq=tq, tk=tk, tkw=tkw, pos0=pos0, wpos0=wpos0, nk=nk, nc=nc,
                             ns=n_sel, nsp=nsp, topk=min(SEL_TOPK, n_sel))
    qspec = pl.BlockSpec((1, GROUP_WIDTH, tq), lambda b, qi, kj: (b, 0, qi))
    stat = pltpu.VMEM((1, N_HEADS * tq), f32)
    acc = pltpu.VMEM((HEAD_DIM, N_HEADS * tq), f32)
    return pl.pallas_call(
        kern,
        out_shape=jax.ShapeDtypeStruct((B, GROUP_WIDTH, Tq), f32),
        grid=(B, nq, nk),
        in_specs=[
            qspec, qspec,
            pl.BlockSpec((1, 3 * N_HEADS, tq), lambda b, qi, kj: (b, 0, qi)),
            pl.BlockSpec((1, nc, 128), lambda b, qi, kj: (b, 0, 0)),
            pl.BlockSpec((1, 128, nc), lambda b, qi, kj: (b, 0, 0)),
            pl.BlockSpec((1, tk, 128), lambda b, qi, kj: (b, jnp.minimum(kj, last(qi)), 0)),
            pl.BlockSpec((1, 128, tk), lambda b, qi, kj: (b, 0, jnp.minimum(kj, last(qi)))),
            pl.BlockSpec((1, tkw, 128), lambda b, qi, kj: (b, w_idx(qi, kj), 0)),
            pl.BlockSpec((1, 128, tkw), lambda b, qi, kj: (b, 0, w_idx(qi, kj))),
        ],
        out_specs=qspec,
        scratch_shapes=[pltpu.VMEM((128, N_HEADS * tq), bf16), pltpu.VMEM((nsp, tq), f32),
                        pltpu.VMEM((nsp, tq), f32), acc, stat, stat, acc, stat, stat, acc],
        compiler_params=_cparams(("parallel", "parallel", "arbitrary")),
        name="nsa_attention",
    )(qrT, qwT, gT, cmp, cmpT, slc, slcT, win, winT)


def _mlstm_kernel(q_ref, k_ref, v_ref, og_ref, gc_ref, gr_ref, bias_ref, ng_ref, c0_ref, n0_ref, m0_ref,
                  o_ref, cout_ref, nout_ref, mout_ref, c_s, n_s, m_s, *, L):
    ci = pl.program_id(1)
    H, d = N_HEADS, HEAD_DIM

    @pl.when(ci == 0)
    def _():
        c_s[...] = c0_ref[0]
        n_s[...] = n0_ref[0]
        m_s[...] = m0_ref[0]

    q = q_ref[...]
    k = k_ref[...] * (d ** -0.5)
    v = v_ref[...]
    gc = gc_ref[...]
    gr = gr_ref[0]
    bias = bias_ref[...]
    t_i = lax.broadcasted_iota(jnp.int32, (L, L), 0)
    s_i = lax.broadcasted_iota(jnp.int32, (L, L), 1)
    causal = s_i <= t_i
    causal_T = t_i <= s_i
    outs = []
    for h in range(H):
        bi = bias[0:1, h:h + 1]
        bf_ = bias[1:2, h:h + 1]
        ig_c = gc[:, h:h + 1] + bi
        lf_c = jax.nn.log_sigmoid(gc[:, H + h:H + h + 1] + bf_)
        ig_r = gr[h:h + 1, :] + bi
        lf_r = jax.nn.log_sigmoid(gr[H + h:H + h + 1, :] + bf_)
        b_c = jnp.sum(jnp.where(causal, lf_r, 0.0), axis=1, keepdims=True)
        b_r = jnp.sum(jnp.where(causal_T, lf_c, 0.0), axis=0, keepdims=True)
        m_prev = m_s[h:h + 1, 0:1]
        D = jnp.where(causal, b_c - b_r + ig_r, -jnp.inf)
        m_t = jnp.maximum(b_c + m_prev, jnp.max(D, axis=1, keepdims=True))
        inter = jnp.exp(b_c + m_prev - m_t)
        qh = q[:, d * h:d * (h + 1)]
        kh = k[:, d * h:d * (h + 1)]
        vh = v[:, d * h:d * (h + 1)]
        qb = qh.astype(bf16)
        S = lax.dot_general(qb, kh.astype(bf16), (((1,), (1,)), ((), ())), preferred_element_type=f32)
        Sw = jnp.exp(D - m_t) * S
        C = c_s[h]
        n_row = n_s[h:h + 1, :]
        num = (jnp.dot(Sw.astype(bf16), vh.astype(bf16), preferred_element_type=f32)
               + inter * jnp.dot(qb, C.astype(bf16), preferred_element_type=f32))
        qn = jnp.sum(qh * n_row, axis=1, keepdims=True)
        den = jnp.sum(Sw, axis=1, keepdims=True) + inter * qn
        hh = num / jnp.maximum(jnp.abs(den), jnp.exp(-m_t))
        m_new = m_t[L - 1:L, :]
        b_last = b_c[L - 1:L, :]
        wl = jnp.exp(b_last - b_c + ig_c - m_new)
        decay = jnp.exp(b_last + m_prev - m_new)
        kw = kh * wl
        c_s[h] = decay * C + lax.dot_general(kw.astype(bf16), vh.astype(bf16), (((0,), (0,)), ((), ())),
                                             preferred_element_type=f32)
        n_s[h:h + 1, :] = decay * n_row + jnp.sum(kw, axis=0, keepdims=True)
        m_s[h:h + 1, :] = jnp.broadcast_to(m_new, (1, 128))
        mu = jnp.mean(hh, axis=1, keepdims=True)
        var = jnp.mean(jnp.square(hh - mu), axis=1, keepdims=True)
        outs.append((hh - mu) * lax.rsqrt(var + EPS))
    hcat = jnp.concatenate(outs, axis=1) * ng_ref[...]
    o_ref[...] = hcat * jax.nn.sigmoid(og_ref[...])
    cout_ref[0] = c_s[...]
    nout_ref[0] = n_s[...]
    mout_ref[0] = m_s[...]


def mlstm(z, gates_T, gate_b, norm_g, C0, n0, m0, *, B, T, L):
    nchunk = T // L
    m0p = jnp.broadcast_to(m0[:, :, None], (B, N_HEADS, 128))
    row = lambda cb: pl.BlockSpec((L, 256), lambda b, c, cb=cb: (b * nchunk + c, cb))
    st = lambda *shape: pl.BlockSpec((1,) + shape, lambda b, c: (b,) + (0,) * len(shape))
    out, C, n, m = pl.pallas_call(
        functools.partial(_mlstm_kernel, L=L),
        out_shape=[jax.ShapeDtypeStruct((B * T, GROUP_WIDTH), f32),
                   jax.ShapeDtypeStruct((B, N_HEADS, HEAD_DIM, HEAD_DIM), f32),
                   jax.ShapeDtypeStruct((B, N_HEADS, HEAD_DIM), f32),
                   jax.ShapeDtypeStruct((B, N_HEADS, 128), f32)],
        grid=(B, nchunk),
        in_specs=[row(CB_ML_Q), row(CB_ML_K), row(CB_ML_V), row(CB_ML_O),
                  pl.BlockSpec((L, 128), lambda b, c: (b * nchunk + c, CB128_SMALL)),
                  pl.BlockSpec((1, 8, L), lambda b, c: (b * nchunk + c, 0, 0)),
                  pl.BlockSpec((2, N_HEADS), lambda b, c: (0, 0)),
                  pl.BlockSpec((1, GROUP_WIDTH), lambda b, c: (0, 0)),
                  st(N_HEADS, HEAD_DIM, HEAD_DIM), st(N_HEADS, HEAD_DIM), st(N_HEADS, 128)],
        out_specs=[pl.BlockSpec((L, GROUP_WIDTH), lambda b, c: (b * nchunk + c, 0)),
                   st(N_HEADS, HEAD_DIM, HEAD_DIM), st(N_HEADS, HEAD_DIM), st(N_HEADS, 128)],
        scratch_shapes=[pltpu.VMEM((N_HEADS, HEAD_DIM, HEAD_DIM), f32), pltpu.VMEM((N_HEADS, HEAD_DIM), f32),
                        pltpu.VMEM((N_HEADS, 128), f32)],
        compiler_params=_cparams(("parallel", "arbitrary")),
        name="mlstm",
    )(z, z, z, z, z, gates_T, gate_b, norm_g.reshape(1, GROUP_WIDTH), C0, n0, m0p)
    return out, C, n, m[:, :, 0]


def _rglru_kernel(x_ref, gate_ref, cw_ref, cb_ref, wa_ref, ba_ref, wx_ref, bx_ref, lam_ref, h0_ref, buf0_ref,
                  y_ref, hout_ref, bufout_ref, xbuf, a_s, u_s, h_s, hs_s, *, tm):
    ti = pl.program_id(0)
    Bb = x_ref.shape[0]
    W = GROUP_WIDTH

    @pl.when(ti == 0)
    def _():
        xbuf[:, 0:8, :] = buf0_ref[...]
        h_s[...] = h0_ref[...]

    xbuf[:, 8:8 + tm, :] = x_ref[...]
    cw = cw_ref[...]
    xc = cb_ref[...] + cw[0:1, :] * xbuf[:, 5:5 + tm, :]
    for j in range(1, CONV_W):
        xc = xc + cw[j:j + 1, :] * xbuf[:, 5 + j:5 + j + tm, :]
    flat = xc.reshape(Bb * tm, W).astype(bf16)
    r = jax.nn.sigmoid(jnp.dot(flat, wa_ref[...], preferred_element_type=f32) + ba_ref[...])
    i = jax.nn.sigmoid(jnp.dot(flat, wx_ref[...], preferred_element_type=f32) + bx_ref[...])
    lam = lam_ref[...]
    softplus = jnp.maximum(-lam, 0.0) + jnp.log1p(jnp.exp(-jnp.abs(lam)))
    log_a = (-LRU_C * r) * softplus
    a = jnp.exp(log_a)
    u = jnp.sqrt(-jnp.tanh(log_a) * (a * a + 1.0)) * (i * xc.reshape(Bb * tm, W))
    a_s[...] = a.reshape(Bb, tm, W)
    u_s[...] = u.reshape(Bb, tm, W)

    def body(t, h):
        h = a_s[:, pl.ds(t, 1), :] * h + u_s[:, pl.ds(t, 1), :]
        hs_s[:, pl.ds(t, 1), :] = h
        return h

    h_last = lax.fori_loop(0, tm, body, h_s[...])
    h_s[...] = h_last
    y_ref[...] = hs_s[...] * jax.nn.gelu(gate_ref[...])
    hout_ref[...] = h_last
    bufout_ref[...] = xbuf[:, tm:tm + 8, :]
    xbuf[:, 0:8, :] = xbuf[:, tm:tm + 8, :]


def rglru(z3, conv_w, conv_b, wa_bd, ba, wx_bd, bx, lam, h0, buf0, *, tm):
    B, T, _ = z3.shape
    tm = min(tm, T)
    assert T % tm == 0 and tm % 8 == 0
    W = GROUP_WIDTH
    buf8 = jnp.concatenate([jnp.zeros((B, 8 - (CONV_W - 1), W), f32), buf0], 1)
    vec = lambda: pl.BlockSpec((1, W), lambda i: (0, 0))
    mat = lambda: pl.BlockSpec((W, W), lambda i: (0, 0))
    y, h, buf = pl.pallas_call(
        functools.partial(_rglru_kernel, tm=tm),
        out_shape=[jax.ShapeDtypeStruct((B, T, W), f32), jax.ShapeDtypeStruct((B, 1, W), f32),
                   jax.ShapeDtypeStruct((B, 8, W), f32)],
        grid=(T // tm,),
        in_specs=[pl.BlockSpec((B, tm, W), lambda i: (0, i, CB_LR_X)),
                  pl.BlockSpec((B, tm, W), lambda i: (0, i, CB_LR_G)),
                  pl.BlockSpec((CONV_W, W), lambda i: (0, 0)), vec(), mat(), vec(), mat(), vec(), vec(),
                  pl.BlockSpec((B, 1, W), lambda i: (0, 0, 0)), pl.BlockSpec((B, 8, W), lambda i: (0, 0, 0))],
        out_specs=[pl.BlockSpec((B, tm, W), lambda i: (0, i, 0)),
                   pl.BlockSpec((B, 1, W), lambda i: (0, 0, 0)), pl.BlockSpec((B, 8, W), lambda i: (0, 0, 0))],
        scratch_shapes=[pltpu.VMEM((B, tm + 8, W), f32), pltpu.VMEM((B, tm, W), f32), pltpu.VMEM((B, tm, W), f32),
                        pltpu.VMEM((B, 1, W), f32), pltpu.VMEM((B, tm, W), f32)],
        compiler_params=_cparams(("arbitrary",)),
        name="rglru",
    )(z3, z3, conv_w, conv_b.reshape(1, W), wa_bd, ba.reshape(1, W), wx_bd, bx.reshape(1, W),
      lam.reshape(1, W), h0.reshape(B, 1, W), buf8)
    return y, h[:, 0], buf[:, 8 - (CONV_W - 1):]


def _mem_attn_kernel(q_ref, k_ref, v_ref, o_ref):
    q = q_ref[0]
    lead = (0,) * (len(k_ref.shape) - 2)
    k = k_ref[lead]
    v = v_ref[lead]
    dh = MEM_HEAD_DIM
    for h in range(MEM_HEADS):
        cs = slice(dh * h, dh * (h + 1))
        s = lax.dot_general(q[:, cs].astype(bf16), k[:, cs].astype(bf16), (((1,), (1,)), ((), ())),
                            preferred_element_type=f32) * (dh ** -0.5)
        e = jnp.exp(s - jnp.max(s, axis=-1, keepdims=True))
        p = e / jnp.sum(e, axis=-1, keepdims=True)
        o_ref[0, :, cs] = jnp.dot(p.astype(bf16), v[:, cs].astype(bf16), preferred_element_type=f32)


def mem_attention(q, k, v, *, layer=None, tq=256):
    B, T, D = q.shape
    tq = min(tq, T)
    M = k.shape[-2]
    if layer is None:
        kv_spec = pl.BlockSpec((1, M, D), lambda b, i: (b, 0, 0))
    else:
        kv_spec = pl.BlockSpec((1, 1, M, D), lambda b, i: (layer, b, 0, 0))
    return pl.pallas_call(
        _mem_attn_kernel,
        out_shape=jax.ShapeDtypeStruct((B, T, D), f32),
        grid=(B, T // tq),
        in_specs=[pl.BlockSpec((1, tq, D), lambda b, i: (b, i, 0)), kv_spec, kv_spec],
        out_specs=pl.BlockSpec((1, tq, D), lambda b, i: (b, i, 0)),
        compiler_params=_cparams(("parallel", "parallel")),
        name="mem_attention",
    )(q, k, v)


def _gather_kernel(pt_ref, *refs, layer, n_pages, page, n_arr):
    pools = refs[:n_arr]
    outs = refs[n_arr:2 * n_arr]
    sem = refs[2 * n_arr]
    b = pl.program_id(0)

    def page_copy(c, p):
        pg = pt_ref[b * n_pages + p]
        return pltpu.make_async_copy(pools[c].at[layer, pg], outs[c].at[b, pl.ds(p * page, page)],
                                     sem.at[c, p])

    def start(p, carry):
        for c in range(n_arr):
            page_copy(c, p).start()
        return carry

    def wait(p, carry):
        for c in range(n_arr):
            page_copy(c, p).wait()
        return carry

    lax.fori_loop(0, n_pages, start, 0)
    lax.fori_loop(0, n_pages, wait, 0)


def gather_pages(page_table, pools, layer):
    B, n_pages = page_table.shape
    page = pools[0].shape[2]
    n_arr = len(pools)
    any_spec = pl.BlockSpec(memory_space=pl.ANY)
    return pl.pallas_call(
        functools.partial(_gather_kernel, layer=layer, n_pages=n_pages, page=page, n_arr=n_arr),
        out_shape=[jax.ShapeDtypeStruct((B, n_pages * page, p.shape[3]), p.dtype) for p in pools],
        grid_spec=pltpu.PrefetchScalarGridSpec(
            num_scalar_prefetch=1, grid=(B,),
            in_specs=[any_spec] * n_arr, out_specs=[any_spec] * n_arr,
            scratch_shapes=[pltpu.SemaphoreType.DMA((n_arr, n_pages))]),
        compiler_params=_cparams(("arbitrary",)),
        name="gather_pages",
    )(page_table.reshape(-1), *pools)


def _to_T(x, B, T):
    return jnp.swapaxes(x.reshape(B, T, x.shape[-1]), 1, 2)


def _from_T(xT):
    B, C, T = xT.shape
    return jnp.swapaxes(xT, 1, 2).reshape(B * T, C)


def _even_odd(cmp):
    B, nc, w = cmp.shape
    return jnp.swapaxes(cmp.reshape(B, nc // 2, 2, w), 1, 2).reshape(B, nc, w)


def _pad_rows(x, n):
    return jnp.concatenate([x, jnp.zeros((x.shape[0], n - x.shape[1], x.shape[2]), x.dtype)], 1)


def _layer(x, lw, lam_init, tables, *, B, T, pos0, mem, ml_state, lr_state, past, cfg):
    N = B * T
    z = matmul(x, lw['w_in'], g=lw['g_mix'])
    dfq, dfk, nsq, slc_new, win_new = rope_prep(z, tables, cfg['rope_tm'])
    df_v = z[:, 256 * CB_DF_V:256 * (CB_DF_V + 1)]
    ns_qraw = z[:, 256 * CB_NS_Q:256 * (CB_NS_Q + 1)]
    cmp_new = z[:, 128 * CB128_CMP:128 * (CB128_CMP + 1)]
    small = z[:, 128 * CB128_SMALL:128 * CB128_SMALL + 20]

    L = cfg['ml_chunk']
    gates_T = jnp.swapaxes(small[:, :8].reshape(N // L, L, 8), 1, 2)
    o_ml, ml_C, ml_n, ml_m = mlstm(z, gates_T, lw['ml_gate_b'], lw['ml_norm_g'], *ml_state, B=B, T=T, L=L)

    if past is None:
        k_all = dfk.reshape(B, T, 256)
        v_all = df_v.reshape(B, T, 256)
    else:
        Tk = cfg['tk_pad']
        k_all = _pad_rows(jnp.concatenate([past['df_k'], dfk.reshape(B, T, 256)], 1), Tk)
        v_all = _pad_rows(jnp.concatenate([past['df_v'], df_v.reshape(B, T, 256)], 1), Tk)
    o_dfT = diff_attention(_to_T(dfq, B, T), k_all, jnp.swapaxes(v_all, 1, 2), lw['df_lam'], lw['df_norm_g'],
                           pos0=pos0, lam_init=lam_init, tq=cfg['df_tq'], tk=cfg['df_tk'])
    o_df = _from_T(o_dfT)

    if past is None:
        blocks = cmp_new.reshape(N // CMP_BLOCK, CMP_BLOCK * 128)
        nc = T // CMP_BLOCK
        slc_all = slc_new.reshape(B, T, 128)
        win_all = win_new.reshape(B, T, 128)
        wpos0 = 0
        n_sel = -(-T // SEL_BLOCK)
        win_state = win_all[:, T - min(WINDOW, T):]
    else:
        n_past = past['nsa_cmp'].shape[1]
        nc = (n_past + T) // CMP_BLOCK
        assert nc * CMP_BLOCK <= n_past
        blocks = past['nsa_cmp'][:, :nc * CMP_BLOCK].reshape(B * nc, CMP_BLOCK * 128)
        slc_all = _pad_rows(jnp.concatenate([past['nsa_slc'], slc_new.reshape(B, T, 128)], 1), cfg['tk_pad'])
        win_cat = jnp.concatenate([past['nsa_win'], win_new.reshape(B, T, 128)], 1)
        wpos0 = n_past - past['nsa_win'].shape[1]
        keep = min(WINDOW, win_cat.shape[1])
        win_state = win_cat[:, win_cat.shape[1] - keep:]
        win_all = _pad_rows(win_cat, cfg['win_pad'])
        n_sel = -(-(n_past + T) // SEL_BLOCK)
    comp = nsa_compress(blocks, lw['nsa_pos_flat'], lw['nsa_w1c'], lw['nsa_w2c']).reshape(B, nc, 128)
    gT = _to_T(small[:, 8:20], B, T)
    o_nsT = nsa_attention(_to_T(nsq, B, T), _to_T(ns_qraw, B, T), gT, _even_odd(comp), slc_all, win_all,
                          pos0=pos0, wpos0=wpos0, n_sel=n_sel, tq=cfg['ns_tq'], tk=cfg['ns_tk'],
                          tkw=cfg['ns_tkw'])
    o_ns = _from_T(o_nsT)

    o_lr, lr_h, lr_buf = rglru(z.reshape(B, T, D_PROJ_PAD), lw['lru_conv_w'], lw['lru_conv_b'], lw['lru_wa_bd'],
                               lw['lru_ba'], lw['lru_wx_bd'], lw['lru_bx'], lw['lru_lambda'], *lr_state,
                               tm=cfg['lru_tm'])

    mix = jnp.concatenate([o_ml, o_df, o_ns, o_lr.reshape(N, 256)], axis=1)
    x = matmul(mix, lw['w_out'], res=x)
    q = matmul(x, lw['w_mq'], g=lw['g_mem_q'])
    if isinstance(mem, tuple):
        att = mem_attention(q.reshape(B, T, D_MODEL), mem[0], mem[1], tq=cfg['mem_tq'])
    else:
        att = mem_attention(q.reshape(B, T, D_MODEL), mem['k'], mem['v'], layer=mem['layer'], tq=cfg['mem_tq'])
    x = matmul(att.reshape(N, D_MODEL), lw['w_mo'], res=x)
    u = matmul(x, lw['w_up'], g=lw['g_mlp'], act='relu2', out_dtype=bf16)
    x = matmul(u, lw['w_down'], res=x)
    new = dict(df_k=dfk.reshape(B, T, N_HEADS, HEAD_DIM), df_v=df_v.reshape(B, T, N_HEADS, HEAD_DIM),
               nsa_cmp=cmp_new.reshape(B, T, 2, HEAD_DIM), nsa_slc=slc_new.reshape(B, T, 2, HEAD_DIM),
               nsa_win=win_state.reshape(B, -1, 2, HEAD_DIM), ml_C=ml_C, ml_n=ml_n, ml_m=ml_m,
               lru_h=lr_h, lru_conv=lr_buf)
    return x, new


def _block_diag(w):
    H, a, b = w.shape
    eye = jnp.eye(H, dtype=w.dtype)
    return jnp.einsum('hij,hg->higj', w, eye).reshape(H * a, H * b)


def kernel(x_prompt, x_sample, mem_prompt, cache_df_k, cache_df_v, cache_nsa_cmp, cache_nsa_slc, cache_nsa_win, state_ml_C, state_ml_n, state_ml_m, state_lru_h, state_lru_conv, cache_mem_k, cache_mem_v, page_table, g_mix, w_in, w_out, ml_gate_b, ml_norm_g, df_lam, df_norm_g, nsa_pos, nsa_w1, nsa_w2, lru_conv_w, lru_conv_b, lru_wa, lru_ba, lru_wx, lru_bx, lru_lambda, g_mem_q, g_mem_kv, w_mq, w_mk, w_mv, w_mo, g_mlp, w_up, w_down, g_final):
    Bp, Tp, _ = x_prompt.shape
    Bs, Ts, _ = x_sample.shape
    depth = w_in.shape[0]
    n_pool, page = cache_df_k.shape[1], cache_df_k.shape[2]
    n_past = page_table.shape[1] * page
    M = mem_prompt.shape[1]

    perm, n_real = _proj_perm()
    col_ok = (jnp.arange(D_PROJ_PAD) < n_real)
    w_in_p = jnp.where(col_ok[None, None, :], jnp.take(w_in, perm, axis=2), 0.0).astype(bf16)
    eye2 = jnp.eye(2, dtype=f32)
    w1 = nsa_w1.reshape(depth, 2, CMP_BLOCK, HEAD_DIM, CMP_HIDDEN)
    w1c = jnp.einsum('lstih,sg->ltsigh', w1, eye2).reshape(depth, CMP_BLOCK * 2 * HEAD_DIM, 2 * CMP_HIDDEN)
    w2c = jnp.einsum('lshd,sg->lshgd', nsa_w2, eye2).reshape(depth, 2 * CMP_HIDDEN, 2 * HEAD_DIM)
    tk_pad = -(-(n_past + Ts) // 512) * 512
    cfg_p = dict(rope_tm=512, ml_chunk=128, df_tq=256, df_tk=512, ns_tq=256, ns_tk=256, ns_tkw=256,
                 lru_tm=256, mem_tq=256)
    cfg_s = dict(rope_tm=Bs * Ts, ml_chunk=Ts, df_tq=Ts, df_tk=512, ns_tq=Ts, ns_tk=512, ns_tkw=512,
                 lru_tm=Ts, mem_tq=Ts, tk_pad=tk_pad, win_pad=1024)
    tab_p = rope_tables(jnp.arange(Tp))
    tab_s = tuple(jnp.tile(t, (Bs, 1)) for t in rope_tables(n_past + jnp.arange(Ts)))

    pools = (cache_df_k.reshape(depth, n_pool, page, 256), cache_df_v.reshape(depth, n_pool, page, 256),
             cache_nsa_cmp.reshape(depth, n_pool, page, 128), cache_nsa_slc.reshape(depth, n_pool, page, 128))
    mem_k_cache = cache_mem_k.reshape(depth, Bs, M, D_MODEL)
    mem_v_cache = cache_mem_v.reshape(depth, Bs, M, D_MODEL)
    mem_rows = mem_prompt.reshape(Bp * M, D_MODEL)

    xp = x_prompt.reshape(Bp * Tp, D_MODEL)
    xs = x_sample.reshape(Bs * Ts, D_MODEL)
    names = ('df_k', 'df_v', 'nsa_cmp', 'nsa_slc', 'nsa_win', 'ml_C', 'ml_n', 'ml_m', 'lru_h', 'lru_conv')
    acc = {pre + n: [] for n in names for pre in ('p_', 's_')}
    acc['p_mem_k'] = []
    acc['p_mem_v'] = []
    for l in range(depth):
        lw = dict(g_mix=g_mix[l], w_in=w_in_p[l], w_out=w_out[l].astype(bf16), ml_gate_b=ml_gate_b[l],
                  ml_norm_g=ml_norm_g[l], df_lam=df_lam[l], df_norm_g=df_norm_g[l],
                  nsa_pos_flat=nsa_pos[l].reshape(1, CMP_BLOCK * 2 * HEAD_DIM), nsa_w1c=w1c[l].astype(bf16),
                  nsa_w2c=w2c[l].astype(bf16), lru_conv_w=lru_conv_w[l], lru_conv_b=lru_conv_b[l],
                  lru_wa_bd=_block_diag(lru_wa[l]).astype(bf16), lru_ba=lru_ba[l],
                  lru_wx_bd=_block_diag(lru_wx[l]).astype(bf16), lru_bx=lru_bx[l], lru_lambda=lru_lambda[l],
                  g_mem_q=g_mem_q[l], w_mq=w_mq[l].astype(bf16), w_mo=w_mo[l].astype(bf16), g_mlp=g_mlp[l],
                  w_up=w_up[l].astype(bf16), w_down=w_down[l].astype(bf16))
        lam_init = 0.8 - 0.6 * math.exp(-0.3 * l)
        mk_p = matmul(mem_rows, w_mk[l].astype(bf16), g=g_mem_kv[l])
        mv_p = matmul(mem_rows, w_mv[l].astype(bf16), g=g_mem_kv[l])
        ml0 = (jnp.zeros((Bp, N_HEADS, HEAD_DIM, HEAD_DIM), f32), jnp.zeros((Bp, N_HEADS, HEAD_DIM), f32),
               jnp.zeros((Bp, N_HEADS), f32))
        lr0 = (jnp.zeros((Bp, GROUP_WIDTH), f32), jnp.zeros((Bp, CONV_W - 1, GROUP_WIDTH), f32))
        xp, new_p = _layer(xp, lw, lam_init, tab_p, B=Bp, T=Tp, pos0=0,
                           mem=(mk_p.reshape(Bp, M, D_MODEL), mv_p.reshape(Bp, M, D_MODEL)),
                           ml_state=ml0, lr_state=lr0, past=None, cfg=cfg_p)
        g_dfk, g_dfv, g_cmp, g_slc = gather_pages(page_table, pools, l)
        past = dict(df_k=g_dfk, df_v=g_dfv, nsa_cmp=g_cmp, nsa_slc=g_slc,
                    nsa_win=cache_nsa_win[l].reshape(Bs, -1, 128))
        xs, new_s = _layer(xs, lw, lam_init, tab_s, B=Bs, T=Ts, pos0=n_past,
                           mem=dict(k=mem_k_cache, v=mem_v_cache, layer=l),
                           ml_state=(state_ml_C[l], state_ml_n[l], state_ml_m[l]),
                           lr_state=(state_lru_h[l], state_lru_conv[l]), past=past, cfg=cfg_s)
        for pre, new in (('p_', new_p), ('s_', new_s)):
            for n in names:
                acc[pre + n].append(new[n])
        acc['p_mem_k'].append(mk_p.reshape(Bp, M, MEM_HEADS, MEM_HEAD_DIM))
        acc['p_mem_v'].append(mv_p.reshape(Bp, M, MEM_HEADS, MEM_HEAD_DIM))
    st = {k: jnp.stack(v) for k, v in acc.items()}
    y_prompt = rmsnorm_rows(xp, g_final).reshape(Bp, Tp, D_MODEL)
    y_sample = rmsnorm_rows(xs, g_final).reshape(Bs, Ts, D_MODEL)
    return (y_prompt, y_sample, st['p_df_k'], st['s_df_k'], st['p_df_v'], st['s_df_v'],
            st['p_nsa_cmp'], st['s_nsa_cmp'], st['p_nsa_slc'], st['s_nsa_slc'],
            st['p_nsa_win'], st['s_nsa_win'], st['p_ml_C'], st['s_ml_C'], st['p_ml_n'], st['s_ml_n'],
            st['p_ml_m'], st['s_ml_m'], st['p_lru_h'], st['s_lru_h'], st['p_lru_conv'], st['s_lru_conv'],
            st['p_mem_k'], st['p_mem_v'])
```

```python
import functools
import math

import jax
import jax.numpy as jnp
from jax import lax
from jax.experimental import pallas as pl
from jax.experimental.pallas import tpu as pltpu

f32 = jnp.float32
bf16 = jnp.bfloat16

D_MODEL = 1024
GROUP_WIDTH = 256
N_HEADS = 4
HEAD_DIM = 64
DQK = 32
ROPE_THETA = 10000.0
CMP_BLOCK = 32
CMP_HIDDEN = 128
SEL_BLOCK = 64
SEL_TOPK = 16
WINDOW = 512
CONV_W = 4
LRU_C = 8.0
MEM_HEADS = 4
MEM_HEAD_DIM = 256
EPS = 1e-6
NEG = -1e30
FORCE_SCORE = 1e9
TINY = 1e-30

VMEM_LIMIT_BYTES = 48 * 1024 * 1024
PAGES_PER_STEP = 16

_SRC_SPLITS = (
    ('ml_q', 256), ('ml_k', 256), ('ml_v', 256), ('ml_i', 4), ('ml_f', 4), ('ml_o', 256),
    ('df_q', 256), ('df_k', 256), ('df_v', 256),
    ('ns_q', 256), ('ns_kc', 64), ('ns_vc', 64), ('ns_ks', 64), ('ns_vs', 64),
    ('ns_kw', 64), ('ns_vw', 64), ('ns_g', 12), ('lr_x', 256), ('lr_g', 256),
)
_DST_ORDER = ('ml_q', 'ml_k', 'ml_v', 'ml_o', 'df_q', 'df_k', 'df_v', 'ns_q', 'lr_x', 'lr_g',
              'ns_kc', 'ns_vc', 'ns_ks', 'ns_vs', 'ns_kw', 'ns_vw', 'ml_i', 'ml_f', 'ns_g')
D_PROJ_PAD = 3072
CB_ML_Q, CB_ML_K, CB_ML_V, CB_ML_O, CB_DF_Q, CB_DF_K, CB_DF_V, CB_NS_Q, CB_LR_X, CB_LR_G = range(10)
CB128_CMP, CB128_SLC, CB128_WIN, CB128_SMALL = 20, 21, 22, 23
SMALL_GATES = 8


def _proj_perm():
    off = {}
    o = 0
    for name, w in _SRC_SPLITS:
        off[name] = (o, w)
        o += w
    idx = []
    for name in _DST_ORDER:
        s, w = off[name]
        idx.extend(range(s, s + w))
    n_real = len(idx)
    idx.extend([0] * (D_PROJ_PAD - n_real))
    return jnp.asarray(idx, jnp.int32), n_real


def _cparams(sem):
    return pltpu.CompilerParams(dimension_semantics=sem, vmem_limit_bytes=VMEM_LIMIT_BYTES)


def _nt(a, b):
    return lax.dot_general(a, b, (((1,), (1,)), ((), ())), preferred_element_type=f32)


def _mm_kernel(*refs, n_parts, has_norm, has_res, act):
    it = iter(refs)
    x_refs = [next(it) for _ in range(n_parts)]
    w_ref = next(it)
    g_ref = next(it) if has_norm else None
    r_ref = next(it) if has_res else None
    o_ref = next(it)
    h_ref = next(it) if has_norm else None
    if has_norm:
        @pl.when(pl.program_id(1) == 0)
        def _():
            x = x_refs[0][...].astype(f32)
            y = x * lax.rsqrt(jnp.mean(x * x, axis=-1, keepdims=True) + EPS)
            h_ref[...] = (y * g_ref[...]).astype(bf16)
        acc = jnp.dot(h_ref[...], w_ref[...], preferred_element_type=f32)
    else:
        acc = None
        off = 0
        for x_ref in x_refs:
            kp = x_ref.shape[-1]
            part = jnp.dot(x_ref[...].astype(bf16), w_ref[off:off + kp, :], preferred_element_type=f32)
            acc = part if acc is None else acc + part
            off += kp
    if act == 'relu2':
        acc = jnp.maximum(acc, 0.0)
        acc = acc * acc
    if has_res:
        acc = acc + r_ref[...]
    o_ref[...] = acc.astype(o_ref.dtype)


def matmul(x, w, *, g=None, res=None, act=None, out_dtype=f32, tm=1024, tn=512):
    parts = list(x) if isinstance(x, (list, tuple)) else [x]
    M = parts[0].shape[0]
    K, N = w.shape
    assert sum(p.shape[1] for p in parts) == K
    tm = min(tm, M)
    tn = min(tn, N)
    assert M % tm == 0 and N % tn == 0
    has_norm = g is not None
    has_res = res is not None
    assert not (has_norm and len(parts) > 1)
    in_specs = [pl.BlockSpec((tm, p.shape[1]), lambda i, j: (i, 0)) for p in parts]
    in_specs.append(pl.BlockSpec((K, tn), lambda i, j: (0, j)))
    args = parts + [w]
    if has_norm:
        in_specs.append(pl.BlockSpec((1, K), lambda i, j: (0, 0)))
        args.append(g.reshape(1, K).astype(f32))
    if has_res:
        in_specs.append(pl.BlockSpec((tm, tn), lambda i, j: (i, j)))
        args.append(res)
    scratch = [pltpu.VMEM((tm, K), bf16)] if has_norm else []
    return pl.pallas_call(
        functools.partial(_mm_kernel, n_parts=len(parts), has_norm=has_norm, has_res=has_res, act=act),
        out_shape=jax.ShapeDtypeStruct((M, N), out_dtype),
        grid=(M // tm, N // tn),
        in_specs=in_specs,
        out_specs=pl.BlockSpec((tm, tn), lambda i, j: (i, j)),
        scratch_shapes=scratch,
        compiler_params=_cparams(("parallel", "arbitrary")),
        name="matmul",
    )(*args)


def _rmsnorm_kernel(x_ref, g_ref, o_ref):
    x = x_ref[...]
    y = x * lax.rsqrt(jnp.mean(x * x, axis=-1, keepdims=True) + EPS)
    o_ref[...] = y * g_ref[...]


def rmsnorm_rows(x, g, tm=1024):
    M, K = x.shape
    tm = min(tm, M)
    return pl.pallas_call(
        _rmsnorm_kernel,
        out_shape=jax.ShapeDtypeStruct((M, K), f32),
        grid=(M // tm,),
        in_specs=[pl.BlockSpec((tm, K), lambda i: (i, 0)), pl.BlockSpec((1, K), lambda i: (0, 0))],
        out_specs=pl.BlockSpec((tm, K), lambda i: (i, 0)),
        compiler_params=_cparams(("parallel",)),
        name="final_norm",
    )(x, g.reshape(1, K))


def _rotate(x, cos, sin, half):
    n = x.shape[-1]
    lane = lax.broadcasted_iota(jnp.int32, x.shape, 1)
    first = (lane & (2 * half - 1)) < half
    partner = jnp.where(first, pltpu.roll(x, n - half, 1), pltpu.roll(x, half, 1))
    return x * cos + partner * sin


def _rope_kernel(dq, dk, nq, sl, wn, ca, sa, cb, sb, cc, sc, odq, odk, onq, osl, own):
    odq[...] = _rotate(dq[...], ca[...], sa[...], DQK // 2)
    odk[...] = _rotate(dk[...], ca[...], sa[...], DQK // 2)
    onq[...] = _rotate(nq[...], cb[...], sb[...], HEAD_DIM // 2)
    osl[...] = _rotate(sl[...], cc[...], sc[...], HEAD_DIM // 2)
    own[...] = _rotate(wn[...], cc[...], sc[...], HEAD_DIM // 2)


def rope_tables(pos):
    posf = pos.astype(f32)[:, None]

    def tab(half, reps):
        inv = ROPE_THETA ** (-jnp.arange(half, dtype=f32) / half)
        ang = posf * inv[None, :]
        c = jnp.cos(ang)
        s = jnp.sin(ang)
        return jnp.tile(jnp.concatenate([c, c], 1), (1, reps)), jnp.tile(jnp.concatenate([-s, s], 1), (1, reps))

    ca, sa = tab(DQK // 2, GROUP_WIDTH // DQK)
    cb, sb = tab(HEAD_DIM // 2, N_HEADS)
    n = pos.shape[0]
    cc = jnp.concatenate([cb[:, :HEAD_DIM], jnp.ones((n, HEAD_DIM), f32)], 1)
    sc = jnp.concatenate([sb[:, :HEAD_DIM], jnp.zeros((n, HEAD_DIM), f32)], 1)
    return ca, sa, cb, sb, cc, sc


def rope_prep(z, tables, tm):
    N = z.shape[0]
    R = tables[0].shape[0]
    tm = min(tm, R)
    assert R % tm == 0 and N % tm == 0
    nr = R // tm
    zs = lambda cb: pl.BlockSpec((tm, 256), lambda i, cb=cb: (i, cb))
    zs128 = lambda cb: pl.BlockSpec((tm, 128), lambda i, cb=cb: (i, cb))
    t256 = pl.BlockSpec((tm, 256), lambda i: (i % nr, 0))
    t128 = pl.BlockSpec((tm, 128), lambda i: (i % nr, 0))
    o256 = pl.BlockSpec((tm, 256), lambda i: (i, 0))
    o128 = pl.BlockSpec((tm, 128), lambda i: (i, 0))
    return pl.pallas_call(
        _rope_kernel,
        out_shape=[jax.ShapeDtypeStruct((N, 256), f32)] * 3 + [jax.ShapeDtypeStruct((N, 128), f32)] * 2,
        grid=(N // tm,),
        in_specs=[zs(CB_DF_Q), zs(CB_DF_K), zs(CB_NS_Q), zs128(CB128_SLC), zs128(CB128_WIN),
                  t256, t256, t256, t256, t128, t128],
        out_specs=[o256, o256, o256, o128, o128],
        compiler_params=_cparams(("parallel",)),
        name="rope_prep",
    )(z, z, z, z, z, *tables)


def _diff_lambda(lam_ref, lam_init):
    lq = lam_ref[...]
    return (jnp.exp(jnp.sum(lq[0:1] * lq[1:2], keepdims=True))
            - jnp.exp(jnp.sum(lq[2:3] * lq[3:4], keepdims=True)) + lam_init)


def _diff_kernel(q_ref, k_ref, v_ref, lam_ref, g_ref, o_ref, qp_ref, m_ref, l_ref, acc_ref, oT_ref,
                 *, tq, tk, nk, lam_init):
    qi = pl.program_id(1)
    kj = pl.program_id(2)
    q_lo = qi * tq
    needed = (q_lo + tq - 1) // tk + 1
    nmap = 2 * N_HEADS

    @pl.when(kj == 0)
    def _():
        qT = jnp.transpose(q_ref[0]) * (DQK ** -0.5)
        row = lax.broadcasted_iota(jnp.int32, qT.shape, 0)
        for c in range(nmap):
            blk = jnp.where((row >= DQK * c) & (row < DQK * (c + 1)), qT, 0.0)
            qp_ref[:, c * tq:(c + 1) * tq] = blk.astype(bf16)
        m_ref[...] = jnp.full(m_ref.shape, NEG, f32)
        l_ref[...] = jnp.zeros(l_ref.shape, f32)
        acc_ref[...] = jnp.zeros(acc_ref.shape, f32)

    def step(masked):
        k = k_ref[0].astype(bf16)
        s = jnp.dot(k, qp_ref[...], preferred_element_type=f32)
        if masked:
            kpos = kj * tk + lax.broadcasted_iota(jnp.int32, s.shape, 0)
            qpos = q_lo + (lax.broadcasted_iota(jnp.int32, s.shape, 1) & (tq - 1))
            ok = kpos <= qpos
            s = jnp.where(ok, s, NEG)
        m_old = m_ref[...]
        m_new = jnp.maximum(m_old, jnp.max(s, axis=0, keepdims=True))
        alpha = jnp.exp(m_old - m_new)
        p = jnp.exp(s - m_new)
        if masked:
            p = jnp.where(ok, p, 0.0)
        l_ref[...] = alpha * l_ref[...] + jnp.sum(p, axis=0, keepdims=True)
        m_ref[...] = m_new
        pb = p.astype(bf16)
        vT = jnp.transpose(v_ref[0]).astype(bf16)
        for h in range(N_HEADS):
            rs = slice(HEAD_DIM * h, HEAD_DIM * (h + 1))
            cs = slice(2 * tq * h, 2 * tq * (h + 1))
            acc_ref[rs, :] = acc_ref[rs, :] * alpha[:, cs] + jnp.dot(
                vT[rs, :], pb[:, cs], preferred_element_type=f32)

    active = kj < needed
    crosses = kj * tk + tk - 1 > q_lo

    @pl.when(active & crosses)
    def _():
        step(True)

    @pl.when(active & jnp.logical_not(crosses))
    def _():
        step(False)

    @pl.when(kj == nk - 1)
    def _():
        lam = _diff_lambda(lam_ref, lam_init)
        l = l_ref[...]
        for h in range(N_HEADS):
            rs = slice(HEAD_DIM * h, HEAD_DIM * (h + 1))
            c1 = slice(2 * h * tq, (2 * h + 1) * tq)
            c2 = slice((2 * h + 1) * tq, (2 * h + 2) * tq)
            o = acc_ref[rs, 0:tq] / l[:, c1] - lam * (acc_ref[rs, tq:2 * tq] / l[:, c2])
            y = o * lax.rsqrt(jnp.mean(o * o, axis=0, keepdims=True) + EPS)
            oT_ref[rs, :] = (y * g_ref[...]) * (1.0 - lam_init)
        o_ref[0] = jnp.transpose(oT_ref[...])


def diff_attention(q, k, z3, lam_q, norm_g, *, lam_init, tq, tk):
    B, T, _ = q.shape
    assert T % tq == 0 and T % tk == 0 and (tq & (tq - 1)) == 0 and tq % 128 == 0 and tk % 128 == 0
    nq, nk = T // tq, T // tk
    last = lambda qi: (qi * tq + tq - 1) // tk
    kern = functools.partial(_diff_kernel, tq=tq, tk=tk, nk=nk, lam_init=lam_init)
    return pl.pallas_call(
        kern,
        out_shape=jax.ShapeDtypeStruct((B, T, GROUP_WIDTH), f32),
        grid=(B, nq, nk),
        in_specs=[
            pl.BlockSpec((1, tq, GROUP_WIDTH), lambda b, qi, kj: (b, qi, 0)),
            pl.BlockSpec((1, tk, GROUP_WIDTH), lambda b, qi, kj: (b, jnp.minimum(kj, last(qi)), 0)),
            pl.BlockSpec((1, tk, GROUP_WIDTH), lambda b, qi, kj: (b, jnp.minimum(kj, last(qi)), CB_DF_V)),
            pl.BlockSpec((4, DQK), lambda b, qi, kj: (0, 0)),
            pl.BlockSpec((HEAD_DIM, 1), lambda b, qi, kj: (0, 0)),
        ],
        out_specs=pl.BlockSpec((1, tq, GROUP_WIDTH), lambda b, qi, kj: (b, qi, 0)),
        scratch_shapes=[pltpu.VMEM((GROUP_WIDTH, 8 * tq), bf16), pltpu.VMEM((1, 8 * tq), f32),
                        pltpu.VMEM((1, 8 * tq), f32), pltpu.VMEM((GROUP_WIDTH, 2 * tq), f32),
                        pltpu.VMEM((GROUP_WIDTH, tq), f32)],
        compiler_params=_cparams(("parallel", "parallel", "arbitrary")),
        name="diff_attention",
    )(q, k, z3, lam_q, norm_g.reshape(HEAD_DIM, 1))


def _diff_decode_kernel(pt_ref, q_ref, kn_ref, vn_ref, lam_ref, g_ref, *rest, G, n_steps, Ts, pos0, n_past,
                        lam_init):
    kpages = rest[:G]
    vpages = rest[G:2 * G]
    o_ref, qp_ref, m_ref, l_ref, acc_ref = rest[2 * G:]
    j = pl.program_id(1)
    nmap = 2 * N_HEADS
    R = nmap * Ts

    @pl.when(j == 0)
    def _():
        q = q_ref[0] * (DQK ** -0.5)
        col = lax.broadcasted_iota(jnp.int32, q.shape, 1)
        for c in range(nmap):
            qp_ref[c * Ts:(c + 1) * Ts, :] = jnp.where((col >= DQK * c) & (col < DQK * (c + 1)), q, 0.0)
        s = _nt(qp_ref[...].astype(bf16), kn_ref[0].astype(bf16))
        qpos = pos0 + (lax.broadcasted_iota(jnp.int32, s.shape, 0) & (Ts - 1))
        kpos = n_past + lax.broadcasted_iota(jnp.int32, s.shape, 1)
        ok = kpos <= qpos
        s = jnp.where(ok, s, NEG)
        m = jnp.max(s, axis=1, keepdims=True)
        p = jnp.where(ok, jnp.exp(s - m), 0.0)
        m_ref[...] = m
        l_ref[...] = jnp.sum(p, axis=1, keepdims=True)
        acc_ref[...] = jnp.dot(p.astype(bf16), vn_ref[0].astype(bf16), preferred_element_type=f32)

    k = jnp.concatenate([kp[...] for kp in kpages], axis=0).astype(bf16)
    v = jnp.concatenate([vp[...] for vp in vpages], axis=0).astype(bf16)
    s = _nt(qp_ref[...].astype(bf16), k)
    m_old = m_ref[...]
    m_new = jnp.maximum(m_old, jnp.max(s, axis=1, keepdims=True))
    alpha = jnp.exp(m_old - m_new)
    p = jnp.exp(s - m_new)
    l_ref[...] = alpha * l_ref[...] + jnp.sum(p, axis=1, keepdims=True)
    m_ref[...] = m_new
    acc_ref[...] = alpha * acc_ref[...] + jnp.dot(p.astype(bf16), v, preferred_element_type=f32)

    @pl.when(j == n_steps - 1)
    def _():
        lam = _diff_lambda(lam_ref, lam_init)
        o_all = acc_ref[...] / l_ref[...]
        ys = []
        for h in range(N_HEADS):
            cs = slice(HEAD_DIM * h, HEAD_DIM * (h + 1))
            o = o_all[2 * h * Ts:(2 * h + 1) * Ts, cs] - lam * o_all[(2 * h + 1) * Ts:(2 * h + 2) * Ts, cs]
            y = o * lax.rsqrt(jnp.mean(o * o, axis=1, keepdims=True) + EPS)
            ys.append((y * g_ref[...]) * (1.0 - lam_init))
        o_ref[0] = jnp.concatenate(ys, axis=1)


def _page_specs(layer, n_pages, G, block):
    def spec(i):
        return pl.BlockSpec((None, None) + block,
                            lambda b, j, pt, i=i: (layer, pt[b * n_pages + j * G + i]) + (0,) * len(block))
    return [spec(i) for i in range(G)]


def diff_decode(page_table, q, k_new, v_new, pool_k, pool_v, lam_q, norm_g, *, layer, pos0, lam_init):
    B, Ts, _ = q.shape
    n_pages = page_table.shape[1]
    page = pool_k.shape[2]
    G = min(PAGES_PER_STEP, n_pages)
    assert n_pages % G == 0 and (Ts & (Ts - 1)) == 0 and Ts % 8 == 0
    n_steps = n_pages // G
    R = 2 * N_HEADS * Ts
    row = lambda w: pl.BlockSpec((1, Ts, w), lambda b, j, pt: (b, 0, 0))
    kern = functools.partial(_diff_decode_kernel, G=G, n_steps=n_steps, Ts=Ts, pos0=pos0,
                             n_past=n_pages * page, lam_init=lam_init)
    return pl.pallas_call(
        kern,
        out_shape=jax.ShapeDtypeStruct((B, Ts, GROUP_WIDTH), f32),
        grid_spec=pltpu.PrefetchScalarGridSpec(
            num_scalar_prefetch=1, grid=(B, n_steps),
            in_specs=[row(256), row(256), row(256),
                      pl.BlockSpec((4, DQK), lambda b, j, pt: (0, 0)),
                      pl.BlockSpec((1, HEAD_DIM), lambda b, j, pt: (0, 0))]
            + _page_specs(layer, n_pages, G, (page, 256)) + _page_specs(layer, n_pages, G, (page, 256)),
            out_specs=row(256),
            scratch_shapes=[pltpu.VMEM((R, GROUP_WIDTH), f32), pltpu.VMEM((R, 1), f32),
                            pltpu.VMEM((R, 1), f32), pltpu.VMEM((R, GROUP_WIDTH), f32)]),
        compiler_params=_cparams(("parallel", "arbitrary")),
        name="diff_decode",
    )(page_table.reshape(-1), q, k_new, v_new, lam_q, norm_g.reshape(1, HEAD_DIM),
      *([pool_k] * G), *([pool_v] * G))


def _compress_rows(x, pos_ref, w1_ref, w2_ref):
    x = (x + pos_ref[...]).astype(bf16)
    hid = jax.nn.gelu(jnp.dot(x, w1_ref[...], preferred_element_type=f32))
    return jnp.dot(hid.astype(bf16), w2_ref[...], preferred_element_type=f32)


def _compress_kernel(x_ref, pos_ref, w1_ref, w2_ref, o_ref):
    o_ref[...] = _compress_rows(x_ref[...], pos_ref, w1_ref, w2_ref)


def nsa_compress(blocks, pos_flat, w1c, w2c, tm=256):
    R, K = blocks.shape
    tm = min(tm, R)
    assert R % tm == 0
    return pl.pallas_call(
        _compress_kernel,
        out_shape=jax.ShapeDtypeStruct((R, 2 * HEAD_DIM), f32),
        grid=(R // tm,),
        in_specs=[pl.BlockSpec((tm, K), lambda i: (i, 0)), pl.BlockSpec((1, K), lambda i: (0, 0)),
                  pl.BlockSpec((K, 2 * CMP_HIDDEN), lambda i: (0, 0)),
                  pl.BlockSpec((2 * CMP_HIDDEN, 2 * HEAD_DIM), lambda i: (0, 0))],
        out_specs=pl.BlockSpec((tm, 2 * HEAD_DIM), lambda i: (i, 0)),
        compiler_params=_cparams(("parallel",)),
        name="nsa_compress",
    )(blocks, pos_flat, w1c, w2c)


def _compress_paged_kernel(pt_ref, pos_ref, w1_ref, w2_ref, *rest, G, rp):
    pages = rest[:G]
    o_ref, xs = rest[G:]
    for i in range(G):
        xs[rp * i:rp * (i + 1), :] = pages[i][...]
    o_ref[0] = _compress_rows(xs[...], pos_ref, w1_ref, w2_ref)


def nsa_compress_paged(page_table, pool, pos_flat, w1c, w2c, *, layer):
    B, n_pages = page_table.shape
    rp, K = pool.shape[2], pool.shape[3]
    G = min(PAGES_PER_STEP, n_pages)
    assert n_pages % G == 0
    const = lambda shape: pl.BlockSpec(shape, lambda b, j, pt: (0, 0))
    return pl.pallas_call(
        functools.partial(_compress_paged_kernel, G=G, rp=rp),
        out_shape=jax.ShapeDtypeStruct((B, n_pages * rp, 2 * HEAD_DIM), f32),
        grid_spec=pltpu.PrefetchScalarGridSpec(
            num_scalar_prefetch=1, grid=(B, n_pages // G),
            in_specs=[const((1, K)), const((K, 2 * CMP_HIDDEN)), const((2 * CMP_HIDDEN, 2 * HEAD_DIM))]
            + _page_specs(layer, n_pages, G, (rp, K)),
            out_specs=pl.BlockSpec((1, G * rp, 2 * HEAD_DIM), lambda b, j, pt: (b, j, 0)),
            scratch_shapes=[pltpu.VMEM((G * rp, K), f32)]),
        compiler_params=_cparams(("parallel", "arbitrary")),
        name="nsa_compress_paged",
    )(page_table.reshape(-1), pos_flat, w1c, w2c, *([pool] * G))


def _cmp_block_end(r, half):
    blk = jnp.where(r < half, 2 * r, 2 * (r - half) + 1)
    return (blk + 1) * CMP_BLOCK - 1


def _nsa_kernel(qr_in, qw_in, sm_in, cmp_ref, slc_ref, win_ref,
                o_ref, qr_ref, v_ref, sel_ref, ocmp_ref, ms_ref, ls_ref, accs_ref, mw_ref, lw_ref, accw_ref,
                oT_ref, *, tq, tk, tkw, nk, nc, nsp, topk):
    qi = pl.program_id(1)
    kj = pl.program_id(2)
    q_lo = qi * tq
    H = N_HEADS
    d = HEAD_DIM
    scale = d ** -0.5
    needed = (q_lo + tq - 1) // tk + 1
    w_lo = jnp.maximum(q_lo - (WINDOW - 1), 0) // tkw
    w_hi = (q_lo + tq - 1) // tkw
    half = nc // 2

    def heads_on_lanes(xT):
        return jnp.concatenate([xT[d * h:d * (h + 1), :] for h in range(H)], axis=1)

    def qpos_row(n):
        return q_lo + (lax.broadcasted_iota(jnp.int32, (1, n), 1) & (tq - 1))

    @pl.when(kj == 0)
    def _():
        zeros = jnp.zeros((d, H * tq), f32)
        qrT = jnp.transpose(qr_in[0]) * scale
        qwT = jnp.transpose(qw_in[0]) * scale
        qr_ref[...] = jnp.concatenate([heads_on_lanes(qrT), zeros], 0).astype(bf16)
        qw = jnp.concatenate([heads_on_lanes(qwT), zeros], 0).astype(bf16)
        cmp = cmp_ref[0]
        s = jnp.dot(cmp.astype(bf16), qw, preferred_element_type=f32)
        c_end = _cmp_block_end(lax.broadcasted_iota(jnp.int32, (nc, 1), 0), half)
        c_ok = c_end <= qpos_row(H * tq)
        s = jnp.where(c_ok, s, NEG)
        p = jnp.where(c_ok, jnp.exp(s - jnp.max(s, axis=0, keepdims=True)), 0.0)
        p = p / jnp.maximum(jnp.sum(p, axis=0, keepdims=True), TINY)
        cmpT = jnp.transpose(cmp)
        ocmp_ref[...] = jnp.dot(cmpT[d:2 * d, :].astype(bf16), p.astype(bf16), preferred_element_type=f32)
        imp = p[:, 0:tq]
        for h in range(1, H):
            imp = imp + p[:, h * tq:(h + 1) * tq]
        imp = imp[:half, :] + imp[half:, :]
        if nsp > half:
            imp = jnp.concatenate([imp, jnp.zeros((nsp - half, tq), f32)], 0)
        sb = lax.broadcasted_iota(jnp.int32, (nsp, tq), 0)
        cur = qpos_row(tq) // SEL_BLOCK
        v = jnp.where((sb == cur) | (sb == 0), FORCE_SCORE, imp)
        v = jnp.where(sb > cur, NEG, v)
        v_ref[...] = v

        def rank(i, cnt):
            vi = v_ref[pl.ds(i, 1), :]
            ahead = (vi > v) | ((vi == v) & (i < sb))
            return cnt + jnp.where(ahead, 1.0, 0.0)

        cnt = lax.fori_loop(0, nsp, rank, jnp.zeros((nsp, tq), f32))
        sel_ref[...] = jnp.where(cnt < topk, 1.0, 0.0)
        for m_r, l_r, a_r in ((ms_ref, ls_ref, accs_ref), (mw_ref, lw_ref, accw_ref)):
            m_r[...] = jnp.full(m_r.shape, NEG, f32)
            l_r[...] = jnp.zeros(l_r.shape, f32)
            a_r[...] = jnp.zeros(a_r.shape, f32)

    def flash(kv_ref, ok, m_r, l_r, a_r):
        kv = kv_ref[0]
        s = jnp.dot(kv.astype(bf16), qr_ref[...], preferred_element_type=f32)
        ok4 = jnp.concatenate([jnp.where(ok, 1.0, 0.0)] * H, axis=1) > 0.5
        s = jnp.where(ok4, s, NEG)
        m_old = m_r[...]
        m_new = jnp.maximum(m_old, jnp.max(s, axis=0, keepdims=True))
        alpha = jnp.exp(m_old - m_new)
        p = jnp.where(ok4, jnp.exp(s - m_new), 0.0)
        l_r[...] = alpha * l_r[...] + jnp.sum(p, axis=0, keepdims=True)
        m_r[...] = m_new
        vT = jnp.transpose(kv)[d:2 * d, :].astype(bf16)
        a_r[...] = a_r[...] * alpha + jnp.dot(vT, p.astype(bf16), preferred_element_type=f32)

    @pl.when(kj < needed)
    def _():
        kpos = kj * tk + lax.broadcasted_iota(jnp.int32, (tk, tq), 0)
        nb = tk // SEL_BLOCK
        rows = [jnp.broadcast_to(sel_ref[pl.ds(kj * nb + c, 1), :], (SEL_BLOCK, tq)) for c in range(nb)]
        chosen = jnp.concatenate(rows, axis=0) > 0.5
        ok = chosen & (kpos <= qpos_row(tq))
        flash(slc_ref, ok, ms_ref, ls_ref, accs_ref)

    @pl.when(kj <= w_hi - w_lo)
    def _():
        wpos = (w_lo + kj) * tkw + lax.broadcasted_iota(jnp.int32, (tkw, tq), 0)
        qp = qpos_row(tq)
        ok = (wpos <= qp) & (qp - wpos < WINDOW) & (wpos >= 0)
        flash(win_ref, ok, mw_ref, lw_ref, accw_ref)

    @pl.when(kj == nk - 1)
    def _():
        smT = jnp.transpose(sm_in[0])
        g = jax.nn.sigmoid(smT[SMALL_GATES:SMALL_GATES + 3 * H, :])
        o_slc = accs_ref[...] / ls_ref[...]
        o_win = accw_ref[...] / lw_ref[...]
        o_cmp = ocmp_ref[...]
        for h in range(H):
            cs = slice(h * tq, (h + 1) * tq)
            oT_ref[d * h:d * (h + 1), :] = (g[3 * h:3 * h + 1] * o_cmp[:, cs]
                                            + g[3 * h + 1:3 * h + 2] * o_slc[:, cs]
                                            + g[3 * h + 2:3 * h + 3] * o_win[:, cs])
        o_ref[0] = jnp.transpose(oT_ref[...])


def nsa_attention(qr, z3, cmp, slc, win, *, tq, tk, tkw):
    B, T, _ = qr.shape
    nc = cmp.shape[1]
    assert T % tq == 0 and T % tk == 0 and T % tkw == 0 and tk % SEL_BLOCK == 0 and nc % 2 == 0
    assert tq % 128 == 0 and tk % 128 == 0 and tkw % 128 == 0 and nc % 8 == 0 and (tq & (tq - 1)) == 0
    nq, nk = T // tq, T // tk
    nsp = T // SEL_BLOCK
    assert nsp >= nc // 2 and nsp % 8 == 0
    last = lambda qi: (qi * tq + tq - 1) // tk
    w_lo = lambda qi: jnp.maximum(qi * tq - (WINDOW - 1), 0) // tkw
    w_hi = lambda qi: (qi * tq + tq - 1) // tkw
    for qi in range(nq):
        lo = max(qi * tq - (WINDOW - 1), 0) // tkw
        assert (qi * tq + tq - 1) // tkw - lo + 1 <= nk
    w_idx = lambda qi, kj: jnp.minimum(w_lo(qi) + kj, w_hi(qi))
    kern = functools.partial(_nsa_kernel, tq=tq, tk=tk, tkw=tkw, nk=nk, nc=nc, nsp=nsp,
                             topk=min(SEL_TOPK, nsp))
    stat = pltpu.VMEM((1, N_HEADS * tq), f32)
    acc = pltpu.VMEM((HEAD_DIM, N_HEADS * tq), f32)
    return pl.pallas_call(
        kern,
        out_shape=jax.ShapeDtypeStruct((B, T, GROUP_WIDTH), f32),
        grid=(B, nq, nk),
        in_specs=[
            pl.BlockSpec((1, tq, GROUP_WIDTH), lambda b, qi, kj: (b, qi, 0)),
            pl.BlockSpec((1, tq, GROUP_WIDTH), lambda b, qi, kj: (b, qi, CB_NS_Q)),
            pl.BlockSpec((1, tq, 128), lambda b, qi, kj: (b, qi, CB128_SMALL)),
            pl.BlockSpec((1, nc, 128), lambda b, qi, kj: (b, 0, 0)),
            pl.BlockSpec((1, tk, 128), lambda b, qi, kj: (b, jnp.minimum(kj, last(qi)), 0)),
            pl.BlockSpec((1, tkw, 128), lambda b, qi, kj: (b, w_idx(qi, kj), 0)),
        ],
        out_specs=pl.BlockSpec((1, tq, GROUP_WIDTH), lambda b, qi, kj: (b, qi, 0)),
        scratch_shapes=[pltpu.VMEM((128, N_HEADS * tq), bf16), pltpu.VMEM((nsp, tq), f32),
                        pltpu.VMEM((nsp, tq), f32), acc, stat, stat, acc, stat, stat, acc,
                        pltpu.VMEM((GROUP_WIDTH, tq), f32)],
        compiler_params=_cparams(("parallel", "parallel", "arbitrary")),
        name="nsa_attention",
    )(qr, z3, z3, cmp, slc, win)


def _nsa_decode_kernel(pt_ref, qr_in, qw_in, sm_in, cmp_ref, sn_ref, wb_ref, wn_ref, *rest,
                       G, n_steps, Ts, pos0, n_past, wpos0, nc, nsl, topk, page):
    pages = rest[:G]
    o_ref, qr_ref, selst_ref, ocmp_ref, owin_ref, m_ref, l_ref, acc_ref = rest[G:]
    j = pl.program_id(1)
    H, d = N_HEADS, HEAD_DIM
    scale = d ** -0.5
    R = H * Ts
    half = nc // 2
    bps = G * page // SEL_BLOCK

    def rows_by_head(x):
        z = jnp.zeros((Ts, d), f32)
        return jnp.concatenate([jnp.concatenate([x[:, d * h:d * (h + 1)], z], axis=1) for h in range(H)], axis=0)

    def tile_heads(x):
        return jnp.concatenate([x] * H, axis=0)

    @pl.when(j == 0)
    def _():
        qr = rows_by_head(qr_in[0] * scale)
        qr_ref[...] = qr
        qrb = qr.astype(bf16)
        qwb = rows_by_head(qw_in[0] * scale).astype(bf16)
        qpos = pos0 + (lax.broadcasted_iota(jnp.int32, (R, 1), 0) & (Ts - 1))
        cmpb = cmp_ref[0].astype(bf16)
        s = _nt(qwb, cmpb)
        c_ok = _cmp_block_end(lax.broadcasted_iota(jnp.int32, (1, nc), 1), half) <= qpos
        s = jnp.where(c_ok, s, NEG)
        p = jnp.where(c_ok, jnp.exp(s - jnp.max(s, axis=1, keepdims=True)), 0.0)
        p = p / jnp.maximum(jnp.sum(p, axis=1, keepdims=True), TINY)
        ocmp_ref[...] = jnp.dot(p.astype(bf16), cmpb, preferred_element_type=f32)
        imp = p[0:Ts, :]
        for h in range(1, H):
            imp = imp + p[h * Ts:(h + 1) * Ts, :]
        imp = imp[:, :half] + imp[:, half:]
        imp = jnp.concatenate([imp, jnp.zeros((Ts, nsl - half), f32)], axis=1)
        lane = lax.broadcasted_iota(jnp.int32, (Ts, nsl), 1)
        cur = (pos0 + lax.broadcasted_iota(jnp.int32, (Ts, 1), 0)) // SEL_BLOCK
        v = jnp.where((lane == cur) | (lane == 0), FORCE_SCORE, imp)
        v = jnp.where(lane > cur, NEG, v)
        lanef = lane.astype(f32)
        sel = jnp.zeros((Ts, nsl), f32)
        for _ in range(topk):
            mx = jnp.max(v, axis=1, keepdims=True)
            first = jnp.min(jnp.where(v == mx, lanef, float(nsl)), axis=1, keepdims=True)
            pick = lanef == first
            sel = jnp.where(pick, 1.0, sel)
            v = jnp.where(pick, -jnp.inf, v)
        for st in range(n_steps):
            selst_ref[st] = sel[:, bps * st:bps * (st + 1)]
        blk_new = n_past // SEL_BLOCK
        sel_new = tile_heads(sel[:, blk_new:blk_new + 1]) > 0.5
        snb = sn_ref[0].astype(bf16)
        s = _nt(qrb, snb)
        kpos = n_past + lax.broadcasted_iota(jnp.int32, (1, Ts), 1)
        ok = sel_new & (kpos <= qpos)
        s = jnp.where(ok, s, NEG)
        m = jnp.max(s, axis=1, keepdims=True)
        p = jnp.where(ok, jnp.exp(s - m), 0.0)
        m_ref[...] = m
        l_ref[...] = jnp.sum(p, axis=1, keepdims=True)
        acc_ref[...] = jnp.dot(p.astype(bf16), snb, preferred_element_type=f32)
        wbb = wb_ref[...].astype(bf16)
        wnb = wn_ref[0].astype(bf16)
        lw = wbb.shape[0]
        s1 = _nt(qrb, wbb)
        s2 = _nt(qrb, wnb)
        wp1 = wpos0 + lax.broadcasted_iota(jnp.int32, (1, lw), 1)
        wp2 = wpos0 + lw + lax.broadcasted_iota(jnp.int32, (1, Ts), 1)
        ok1 = (wp1 <= qpos) & (qpos - wp1 < WINDOW) & (wp1 >= 0)
        ok2 = (wp2 <= qpos) & (qpos - wp2 < WINDOW) & (wp2 >= 0)
        s1 = jnp.where(ok1, s1, NEG)
        s2 = jnp.where(ok2, s2, NEG)
        mw = jnp.maximum(jnp.max(s1, axis=1, keepdims=True), jnp.max(s2, axis=1, keepdims=True))
        p1 = jnp.where(ok1, jnp.exp(s1 - mw), 0.0)
        p2 = jnp.where(ok2, jnp.exp(s2 - mw), 0.0)
        lsum = jnp.sum(p1, axis=1, keepdims=True) + jnp.sum(p2, axis=1, keepdims=True)
        owin_ref[...] = (jnp.dot(p1.astype(bf16), wbb, preferred_element_type=f32)
                         + jnp.dot(p2.astype(bf16), wnb, preferred_element_type=f32)) / lsum

    kv = jnp.concatenate([pg[...] for pg in pages], axis=0).astype(bf16)
    n = kv.shape[0]
    s = _nt(qr_ref[...].astype(bf16), kv)
    expand = (lax.broadcasted_iota(jnp.int32, (bps, n), 1) // SEL_BLOCK
              == lax.broadcasted_iota(jnp.int32, (bps, n), 0))
    chosen = jnp.dot(selst_ref[j].astype(bf16), jnp.where(expand, 1.0, 0.0).astype(bf16),
                     preferred_element_type=f32)
    ok = tile_heads(chosen) > 0.5
    s = jnp.where(ok, s, NEG)
    m_old = m_ref[...]
    m_new = jnp.maximum(m_old, jnp.max(s, axis=1, keepdims=True))
    alpha = jnp.exp(m_old - m_new)
    p = jnp.where(ok, jnp.exp(s - m_new), 0.0)
    l_ref[...] = alpha * l_ref[...] + jnp.sum(p, axis=1, keepdims=True)
    m_ref[...] = m_new
    acc_ref[...] = alpha * acc_ref[...] + jnp.dot(p.astype(bf16), kv, preferred_element_type=f32)

    @pl.when(j == n_steps - 1)
    def _():
        g = jax.nn.sigmoid(sm_in[0][:, SMALL_GATES:SMALL_GATES + 3 * H])
        o_slc = acc_ref[...] / l_ref[...]
        outs = []
        for h in range(H):
            rs = slice(h * Ts, (h + 1) * Ts)
            outs.append(g[:, 3 * h:3 * h + 1] * ocmp_ref[rs, d:2 * d]
                        + g[:, 3 * h + 1:3 * h + 2] * o_slc[rs, d:2 * d]
                        + g[:, 3 * h + 2:3 * h + 3] * owin_ref[rs, d:2 * d])
        o_ref[0] = jnp.concatenate(outs, axis=1)


def nsa_decode(page_table, qr, z3, cmp, slc_new, win_cache, win_new, pool_slc, *, layer, pos0):
    B, Ts, _ = qr.shape
    n_pages = page_table.shape[1]
    page = pool_slc.shape[2]
    n_past = n_pages * page
    lw = win_cache.shape[2]
    nc = cmp.shape[1]
    G = min(PAGES_PER_STEP, n_pages)
    n_steps = n_pages // G
    n_sel = -(-(n_past + Ts) // SEL_BLOCK)
    nsl = -(-n_sel // 128) * 128
    assert n_pages % G == 0 and (Ts & (Ts - 1)) == 0 and Ts % 8 == 0 and page % SEL_BLOCK == 0
    assert pos0 == n_past and n_past % SEL_BLOCK + Ts <= SEL_BLOCK and nc % 2 == 0 and nc // 2 <= nsl
    R = N_HEADS * Ts
    row = lambda w, cb=0: pl.BlockSpec((1, Ts, w), lambda b, j, pt, cb=cb: (b, 0, cb))
    kern = functools.partial(_nsa_decode_kernel, G=G, n_steps=n_steps, Ts=Ts, pos0=pos0, n_past=n_past,
                             wpos0=n_past - lw, nc=nc, nsl=nsl, topk=min(SEL_TOPK, n_sel), page=page)
    buf = lambda: pltpu.VMEM((R, 128), f32)
    stat = lambda: pltpu.VMEM((R, 1), f32)
    return pl.pallas_call(
        kern,
        out_shape=jax.ShapeDtypeStruct((B, Ts, GROUP_WIDTH), f32),
        grid_spec=pltpu.PrefetchScalarGridSpec(
            num_scalar_prefetch=1, grid=(B, n_steps),
            in_specs=[row(256), row(256, CB_NS_Q), row(128, CB128_SMALL),
                      pl.BlockSpec((1, nc, 128), lambda b, j, pt: (b, 0, 0)),
                      row(128),
                      pl.BlockSpec((None, None, lw, 128), lambda b, j, pt: (layer, b, 0, 0)),
                      row(128)]
            + _page_specs(layer, n_pages, G, (page, 128)),
            out_specs=row(256),
            scratch_shapes=[buf(), pltpu.VMEM((n_steps, Ts, G * page // SEL_BLOCK), f32), buf(), buf(),
                            stat(), stat(), buf()]),
        compiler_params=_cparams(("parallel", "arbitrary")),
        name="nsa_decode",
    )(page_table.reshape(-1), qr, z3, z3, cmp, slc_new, win_cache, win_new, *([pool_slc] * G))


def _mlstm_kernel(q_ref, k_ref, v_ref, og_ref, gc_ref, gr_ref, bias_ref, ng_ref, c0_ref, n0_ref, m0_ref,
                  o_ref, cout_ref, nout_ref, mout_ref, c_s, n_s, m_s, *, L):
    ci = pl.program_id(1)
    H, d = N_HEADS, HEAD_DIM

    @pl.when(ci == 0)
    def _():
        c_s[...] = c0_ref[0]
        n_s[...] = n0_ref[0]
        m_s[...] = m0_ref[0]

    q = q_ref[...]
    k = k_ref[...] * (d ** -0.5)
    v = v_ref[...]
    gc = gc_ref[...]
    gr = gr_ref[0]
    bias = bias_ref[...]
    t_i = lax.broadcasted_iota(jnp.int32, (L, L), 0)
    s_i = lax.broadcasted_iota(jnp.int32, (L, L), 1)
    causal = s_i <= t_i
    causal_T = t_i <= s_i
    outs = []
    for h in range(H):
        bi = bias[0:1, h:h + 1]
        bf_ = bias[1:2, h:h + 1]
        ig_c = gc[:, h:h + 1] + bi
        lf_c = jax.nn.log_sigmoid(gc[:, H + h:H + h + 1] + bf_)
        ig_r = gr[h:h + 1, :] + bi
        lf_r = jax.nn.log_sigmoid(gr[H + h:H + h + 1, :] + bf_)
        b_c = jnp.sum(jnp.where(causal, lf_r, 0.0), axis=1, keepdims=True)
        b_r = jnp.sum(jnp.where(causal_T, lf_c, 0.0), axis=0, keepdims=True)
        m_prev = m_s[h:h + 1, 0:1]
        D = jnp.where(causal, b_c - b_r + ig_r, -jnp.inf)
        m_t = jnp.maximum(b_c + m_prev, jnp.max(D, axis=1, keepdims=True))
        inter = jnp.exp(b_c + m_prev - m_t)
        qh = q[:, d * h:d * (h + 1)]
        kh = k[:, d * h:d * (h + 1)]
        vh = v[:, d * h:d * (h + 1)]
        qb = qh.astype(bf16)
        S = _nt(qb, kh.astype(bf16))
        Sw = jnp.exp(D - m_t) * S
        C = c_s[h]
        n_row = n_s[h:h + 1, :]
        num = (jnp.dot(Sw.astype(bf16), vh.astype(bf16), preferred_element_type=f32)
               + inter * jnp.dot(qb, C.astype(bf16), preferred_element_type=f32))
        qn = jnp.sum(qh * n_row, axis=1, keepdims=True)
        den = jnp.sum(Sw, axis=1, keepdims=True) + inter * qn
        hh = num / jnp.maximum(jnp.abs(den), jnp.exp(-m_t))
        m_new = m_t[L - 1:L, :]
        b_last = b_c[L - 1:L, :]
        wl = jnp.exp(b_last - b_c + ig_c - m_new)
        decay = jnp.exp(b_last + m_prev - m_new)
        kw = kh * wl
        c_s[h] = decay * C + lax.dot_general(kw.astype(bf16), vh.astype(bf16), (((0,), (0,)), ((), ())),
                                             preferred_element_type=f32)
        n_s[h:h + 1, :] = decay * n_row + jnp.sum(kw, axis=0, keepdims=True)
        m_s[h:h + 1, :] = jnp.broadcast_to(m_new, (1, 128))
        mu = jnp.mean(hh, axis=1, keepdims=True)
        var = jnp.mean(jnp.square(hh - mu), axis=1, keepdims=True)
        outs.append((hh - mu) * lax.rsqrt(var + EPS))
    hcat = jnp.concatenate(outs, axis=1) * ng_ref[...]
    o_ref[...] = hcat * jax.nn.sigmoid(og_ref[...])
    cout_ref[0] = c_s[...]
    nout_ref[0] = n_s[...]
    mout_ref[0] = m_s[...]


def mlstm(z, gates_T, gate_b, norm_g, C0, n0, m0, *, B, T, L):
    nchunk = T // L
    m0p = jnp.broadcast_to(m0[:, :, None], (B, N_HEADS, 128))
    row = lambda cb: pl.BlockSpec((L, 256), lambda b, c, cb=cb: (b * nchunk + c, cb))
    st = lambda *shape: pl.BlockSpec((1,) + shape, lambda b, c: (b,) + (0,) * len(shape))
    out, C, n, m = pl.pallas_call(
        functools.partial(_mlstm_kernel, L=L),
        out_shape=[jax.ShapeDtypeStruct((B * T, GROUP_WIDTH), f32),
                   jax.ShapeDtypeStruct((B, N_HEADS, HEAD_DIM, HEAD_DIM), f32),
                   jax.ShapeDtypeStruct((B, N_HEADS, HEAD_DIM), f32),
                   jax.ShapeDtypeStruct((B, N_HEADS, 128), f32)],
        grid=(B, nchunk),
        in_specs=[row(CB_ML_Q), row(CB_ML_K), row(CB_ML_V), row(CB_ML_O),
                  pl.BlockSpec((L, 128), lambda b, c: (b * nchunk + c, CB128_SMALL)),
                  pl.BlockSpec((1, 8, L), lambda b, c: (b * nchunk + c, 0, 0)),
                  pl.BlockSpec((2, N_HEADS), lambda b, c: (0, 0)),
                  pl.BlockSpec((1, GROUP_WIDTH), lambda b, c: (0, 0)),
                  st(N_HEADS, HEAD_DIM, HEAD_DIM), st(N_HEADS, HEAD_DIM), st(N_HEADS, 128)],
        out_specs=[pl.BlockSpec((L, GROUP_WIDTH), lambda b, c: (b * nchunk + c, 0)),
                   st(N_HEADS, HEAD_DIM, HEAD_DIM), st(N_HEADS, HEAD_DIM), st(N_HEADS, 128)],
        scratch_shapes=[pltpu.VMEM((N_HEADS, HEAD_DIM, HEAD_DIM), f32), pltpu.VMEM((N_HEADS, HEAD_DIM), f32),
                        pltpu.VMEM((N_HEADS, 128), f32)],
        compiler_params=_cparams(("parallel", "arbitrary")),
        name="mlstm",
    )(z, z, z, z, z, gates_T, gate_b, norm_g.reshape(1, GROUP_WIDTH), C0, n0, m0p)
    return out, C, n, m[:, :, 0]


def _rglru_kernel(x_ref, gate_ref, cw_ref, cb_ref, wa_ref, ba_ref, wx_ref, bx_ref, lam_ref, h0_ref, buf0_ref,
                  y_ref, hout_ref, bufout_ref, xbuf, a_s, u_s, h_s, hs_s, *, tm):
    ti = pl.program_id(0)
    Bb = x_ref.shape[0]
    W = GROUP_WIDTH

    @pl.when(ti == 0)
    def _():
        xbuf[:, 0:8, :] = buf0_ref[...]
        h_s[...] = h0_ref[...]

    xbuf[:, 8:8 + tm, :] = x_ref[...]
    cw = cw_ref[...]
    xc = cb_ref[...] + cw[0:1, :] * xbuf[:, 5:5 + tm, :]
    for j in range(1, CONV_W):
        xc = xc + cw[j:j + 1, :] * xbuf[:, 5 + j:5 + j + tm, :]
    flat = xc.reshape(Bb * tm, W).astype(bf16)
    r = jax.nn.sigmoid(jnp.dot(flat, wa_ref[...], preferred_element_type=f32) + ba_ref[...])
    i = jax.nn.sigmoid(jnp.dot(flat, wx_ref[...], preferred_element_type=f32) + bx_ref[...])
    lam = lam_ref[...]
    softplus = jnp.maximum(-lam, 0.0) + jnp.log1p(jnp.exp(-jnp.abs(lam)))
    log_a = (-LRU_C * r) * softplus
    a = jnp.exp(log_a)
    u = jnp.sqrt(-jnp.tanh(log_a) * (a * a + 1.0)) * (i * xc.reshape(Bb * tm, W))
    a_s[...] = a.reshape(Bb, tm, W)
    u_s[...] = u.reshape(Bb, tm, W)

    def body(t, h):
        h = a_s[:, pl.ds(t, 1), :] * h + u_s[:, pl.ds(t, 1), :]
        hs_s[:, pl.ds(t, 1), :] = h
        return h

    h_last = lax.fori_loop(0, tm, body, h_s[...])
    h_s[...] = h_last
    y_ref[...] = hs_s[...] * jax.nn.gelu(gate_ref[...])
    hout_ref[...] = h_last
    bufout_ref[...] = xbuf[:, tm:tm + 8, :]
    xbuf[:, 0:8, :] = xbuf[:, tm:tm + 8, :]


def rglru(z3, conv_w, conv_b, wa_bd, ba, wx_bd, bx, lam, h0, buf0, *, tm):
    B, T, _ = z3.shape
    tm = min(tm, T)
    assert T % tm == 0 and tm % 8 == 0
    W = GROUP_WIDTH
    buf8 = jnp.concatenate([jnp.zeros((B, 8 - (CONV_W - 1), W), f32), buf0], 1)
    vec = lambda: pl.BlockSpec((1, W), lambda i: (0, 0))
    mat = lambda: pl.BlockSpec((W, W), lambda i: (0, 0))
    y, h, buf = pl.pallas_call(
        functools.partial(_rglru_kernel, tm=tm),
        out_shape=[jax.ShapeDtypeStruct((B, T, W), f32), jax.ShapeDtypeStruct((B, 1, W), f32),
                   jax.ShapeDtypeStruct((B, 8, W), f32)],
        grid=(T // tm,),
        in_specs=[pl.BlockSpec((B, tm, W), lambda i: (0, i, CB_LR_X)),
                  pl.BlockSpec((B, tm, W), lambda i: (0, i, CB_LR_G)),
                  pl.BlockSpec((CONV_W, W), lambda i: (0, 0)), vec(), mat(), vec(), mat(), vec(), vec(),
                  pl.BlockSpec((B, 1, W), lambda i: (0, 0, 0)), pl.BlockSpec((B, 8, W), lambda i: (0, 0, 0))],
        out_specs=[pl.BlockSpec((B, tm, W), lambda i: (0, i, 0)),
                   pl.BlockSpec((B, 1, W), lambda i: (0, 0, 0)), pl.BlockSpec((B, 8, W), lambda i: (0, 0, 0))],
        scratch_shapes=[pltpu.VMEM((B, tm + 8, W), f32), pltpu.VMEM((B, tm, W), f32), pltpu.VMEM((B, tm, W), f32),
                        pltpu.VMEM((B, 1, W), f32), pltpu.VMEM((B, tm, W), f32)],
        compiler_params=_cparams(("arbitrary",)),
        name="rglru",
    )(z3, z3, conv_w, conv_b.reshape(1, W), wa_bd, ba.reshape(1, W), wx_bd, bx.reshape(1, W),
      lam.reshape(1, W), h0.reshape(B, 1, W), buf8)
    return y, h[:, 0], buf[:, 8 - (CONV_W - 1):]


def _mem_attn_kernel(q_ref, k_ref, v_ref, o_ref):
    q = q_ref[0]
    lead = (0,) * (len(k_ref.shape) - 2)
    k = k_ref[lead]
    v = v_ref[lead]
    dh = MEM_HEAD_DIM
    for h in range(MEM_HEADS):
        cs = slice(dh * h, dh * (h + 1))
        s = _nt(q[:, cs].astype(bf16), k[:, cs].astype(bf16)) * (dh ** -0.5)
        e = jnp.exp(s - jnp.max(s, axis=-1, keepdims=True))
        p = e / jnp.sum(e, axis=-1, keepdims=True)
        o_ref[0, :, cs] = jnp.dot(p.astype(bf16), v[:, cs].astype(bf16), preferred_element_type=f32)


def mem_attention(q, k, v, *, layer=None, tq=256):
    B, T, D = q.shape
    tq = min(tq, T)
    M = k.shape[-2]
    if layer is None:
        kv_spec = pl.BlockSpec((1, M, D), lambda b, i: (b, 0, 0))
    else:
        kv_spec = pl.BlockSpec((1, 1, M, D), lambda b, i: (layer, b, 0, 0))
    return pl.pallas_call(
        _mem_attn_kernel,
        out_shape=jax.ShapeDtypeStruct((B, T, D), f32),
        grid=(B, T // tq),
        in_specs=[pl.BlockSpec((1, tq, D), lambda b, i: (b, i, 0)), kv_spec, kv_spec],
        out_specs=pl.BlockSpec((1, tq, D), lambda b, i: (b, i, 0)),
        compiler_params=_cparams(("parallel", "parallel")),
        name="mem_attention",
    )(q, k, v)


def _even_odd(cmp):
    B, nc, w = cmp.shape
    return jnp.swapaxes(cmp.reshape(B, nc // 2, 2, w), 1, 2).reshape(B, nc, w)


def _layer(x, lw, lam_init, tables, *, B, T, mem, ml_state, lr_state, paged, cfg):
    N = B * T
    z = matmul(x, lw['w_in'], g=lw['g_mix'])
    z3 = z.reshape(B, T, D_PROJ_PAD)
    dfq, dfk, nsq, slc_new, win_new = rope_prep(z, tables, cfg['rope_tm'])
    df_v = z[:, 256 * CB_DF_V:256 * (CB_DF_V + 1)]
    cmp_new = z[:, 128 * CB128_CMP:128 * (CB128_CMP + 1)]
    small = z[:, 128 * CB128_SMALL:128 * CB128_SMALL + SMALL_GATES]
    r3 = lambda a: a.reshape(B, T, a.shape[-1])

    L = cfg['ml_chunk']
    gates_T = jnp.swapaxes(small.reshape(N // L, L, SMALL_GATES), 1, 2)
    o_ml, ml_C, ml_n, ml_m = mlstm(z, gates_T, lw['ml_gate_b'], lw['ml_norm_g'], *ml_state, B=B, T=T, L=L)

    if paged is None:
        o_df = diff_attention(r3(dfq), r3(dfk), z3, lw['df_lam'], lw['df_norm_g'], lam_init=lam_init,
                              tq=cfg['df_tq'], tk=cfg['df_tk'])
        comp = nsa_compress(cmp_new.reshape(N // CMP_BLOCK, CMP_BLOCK * 128), lw['nsa_pos_flat'],
                            lw['nsa_w1c'], lw['nsa_w2c']).reshape(B, T // CMP_BLOCK, 128)
        o_ns = nsa_attention(r3(nsq), z3, _even_odd(comp), r3(slc_new), r3(win_new),
                             tq=cfg['ns_tq'], tk=cfg['ns_tk'], tkw=cfg['ns_tkw'])
        win_state = r3(win_new)[:, T - min(WINDOW, T):]
    else:
        pt, l = paged['page_table'], paged['layer']
        n_past = pt.shape[1] * paged['df_k'].shape[2]
        assert (n_past + T) // CMP_BLOCK == n_past // CMP_BLOCK
        o_df = diff_decode(pt, r3(dfq), r3(dfk), r3(df_v), paged['df_k'], paged['df_v'], lw['df_lam'],
                           lw['df_norm_g'], layer=l, pos0=n_past, lam_init=lam_init)
        comp = nsa_compress_paged(pt, paged['nsa_cmp'], lw['nsa_pos_flat'], lw['nsa_w1c'], lw['nsa_w2c'], layer=l)
        o_ns = nsa_decode(pt, r3(nsq), z3, _even_odd(comp), r3(slc_new), paged['nsa_win'], r3(win_new),
                          paged['nsa_slc'], layer=l, pos0=n_past)
        win_cat = jnp.concatenate([paged['nsa_win'][l], r3(win_new)], 1)
        win_state = win_cat[:, win_cat.shape[1] - min(WINDOW, win_cat.shape[1]):]

    o_lr, lr_h, lr_buf = rglru(z3, lw['lru_conv_w'], lw['lru_conv_b'], lw['lru_wa_bd'],
                               lw['lru_ba'], lw['lru_wx_bd'], lw['lru_bx'], lw['lru_lambda'], *lr_state,
                               tm=cfg['lru_tm'])

    x = matmul([o_ml, o_df.reshape(N, 256), o_ns.reshape(N, 256), o_lr.reshape(N, 256)], lw['w_out'], res=x)
    q = matmul(x, lw['w_mq'], g=lw['g_mem_q'])
    if isinstance(mem, tuple):
        att = mem_attention(q.reshape(B, T, D_MODEL), mem[0], mem[1], tq=cfg['mem_tq'])
    else:
        att = mem_attention(q.reshape(B, T, D_MODEL), mem['k'], mem['v'], layer=mem['layer'], tq=cfg['mem_tq'])
    x = matmul(att.reshape(N, D_MODEL), lw['w_mo'], res=x)
    u = matmul(x, lw['w_up'], g=lw['g_mlp'], act='relu2', out_dtype=bf16)
    x = matmul(u, lw['w_down'], res=x)
    new = dict(df_k=dfk.reshape(B, T, N_HEADS, HEAD_DIM), df_v=df_v.reshape(B, T, N_HEADS, HEAD_DIM),
               nsa_cmp=cmp_new.reshape(B, T, 2, HEAD_DIM), nsa_slc=slc_new.reshape(B, T, 2, HEAD_DIM),
               nsa_win=win_state.reshape(B, -1, 2, HEAD_DIM), ml_C=ml_C, ml_n=ml_n, ml_m=ml_m,
               lru_h=lr_h, lru_conv=lr_buf)
    return x, new


def _block_diag(w):
    H, a, b = w.shape
    eye = jnp.eye(H, dtype=w.dtype)
    return jnp.einsum('hij,hg->higj', w, eye).reshape(H * a, H * b)


def kernel(x_prompt, x_sample, mem_prompt, cache_df_k, cache_df_v, cache_nsa_cmp, cache_nsa_slc, cache_nsa_win, state_ml_C, state_ml_n, state_ml_m, state_lru_h, state_lru_conv, cache_mem_k, cache_mem_v, page_table, g_mix, w_in, w_out, ml_gate_b, ml_norm_g, df_lam, df_norm_g, nsa_pos, nsa_w1, nsa_w2, lru_conv_w, lru_conv_b, lru_wa, lru_ba, lru_wx, lru_bx, lru_lambda, g_mem_q, g_mem_kv, w_mq, w_mk, w_mv, w_mo, g_mlp, w_up, w_down, g_final):
    Bp, Tp, _ = x_prompt.shape
    Bs, Ts, _ = x_sample.shape
    depth = w_in.shape[0]
    n_pool, page = cache_df_k.shape[1], cache_df_k.shape[2]
    n_past = page_table.shape[1] * page
    M = mem_prompt.shape[1]

    perm, n_real = _proj_perm()
    col_ok = (jnp.arange(D_PROJ_PAD) < n_real)
    w_in_p = jnp.where(col_ok[None, None, :], jnp.take(w_in, perm, axis=2), 0.0).astype(bf16)
    eye2 = jnp.eye(2, dtype=f32)
    w1 = nsa_w1.reshape(depth, 2, CMP_BLOCK, HEAD_DIM, CMP_HIDDEN)
    w1c = jnp.einsum('lstih,sg->ltsigh', w1, eye2).reshape(depth, CMP_BLOCK * 2 * HEAD_DIM, 2 * CMP_HIDDEN)
    w2c = jnp.einsum('lshd,sg->lshgd', nsa_w2, eye2).reshape(depth, 2 * CMP_HIDDEN, 2 * HEAD_DIM)
    cfg_p = dict(rope_tm=512, ml_chunk=128, df_tq=256, df_tk=512, ns_tq=256, ns_tk=256, ns_tkw=256,
                 lru_tm=256, mem_tq=256)
    cfg_s = dict(rope_tm=Bs * Ts, ml_chunk=Ts, lru_tm=Ts, mem_tq=Ts)
    tab_p = rope_tables(jnp.arange(Tp))
    tab_s = tuple(jnp.tile(t, (Bs, 1)) for t in rope_tables(n_past + jnp.arange(Ts)))

    pool_df_k = cache_df_k.reshape(depth, n_pool, page, 256)
    pool_df_v = cache_df_v.reshape(depth, n_pool, page, 256)
    pool_cmp = cache_nsa_cmp.reshape(depth, n_pool, page // CMP_BLOCK, CMP_BLOCK * 128)
    pool_slc = cache_nsa_slc.reshape(depth, n_pool, page, 128)
    win_cache = cache_nsa_win.reshape(depth, Bs, -1, 128)
    mem_k_cache = cache_mem_k.reshape(depth, Bs, M, D_MODEL)
    mem_v_cache = cache_mem_v.reshape(depth, Bs, M, D_MODEL)
    mem_rows = mem_prompt.reshape(Bp * M, D_MODEL)

    xp = x_prompt.reshape(Bp * Tp, D_MODEL)
    xs = x_sample.reshape(Bs * Ts, D_MODEL)
    names = ('df_k', 'df_v', 'nsa_cmp', 'nsa_slc', 'nsa_win', 'ml_C', 'ml_n', 'ml_m', 'lru_h', 'lru_conv')
    acc = {pre + n: [] for n in names for pre in ('p_', 's_')}
    acc['p_mem_k'] = []
    acc['p_mem_v'] = []
    for l in range(depth):
        lw = dict(g_mix=g_mix[l], w_in=w_in_p[l], w_out=w_out[l].astype(bf16), ml_gate_b=ml_gate_b[l],
                  ml_norm_g=ml_norm_g[l], df_lam=df_lam[l], df_norm_g=df_norm_g[l],
                  nsa_pos_flat=nsa_pos[l].reshape(1, CMP_BLOCK * 2 * HEAD_DIM), nsa_w1c=w1c[l].astype(bf16),
                  nsa_w2c=w2c[l].astype(bf16), lru_conv_w=lru_conv_w[l], lru_conv_b=lru_conv_b[l],
                  lru_wa_bd=_block_diag(lru_wa[l]).astype(bf16), lru_ba=lru_ba[l],
                  lru_wx_bd=_block_diag(lru_wx[l]).astype(bf16), lru_bx=lru_bx[l], lru_lambda=lru_lambda[l],
                  g_mem_q=g_mem_q[l], w_mq=w_mq[l].astype(bf16), w_mo=w_mo[l].astype(bf16), g_mlp=g_mlp[l],
                  w_up=w_up[l].astype(bf16), w_down=w_down[l].astype(bf16))
        lam_init = 0.8 - 0.6 * math.exp(-0.3 * l)
        mk_p = matmul(mem_rows, w_mk[l].astype(bf16), g=g_mem_kv[l])
        mv_p = matmul(mem_rows, w_mv[l].astype(bf16), g=g_mem_kv[l])
        ml0 = (jnp.zeros((Bp, N_HEADS, HEAD_DIM, HEAD_DIM), f32), jnp.zeros((Bp, N_HEADS, HEAD_DIM), f32),
               jnp.zeros((Bp, N_HEADS), f32))
        lr0 = (jnp.zeros((Bp, GROUP_WIDTH), f32), jnp.zeros((Bp, CONV_W - 1, GROUP_WIDTH), f32))
        xp, new_p = _layer(xp, lw, lam_init, tab_p, B=Bp, T=Tp,
                           mem=(mk_p.reshape(Bp, M, D_MODEL), mv_p.reshape(Bp, M, D_MODEL)),
                           ml_state=ml0, lr_state=lr0, paged=None, cfg=cfg_p)
        paged = dict(page_table=page_table, layer=l, df_k=pool_df_k, df_v=pool_df_v, nsa_cmp=pool_cmp,
                     nsa_slc=pool_slc, nsa_win=win_cache)
        xs, new_s = _layer(xs, lw, lam_init, tab_s, B=Bs, T=Ts,
                           mem=dict(k=mem_k_cache, v=mem_v_cache, layer=l),
                           ml_state=(state_ml_C[l], state_ml_n[l], state_ml_m[l]),
                           lr_state=(state_lru_h[l], state_lru_conv[l]), paged=paged, cfg=cfg_s)
        for pre, new in (('p_', new_p), ('s_', new_s)):
            for n in names:
                acc[pre + n].append(new[n])
        acc['p_mem_k'].append(mk_p.reshape(Bp, M, MEM_HEADS, MEM_HEAD_DIM))
        acc['p_mem_v'].append(mv_p.reshape(Bp, M, MEM_HEADS, MEM_HEAD_DIM))
    st = {k: jnp.stack(v) for k, v in acc.items()}
    y_prompt = rmsnorm_rows(xp, g_final).reshape(Bp, Tp, D_MODEL)
    y_sample = rmsnorm_rows(xs, g_final).reshape(Bs, Ts, D_MODEL)
    return (y_prompt, y_sample, st['p_df_k'], st['s_df_k'], st['p_df_v'], st['s_df_v'],
            st['p_nsa_cmp'], st['s_nsa_cmp'], st['p_nsa_slc'], st['s_nsa_slc'],
            st['p_nsa_win'], st['s_nsa_win'], st['p_ml_C'], st['s_ml_C'], st['p_ml_n'], st['s_ml_n'],
            st['p_ml_m'], st['s_ml_m'], st['p_lru_h'], st['s_lru_h'], st['p_lru_conv'], st['s_lru_conv'],
            st['p_mem_k'], st['p_mem_v'])
```

```python
import functools
import math

import jax
import jax.numpy as jnp
from jax import lax
from jax.experimental import pallas as pl
from jax.experimental.pallas import tpu as pltpu

f32 = jnp.float32
bf16 = jnp.bfloat16

D_MODEL = 1024
GROUP_WIDTH = 256
N_HEADS = 4
HEAD_DIM = 64
DQK = 32
ROPE_THETA = 10000.0
CMP_BLOCK = 32
CMP_HIDDEN = 128
SEL_BLOCK = 64
SEL_TOPK = 16
WINDOW = 512
CONV_W = 4
LRU_C = 8.0
MEM_HEADS = 4
MEM_HEAD_DIM = 256
EPS = 1e-6
NEG = -1e30
FORCE_SCORE = 1e9
TINY = 1e-30

VMEM_LIMIT_BYTES = 48 * 1024 * 1024
PAGES_PER_STEP = 16

_SRC_SPLITS = (
    ('ml_q', 256), ('ml_k', 256), ('ml_v', 256), ('ml_i', 4), ('ml_f', 4), ('ml_o', 256),
    ('df_q', 256), ('df_k', 256), ('df_v', 256),
    ('ns_q', 256), ('ns_kc', 64), ('ns_vc', 64), ('ns_ks', 64), ('ns_vs', 64),
    ('ns_kw', 64), ('ns_vw', 64), ('ns_g', 12), ('lr_x', 256), ('lr_g', 256),
)
_DST_ORDER = ('ml_q', 'ml_k', 'ml_v', 'ml_o', 'df_q', 'df_k', 'df_v', 'ns_q', 'lr_x', 'lr_g',
              'ns_kc', 'ns_vc', 'ns_ks', 'ns_vs', 'ns_kw', 'ns_vw', 'ml_i', 'ml_f', 'ns_g')
D_PROJ_PAD = 3072
CB_ML_Q, CB_ML_K, CB_ML_V, CB_ML_O, CB_DF_Q, CB_DF_K, CB_DF_V, CB_NS_Q, CB_LR_X, CB_LR_G = range(10)
CB128_CMP, CB128_SLC, CB128_WIN, CB128_SMALL = 20, 21, 22, 23
SMALL_GATES = 8


def _proj_perm():
    off = {}
    o = 0
    for name, w in _SRC_SPLITS:
        off[name] = (o, w)
        o += w
    idx = []
    for name in _DST_ORDER:
        s, w = off[name]
        idx.extend(range(s, s + w))
    n_real = len(idx)
    idx.extend([0] * (D_PROJ_PAD - n_real))
    return jnp.asarray(idx, jnp.int32), n_real


def _cparams(sem):
    return pltpu.CompilerParams(dimension_semantics=sem, vmem_limit_bytes=VMEM_LIMIT_BYTES)


def _act_dtype(rows):
    return bf16 if rows % 16 == 0 else f32


def _nt(a, b):
    return lax.dot_general(a, b, (((1,), (1,)), ((), ())), preferred_element_type=f32)


def _mm_kernel(*refs, n_parts, has_norm, has_res, act):
    it = iter(refs)
    x_refs = [next(it) for _ in range(n_parts)]
    w_ref = next(it)
    g_ref = next(it) if has_norm else None
    r_ref = next(it) if has_res else None
    o_ref = next(it)
    h_ref = next(it) if has_norm else None
    if has_norm:
        @pl.when(pl.program_id(1) == 0)
        def _():
            x = x_refs[0][...].astype(f32)
            y = x * lax.rsqrt(jnp.mean(x * x, axis=-1, keepdims=True) + EPS)
            h_ref[...] = (y * g_ref[...]).astype(bf16)
        acc = jnp.dot(h_ref[...], w_ref[...], preferred_element_type=f32)
    else:
        acc = None
        off = 0
        for x_ref in x_refs:
            kp = x_ref.shape[-1]
            part = jnp.dot(x_ref[...].astype(bf16), w_ref[off:off + kp, :], preferred_element_type=f32)
            acc = part if acc is None else acc + part
            off += kp
    if act == 'relu2':
        acc = jnp.maximum(acc, 0.0)
        acc = acc * acc
    if has_res:
        acc = acc + r_ref[...]
    o_ref[...] = acc.astype(o_ref.dtype)


def matmul(x, w, *, g=None, res=None, act=None, out_dtype=f32, tm=1024, tn=512):
    parts = list(x) if isinstance(x, (list, tuple)) else [x]
    M = parts[0].shape[0]
    K, N = w.shape
    assert sum(p.shape[1] for p in parts) == K
    tm = min(tm, M)
    tn = min(tn, N)
    assert M % tm == 0 and N % tn == 0
    has_norm = g is not None
    has_res = res is not None
    assert not (has_norm and len(parts) > 1)
    in_specs = [pl.BlockSpec((tm, p.shape[1]), lambda i, j: (i, 0)) for p in parts]
    in_specs.append(pl.BlockSpec((K, tn), lambda i, j: (0, j)))
    args = parts + [w]
    if has_norm:
        in_specs.append(pl.BlockSpec((1, K), lambda i, j: (0, 0)))
        args.append(g.reshape(1, K).astype(f32))
    if has_res:
        in_specs.append(pl.BlockSpec((tm, tn), lambda i, j: (i, j)))
        args.append(res)
    scratch = [pltpu.VMEM((tm, K), bf16)] if has_norm else []
    return pl.pallas_call(
        functools.partial(_mm_kernel, n_parts=len(parts), has_norm=has_norm, has_res=has_res, act=act),
        out_shape=jax.ShapeDtypeStruct((M, N), out_dtype),
        grid=(M // tm, N // tn),
        in_specs=in_specs,
        out_specs=pl.BlockSpec((tm, tn), lambda i, j: (i, j)),
        scratch_shapes=scratch,
        compiler_params=_cparams(("parallel", "arbitrary")),
        name="matmul",
    )(*args)


def _rmsnorm_kernel(x_ref, g_ref, o_ref):
    x = x_ref[...]
    y = x * lax.rsqrt(jnp.mean(x * x, axis=-1, keepdims=True) + EPS)
    o_ref[...] = y * g_ref[...]


def rmsnorm_rows(x, g, tm=1024):
    M, K = x.shape
    tm = min(tm, M)
    return pl.pallas_call(
        _rmsnorm_kernel,
        out_shape=jax.ShapeDtypeStruct((M, K), f32),
        grid=(M // tm,),
        in_specs=[pl.BlockSpec((tm, K), lambda i: (i, 0)), pl.BlockSpec((1, K), lambda i: (0, 0))],
        out_specs=pl.BlockSpec((tm, K), lambda i: (i, 0)),
        compiler_params=_cparams(("parallel",)),
        name="final_norm",
    )(x, g.reshape(1, K))


def _rotate(x, cos, sin, half):
    n = x.shape[-1]
    lane = lax.broadcasted_iota(jnp.int32, x.shape, 1)
    first = (lane & (2 * half - 1)) < half
    partner = jnp.where(first, pltpu.roll(x, n - half, 1), pltpu.roll(x, half, 1))
    return x * cos + partner * sin


def _rope_kernel(dq, dk, nq, sl, wn, ca, sa, cb, sb, cc, sc, odq, odk, onq, osl, own):
    odq[...] = _rotate(dq[...], ca[...], sa[...], DQK // 2)
    odk[...] = _rotate(dk[...], ca[...], sa[...], DQK // 2)
    onq[...] = _rotate(nq[...], cb[...], sb[...], HEAD_DIM // 2)
    osl[...] = _rotate(sl[...], cc[...], sc[...], HEAD_DIM // 2)
    own[...] = _rotate(wn[...], cc[...], sc[...], HEAD_DIM // 2)


def rope_tables(pos):
    posf = pos.astype(f32)[:, None]

    def tab(half, reps):
        inv = ROPE_THETA ** (-jnp.arange(half, dtype=f32) / half)
        ang = posf * inv[None, :]
        c = jnp.cos(ang)
        s = jnp.sin(ang)
        return jnp.tile(jnp.concatenate([c, c], 1), (1, reps)), jnp.tile(jnp.concatenate([-s, s], 1), (1, reps))

    ca, sa = tab(DQK // 2, GROUP_WIDTH // DQK)
    cb, sb = tab(HEAD_DIM // 2, N_HEADS)
    n = pos.shape[0]
    cc = jnp.concatenate([cb[:, :HEAD_DIM], jnp.ones((n, HEAD_DIM), f32)], 1)
    sc = jnp.concatenate([sb[:, :HEAD_DIM], jnp.zeros((n, HEAD_DIM), f32)], 1)
    return ca, sa, cb, sb, cc, sc


def rope_prep(z, tables, tm):
    N = z.shape[0]
    R = tables[0].shape[0]
    tm = min(tm, R)
    assert R % tm == 0 and N % tm == 0
    nr = R // tm
    zs = lambda cb: pl.BlockSpec((tm, 256), lambda i, cb=cb: (i, cb))
    zs128 = lambda cb: pl.BlockSpec((tm, 128), lambda i, cb=cb: (i, cb))
    t256 = pl.BlockSpec((tm, 256), lambda i: (i % nr, 0))
    t128 = pl.BlockSpec((tm, 128), lambda i: (i % nr, 0))
    o256 = pl.BlockSpec((tm, 256), lambda i: (i, 0))
    o128 = pl.BlockSpec((tm, 128), lambda i: (i, 0))
    return pl.pallas_call(
        _rope_kernel,
        out_shape=[jax.ShapeDtypeStruct((N, 256), f32)] * 3 + [jax.ShapeDtypeStruct((N, 128), f32)] * 2,
        grid=(N // tm,),
        in_specs=[zs(CB_DF_Q), zs(CB_DF_K), zs(CB_NS_Q), zs128(CB128_SLC), zs128(CB128_WIN),
                  t256, t256, t256, t256, t128, t128],
        out_specs=[o256, o256, o256, o128, o128],
        compiler_params=_cparams(("parallel",)),
        name="rope_prep",
    )(z, z, z, z, z, *tables)


def _diff_lambda(lam_ref, lam_init):
    lq = lam_ref[...]
    return (jnp.exp(jnp.sum(lq[0:1] * lq[1:2], keepdims=True))
            - jnp.exp(jnp.sum(lq[2:3] * lq[3:4], keepdims=True)) + lam_init)


def _diff_kernel(q_ref, k_ref, v_ref, lam_ref, g_ref, o_ref, qp_ref, m_ref, l_ref, acc_ref, oT_ref,
                 *, tq, tk, nk, lam_init):
    qi = pl.program_id(1)
    kj = pl.program_id(2)
    q_lo = qi * tq
    needed = (q_lo + tq - 1) // tk + 1
    nmap = 2 * N_HEADS

    @pl.when(kj == 0)
    def _():
        qT = jnp.transpose(q_ref[0]) * (DQK ** -0.5)
        row = lax.broadcasted_iota(jnp.int32, qT.shape, 0)
        for c in range(nmap):
            blk = jnp.where((row >= DQK * c) & (row < DQK * (c + 1)), qT, 0.0)
            qp_ref[:, c * tq:(c + 1) * tq] = blk.astype(bf16)
        m_ref[...] = jnp.full(m_ref.shape, NEG, f32)
        l_ref[...] = jnp.zeros(l_ref.shape, f32)
        acc_ref[...] = jnp.zeros(acc_ref.shape, f32)

    def step(masked):
        k = k_ref[0].astype(bf16)
        s = jnp.dot(k, qp_ref[...], preferred_element_type=f32)
        if masked:
            kpos = kj * tk + lax.broadcasted_iota(jnp.int32, s.shape, 0)
            qpos = q_lo + (lax.broadcasted_iota(jnp.int32, s.shape, 1) & (tq - 1))
            ok = kpos <= qpos
            s = jnp.where(ok, s, NEG)
        m_old = m_ref[...]
        m_new = jnp.maximum(m_old, jnp.max(s, axis=0, keepdims=True))
        alpha = jnp.exp(m_old - m_new)
        p = jnp.exp(s - m_new)
        l_ref[...] = alpha * l_ref[...] + jnp.sum(p, axis=0, keepdims=True)
        m_ref[...] = m_new
        pb = p.astype(bf16)
        vT = jnp.transpose(v_ref[0]).astype(bf16)
        for h in range(N_HEADS):
            rs = slice(HEAD_DIM * h, HEAD_DIM * (h + 1))
            cs = slice(2 * tq * h, 2 * tq * (h + 1))
            acc_ref[rs, :] = acc_ref[rs, :] * alpha[:, cs] + jnp.dot(
                vT[rs, :], pb[:, cs], preferred_element_type=f32)

    active = kj < needed
    crosses = kj * tk + tk - 1 > q_lo

    @pl.when(active & crosses)
    def _():
        step(True)

    @pl.when(active & jnp.logical_not(crosses))
    def _():
        step(False)

    @pl.when(kj == nk - 1)
    def _():
        lam = _diff_lambda(lam_ref, lam_init)
        l = l_ref[...]
        for h in range(N_HEADS):
            rs = slice(HEAD_DIM * h, HEAD_DIM * (h + 1))
            c1 = slice(2 * h * tq, (2 * h + 1) * tq)
            c2 = slice((2 * h + 1) * tq, (2 * h + 2) * tq)
            o = acc_ref[rs, 0:tq] / l[:, c1] - lam * (acc_ref[rs, tq:2 * tq] / l[:, c2])
            y = o * lax.rsqrt(jnp.mean(o * o, axis=0, keepdims=True) + EPS)
            oT_ref[rs, :] = (y * g_ref[...]) * (1.0 - lam_init)
        o_ref[0] = jnp.transpose(oT_ref[...]).astype(o_ref.dtype)


def diff_attention(q, k, z3, lam_q, norm_g, *, lam_init, tq, tk):
    B, T, _ = q.shape
    assert T % tq == 0 and T % tk == 0 and (tq & (tq - 1)) == 0 and tq % 128 == 0 and tk % 128 == 0
    nq, nk = T // tq, T // tk
    last = lambda qi: (qi * tq + tq - 1) // tk
    kern = functools.partial(_diff_kernel, tq=tq, tk=tk, nk=nk, lam_init=lam_init)
    return pl.pallas_call(
        kern,
        out_shape=jax.ShapeDtypeStruct((B, T, GROUP_WIDTH), bf16),
        grid=(B, nq, nk),
        in_specs=[
            pl.BlockSpec((1, tq, GROUP_WIDTH), lambda b, qi, kj: (b, qi, 0)),
            pl.BlockSpec((1, tk, GROUP_WIDTH), lambda b, qi, kj: (b, jnp.minimum(kj, last(qi)), 0)),
            pl.BlockSpec((1, tk, GROUP_WIDTH), lambda b, qi, kj: (b, jnp.minimum(kj, last(qi)), CB_DF_V)),
            pl.BlockSpec((4, DQK), lambda b, qi, kj: (0, 0)),
            pl.BlockSpec((HEAD_DIM, 1), lambda b, qi, kj: (0, 0)),
        ],
        out_specs=pl.BlockSpec((1, tq, GROUP_WIDTH), lambda b, qi, kj: (b, qi, 0)),
        scratch_shapes=[pltpu.VMEM((GROUP_WIDTH, 8 * tq), bf16), pltpu.VMEM((1, 8 * tq), f32),
                        pltpu.VMEM((1, 8 * tq), f32), pltpu.VMEM((GROUP_WIDTH, 2 * tq), f32),
                        pltpu.VMEM((GROUP_WIDTH, tq), f32)],
        compiler_params=_cparams(("parallel", "parallel", "arbitrary")),
        name="diff_attention",
    )(q, k, z3, lam_q, norm_g.reshape(HEAD_DIM, 1))


def _diff_decode_kernel(pt_ref, q_ref, kn_ref, vn_ref, lam_ref, g_ref, *rest, G, n_steps, Ts, pos0, n_past,
                        lam_init):
    kpages = rest[:G]
    vpages = rest[G:2 * G]
    o_ref, qp_ref, m_ref, l_ref, acc_ref = rest[2 * G:]
    j = pl.program_id(1)
    nmap = 2 * N_HEADS
    R = nmap * Ts

    @pl.when(j == 0)
    def _():
        q = q_ref[0] * (DQK ** -0.5)
        col = lax.broadcasted_iota(jnp.int32, q.shape, 1)
        for c in range(nmap):
            qp_ref[c * Ts:(c + 1) * Ts, :] = jnp.where((col >= DQK * c) & (col < DQK * (c + 1)), q, 0.0)
        s = _nt(qp_ref[...].astype(bf16), kn_ref[0].astype(bf16))
        qpos = pos0 + (lax.broadcasted_iota(jnp.int32, s.shape, 0) & (Ts - 1))
        kpos = n_past + lax.broadcasted_iota(jnp.int32, s.shape, 1)
        ok = kpos <= qpos
        s = jnp.where(ok, s, NEG)
        m = jnp.max(s, axis=1, keepdims=True)
        p = jnp.where(ok, jnp.exp(s - m), 0.0)
        m_ref[...] = m
        l_ref[...] = jnp.sum(p, axis=1, keepdims=True)
        acc_ref[...] = jnp.dot(p.astype(bf16), vn_ref[0].astype(bf16), preferred_element_type=f32)

    kT = jnp.concatenate([kp[...] for kp in kpages], axis=1).astype(bf16)
    vT = jnp.concatenate([vp[...] for vp in vpages], axis=1).astype(bf16)
    s = jnp.dot(qp_ref[...].astype(bf16), kT, preferred_element_type=f32)
    m_old = m_ref[...]
    m_new = jnp.maximum(m_old, jnp.max(s, axis=1, keepdims=True))
    alpha = jnp.exp(m_old - m_new)
    p = jnp.exp(s - m_new)
    l_ref[...] = alpha * l_ref[...] + jnp.sum(p, axis=1, keepdims=True)
    m_ref[...] = m_new
    acc_ref[...] = alpha * acc_ref[...] + _nt(p.astype(bf16), vT)

    @pl.when(j == n_steps - 1)
    def _():
        lam = _diff_lambda(lam_ref, lam_init)
        o_all = acc_ref[...] / l_ref[...]
        ys = []
        for h in range(N_HEADS):
            cs = slice(HEAD_DIM * h, HEAD_DIM * (h + 1))
            o = o_all[2 * h * Ts:(2 * h + 1) * Ts, cs] - lam * o_all[(2 * h + 1) * Ts:(2 * h + 2) * Ts, cs]
            y = o * lax.rsqrt(jnp.mean(o * o, axis=1, keepdims=True) + EPS)
            ys.append((y * g_ref[...]) * (1.0 - lam_init))
        o_ref[0] = jnp.concatenate(ys, axis=1)


def _page_specs(layer, n_pages, G, block):
    def spec(i):
        return pl.BlockSpec((None, None) + block,
                            lambda b, j, pt, i=i: (layer, pt[b * n_pages + j * G + i]) + (0,) * len(block))
    return [spec(i) for i in range(G)]


def diff_decode(page_table, q, k_new, v_new, pool_k, pool_v, lam_q, norm_g, *, layer, pos0, lam_init):
    B, Ts, _ = q.shape
    n_pages = page_table.shape[1]
    page = pool_k.shape[3]
    G = min(PAGES_PER_STEP, n_pages)
    assert n_pages % G == 0 and (Ts & (Ts - 1)) == 0 and Ts % 8 == 0
    n_steps = n_pages // G
    R = 2 * N_HEADS * Ts
    row = lambda w: pl.BlockSpec((1, Ts, w), lambda b, j, pt: (b, 0, 0))
    kern = functools.partial(_diff_decode_kernel, G=G, n_steps=n_steps, Ts=Ts, pos0=pos0,
                             n_past=n_pages * page, lam_init=lam_init)
    return pl.pallas_call(
        kern,
        out_shape=jax.ShapeDtypeStruct((B, Ts, GROUP_WIDTH), f32),
        grid_spec=pltpu.PrefetchScalarGridSpec(
            num_scalar_prefetch=1, grid=(B, n_steps),
            in_specs=[row(256), row(256), row(256),
                      pl.BlockSpec((4, DQK), lambda b, j, pt: (0, 0)),
                      pl.BlockSpec((1, HEAD_DIM), lambda b, j, pt: (0, 0))]
            + _page_specs(layer, n_pages, G, (256, page)) + _page_specs(layer, n_pages, G, (256, page)),
            out_specs=row(256),
            scratch_shapes=[pltpu.VMEM((R, GROUP_WIDTH), f32), pltpu.VMEM((R, 1), f32),
                            pltpu.VMEM((R, 1), f32), pltpu.VMEM((R, GROUP_WIDTH), f32)]),
        compiler_params=_cparams(("parallel", "arbitrary")),
        name="diff_decode",
    )(page_table.reshape(-1), q, k_new, v_new, lam_q, norm_g.reshape(1, HEAD_DIM),
      *([pool_k] * G), *([pool_v] * G))


def _compress_rows(x, pos_ref, w1_ref, w2_ref):
    x = (x + pos_ref[...]).astype(bf16)
    hid = jax.nn.gelu(jnp.dot(x, w1_ref[...], preferred_element_type=f32))
    return jnp.dot(hid.astype(bf16), w2_ref[...], preferred_element_type=f32)


def _compress_kernel(x_ref, pos_ref, w1_ref, w2_ref, o_ref):
    o_ref[...] = _compress_rows(x_ref[...], pos_ref, w1_ref, w2_ref)


def nsa_compress(blocks, pos_flat, w1c, w2c, tm=256):
    R, K = blocks.shape
    tm = min(tm, R)
    assert R % tm == 0
    return pl.pallas_call(
        _compress_kernel,
        out_shape=jax.ShapeDtypeStruct((R, 2 * HEAD_DIM), f32),
        grid=(R // tm,),
        in_specs=[pl.BlockSpec((tm, K), lambda i: (i, 0)), pl.BlockSpec((1, K), lambda i: (0, 0)),
                  pl.BlockSpec((K, 2 * CMP_HIDDEN), lambda i: (0, 0)),
                  pl.BlockSpec((2 * CMP_HIDDEN, 2 * HEAD_DIM), lambda i: (0, 0))],
        out_specs=pl.BlockSpec((tm, 2 * HEAD_DIM), lambda i: (i, 0)),
        compiler_params=_cparams(("parallel",)),
        name="nsa_compress",
    )(blocks, pos_flat, w1c, w2c)


def _compress_paged_kernel(pt_ref, pos_ref, w1_ref, w2_ref, *rest, G, page):
    pages = rest[:G]
    o_ref, xs = rest[G:]
    w = 2 * HEAD_DIM
    for i in range(G):
        xs[page * i:page * (i + 1), :] = jnp.transpose(pages[i][...])
    nb = G * page // CMP_BLOCK
    acc = None
    for t in range(CMP_BLOCK):
        xt = xs[pl.ds(t, nb, stride=CMP_BLOCK), :] + pos_ref[:, w * t:w * (t + 1)]
        part = jnp.dot(xt.astype(bf16), w1_ref[w * t:w * (t + 1), :], preferred_element_type=f32)
        acc = part if acc is None else acc + part
    hid = jax.nn.gelu(acc)
    o_ref[0] = jnp.dot(hid.astype(bf16), w2_ref[...], preferred_element_type=f32)


def nsa_compress_paged(page_table, pool, pos_flat, w1c, w2c, *, layer):
    B, n_pages = page_table.shape
    page = pool.shape[3]
    K = CMP_BLOCK * 2 * HEAD_DIM
    rp = page // CMP_BLOCK
    G = min(PAGES_PER_STEP, n_pages)
    assert n_pages % G == 0 and page % CMP_BLOCK == 0 and (G * rp) % 8 == 0
    const = lambda shape: pl.BlockSpec(shape, lambda b, j, pt: (0, 0))
    return pl.pallas_call(
        functools.partial(_compress_paged_kernel, G=G, page=page),
        out_shape=jax.ShapeDtypeStruct((B, n_pages * rp, 2 * HEAD_DIM), f32),
        grid_spec=pltpu.PrefetchScalarGridSpec(
            num_scalar_prefetch=1, grid=(B, n_pages // G),
            in_specs=[const((1, K)), const((K, 2 * CMP_HIDDEN)), const((2 * CMP_HIDDEN, 2 * HEAD_DIM))]
            + _page_specs(layer, n_pages, G, (2 * HEAD_DIM, page)),
            out_specs=pl.BlockSpec((1, G * rp, 2 * HEAD_DIM), lambda b, j, pt: (b, j, 0)),
            scratch_shapes=[pltpu.VMEM((G * page, 2 * HEAD_DIM), f32)]),
        compiler_params=_cparams(("parallel", "arbitrary")),
        name="nsa_compress_paged",
    )(page_table.reshape(-1), pos_flat, w1c, w2c, *([pool] * G))


def _cmp_block_end(r, half):
    blk = jnp.where(r < half, 2 * r, 2 * (r - half) + 1)
    return (blk + 1) * CMP_BLOCK - 1


def _nsa_kernel(qr_in, qw_in, sm_in, cmp_ref, slc_ref, win_ref,
                o_ref, qr_ref, v_ref, sel_ref, ocmp_ref, ms_ref, ls_ref, accs_ref, mw_ref, lw_ref, accw_ref,
                oT_ref, *, tq, tk, tkw, nk, nc, nsp, topk):
    qi = pl.program_id(1)
    kj = pl.program_id(2)
    q_lo = qi * tq
    H = N_HEADS
    d = HEAD_DIM
    scale = d ** -0.5
    needed = (q_lo + tq - 1) // tk + 1
    w_lo = jnp.maximum(q_lo - (WINDOW - 1), 0) // tkw
    w_hi = (q_lo + tq - 1) // tkw
    half = nc // 2

    def heads_on_lanes(xT):
        return jnp.concatenate([xT[d * h:d * (h + 1), :] for h in range(H)], axis=1)

    def qpos_row(n):
        return q_lo + (lax.broadcasted_iota(jnp.int32, (1, n), 1) & (tq - 1))

    @pl.when(kj == 0)
    def _():
        zeros = jnp.zeros((d, H * tq), f32)
        qrT = jnp.transpose(qr_in[0]) * scale
        qwT = jnp.transpose(qw_in[0]) * scale
        qr_ref[...] = jnp.concatenate([heads_on_lanes(qrT), zeros], 0).astype(bf16)
        qw = jnp.concatenate([heads_on_lanes(qwT), zeros], 0).astype(bf16)
        cmp = cmp_ref[0]
        s = jnp.dot(cmp.astype(bf16), qw, preferred_element_type=f32)
        c_end = _cmp_block_end(lax.broadcasted_iota(jnp.int32, (nc, 1), 0), half)
        c_ok = c_end <= qpos_row(H * tq)
        s = jnp.where(c_ok, s, NEG)
        p = jnp.where(c_ok, jnp.exp(s - jnp.max(s, axis=0, keepdims=True)), 0.0)
        p = p / jnp.maximum(jnp.sum(p, axis=0, keepdims=True), TINY)
        cmpT = jnp.transpose(cmp)
        ocmp_ref[...] = jnp.dot(cmpT[d:2 * d, :].astype(bf16), p.astype(bf16), preferred_element_type=f32)
        imp = p[:, 0:tq]
        for h in range(1, H):
            imp = imp + p[:, h * tq:(h + 1) * tq]
        imp = imp[:half, :] + imp[half:, :]
        if nsp > half:
            imp = jnp.concatenate([imp, jnp.zeros((nsp - half, tq), f32)], 0)
        sb = lax.broadcasted_iota(jnp.int32, (nsp, tq), 0)
        cur = qpos_row(tq) // SEL_BLOCK
        v = jnp.where((sb == cur) | (sb == 0), FORCE_SCORE, imp)
        v = jnp.where(sb > cur, NEG, v)
        v_ref[...] = v

        def rank(i, cnt):
            vi = v_ref[pl.ds(i, 1), :]
            ahead = (vi > v) | ((vi == v) & (i < sb))
            return cnt + jnp.where(ahead, 1.0, 0.0)

        cnt = lax.fori_loop(0, nsp, rank, jnp.zeros((nsp, tq), f32))
        sel_ref[...] = jnp.where(cnt < topk, 1.0, 0.0)
        for m_r, l_r, a_r in ((ms_ref, ls_ref, accs_ref), (mw_ref, lw_ref, accw_ref)):
            m_r[...] = jnp.full(m_r.shape, NEG, f32)
            l_r[...] = jnp.zeros(l_r.shape, f32)
            a_r[...] = jnp.zeros(a_r.shape, f32)

    def flash(kv_ref, ok, m_r, l_r, a_r):
        kv = kv_ref[0]
        s = jnp.dot(kv.astype(bf16), qr_ref[...], preferred_element_type=f32)
        ok4 = jnp.concatenate([jnp.where(ok, 1.0, 0.0)] * H, axis=1) > 0.5
        s = jnp.where(ok4, s, NEG)
        m_old = m_r[...]
        m_new = jnp.maximum(m_old, jnp.max(s, axis=0, keepdims=True))
        alpha = jnp.exp(m_old - m_new)
        p = jnp.exp(s - m_new)
        l_r[...] = alpha * l_r[...] + jnp.sum(p, axis=0, keepdims=True)
        m_r[...] = m_new
        vT = jnp.transpose(kv)[d:2 * d, :].astype(bf16)
        a_r[...] = a_r[...] * alpha + jnp.dot(vT, p.astype(bf16), preferred_element_type=f32)

    @pl.when(kj < needed)
    def _():
        kpos = kj * tk + lax.broadcasted_iota(jnp.int32, (tk, tq), 0)
        nb = tk // SEL_BLOCK
        rows = [jnp.broadcast_to(sel_ref[pl.ds(kj * nb + c, 1), :], (SEL_BLOCK, tq)) for c in range(nb)]
        chosen = jnp.concatenate(rows, axis=0) > 0.5
        ok = chosen & (kpos <= qpos_row(tq))
        flash(slc_ref, ok, ms_ref, ls_ref, accs_ref)

    @pl.when(kj <= w_hi - w_lo)
    def _():
        wpos = (w_lo + kj) * tkw + lax.broadcasted_iota(jnp.int32, (tkw, tq), 0)
        qp = qpos_row(tq)
        ok = (wpos <= qp) & (qp - wpos < WINDOW) & (wpos >= 0)
        flash(win_ref, ok, mw_ref, lw_ref, accw_ref)

    @pl.when(kj == nk - 1)
    def _():
        smT = jnp.transpose(sm_in[0])
        g = jax.nn.sigmoid(smT[SMALL_GATES:SMALL_GATES + 3 * H, :])
        o_slc = accs_ref[...] / ls_ref[...]
        o_win = accw_ref[...] / lw_ref[...]
        o_cmp = ocmp_ref[...]
        for h in range(H):
            cs = slice(h * tq, (h + 1) * tq)
            oT_ref[d * h:d * (h + 1), :] = (g[3 * h:3 * h + 1] * o_cmp[:, cs]
                                            + g[3 * h + 1:3 * h + 2] * o_slc[:, cs]
                                            + g[3 * h + 2:3 * h + 3] * o_win[:, cs])
        o_ref[0] = jnp.transpose(oT_ref[...]).astype(o_ref.dtype)


def nsa_attention(qr, z3, cmp, slc, win, *, tq, tk, tkw):
    B, T, _ = qr.shape
    nc = cmp.shape[1]
    assert T % tq == 0 and T % tk == 0 and T % tkw == 0 and tk % SEL_BLOCK == 0 and nc % 2 == 0
    assert tq % 128 == 0 and tk % 128 == 0 and tkw % 128 == 0 and nc % 8 == 0 and (tq & (tq - 1)) == 0
    nq, nk = T // tq, T // tk
    nsp = T // SEL_BLOCK
    assert nsp >= nc // 2 and nsp % 8 == 0
    last = lambda qi: (qi * tq + tq - 1) // tk
    w_lo = lambda qi: jnp.maximum(qi * tq - (WINDOW - 1), 0) // tkw
    w_hi = lambda qi: (qi * tq + tq - 1) // tkw
    for qi in range(nq):
        lo = max(qi * tq - (WINDOW - 1), 0) // tkw
        assert (qi * tq + tq - 1) // tkw - lo + 1 <= nk
    w_idx = lambda qi, kj: jnp.minimum(w_lo(qi) + kj, w_hi(qi))
    kern = functools.partial(_nsa_kernel, tq=tq, tk=tk, tkw=tkw, nk=nk, nc=nc, nsp=nsp,
                             topk=min(SEL_TOPK, nsp))
    stat = pltpu.VMEM((1, N_HEADS * tq), f32)
    acc = pltpu.VMEM((HEAD_DIM, N_HEADS * tq), f32)
    return pl.pallas_call(
        kern,
        out_shape=jax.ShapeDtypeStruct((B, T, GROUP_WIDTH), bf16),
        grid=(B, nq, nk),
        in_specs=[
            pl.BlockSpec((1, tq, GROUP_WIDTH), lambda b, qi, kj: (b, qi, 0)),
            pl.BlockSpec((1, tq, GROUP_WIDTH), lambda b, qi, kj: (b, qi, CB_NS_Q)),
            pl.BlockSpec((1, tq, 128), lambda b, qi, kj: (b, qi, CB128_SMALL)),
            pl.BlockSpec((1, nc, 128), lambda b, qi, kj: (b, 0, 0)),
            pl.BlockSpec((1, tk, 128), lambda b, qi, kj: (b, jnp.minimum(kj, last(qi)), 0)),
            pl.BlockSpec((1, tkw, 128), lambda b, qi, kj: (b, w_idx(qi, kj), 0)),
        ],
        out_specs=pl.BlockSpec((1, tq, GROUP_WIDTH), lambda b, qi, kj: (b, qi, 0)),
        scratch_shapes=[pltpu.VMEM((128, N_HEADS * tq), bf16), pltpu.VMEM((nsp, tq), f32),
                        pltpu.VMEM((nsp, tq), f32), acc, stat, stat, acc, stat, stat, acc,
                        pltpu.VMEM((GROUP_WIDTH, tq), f32)],
        compiler_params=_cparams(("parallel", "parallel", "arbitrary")),
        name="nsa_attention",
    )(qr, z3, z3, cmp, slc, win)


def _nsa_decode_kernel(pt_ref, qr_in, qw_in, sm_in, cmp_ref, sn_ref, wb_ref, wn_ref, *rest,
                       G, n_steps, Ts, pos0, n_past, wpos0, nc, nsl, topk, page):
    pages = rest[:G]
    o_ref, qr_ref, selst_ref, ocmp_ref, owin_ref, m_ref, l_ref, acc_ref = rest[G:]
    j = pl.program_id(1)
    H, d = N_HEADS, HEAD_DIM
    scale = d ** -0.5
    R = H * Ts
    half = nc // 2
    bps = G * page // SEL_BLOCK

    def rows_by_head(x):
        z = jnp.zeros((Ts, d), f32)
        return jnp.concatenate([jnp.concatenate([x[:, d * h:d * (h + 1)], z], axis=1) for h in range(H)], axis=0)

    def tile_heads(x):
        return jnp.concatenate([x] * H, axis=0)

    @pl.when(j == 0)
    def _():
        qr = rows_by_head(qr_in[0] * scale)
        qr_ref[...] = qr
        qrb = qr.astype(bf16)
        qwb = rows_by_head(qw_in[0] * scale).astype(bf16)
        qpos = pos0 + (lax.broadcasted_iota(jnp.int32, (R, 1), 0) & (Ts - 1))
        cmpb = cmp_ref[0].astype(bf16)
        s = _nt(qwb, cmpb)
        c_ok = _cmp_block_end(lax.broadcasted_iota(jnp.int32, (1, nc), 1), half) <= qpos
        s = jnp.where(c_ok, s, NEG)
        p = jnp.where(c_ok, jnp.exp(s - jnp.max(s, axis=1, keepdims=True)), 0.0)
        p = p / jnp.maximum(jnp.sum(p, axis=1, keepdims=True), TINY)
        ocmp_ref[...] = jnp.dot(p.astype(bf16), cmpb, preferred_element_type=f32)
        imp = p[0:Ts, :]
        for h in range(1, H):
            imp = imp + p[h * Ts:(h + 1) * Ts, :]
        imp = imp[:, :half] + imp[:, half:]
        imp = jnp.concatenate([imp, jnp.zeros((Ts, nsl - half), f32)], axis=1)
        lane = lax.broadcasted_iota(jnp.int32, (Ts, nsl), 1)
        cur = (pos0 + lax.broadcasted_iota(jnp.int32, (Ts, 1), 0)) // SEL_BLOCK
        v = jnp.where((lane == cur) | (lane == 0), FORCE_SCORE, imp)
        v = jnp.where(lane > cur, NEG, v)
        lanef = lane.astype(f32)
        sel = jnp.zeros((Ts, nsl), f32)
        for _ in range(topk):
            mx = jnp.max(v, axis=1, keepdims=True)
            first = jnp.min(jnp.where(v == mx, lanef, float(nsl)), axis=1, keepdims=True)
            pick = lanef == first
            sel = jnp.where(pick, 1.0, sel)
            v = jnp.where(pick, -jnp.inf, v)
        for st in range(n_steps):
            selst_ref[st] = sel[:, bps * st:bps * (st + 1)]
        blk_new = n_past // SEL_BLOCK
        sel_new = tile_heads(sel[:, blk_new:blk_new + 1]) > 0.5
        snb = sn_ref[0].astype(bf16)
        s = _nt(qrb, snb)
        kpos = n_past + lax.broadcasted_iota(jnp.int32, (1, Ts), 1)
        ok = sel_new & (kpos <= qpos)
        s = jnp.where(ok, s, NEG)
        m = jnp.max(s, axis=1, keepdims=True)
        p = jnp.where(ok, jnp.exp(s - m), 0.0)
        m_ref[...] = m
        l_ref[...] = jnp.sum(p, axis=1, keepdims=True)
        acc_ref[...] = jnp.dot(p.astype(bf16), snb, preferred_element_type=f32)
        wbT = wb_ref[...].astype(bf16)
        wnb = wn_ref[0].astype(bf16)
        lw = wbT.shape[1]
        s1 = jnp.dot(qrb, wbT, preferred_element_type=f32)
        s2 = _nt(qrb, wnb)
        wp1 = wpos0 + lax.broadcasted_iota(jnp.int32, (1, lw), 1)
        wp2 = wpos0 + lw + lax.broadcasted_iota(jnp.int32, (1, Ts), 1)
        ok1 = (wp1 <= qpos) & (qpos - wp1 < WINDOW) & (wp1 >= 0)
        ok2 = (wp2 <= qpos) & (qpos - wp2 < WINDOW) & (wp2 >= 0)
        s1 = jnp.where(ok1, s1, NEG)
        s2 = jnp.where(ok2, s2, NEG)
        mw = jnp.maximum(jnp.max(s1, axis=1, keepdims=True), jnp.max(s2, axis=1, keepdims=True))
        p1 = jnp.where(ok1, jnp.exp(s1 - mw), 0.0)
        p2 = jnp.where(ok2, jnp.exp(s2 - mw), 0.0)
        lsum = jnp.sum(p1, axis=1, keepdims=True) + jnp.sum(p2, axis=1, keepdims=True)
        owin_ref[...] = (_nt(p1.astype(bf16), wbT)
                         + jnp.dot(p2.astype(bf16), wnb, preferred_element_type=f32)) / lsum

    kvT = jnp.concatenate([pg[...] for pg in pages], axis=1).astype(bf16)
    n = kvT.shape[1]
    s = jnp.dot(qr_ref[...].astype(bf16), kvT, preferred_element_type=f32)
    expand = (lax.broadcasted_iota(jnp.int32, (bps, n), 1) // SEL_BLOCK
              == lax.broadcasted_iota(jnp.int32, (bps, n), 0))
    chosen = jnp.dot(selst_ref[j].astype(bf16), jnp.where(expand, 1.0, 0.0).astype(bf16),
                     preferred_element_type=f32)
    ok = tile_heads(chosen) > 0.5
    s = jnp.where(ok, s, NEG)
    m_old = m_ref[...]
    m_new = jnp.maximum(m_old, jnp.max(s, axis=1, keepdims=True))
    alpha = jnp.exp(m_old - m_new)
    p = jnp.where(ok, jnp.exp(s - m_new), 0.0)
    l_ref[...] = alpha * l_ref[...] + jnp.sum(p, axis=1, keepdims=True)
    m_ref[...] = m_new
    acc_ref[...] = alpha * acc_ref[...] + _nt(p.astype(bf16), kvT)

    @pl.when(j == n_steps - 1)
    def _():
        g = jax.nn.sigmoid(sm_in[0][:, SMALL_GATES:SMALL_GATES + 3 * H])
        o_slc = acc_ref[...] / l_ref[...]
        outs = []
        for h in range(H):
            rs = slice(h * Ts, (h + 1) * Ts)
            outs.append(g[:, 3 * h:3 * h + 1] * ocmp_ref[rs, d:2 * d]
                        + g[:, 3 * h + 1:3 * h + 2] * o_slc[rs, d:2 * d]
                        + g[:, 3 * h + 2:3 * h + 3] * owin_ref[rs, d:2 * d])
        o_ref[0] = jnp.concatenate(outs, axis=1)


def nsa_decode(page_table, qr, z3, cmp, slc_new, win_cache, win_new, pool_slc, *, layer, pos0):
    B, Ts, _ = qr.shape
    n_pages = page_table.shape[1]
    page = pool_slc.shape[3]
    n_past = n_pages * page
    lw = win_cache.shape[3]
    nc = cmp.shape[1]
    G = min(PAGES_PER_STEP, n_pages)
    n_steps = n_pages // G
    n_sel = -(-(n_past + Ts) // SEL_BLOCK)
    nsl = -(-n_sel // 128) * 128
    assert n_pages % G == 0 and (Ts & (Ts - 1)) == 0 and Ts % 8 == 0 and page % SEL_BLOCK == 0
    assert pos0 == n_past and n_past % SEL_BLOCK + Ts <= SEL_BLOCK and nc % 2 == 0 and nc // 2 <= nsl
    R = N_HEADS * Ts
    row = lambda w, cb=0: pl.BlockSpec((1, Ts, w), lambda b, j, pt, cb=cb: (b, 0, cb))
    kern = functools.partial(_nsa_decode_kernel, G=G, n_steps=n_steps, Ts=Ts, pos0=pos0, n_past=n_past,
                             wpos0=n_past - lw, nc=nc, nsl=nsl, topk=min(SEL_TOPK, n_sel), page=page)
    buf = lambda: pltpu.VMEM((R, 128), f32)
    stat = lambda: pltpu.VMEM((R, 1), f32)
    return pl.pallas_call(
        kern,
        out_shape=jax.ShapeDtypeStruct((B, Ts, GROUP_WIDTH), f32),
        grid_spec=pltpu.PrefetchScalarGridSpec(
            num_scalar_prefetch=1, grid=(B, n_steps),
            in_specs=[row(256), row(256, CB_NS_Q), row(128, CB128_SMALL),
                      pl.BlockSpec((1, nc, 128), lambda b, j, pt: (b, 0, 0)),
                      row(128),
                      pl.BlockSpec((None, None, 128, lw), lambda b, j, pt: (layer, b, 0, 0)),
                      row(128)]
            + _page_specs(layer, n_pages, G, (128, page)),
            out_specs=row(256),
            scratch_shapes=[buf(), pltpu.VMEM((n_steps, Ts, G * page // SEL_BLOCK), f32), buf(), buf(),
                            stat(), stat(), buf()]),
        compiler_params=_cparams(("parallel", "arbitrary")),
        name="nsa_decode",
    )(page_table.reshape(-1), qr, z3, z3, cmp, slc_new, win_cache, win_new, *([pool_slc] * G))


def _mlstm_kernel(q_ref, k_ref, v_ref, og_ref, gc_ref, gr_ref, bias_ref, ng_ref, c0_ref, n0_ref, m0_ref,
                  o_ref, cout_ref, nout_ref, mout_ref, c_s, n_s, m_s, *, L):
    ci = pl.program_id(1)
    H, d = N_HEADS, HEAD_DIM

    @pl.when(ci == 0)
    def _():
        c_s[...] = c0_ref[0]
        n_s[...] = n0_ref[0]
        m_s[...] = m0_ref[0]

    q = q_ref[...]
    k = k_ref[...] * (d ** -0.5)
    v = v_ref[...]
    gc = gc_ref[...]
    gr = gr_ref[0]
    bias = bias_ref[...]
    t_i = lax.broadcasted_iota(jnp.int32, (L, L), 0)
    s_i = lax.broadcasted_iota(jnp.int32, (L, L), 1)
    causal = s_i <= t_i
    causal_T = t_i <= s_i
    outs = []
    for h in range(H):
        bi = bias[0:1, h:h + 1]
        bf_ = bias[1:2, h:h + 1]
        ig_c = gc[:, h:h + 1] + bi
        lf_c = jax.nn.log_sigmoid(gc[:, H + h:H + h + 1] + bf_)
        ig_r = gr[h:h + 1, :] + bi
        lf_r = jax.nn.log_sigmoid(gr[H + h:H + h + 1, :] + bf_)
        b_c = jnp.sum(jnp.where(causal, lf_r, 0.0), axis=1, keepdims=True)
        b_r = jnp.sum(jnp.where(causal_T, lf_c, 0.0), axis=0, keepdims=True)
        m_prev = m_s[h:h + 1, 0:1]
        D = jnp.where(causal, b_c - b_r + ig_r, -jnp.inf)
        m_t = jnp.maximum(b_c + m_prev, jnp.max(D, axis=1, keepdims=True))
        inter = jnp.exp(b_c + m_prev - m_t)
        qh = q[:, d * h:d * (h + 1)]
        kh = k[:, d * h:d * (h + 1)]
        vh = v[:, d * h:d * (h + 1)]
        qb = qh.astype(bf16)
        S = _nt(qb, kh.astype(bf16))
        Sw = jnp.exp(D - m_t) * S
        C = c_s[h]
        n_row = n_s[h:h + 1, :]
        num = (jnp.dot(Sw.astype(bf16), vh.astype(bf16), preferred_element_type=f32)
               + inter * jnp.dot(qb, C.astype(bf16), preferred_element_type=f32))
        qn = jnp.sum(qh * n_row, axis=1, keepdims=True)
        den = jnp.sum(Sw, axis=1, keepdims=True) + inter * qn
        hh = num / jnp.maximum(jnp.abs(den), jnp.exp(-m_t))
        m_new = m_t[L - 1:L, :]
        b_last = b_c[L - 1:L, :]
        wl = jnp.exp(b_last - b_c + ig_c - m_new)
        decay = jnp.exp(b_last + m_prev - m_new)
        kw = kh * wl
        c_s[h] = decay * C + lax.dot_general(kw.astype(bf16), vh.astype(bf16), (((0,), (0,)), ((), ())),
                                             preferred_element_type=f32)
        n_s[h:h + 1, :] = decay * n_row + jnp.sum(kw, axis=0, keepdims=True)
        m_s[h:h + 1, :] = jnp.broadcast_to(m_new, (1, 128))
        mu = jnp.mean(hh, axis=1, keepdims=True)
        var = jnp.mean(jnp.square(hh - mu), axis=1, keepdims=True)
        outs.append((hh - mu) * lax.rsqrt(var + EPS))
    hcat = jnp.concatenate(outs, axis=1) * ng_ref[...]
    o_ref[...] = (hcat * jax.nn.sigmoid(og_ref[...])).astype(o_ref.dtype)
    cout_ref[0] = c_s[...]
    nout_ref[0] = n_s[...]
    mout_ref[0] = m_s[...]


def mlstm(z, gates_T, gate_b, norm_g, C0, n0, m0, *, B, T, L):
    nchunk = T // L
    m0p = jnp.broadcast_to(m0[:, :, None], (B, N_HEADS, 128))
    row = lambda cb: pl.BlockSpec((L, 256), lambda b, c, cb=cb: (b * nchunk + c, cb))
    st = lambda *shape: pl.BlockSpec((1,) + shape, lambda b, c: (b,) + (0,) * len(shape))
    out, C, n, m = pl.pallas_call(
        functools.partial(_mlstm_kernel, L=L),
        out_shape=[jax.ShapeDtypeStruct((B * T, GROUP_WIDTH), _act_dtype(L)),
                   jax.ShapeDtypeStruct((B, N_HEADS, HEAD_DIM, HEAD_DIM), f32),
                   jax.ShapeDtypeStruct((B, N_HEADS, HEAD_DIM), f32),
                   jax.ShapeDtypeStruct((B, N_HEADS, 128), f32)],
        grid=(B, nchunk),
        in_specs=[row(CB_ML_Q), row(CB_ML_K), row(CB_ML_V), row(CB_ML_O),
                  pl.BlockSpec((L, 128), lambda b, c: (b * nchunk + c, CB128_SMALL)),
                  pl.BlockSpec((1, 8, L), lambda b, c: (b * nchunk + c, 0, 0)),
                  pl.BlockSpec((2, N_HEADS), lambda b, c: (0, 0)),
                  pl.BlockSpec((1, GROUP_WIDTH), lambda b, c: (0, 0)),
                  st(N_HEADS, HEAD_DIM, HEAD_DIM), st(N_HEADS, HEAD_DIM), st(N_HEADS, 128)],
        out_specs=[pl.BlockSpec((L, GROUP_WIDTH), lambda b, c: (b * nchunk + c, 0)),
                   st(N_HEADS, HEAD_DIM, HEAD_DIM), st(N_HEADS, HEAD_DIM), st(N_HEADS, 128)],
        scratch_shapes=[pltpu.VMEM((N_HEADS, HEAD_DIM, HEAD_DIM), f32), pltpu.VMEM((N_HEADS, HEAD_DIM), f32),
                        pltpu.VMEM((N_HEADS, 128), f32)],
        compiler_params=_cparams(("parallel", "arbitrary")),
        name="mlstm",
    )(z, z, z, z, z, gates_T, gate_b, norm_g.reshape(1, GROUP_WIDTH), C0, n0, m0p)
    return out, C, n, m[:, :, 0]


def _rglru_kernel(x_ref, gate_ref, cw_ref, cb_ref, wa_ref, ba_ref, wx_ref, bx_ref, lam_ref, h0_ref, buf0_ref,
                  y_ref, hout_ref, bufout_ref, xbuf, a_s, u_s, h_s, hs_s, *, tm):
    ti = pl.program_id(0)
    Bb = x_ref.shape[0]
    W = GROUP_WIDTH

    @pl.when(ti == 0)
    def _():
        xbuf[:, 0:8, :] = buf0_ref[...]
        h_s[...] = h0_ref[...]

    xbuf[:, 8:8 + tm, :] = x_ref[...]
    cw = cw_ref[...]
    xc = cb_ref[...] + cw[0:1, :] * xbuf[:, 5:5 + tm, :]
    for j in range(1, CONV_W):
        xc = xc + cw[j:j + 1, :] * xbuf[:, 5 + j:5 + j + tm, :]
    flat = xc.reshape(Bb * tm, W).astype(bf16)
    r = jax.nn.sigmoid(jnp.dot(flat, wa_ref[...], preferred_element_type=f32) + ba_ref[...])
    i = jax.nn.sigmoid(jnp.dot(flat, wx_ref[...], preferred_element_type=f32) + bx_ref[...])
    lam = lam_ref[...]
    softplus = jnp.maximum(-lam, 0.0) + jnp.log1p(jnp.exp(-jnp.abs(lam)))
    log_a = (-LRU_C * r) * softplus
    a = jnp.exp(log_a)
    u = jnp.sqrt(-jnp.tanh(log_a) * (a * a + 1.0)) * (i * xc.reshape(Bb * tm, W))
    a_s[...] = a.reshape(Bb, tm, W)
    u_s[...] = u.reshape(Bb, tm, W)

    def body(t, h):
        h = a_s[:, pl.ds(t, 1), :] * h + u_s[:, pl.ds(t, 1), :]
        hs_s[:, pl.ds(t, 1), :] = h
        return h

    h_last = lax.fori_loop(0, tm, body, h_s[...])
    h_s[...] = h_last
    y_ref[...] = (hs_s[...] * jax.nn.gelu(gate_ref[...])).astype(y_ref.dtype)
    hout_ref[...] = h_last
    bufout_ref[...] = xbuf[:, tm:tm + 8, :]
    xbuf[:, 0:8, :] = xbuf[:, tm:tm + 8, :]


def rglru(z3, conv_w, conv_b, wa_bd, ba, wx_bd, bx, lam, h0, buf0, *, tm):
    B, T, _ = z3.shape
    tm = min(tm, T)
    assert T % tm == 0 and tm % 8 == 0
    W = GROUP_WIDTH
    buf8 = jnp.concatenate([jnp.zeros((B, 8 - (CONV_W - 1), W), f32), buf0], 1)
    vec = lambda: pl.BlockSpec((1, W), lambda i: (0, 0))
    mat = lambda: pl.BlockSpec((W, W), lambda i: (0, 0))
    y, h, buf = pl.pallas_call(
        functools.partial(_rglru_kernel, tm=tm),
        out_shape=[jax.ShapeDtypeStruct((B, T, W), _act_dtype(tm)), jax.ShapeDtypeStruct((B, 1, W), f32),
                   jax.ShapeDtypeStruct((B, 8, W), f32)],
        grid=(T // tm,),
        in_specs=[pl.BlockSpec((B, tm, W), lambda i: (0, i, CB_LR_X)),
                  pl.BlockSpec((B, tm, W), lambda i: (0, i, CB_LR_G)),
                  pl.BlockSpec((CONV_W, W), lambda i: (0, 0)), vec(), mat(), vec(), mat(), vec(), vec(),
                  pl.BlockSpec((B, 1, W), lambda i: (0, 0, 0)), pl.BlockSpec((B, 8, W), lambda i: (0, 0, 0))],
        out_specs=[pl.BlockSpec((B, tm, W), lambda i: (0, i, 0)),
                   pl.BlockSpec((B, 1, W), lambda i: (0, 0, 0)), pl.BlockSpec((B, 8, W), lambda i: (0, 0, 0))],
        scratch_shapes=[pltpu.VMEM((B, tm + 8, W), f32), pltpu.VMEM((B, tm, W), f32), pltpu.VMEM((B, tm, W), f32),
                        pltpu.VMEM((B, 1, W), f32), pltpu.VMEM((B, tm, W), f32)],
        compiler_params=_cparams(("arbitrary",)),
        name="rglru",
    )(z3, z3, conv_w, conv_b.reshape(1, W), wa_bd, ba.reshape(1, W), wx_bd, bx.reshape(1, W),
      lam.reshape(1, W), h0.reshape(B, 1, W), buf8)
    return y, h[:, 0], buf[:, 8 - (CONV_W - 1):]


def _mem_attn_kernel(q_ref, k_ref, v_ref, o_ref):
    q = q_ref[0]
    lead = (0,) * (len(k_ref.shape) - 2)
    k = k_ref[lead]
    v = v_ref[lead]
    dh = MEM_HEAD_DIM
    for h in range(MEM_HEADS):
        cs = slice(dh * h, dh * (h + 1))
        s = _nt(q[:, cs].astype(bf16), k[:, cs].astype(bf16)) * (dh ** -0.5)
        e = jnp.exp(s - jnp.max(s, axis=-1, keepdims=True))
        p = e / jnp.sum(e, axis=-1, keepdims=True)
        o_ref[0, :, cs] = jnp.dot(p.astype(bf16), v[:, cs].astype(bf16),
                                  preferred_element_type=f32).astype(o_ref.dtype)


def mem_attention(q, k, v, *, layer=None, tq=256):
    B, T, D = q.shape
    tq = min(tq, T)
    M = k.shape[-2]
    if layer is None:
        kv_spec = pl.BlockSpec((1, M, D), lambda b, i: (b, 0, 0))
    else:
        kv_spec = pl.BlockSpec((1, 1, M, D), lambda b, i: (layer, b, 0, 0))
    return pl.pallas_call(
        _mem_attn_kernel,
        out_shape=jax.ShapeDtypeStruct((B, T, D), _act_dtype(tq)),
        grid=(B, T // tq),
        in_specs=[pl.BlockSpec((1, tq, D), lambda b, i: (b, i, 0)), kv_spec, kv_spec],
        out_specs=pl.BlockSpec((1, tq, D), lambda b, i: (b, i, 0)),
        compiler_params=_cparams(("parallel", "parallel")),
        name="mem_attention",
    )(q, k, v)


def _even_odd(cmp):
    B, nc, w = cmp.shape
    return jnp.swapaxes(cmp.reshape(B, nc // 2, 2, w), 1, 2).reshape(B, nc, w)


def _layer(x, lw, lam_init, tables, *, B, T, mem, ml_state, lr_state, paged, cfg):
    N = B * T
    z = matmul(x, lw['w_in'], g=lw['g_mix'], tm=2048)
    z3 = z.reshape(B, T, D_PROJ_PAD)
    dfq, dfk, nsq, slc_new, win_new = rope_prep(z, tables, cfg['rope_tm'])
    df_v = z[:, 256 * CB_DF_V:256 * (CB_DF_V + 1)]
    cmp_new = z[:, 128 * CB128_CMP:128 * (CB128_CMP + 1)]
    small = z[:, 128 * CB128_SMALL:128 * CB128_SMALL + SMALL_GATES]
    r3 = lambda a: a.reshape(B, T, a.shape[-1])

    L = cfg['ml_chunk']
    gates_T = jnp.swapaxes(small.reshape(N // L, L, SMALL_GATES), 1, 2)
    o_ml, ml_C, ml_n, ml_m = mlstm(z, gates_T, lw['ml_gate_b'], lw['ml_norm_g'], *ml_state, B=B, T=T, L=L)

    if paged is None:
        o_df = diff_attention(r3(dfq), r3(dfk), z3, lw['df_lam'], lw['df_norm_g'], lam_init=lam_init,
                              tq=cfg['df_tq'], tk=cfg['df_tk'])
        comp = nsa_compress(cmp_new.reshape(N // CMP_BLOCK, CMP_BLOCK * 128), lw['nsa_pos_flat'],
                            lw['nsa_w1c'], lw['nsa_w2c']).reshape(B, T // CMP_BLOCK, 128)
        o_ns = nsa_attention(r3(nsq), z3, _even_odd(comp), r3(slc_new), r3(win_new),
                             tq=cfg['ns_tq'], tk=cfg['ns_tk'], tkw=cfg['ns_tkw'])
        win_state = r3(win_new)[:, T - min(WINDOW, T):]
    else:
        pt, l = paged['page_table'], paged['layer']
        n_past = pt.shape[1] * paged['df_k'].shape[3]
        assert (n_past + T) // CMP_BLOCK == n_past // CMP_BLOCK
        o_df = diff_decode(pt, r3(dfq), r3(dfk), r3(df_v), paged['df_k'], paged['df_v'], lw['df_lam'],
                           lw['df_norm_g'], layer=l, pos0=n_past, lam_init=lam_init)
        comp = nsa_compress_paged(pt, paged['nsa_cmp'], lw['nsa_pos_flat'], lw['nsa_w1c'], lw['nsa_w2c'], layer=l)
        o_ns = nsa_decode(pt, r3(nsq), z3, _even_odd(comp), r3(slc_new), paged['nsa_win'], r3(win_new),
                          paged['nsa_slc'], layer=l, pos0=n_past)
        win_cat = jnp.concatenate([paged['nsa_win'][l], jnp.swapaxes(r3(win_new), 1, 2)], 2)
        win_state = jnp.swapaxes(win_cat[:, :, win_cat.shape[2] - min(WINDOW, win_cat.shape[2]):], 1, 2)

    o_lr, lr_h, lr_buf = rglru(z3, lw['lru_conv_w'], lw['lru_conv_b'], lw['lru_wa_bd'],
                               lw['lru_ba'], lw['lru_wx_bd'], lw['lru_bx'], lw['lru_lambda'], *lr_state,
                               tm=cfg['lru_tm'])

    x = matmul([o_ml, o_df.reshape(N, 256), o_ns.reshape(N, 256), o_lr.reshape(N, 256)], lw['w_out'], res=x,
               tm=2048)
    q = matmul(x, lw['w_mq'], g=lw['g_mem_q'], tm=2048, out_dtype=_act_dtype(cfg['mem_tq']))
    if isinstance(mem, tuple):
        att = mem_attention(q.reshape(B, T, D_MODEL), mem[0], mem[1], tq=cfg['mem_tq'])
    else:
        att = mem_attention(q.reshape(B, T, D_MODEL), mem['k'], mem['v'], layer=mem['layer'], tq=cfg['mem_tq'])
    x = matmul(att.reshape(N, D_MODEL), lw['w_mo'], res=x, tm=2048)
    u = matmul(x, lw['w_up'], g=lw['g_mlp'], act='relu2', out_dtype=bf16, tm=2048)
    x = matmul(u, lw['w_down'], res=x)
    new = dict(df_k=dfk.reshape(B, T, N_HEADS, HEAD_DIM), df_v=df_v.reshape(B, T, N_HEADS, HEAD_DIM),
               nsa_cmp=cmp_new.reshape(B, T, 2, HEAD_DIM), nsa_slc=slc_new.reshape(B, T, 2, HEAD_DIM),
               nsa_win=win_state.reshape(B, -1, 2, HEAD_DIM), ml_C=ml_C, ml_n=ml_n, ml_m=ml_m,
               lru_h=lr_h, lru_conv=lr_buf)
    return x, new


def _block_diag(w):
    H, a, b = w.shape
    eye = jnp.eye(H, dtype=w.dtype)
    return jnp.einsum('hij,hg->higj', w, eye).reshape(H * a, H * b)


def kernel(x_prompt, x_sample, mem_prompt, cache_df_k, cache_df_v, cache_nsa_cmp, cache_nsa_slc, cache_nsa_win, state_ml_C, state_ml_n, state_ml_m, state_lru_h, state_lru_conv, cache_mem_k, cache_mem_v, page_table, g_mix, w_in, w_out, ml_gate_b, ml_norm_g, df_lam, df_norm_g, nsa_pos, nsa_w1, nsa_w2, lru_conv_w, lru_conv_b, lru_wa, lru_ba, lru_wx, lru_bx, lru_lambda, g_mem_q, g_mem_kv, w_mq, w_mk, w_mv, w_mo, g_mlp, w_up, w_down, g_final):
    Bp, Tp, _ = x_prompt.shape
    Bs, Ts, _ = x_sample.shape
    depth = w_in.shape[0]
    n_pool, page = cache_df_k.shape[1], cache_df_k.shape[2]
    n_past = page_table.shape[1] * page
    M = mem_prompt.shape[1]

    perm, n_real = _proj_perm()
    col_ok = (jnp.arange(D_PROJ_PAD) < n_real)
    w_in_p = jnp.where(col_ok[None, None, :], jnp.take(w_in, perm, axis=2), 0.0).astype(bf16)
    eye2 = jnp.eye(2, dtype=f32)
    w1 = nsa_w1.reshape(depth, 2, CMP_BLOCK, HEAD_DIM, CMP_HIDDEN)
    w1c = jnp.einsum('lstih,sg->ltsigh', w1, eye2).reshape(depth, CMP_BLOCK * 2 * HEAD_DIM, 2 * CMP_HIDDEN)
    w2c = jnp.einsum('lshd,sg->lshgd', nsa_w2, eye2).reshape(depth, 2 * CMP_HIDDEN, 2 * HEAD_DIM)
    cfg_p = dict(rope_tm=512, ml_chunk=128, df_tq=256, df_tk=512, ns_tq=256, ns_tk=512, ns_tkw=256,
                 lru_tm=256, mem_tq=256)
    cfg_s = dict(rope_tm=Bs * Ts, ml_chunk=Ts, lru_tm=Ts, mem_tq=Ts)
    tab_p = rope_tables(jnp.arange(Tp))
    tab_s = tuple(jnp.tile(t, (Bs, 1)) for t in rope_tables(n_past + jnp.arange(Ts)))

    tok_minor = lambda c: jnp.transpose(c, (0, 1, 3, 4, 2)).reshape(c.shape[0], c.shape[1], -1, c.shape[2])
    pool_df_k = tok_minor(cache_df_k)
    pool_df_v = tok_minor(cache_df_v)
    pool_cmp = tok_minor(cache_nsa_cmp)
    pool_slc = tok_minor(cache_nsa_slc)
    win_cache = tok_minor(cache_nsa_win)
    mem_k_cache = cache_mem_k.reshape(depth, Bs, M, D_MODEL)
    mem_v_cache = cache_mem_v.reshape(depth, Bs, M, D_MODEL)
    mem_rows = mem_prompt.reshape(Bp * M, D_MODEL)

    xp = x_prompt.reshape(Bp * Tp, D_MODEL)
    xs = x_sample.reshape(Bs * Ts, D_MODEL)
    names = ('df_k', 'df_v', 'nsa_cmp', 'nsa_slc', 'nsa_win', 'ml_C', 'ml_n', 'ml_m', 'lru_h', 'lru_conv')
    acc = {pre + n: [] for n in names for pre in ('p_', 's_')}
    acc['p_mem_k'] = []
    acc['p_mem_v'] = []
    for l in range(depth):
        lw = dict(g_mix=g_mix[l], w_in=w_in_p[l], w_out=w_out[l].astype(bf16), ml_gate_b=ml_gate_b[l],
                  ml_norm_g=ml_norm_g[l], df_lam=df_lam[l], df_norm_g=df_norm_g[l],
                  nsa_pos_flat=nsa_pos[l].reshape(1, CMP_BLOCK * 2 * HEAD_DIM), nsa_w1c=w1c[l].astype(bf16),
                  nsa_w2c=w2c[l].astype(bf16), lru_conv_w=lru_conv_w[l], lru_conv_b=lru_conv_b[l],
                  lru_wa_bd=_block_diag(lru_wa[l]).astype(bf16), lru_ba=lru_ba[l],
                  lru_wx_bd=_block_diag(lru_wx[l]).astype(bf16), lru_bx=lru_bx[l], lru_lambda=lru_lambda[l],
                  g_mem_q=g_mem_q[l], w_mq=w_mq[l].astype(bf16), w_mo=w_mo[l].astype(bf16), g_mlp=g_mlp[l],
                  w_up=w_up[l].astype(bf16), w_down=w_down[l].astype(bf16))
        lam_init = 0.8 - 0.6 * math.exp(-0.3 * l)
        mk_p = matmul(mem_rows, w_mk[l].astype(bf16), g=g_mem_kv[l])
        mv_p = matmul(mem_rows, w_mv[l].astype(bf16), g=g_mem_kv[l])
        ml0 = (jnp.zeros((Bp, N_HEADS, HEAD_DIM, HEAD_DIM), f32), jnp.zeros((Bp, N_HEADS, HEAD_DIM), f32),
               jnp.zeros((Bp, N_HEADS), f32))
        lr0 = (jnp.zeros((Bp, GROUP_WIDTH), f32), jnp.zeros((Bp, CONV_W - 1, GROUP_WIDTH), f32))
        xp, new_p = _layer(xp, lw, lam_init, tab_p, B=Bp, T=Tp,
                           mem=(mk_p.reshape(Bp, M, D_MODEL), mv_p.reshape(Bp, M, D_MODEL)),
                           ml_state=ml0, lr_state=lr0, paged=None, cfg=cfg_p)
        paged = dict(page_table=page_table, layer=l, df_k=pool_df_k, df_v=pool_df_v, nsa_cmp=pool_cmp,
                     nsa_slc=pool_slc, nsa_win=win_cache)
        xs, new_s = _layer(xs, lw, lam_init, tab_s, B=Bs, T=Ts,
                           mem=dict(k=mem_k_cache, v=mem_v_cache, layer=l),
                           ml_state=(state_ml_C[l], state_ml_n[l], state_ml_m[l]),
                           lr_state=(state_lru_h[l], state_lru_conv[l]), paged=paged, cfg=cfg_s)
        for pre, new in (('p_', new_p), ('s_', new_s)):
            for n in names:
                acc[pre + n].append(new[n])
        acc['p_mem_k'].append(mk_p.reshape(Bp, M, MEM_HEADS, MEM_HEAD_DIM))
        acc['p_mem_v'].append(mv_p.reshape(Bp, M, MEM_HEADS, MEM_HEAD_DIM))
    st = {k: jnp.stack(v) for k, v in acc.items()}
    y_prompt = rmsnorm_rows(xp, g_final).reshape(Bp, Tp, D_MODEL)
    y_sample = rmsnorm_rows(xs, g_final).reshape(Bs, Ts, D_MODEL)
    return (y_prompt, y_sample, st['p_df_k'], st['s_df_k'], st['p_df_v'], st['s_df_v'],
            st['p_nsa_cmp'], st['s_nsa_cmp'], st['p_nsa_slc'], st['s_nsa_slc'],
            st['p_nsa_win'], st['s_nsa_win'], st['p_ml_C'], st['s_ml_C'], st['p_ml_n'], st['s_ml_n'],
            st['p_ml_m'], st['s_ml_m'], st['p_lru_h'], st['s_lru_h'], st['p_lru_conv'], st['s_lru_conv'],
            st['p_mem_k'], st['p_mem_v'])
```

```python
import functools
import math

import jax
import jax.numpy as jnp
from jax import lax
from jax.experimental import pallas as pl
from jax.experimental.pallas import tpu as pltpu

f32 = jnp.float32
bf16 = jnp.bfloat16

D_MODEL = 1024
GROUP_WIDTH = 256
N_HEADS = 4
HEAD_DIM = 64
DQK = 32
ROPE_THETA = 10000.0
CMP_BLOCK = 32
CMP_HIDDEN = 128
SEL_BLOCK = 64
SEL_TOPK = 16
WINDOW = 512
CONV_W = 4
LRU_C = 8.0
MEM_HEADS = 4
MEM_HEAD_DIM = 256
EPS = 1e-6
NEG = -1e30
FORCE_SCORE = 1e9
TINY = 1e-30

VMEM_LIMIT_BYTES = 48 * 1024 * 1024
PAGES_PER_STEP = 16

_SRC_SPLITS = (
    ('ml_q', 256), ('ml_k', 256), ('ml_v', 256), ('ml_i', 4), ('ml_f', 4), ('ml_o', 256),
    ('df_q', 256), ('df_k', 256), ('df_v', 256),
    ('ns_q', 256), ('ns_kc', 64), ('ns_vc', 64), ('ns_ks', 64), ('ns_vs', 64),
    ('ns_kw', 64), ('ns_vw', 64), ('ns_g', 12), ('lr_x', 256), ('lr_g', 256),
)
_DST_ORDER = ('ml_q', 'ml_k', 'ml_v', 'ml_o', 'df_q', 'df_k', 'df_v', 'ns_q', 'lr_x', 'lr_g',
              'ns_kc', 'ns_vc', 'ns_ks', 'ns_vs', 'ns_kw', 'ns_vw', 'ml_i', 'ml_f', 'ns_g')
D_PROJ_PAD = 3072
CB_ML_Q, CB_ML_K, CB_ML_V, CB_ML_O, CB_DF_Q, CB_DF_K, CB_DF_V, CB_NS_Q, CB_LR_X, CB_LR_G = range(10)
CB128_CMP, CB128_SLC, CB128_WIN, CB128_SMALL = 20, 21, 22, 23
SMALL_GATES = 8


def _proj_perm():
    off = {}
    o = 0
    for name, w in _SRC_SPLITS:
        off[name] = (o, w)
        o += w
    idx = []
    for name in _DST_ORDER:
        s, w = off[name]
        idx.extend(range(s, s + w))
    n_real = len(idx)
    idx.extend([0] * (D_PROJ_PAD - n_real))
    return jnp.asarray(idx, jnp.int32), n_real


def _cparams(sem):
    return pltpu.CompilerParams(dimension_semantics=sem, vmem_limit_bytes=VMEM_LIMIT_BYTES)


def _act_dtype(rows):
    return bf16 if rows % 16 == 0 else f32


def _nt(a, b):
    return lax.dot_general(a, b, (((1,), (1,)), ((), ())), preferred_element_type=f32)


def _mm_kernel(*refs, n_parts, has_norm, has_res, act, tn):
    it = iter(refs)
    x_refs = [next(it) for _ in range(n_parts)]
    w_ref = next(it)
    g_ref = next(it) if has_norm else None
    r_ref = next(it) if has_res else None
    o_ref = next(it)
    h_ref = next(it)
    if has_norm:
        x = x_refs[0][...].astype(f32)
        y = x * lax.rsqrt(jnp.mean(x * x, axis=-1, keepdims=True) + EPS)
        h_ref[...] = (y * g_ref[...]).astype(bf16)
    elif n_parts > 1 or x_refs[0].dtype != bf16:
        off = 0
        for x_ref in x_refs:
            kp = x_ref.shape[-1]
            h_ref[:, off:off + kp] = x_ref[...].astype(bf16)
            off += kp
    else:
        h_ref = x_refs[0]
    N = o_ref.shape[-1]
    for c0 in range(0, N, tn):
        cs = slice(c0, min(c0 + tn, N))
        acc = jnp.dot(h_ref[...], w_ref[:, cs], preferred_element_type=f32)
        if act == 'relu2':
            acc = jnp.maximum(acc, 0.0)
            acc = acc * acc
        if has_res:
            acc = acc + r_ref[:, cs]
        o_ref[:, cs] = acc.astype(o_ref.dtype)


def matmul(x, w, *, g=None, res=None, act=None, out_dtype=f32, tm=512, tn=512):
    parts = list(x) if isinstance(x, (list, tuple)) else [x]
    M = parts[0].shape[0]
    K, N = w.shape
    assert sum(p.shape[1] for p in parts) == K
    tm = min(tm, M)
    assert M % tm == 0
    has_norm = g is not None
    has_res = res is not None
    assert not (has_norm and len(parts) > 1)
    in_specs = [pl.BlockSpec((tm, p.shape[1]), lambda i: (i, 0)) for p in parts]
    in_specs.append(pl.BlockSpec((K, N), lambda i: (0, 0)))
    args = parts + [w]
    if has_norm:
        in_specs.append(pl.BlockSpec((1, K), lambda i: (0, 0)))
        args.append(g.reshape(1, K).astype(f32))
    if has_res:
        in_specs.append(pl.BlockSpec((tm, N), lambda i: (i, 0)))
        args.append(res)
    return pl.pallas_call(
        functools.partial(_mm_kernel, n_parts=len(parts), has_norm=has_norm, has_res=has_res, act=act, tn=tn),
        out_shape=jax.ShapeDtypeStruct((M, N), out_dtype),
        grid=(M // tm,),
        in_specs=in_specs,
        out_specs=pl.BlockSpec((tm, N), lambda i: (i, 0)),
        scratch_shapes=[pltpu.VMEM((tm, K), bf16)],
        compiler_params=_cparams(("parallel",)),
        name="matmul",
    )(*args)


def _rmsnorm_kernel(x_ref, g_ref, o_ref):
    x = x_ref[...]
    y = x * lax.rsqrt(jnp.mean(x * x, axis=-1, keepdims=True) + EPS)
    o_ref[...] = y * g_ref[...]


def rmsnorm_rows(x, g, tm=1024):
    M, K = x.shape
    tm = min(tm, M)
    return pl.pallas_call(
        _rmsnorm_kernel,
        out_shape=jax.ShapeDtypeStruct((M, K), f32),
        grid=(M // tm,),
        in_specs=[pl.BlockSpec((tm, K), lambda i: (i, 0)), pl.BlockSpec((1, K), lambda i: (0, 0))],
        out_specs=pl.BlockSpec((tm, K), lambda i: (i, 0)),
        compiler_params=_cparams(("parallel",)),
        name="final_norm",
    )(x, g.reshape(1, K))


def _rotate(x, cos, sin, half):
    n = x.shape[-1]
    lane = lax.broadcasted_iota(jnp.int32, x.shape, 1)
    first = (lane & (2 * half - 1)) < half
    partner = jnp.where(first, pltpu.roll(x, n - half, 1), pltpu.roll(x, half, 1))
    return x * cos + partner * sin


def _rope_kernel(dq, dk, nq, sl, wn, ca, sa, cb, sb, cc, sc, odq, odk, onq, osl, own):
    odq[...] = _rotate(dq[...], ca[...], sa[...], DQK // 2)
    odk[...] = _rotate(dk[...], ca[...], sa[...], DQK // 2)
    onq[...] = _rotate(nq[...], cb[...], sb[...], HEAD_DIM // 2)
    osl[...] = _rotate(sl[...], cc[...], sc[...], HEAD_DIM // 2)
    own[...] = _rotate(wn[...], cc[...], sc[...], HEAD_DIM // 2)


def rope_tables(pos):
    posf = pos.astype(f32)[:, None]

    def tab(half, reps):
        inv = ROPE_THETA ** (-jnp.arange(half, dtype=f32) / half)
        ang = posf * inv[None, :]
        c = jnp.cos(ang)
        s = jnp.sin(ang)
        return jnp.tile(jnp.concatenate([c, c], 1), (1, reps)), jnp.tile(jnp.concatenate([-s, s], 1), (1, reps))

    ca, sa = tab(DQK // 2, GROUP_WIDTH // DQK)
    cb, sb = tab(HEAD_DIM // 2, N_HEADS)
    n = pos.shape[0]
    cc = jnp.concatenate([cb[:, :HEAD_DIM], jnp.ones((n, HEAD_DIM), f32)], 1)
    sc = jnp.concatenate([sb[:, :HEAD_DIM], jnp.zeros((n, HEAD_DIM), f32)], 1)
    return ca, sa, cb, sb, cc, sc


def rope_prep(z, tables, tm):
    N = z.shape[0]
    R = tables[0].shape[0]
    tm = min(tm, R)
    assert R % tm == 0 and N % tm == 0
    nr = R // tm
    zs = lambda cb: pl.BlockSpec((tm, 256), lambda i, cb=cb: (i, cb))
    zs128 = lambda cb: pl.BlockSpec((tm, 128), lambda i, cb=cb: (i, cb))
    t256 = pl.BlockSpec((tm, 256), lambda i: (i % nr, 0))
    t128 = pl.BlockSpec((tm, 128), lambda i: (i % nr, 0))
    o256 = pl.BlockSpec((tm, 256), lambda i: (i, 0))
    o128 = pl.BlockSpec((tm, 128), lambda i: (i, 0))
    return pl.pallas_call(
        _rope_kernel,
        out_shape=[jax.ShapeDtypeStruct((N, 256), f32)] * 3 + [jax.ShapeDtypeStruct((N, 128), f32)] * 2,
        grid=(N // tm,),
        in_specs=[zs(CB_DF_Q), zs(CB_DF_K), zs(CB_NS_Q), zs128(CB128_SLC), zs128(CB128_WIN),
                  t256, t256, t256, t256, t128, t128],
        out_specs=[o256, o256, o256, o128, o128],
        compiler_params=_cparams(("parallel",)),
        name="rope_prep",
    )(z, z, z, z, z, *tables)


ROW_CHUNK = 16
LANE_CHUNK = 512


def _softmax_tile(s, s_ref, pb_ref, m_ref, l_ref):
    rows, n = s.shape
    s_ref[0:rows, :] = s
    m_old = m_ref[...]
    m_new = jnp.maximum(m_old, jnp.max(s, axis=0, keepdims=True))
    alpha = jnp.exp(m_old - m_new)
    m_ref[...] = m_new
    for c0 in range(0, n, LANE_CHUNK):
        cs = slice(c0, c0 + LANE_CHUNK)
        mb = jnp.broadcast_to(m_new[:, cs], (ROW_CHUNK, LANE_CHUNK))
        part = jnp.zeros((8, LANE_CHUNK), f32)
        for r in range(0, rows, ROW_CHUNK):
            p = jnp.exp(s_ref[r:r + ROW_CHUNK, cs] - mb)
            pb_ref[r:r + ROW_CHUNK, cs] = p.astype(bf16)
            part = part + (p[0:8, :] + p[8:16, :])
        l_ref[:, cs] = alpha[:, cs] * l_ref[:, cs] + jnp.sum(part, axis=0, keepdims=True)
    return alpha


def _diff_lambda(lam_ref, lam_init):
    lq = lam_ref[...]
    return (jnp.exp(jnp.sum(lq[0:1] * lq[1:2], keepdims=True))
            - jnp.exp(jnp.sum(lq[2:3] * lq[3:4], keepdims=True)) + lam_init)


def _diff_kernel(q_ref, k_ref, v_ref, lam_ref, g_ref, o_ref, qp_ref, m_ref, l_ref, acc_ref, oT_ref,
                 s_ref, pb_ref, *, tq, tk, nk, lam_init):
    qi = pl.program_id(1)
    kj = pl.program_id(2)
    q_lo = qi * tq
    needed = (q_lo + tq - 1) // tk + 1
    nmap = 2 * N_HEADS

    @pl.when(kj == 0)
    def _():
        qT = jnp.transpose(q_ref[0]) * (DQK ** -0.5)
        row = lax.broadcasted_iota(jnp.int32, qT.shape, 0)
        for c in range(nmap):
            blk = jnp.where((row >= DQK * c) & (row < DQK * (c + 1)), qT, 0.0)
            qp_ref[:, c * tq:(c + 1) * tq] = blk.astype(bf16)
        m_ref[...] = jnp.full(m_ref.shape, NEG, f32)
        l_ref[...] = jnp.zeros(l_ref.shape, f32)
        acc_ref[...] = jnp.zeros(acc_ref.shape, f32)

    def step(masked):
        k = k_ref[0].astype(bf16)
        s = jnp.dot(k, qp_ref[...], preferred_element_type=f32)
        if masked:
            kpos = kj * tk + lax.broadcasted_iota(jnp.int32, s.shape, 0)
            qpos = q_lo + (lax.broadcasted_iota(jnp.int32, s.shape, 1) & (tq - 1))
            ok = kpos <= qpos
            s = jnp.where(ok, s, NEG)
        alpha = _softmax_tile(s, s_ref, pb_ref, m_ref, l_ref)
        vT = jnp.transpose(v_ref[0]).astype(bf16)
        for h in range(N_HEADS):
            rs = slice(HEAD_DIM * h, HEAD_DIM * (h + 1))
            cs = slice(2 * tq * h, 2 * tq * (h + 1))
            acc_ref[rs, :] = acc_ref[rs, :] * alpha[:, cs] + jnp.dot(
                vT[rs, :], pb_ref[:, cs], preferred_element_type=f32)

    active = kj < needed
    crosses = kj * tk + tk - 1 > q_lo

    @pl.when(active & crosses)
    def _():
        step(True)

    @pl.when(active & jnp.logical_not(crosses))
    def _():
        step(False)

    @pl.when(kj == nk - 1)
    def _():
        lam = _diff_lambda(lam_ref, lam_init)
        l = l_ref[...]
        for h in range(N_HEADS):
            rs = slice(HEAD_DIM * h, HEAD_DIM * (h + 1))
            c1 = slice(2 * h * tq, (2 * h + 1) * tq)
            c2 = slice((2 * h + 1) * tq, (2 * h + 2) * tq)
            o = acc_ref[rs, 0:tq] / l[:, c1] - lam * (acc_ref[rs, tq:2 * tq] / l[:, c2])
            y = o * lax.rsqrt(jnp.mean(o * o, axis=0, keepdims=True) + EPS)
            oT_ref[rs, :] = (y * g_ref[...]) * (1.0 - lam_init)
        o_ref[0] = jnp.transpose(oT_ref[...]).astype(o_ref.dtype)


def diff_attention(q, k, z3, lam_q, norm_g, *, lam_init, tq, tk):
    B, T, _ = q.shape
    assert T % tq == 0 and T % tk == 0 and (tq & (tq - 1)) == 0 and tq % 128 == 0 and tk % 128 == 0
    nq, nk = T // tq, T // tk
    last = lambda qi: (qi * tq + tq - 1) // tk
    kern = functools.partial(_diff_kernel, tq=tq, tk=tk, nk=nk, lam_init=lam_init)
    return pl.pallas_call(
        kern,
        out_shape=jax.ShapeDtypeStruct((B, T, GROUP_WIDTH), bf16),
        grid=(B, nq, nk),
        in_specs=[
            pl.BlockSpec((1, tq, GROUP_WIDTH), lambda b, qi, kj: (b, qi, 0)),
            pl.BlockSpec((1, tk, GROUP_WIDTH), lambda b, qi, kj: (b, jnp.minimum(kj, last(qi)), 0)),
            pl.BlockSpec((1, tk, GROUP_WIDTH), lambda b, qi, kj: (b, jnp.minimum(kj, last(qi)), CB_DF_V)),
            pl.BlockSpec((4, DQK), lambda b, qi, kj: (0, 0)),
            pl.BlockSpec((HEAD_DIM, 1), lambda b, qi, kj: (0, 0)),
        ],
        out_specs=pl.BlockSpec((1, tq, GROUP_WIDTH), lambda b, qi, kj: (b, qi, 0)),
        scratch_shapes=[pltpu.VMEM((GROUP_WIDTH, 8 * tq), bf16), pltpu.VMEM((1, 8 * tq), f32),
                        pltpu.VMEM((1, 8 * tq), f32), pltpu.VMEM((GROUP_WIDTH, 2 * tq), f32),
                        pltpu.VMEM((GROUP_WIDTH, tq), f32),
                        pltpu.VMEM((tk, 8 * tq), f32), pltpu.VMEM((tk, 8 * tq), bf16)],
        compiler_params=_cparams(("parallel", "parallel", "arbitrary")),
        name="diff_attention",
    )(q, k, z3, lam_q, norm_g.reshape(HEAD_DIM, 1))


def _diff_decode_kernel(pt_ref, q_ref, kn_ref, vn_ref, lam_ref, g_ref, *rest, G, n_steps, Ts, pos0, n_past,
                        lam_init):
    kpages = rest[:G]
    vpages = rest[G:2 * G]
    o_ref, qp_ref, m_ref, l_ref, acc_ref = rest[2 * G:]
    j = pl.program_id(1)
    nmap = 2 * N_HEADS
    R = nmap * Ts

    @pl.when(j == 0)
    def _():
        q = q_ref[0] * (DQK ** -0.5)
        col = lax.broadcasted_iota(jnp.int32, q.shape, 1)
        for c in range(nmap):
            qp_ref[c * Ts:(c + 1) * Ts, :] = jnp.where((col >= DQK * c) & (col < DQK * (c + 1)), q, 0.0)
        s = _nt(qp_ref[...].astype(bf16), kn_ref[0].astype(bf16))
        qpos = pos0 + (lax.broadcasted_iota(jnp.int32, s.shape, 0) & (Ts - 1))
        kpos = n_past + lax.broadcasted_iota(jnp.int32, s.shape, 1)
        ok = kpos <= qpos
        s = jnp.where(ok, s, NEG)
        m = jnp.max(s, axis=1, keepdims=True)
        p = jnp.where(ok, jnp.exp(s - m), 0.0)
        m_ref[...] = m
        l_ref[...] = jnp.sum(p, axis=1, keepdims=True)
        acc_ref[...] = jnp.dot(p.astype(bf16), vn_ref[0].astype(bf16), preferred_element_type=f32)

    kT = jnp.concatenate([kp[...] for kp in kpages], axis=1).astype(bf16)
    vT = jnp.concatenate([vp[...] for vp in vpages], axis=1).astype(bf16)
    s = jnp.dot(qp_ref[...].astype(bf16), kT, preferred_element_type=f32)
    m_old = m_ref[...]
    m_new = jnp.maximum(m_old, jnp.max(s, axis=1, keepdims=True))
    alpha = jnp.exp(m_old - m_new)
    p = jnp.exp(s - m_new)
    l_ref[...] = alpha * l_ref[...] + jnp.sum(p, axis=1, keepdims=True)
    m_ref[...] = m_new
    acc_ref[...] = alpha * acc_ref[...] + _nt(p.astype(bf16), vT)

    @pl.when(j == n_steps - 1)
    def _():
        lam = _diff_lambda(lam_ref, lam_init)
        o_all = acc_ref[...] / l_ref[...]
        ys = []
        for h in range(N_HEADS):
            cs = slice(HEAD_DIM * h, HEAD_DIM * (h + 1))
            o = o_all[2 * h * Ts:(2 * h + 1) * Ts, cs] - lam * o_all[(2 * h + 1) * Ts:(2 * h + 2) * Ts, cs]
            y = o * lax.rsqrt(jnp.mean(o * o, axis=1, keepdims=True) + EPS)
            ys.append((y * g_ref[...]) * (1.0 - lam_init))
        o_ref[0] = jnp.concatenate(ys, axis=1)


def _page_specs(layer, n_pages, G, block):
    def spec(i):
        return pl.BlockSpec((None, None) + block,
                            lambda b, j, pt, i=i: (layer, pt[b * n_pages + j * G + i]) + (0,) * len(block))
    return [spec(i) for i in range(G)]


def diff_decode(page_table, q, k_new, v_new, pool_k, pool_v, lam_q, norm_g, *, layer, pos0, lam_init):
    B, Ts, _ = q.shape
    n_pages = page_table.shape[1]
    page = pool_k.shape[3]
    G = min(PAGES_PER_STEP, n_pages)
    assert n_pages % G == 0 and (Ts & (Ts - 1)) == 0 and Ts % 8 == 0
    n_steps = n_pages // G
    R = 2 * N_HEADS * Ts
    row = lambda w: pl.BlockSpec((1, Ts, w), lambda b, j, pt: (b, 0, 0))
    kern = functools.partial(_diff_decode_kernel, G=G, n_steps=n_steps, Ts=Ts, pos0=pos0,
                             n_past=n_pages * page, lam_init=lam_init)
    return pl.pallas_call(
        kern,
        out_shape=jax.ShapeDtypeStruct((B, Ts, GROUP_WIDTH), f32),
        grid_spec=pltpu.PrefetchScalarGridSpec(
            num_scalar_prefetch=1, grid=(B, n_steps),
            in_specs=[row(256), row(256), row(256),
                      pl.BlockSpec((4, DQK), lambda b, j, pt: (0, 0)),
                      pl.BlockSpec((1, HEAD_DIM), lambda b, j, pt: (0, 0))]
            + _page_specs(layer, n_pages, G, (256, page)) + _page_specs(layer, n_pages, G, (256, page)),
            out_specs=row(256),
            scratch_shapes=[pltpu.VMEM((R, GROUP_WIDTH), f32), pltpu.VMEM((R, 1), f32),
                            pltpu.VMEM((R, 1), f32), pltpu.VMEM((R, GROUP_WIDTH), f32)]),
        compiler_params=_cparams(("parallel", "arbitrary")),
        name="diff_decode",
    )(page_table.reshape(-1), q, k_new, v_new, lam_q, norm_g.reshape(1, HEAD_DIM),
      *([pool_k] * G), *([pool_v] * G))


def _compress_rows(x, pos_ref, w1_ref, w2_ref):
    x = (x + pos_ref[...]).astype(bf16)
    hid = jax.nn.gelu(jnp.dot(x, w1_ref[...], preferred_element_type=f32))
    return jnp.dot(hid.astype(bf16), w2_ref[...], preferred_element_type=f32)


def _compress_kernel(x_ref, pos_ref, w1_ref, w2_ref, o_ref):
    o_ref[...] = _compress_rows(x_ref[...], pos_ref, w1_ref, w2_ref)


def nsa_compress(blocks, pos_flat, w1c, w2c, tm=256):
    R, K = blocks.shape
    tm = min(tm, R)
    assert R % tm == 0
    return pl.pallas_call(
        _compress_kernel,
        out_shape=jax.ShapeDtypeStruct((R, 2 * HEAD_DIM), f32),
        grid=(R // tm,),
        in_specs=[pl.BlockSpec((tm, K), lambda i: (i, 0)), pl.BlockSpec((1, K), lambda i: (0, 0)),
                  pl.BlockSpec((K, 2 * CMP_HIDDEN), lambda i: (0, 0)),
                  pl.BlockSpec((2 * CMP_HIDDEN, 2 * HEAD_DIM), lambda i: (0, 0))],
        out_specs=pl.BlockSpec((tm, 2 * HEAD_DIM), lambda i: (i, 0)),
        compiler_params=_cparams(("parallel",)),
        name="nsa_compress",
    )(blocks, pos_flat, w1c, w2c)


def _compress_paged_kernel(pt_ref, pos_ref, w1_ref, w2_ref, *rest, G, page):
    pages = rest[:G]
    o_ref, xs = rest[G:]
    w = 2 * HEAD_DIM
    for i in range(G):
        xs[page * i:page * (i + 1), :] = jnp.transpose(pages[i][...])
    nb = G * page // CMP_BLOCK
    acc = None
    for t in range(CMP_BLOCK):
        xt = xs[pl.ds(t, nb, stride=CMP_BLOCK), :] + pos_ref[:, w * t:w * (t + 1)]
        part = jnp.dot(xt.astype(bf16), w1_ref[w * t:w * (t + 1), :], preferred_element_type=f32)
        acc = part if acc is None else acc + part
    hid = jax.nn.gelu(acc)
    o_ref[0] = jnp.dot(hid.astype(bf16), w2_ref[...], preferred_element_type=f32)


def nsa_compress_paged(page_table, pool, pos_flat, w1c, w2c, *, layer):
    B, n_pages = page_table.shape
    page = pool.shape[3]
    K = CMP_BLOCK * 2 * HEAD_DIM
    rp = page // CMP_BLOCK
    G = min(PAGES_PER_STEP, n_pages)
    assert n_pages % G == 0 and page % CMP_BLOCK == 0 and (G * rp) % 8 == 0
    const = lambda shape: pl.BlockSpec(shape, lambda b, j, pt: (0, 0))
    return pl.pallas_call(
        functools.partial(_compress_paged_kernel, G=G, page=page),
        out_shape=jax.ShapeDtypeStruct((B, n_pages * rp, 2 * HEAD_DIM), f32),
        grid_spec=pltpu.PrefetchScalarGridSpec(
            num_scalar_prefetch=1, grid=(B, n_pages // G),
            in_specs=[const((1, K)), const((K, 2 * CMP_HIDDEN)), const((2 * CMP_HIDDEN, 2 * HEAD_DIM))]
            + _page_specs(layer, n_pages, G, (2 * HEAD_DIM, page)),
            out_specs=pl.BlockSpec((1, G * rp, 2 * HEAD_DIM), lambda b, j, pt: (b, j, 0)),
            scratch_shapes=[pltpu.VMEM((G * page, 2 * HEAD_DIM), f32)]),
        compiler_params=_cparams(("parallel", "arbitrary")),
        name="nsa_compress_paged",
    )(page_table.reshape(-1), pos_flat, w1c, w2c, *([pool] * G))


def _cmp_block_end(r, half):
    blk = jnp.where(r < half, 2 * r, 2 * (r - half) + 1)
    return (blk + 1) * CMP_BLOCK - 1


def _nsa_kernel(qr_in, qw_in, sm_in, cmp_ref, slc_ref, win_ref,
                o_ref, qr_ref, v_ref, sel_ref, ocmp_ref, ms_ref, ls_ref, accs_ref, mw_ref, lw_ref, accw_ref,
                oT_ref, s_ref, pb_ref, *, tq, tk, tkw, nk, nc, nsp, topk):
    qi = pl.program_id(1)
    kj = pl.program_id(2)
    q_lo = qi * tq
    H = N_HEADS
    d = HEAD_DIM
    scale = d ** -0.5
    needed = (q_lo + tq - 1) // tk + 1
    w_lo = jnp.maximum(q_lo - (WINDOW - 1), 0) // tkw
    w_hi = (q_lo + tq - 1) // tkw
    half = nc // 2

    def heads_on_lanes(xT):
        return jnp.concatenate([xT[d * h:d * (h + 1), :] for h in range(H)], axis=1)

    def qpos_row(n):
        return q_lo + (lax.broadcasted_iota(jnp.int32, (1, n), 1) & (tq - 1))

    @pl.when(kj == 0)
    def _():
        zeros = jnp.zeros((d, H * tq), f32)
        qrT = jnp.transpose(qr_in[0]) * scale
        qwT = jnp.transpose(qw_in[0]) * scale
        qr_ref[...] = jnp.concatenate([heads_on_lanes(qrT), zeros], 0).astype(bf16)
        qw = jnp.concatenate([heads_on_lanes(qwT), zeros], 0).astype(bf16)
        cmp = cmp_ref[0]
        s = jnp.dot(cmp.astype(bf16), qw, preferred_element_type=f32)
        c_end = _cmp_block_end(lax.broadcasted_iota(jnp.int32, (nc, 1), 0), half)
        c_ok = c_end <= qpos_row(H * tq)
        s = jnp.where(c_ok, s, NEG)
        p = jnp.where(c_ok, jnp.exp(s - jnp.max(s, axis=0, keepdims=True)), 0.0)
        p = p / jnp.maximum(jnp.sum(p, axis=0, keepdims=True), TINY)
        cmpT = jnp.transpose(cmp)
        ocmp_ref[...] = jnp.dot(cmpT[d:2 * d, :].astype(bf16), p.astype(bf16), preferred_element_type=f32)
        imp = p[:, 0:tq]
        for h in range(1, H):
            imp = imp + p[:, h * tq:(h + 1) * tq]
        imp = imp[:half, :] + imp[half:, :]
        if nsp > half:
            imp = jnp.concatenate([imp, jnp.zeros((nsp - half, tq), f32)], 0)
        sb = lax.broadcasted_iota(jnp.int32, (nsp, tq), 0)
        cur = qpos_row(tq) // SEL_BLOCK
        v = jnp.where((sb == cur) | (sb == 0), FORCE_SCORE, imp)
        v = jnp.where(sb > cur, NEG, v)
        v_ref[...] = v

        def rank(i, cnt):
            vi = v_ref[pl.ds(i, 1), :]
            ahead = (vi > v) | ((vi == v) & (i < sb))
            return cnt + jnp.where(ahead, 1.0, 0.0)

        cnt = lax.fori_loop(0, nsp, rank, jnp.zeros((nsp, tq), f32))
        sel_ref[...] = jnp.where(cnt < topk, 1.0, 0.0)
        for m_r, l_r, a_r in ((ms_ref, ls_ref, accs_ref), (mw_ref, lw_ref, accw_ref)):
            m_r[...] = jnp.full(m_r.shape, NEG, f32)
            l_r[...] = jnp.zeros(l_r.shape, f32)
            a_r[...] = jnp.zeros(a_r.shape, f32)

    def flash(kv_ref, ok, m_r, l_r, a_r):
        kv = kv_ref[0]
        s = jnp.dot(kv.astype(bf16), qr_ref[...], preferred_element_type=f32)
        ok4 = jnp.concatenate([jnp.where(ok, 1.0, 0.0)] * H, axis=1) > 0.5
        s = jnp.where(ok4, s, NEG)
        alpha = _softmax_tile(s, s_ref, pb_ref, m_r, l_r)
        vT = jnp.transpose(kv)[d:2 * d, :].astype(bf16)
        a_r[...] = a_r[...] * alpha + jnp.dot(vT, pb_ref[0:kv.shape[0], :], preferred_element_type=f32)

    @pl.when(kj < needed)
    def _():
        kpos = kj * tk + lax.broadcasted_iota(jnp.int32, (tk, tq), 0)
        nb = tk // SEL_BLOCK
        rows = [jnp.broadcast_to(sel_ref[pl.ds(kj * nb + c, 1), :], (SEL_BLOCK, tq)) for c in range(nb)]
        chosen = jnp.concatenate(rows, axis=0) > 0.5
        ok = chosen & (kpos <= qpos_row(tq))
        flash(slc_ref, ok, ms_ref, ls_ref, accs_ref)

    @pl.when(kj <= w_hi - w_lo)
    def _():
        wpos = (w_lo + kj) * tkw + lax.broadcasted_iota(jnp.int32, (tkw, tq), 0)
        qp = qpos_row(tq)
        ok = (wpos <= qp) & (qp - wpos < WINDOW) & (wpos >= 0)
        flash(win_ref, ok, mw_ref, lw_ref, accw_ref)

    @pl.when(kj == nk - 1)
    def _():
        smT = jnp.transpose(sm_in[0])
        g = jax.nn.sigmoid(smT[SMALL_GATES:SMALL_GATES + 3 * H, :])
        o_slc = accs_ref[...] / ls_ref[...]
        o_win = accw_ref[...] / lw_ref[...]
        o_cmp = ocmp_ref[...]
        for h in range(H):
            cs = slice(h * tq, (h + 1) * tq)
            oT_ref[d * h:d * (h + 1), :] = (g[3 * h:3 * h + 1] * o_cmp[:, cs]
                                            + g[3 * h + 1:3 * h + 2] * o_slc[:, cs]
                                            + g[3 * h + 2:3 * h + 3] * o_win[:, cs])
        o_ref[0] = jnp.transpose(oT_ref[...]).astype(o_ref.dtype)


def nsa_attention(qr, z3, cmp, slc, win, *, tq, tk, tkw):
    B, T, _ = qr.shape
    nc = cmp.shape[1]
    assert T % tq == 0 and T % tk == 0 and T % tkw == 0 and tk % SEL_BLOCK == 0 and nc % 2 == 0
    assert tq % 128 == 0 and tk % 128 == 0 and tkw % 128 == 0 and nc % 8 == 0 and (tq & (tq - 1)) == 0
    nq, nk = T // tq, T // tk
    nsp = T // SEL_BLOCK
    assert nsp >= nc // 2 and nsp % 8 == 0
    last = lambda qi: (qi * tq + tq - 1) // tk
    w_lo = lambda qi: jnp.maximum(qi * tq - (WINDOW - 1), 0) // tkw
    w_hi = lambda qi: (qi * tq + tq - 1) // tkw
    for qi in range(nq):
        lo = max(qi * tq - (WINDOW - 1), 0) // tkw
        assert (qi * tq + tq - 1) // tkw - lo + 1 <= nk
    w_idx = lambda qi, kj: jnp.minimum(w_lo(qi) + kj, w_hi(qi))
    kern = functools.partial(_nsa_kernel, tq=tq, tk=tk, tkw=tkw, nk=nk, nc=nc, nsp=nsp,
                             topk=min(SEL_TOPK, nsp))
    stat = pltpu.VMEM((1, N_HEADS * tq), f32)
    acc = pltpu.VMEM((HEAD_DIM, N_HEADS * tq), f32)
    return pl.pallas_call(
        kern,
        out_shape=jax.ShapeDtypeStruct((B, T, GROUP_WIDTH), bf16),
        grid=(B, nq, nk),
        in_specs=[
            pl.BlockSpec((1, tq, GROUP_WIDTH), lambda b, qi, kj: (b, qi, 0)),
            pl.BlockSpec((1, tq, GROUP_WIDTH), lambda b, qi, kj: (b, qi, CB_NS_Q)),
            pl.BlockSpec((1, tq, 128), lambda b, qi, kj: (b, qi, CB128_SMALL)),
            pl.BlockSpec((1, nc, 128), lambda b, qi, kj: (b, 0, 0)),
            pl.BlockSpec((1, tk, 128), lambda b, qi, kj: (b, jnp.minimum(kj, last(qi)), 0)),
            pl.BlockSpec((1, tkw, 128), lambda b, qi, kj: (b, w_idx(qi, kj), 0)),
        ],
        out_specs=pl.BlockSpec((1, tq, GROUP_WIDTH), lambda b, qi, kj: (b, qi, 0)),
        scratch_shapes=[pltpu.VMEM((128, N_HEADS * tq), bf16), pltpu.VMEM((nsp, tq), f32),
                        pltpu.VMEM((nsp, tq), f32), acc, stat, stat, acc, stat, stat, acc,
                        pltpu.VMEM((GROUP_WIDTH, tq), f32),
                        pltpu.VMEM((max(tk, tkw), N_HEADS * tq), f32),
                        pltpu.VMEM((max(tk, tkw), N_HEADS * tq), bf16)],
        compiler_params=_cparams(("parallel", "parallel", "arbitrary")),
        name="nsa_attention",
    )(qr, z3, z3, cmp, slc, win)


def _nsa_decode_kernel(pt_ref, qr_in, qw_in, sm_in, cmp_ref, sn_ref, wb_ref, wn_ref, *rest,
                       G, n_steps, Ts, pos0, n_past, wpos0, nc, nsl, topk, page):
    pages = rest[:G]
    o_ref, qr_ref, selst_ref, ocmp_ref, owin_ref, m_ref, l_ref, acc_ref = rest[G:]
    j = pl.program_id(1)
    H, d = N_HEADS, HEAD_DIM
    scale = d ** -0.5
    R = H * Ts
    half = nc // 2
    bps = G * page // SEL_BLOCK

    def rows_by_head(x):
        z = jnp.zeros((Ts, d), f32)
        return jnp.concatenate([jnp.concatenate([x[:, d * h:d * (h + 1)], z], axis=1) for h in range(H)], axis=0)

    def tile_heads(x):
        return jnp.concatenate([x] * H, axis=0)

    @pl.when(j == 0)
    def _():
        qr = rows_by_head(qr_in[0] * scale)
        qr_ref[...] = qr
        qrb = qr.astype(bf16)
        qwb = rows_by_head(qw_in[0] * scale).astype(bf16)
        qpos = pos0 + (lax.broadcasted_iota(jnp.int32, (R, 1), 0) & (Ts - 1))
        cmpb = cmp_ref[0].astype(bf16)
        s = _nt(qwb, cmpb)
        c_ok = _cmp_block_end(lax.broadcasted_iota(jnp.int32, (1, nc), 1), half) <= qpos
        s = jnp.where(c_ok, s, NEG)
        p = jnp.where(c_ok, jnp.exp(s - jnp.max(s, axis=1, keepdims=True)), 0.0)
        p = p / jnp.maximum(jnp.sum(p, axis=1, keepdims=True), TINY)
        ocmp_ref[...] = jnp.dot(p.astype(bf16), cmpb, preferred_element_type=f32)
        imp = p[0:Ts, :]
        for h in range(1, H):
            imp = imp + p[h * Ts:(h + 1) * Ts, :]
        imp = imp[:, :half] + imp[:, half:]
        imp = jnp.concatenate([imp, jnp.zeros((Ts, nsl - half), f32)], axis=1)
        lane = lax.broadcasted_iota(jnp.int32, (Ts, nsl), 1)
        cur = (pos0 + lax.broadcasted_iota(jnp.int32, (Ts, 1), 0)) // SEL_BLOCK
        v = jnp.where((lane == cur) | (lane == 0), FORCE_SCORE, imp)
        v = jnp.where(lane > cur, NEG, v)
        lanef = lane.astype(f32)
        sel = jnp.zeros((Ts, nsl), f32)
        for _ in range(topk):
            mx = jnp.max(v, axis=1, keepdims=True)
            first = jnp.min(jnp.where(v == mx, lanef, float(nsl)), axis=1, keepdims=True)
            pick = lanef == first
            sel = jnp.where(pick, 1.0, sel)
            v = jnp.where(pick, -jnp.inf, v)
        for st in range(n_steps):
            selst_ref[st] = sel[:, bps * st:bps * (st + 1)]
        blk_new = n_past // SEL_BLOCK
        sel_new = tile_heads(sel[:, blk_new:blk_new + 1]) > 0.5
        snb = sn_ref[0].astype(bf16)
        s = _nt(qrb, snb)
        kpos = n_past + lax.broadcasted_iota(jnp.int32, (1, Ts), 1)
        ok = sel_new & (kpos <= qpos)
        s = jnp.where(ok, s, NEG)
        m = jnp.max(s, axis=1, keepdims=True)
        p = jnp.where(ok, jnp.exp(s - m), 0.0)
        m_ref[...] = m
        l_ref[...] = jnp.sum(p, axis=1, keepdims=True)
        acc_ref[...] = jnp.dot(p.astype(bf16), snb, preferred_element_type=f32)
        wbT = wb_ref[...].astype(bf16)
        wnb = wn_ref[0].astype(bf16)
        lw = wbT.shape[1]
        s1 = jnp.dot(qrb, wbT, preferred_element_type=f32)
        s2 = _nt(qrb, wnb)
        wp1 = wpos0 + lax.broadcasted_iota(jnp.int32, (1, lw), 1)
        wp2 = wpos0 + lw + lax.broadcasted_iota(jnp.int32, (1, Ts), 1)
        ok1 = (wp1 <= qpos) & (qpos - wp1 < WINDOW) & (wp1 >= 0)
        ok2 = (wp2 <= qpos) & (qpos - wp2 < WINDOW) & (wp2 >= 0)
        s1 = jnp.where(ok1, s1, NEG)
        s2 = jnp.where(ok2, s2, NEG)
        mw = jnp.maximum(jnp.max(s1, axis=1, keepdims=True), jnp.max(s2, axis=1, keepdims=True))
        p1 = jnp.where(ok1, jnp.exp(s1 - mw), 0.0)
        p2 = jnp.where(ok2, jnp.exp(s2 - mw), 0.0)
        lsum = jnp.sum(p1, axis=1, keepdims=True) + jnp.sum(p2, axis=1, keepdims=True)
        owin_ref[...] = (_nt(p1.astype(bf16), wbT)
                         + jnp.dot(p2.astype(bf16), wnb, preferred_element_type=f32)) / lsum

    kvT = jnp.concatenate([pg[...] for pg in pages], axis=1).astype(bf16)
    n = kvT.shape[1]
    s = jnp.dot(qr_ref[...].astype(bf16), kvT, preferred_element_type=f32)
    expand = (lax.broadcasted_iota(jnp.int32, (bps, n), 1) // SEL_BLOCK
              == lax.broadcasted_iota(jnp.int32, (bps, n), 0))
    chosen = jnp.dot(selst_ref[j].astype(bf16), jnp.where(expand, 1.0, 0.0).astype(bf16),
                     preferred_element_type=f32)
    ok = tile_heads(chosen) > 0.5
    s = jnp.where(ok, s, NEG)
    m_old = m_ref[...]
    m_new = jnp.maximum(m_old, jnp.max(s, axis=1, keepdims=True))
    alpha = jnp.exp(m_old - m_new)
    p = jnp.where(ok, jnp.exp(s - m_new), 0.0)
    l_ref[...] = alpha * l_ref[...] + jnp.sum(p, axis=1, keepdims=True)
    m_ref[...] = m_new
    acc_ref[...] = alpha * acc_ref[...] + _nt(p.astype(bf16), kvT)

    @pl.when(j == n_steps - 1)
    def _():
        g = jax.nn.sigmoid(sm_in[0][:, SMALL_GATES:SMALL_GATES + 3 * H])
        o_slc = acc_ref[...] / l_ref[...]
        outs = []
        for h in range(H):
            rs = slice(h * Ts, (h + 1) * Ts)
            outs.append(g[:, 3 * h:3 * h + 1] * ocmp_ref[rs, d:2 * d]
                        + g[:, 3 * h + 1:3 * h + 2] * o_slc[rs, d:2 * d]
                        + g[:, 3 * h + 2:3 * h + 3] * owin_ref[rs, d:2 * d])
        o_ref[0] = jnp.concatenate(outs, axis=1)


def nsa_decode(page_table, qr, z3, cmp, slc_new, win_cache, win_new, pool_slc, *, layer, pos0):
    B, Ts, _ = qr.shape
    n_pages = page_table.shape[1]
    page = pool_slc.shape[3]
    n_past = n_pages * page
    lw = win_cache.shape[3]
    nc = cmp.shape[1]
    G = min(PAGES_PER_STEP, n_pages)
    n_steps = n_pages // G
    n_sel = -(-(n_past + Ts) // SEL_BLOCK)
    nsl = -(-n_sel // 128) * 128
    assert n_pages % G == 0 and (Ts & (Ts - 1)) == 0 and Ts % 8 == 0 and page % SEL_BLOCK == 0
    assert pos0 == n_past and n_past % SEL_BLOCK + Ts <= SEL_BLOCK and nc % 2 == 0 and nc // 2 <= nsl
    R = N_HEADS * Ts
    row = lambda w, cb=0: pl.BlockSpec((1, Ts, w), lambda b, j, pt, cb=cb: (b, 0, cb))
    kern = functools.partial(_nsa_decode_kernel, G=G, n_steps=n_steps, Ts=Ts, pos0=pos0, n_past=n_past,
                             wpos0=n_past - lw, nc=nc, nsl=nsl, topk=min(SEL_TOPK, n_sel), page=page)
    buf = lambda: pltpu.VMEM((R, 128), f32)
    stat = lambda: pltpu.VMEM((R, 1), f32)
    return pl.pallas_call(
        kern,
        out_shape=jax.ShapeDtypeStruct((B, Ts, GROUP_WIDTH), f32),
        grid_spec=pltpu.PrefetchScalarGridSpec(
            num_scalar_prefetch=1, grid=(B, n_steps),
            in_specs=[row(256), row(256, CB_NS_Q), row(128, CB128_SMALL),
                      pl.BlockSpec((1, nc, 128), lambda b, j, pt: (b, 0, 0)),
                      row(128),
                      pl.BlockSpec((None, None, 128, lw), lambda b, j, pt: (layer, b, 0, 0)),
                      row(128)]
            + _page_specs(layer, n_pages, G, (128, page)),
            out_specs=row(256),
            scratch_shapes=[buf(), pltpu.VMEM((n_steps, Ts, G * page // SEL_BLOCK), f32), buf(), buf(),
                            stat(), stat(), buf()]),
        compiler_params=_cparams(("parallel", "arbitrary")),
        name="nsa_decode",
    )(page_table.reshape(-1), qr, z3, z3, cmp, slc_new, win_cache, win_new, *([pool_slc] * G))


def _mlstm_kernel(q_ref, k_ref, v_ref, og_ref, gc_ref, gr_ref, bias_ref, ng_ref, c0_ref, n0_ref, m0_ref,
                  o_ref, cout_ref, nout_ref, mout_ref, c_s, n_s, m_s, *, L):
    ci = pl.program_id(1)
    H, d = N_HEADS, HEAD_DIM

    @pl.when(ci == 0)
    def _():
        c_s[...] = c0_ref[0]
        n_s[...] = n0_ref[0]
        m_s[...] = m0_ref[0]

    q = q_ref[...]
    k = k_ref[...] * (d ** -0.5)
    v = v_ref[...]
    gc = gc_ref[...]
    gr = gr_ref[0]
    bias = bias_ref[...]
    t_i = lax.broadcasted_iota(jnp.int32, (L, L), 0)
    s_i = lax.broadcasted_iota(jnp.int32, (L, L), 1)
    causal = s_i <= t_i
    causal_T = t_i <= s_i
    outs = []
    for h in range(H):
        bi = bias[0:1, h:h + 1]
        bf_ = bias[1:2, h:h + 1]
        ig_c = gc[:, h:h + 1] + bi
        lf_c = jax.nn.log_sigmoid(gc[:, H + h:H + h + 1] + bf_)
        ig_r = gr[h:h + 1, :] + bi
        lf_r = jax.nn.log_sigmoid(gr[H + h:H + h + 1, :] + bf_)
        b_c = jnp.sum(jnp.where(causal, lf_r, 0.0), axis=1, keepdims=True)
        b_r = jnp.sum(jnp.where(causal_T, lf_c, 0.0), axis=0, keepdims=True)
        m_prev = m_s[h:h + 1, 0:1]
        D = jnp.where(causal, b_c - b_r + ig_r, -jnp.inf)
        m_t = jnp.maximum(b_c + m_prev, jnp.max(D, axis=1, keepdims=True))
        inter = jnp.exp(b_c + m_prev - m_t)
        qh = q[:, d * h:d * (h + 1)]
        kh = k[:, d * h:d * (h + 1)]
        vh = v[:, d * h:d * (h + 1)]
        qb = qh.astype(bf16)
        S = _nt(qb, kh.astype(bf16))
        Sw = jnp.exp(D - m_t) * S
        C = c_s[h]
        n_row = n_s[h:h + 1, :]
        num = (jnp.dot(Sw.astype(bf16), vh.astype(bf16), preferred_element_type=f32)
               + inter * jnp.dot(qb, C.astype(bf16), preferred_element_type=f32))
        qn = jnp.sum(qh * n_row, axis=1, keepdims=True)
        den = jnp.sum(Sw, axis=1, keepdims=True) + inter * qn
        hh = num / jnp.maximum(jnp.abs(den), jnp.exp(-m_t))
        m_new = m_t[L - 1:L, :]
        b_last = b_c[L - 1:L, :]
        wl = jnp.exp(b_last - b_c + ig_c - m_new)
        decay = jnp.exp(b_last + m_prev - m_new)
        kw = kh * wl
        c_s[h] = decay * C + lax.dot_general(kw.astype(bf16), vh.astype(bf16), (((0,), (0,)), ((), ())),
                                             preferred_element_type=f32)
        n_s[h:h + 1, :] = decay * n_row + jnp.sum(kw, axis=0, keepdims=True)
        m_s[h:h + 1, :] = jnp.broadcast_to(m_new, (1, 128))
        mu = jnp.mean(hh, axis=1, keepdims=True)
        var = jnp.mean(jnp.square(hh - mu), axis=1, keepdims=True)
        outs.append((hh - mu) * lax.rsqrt(var + EPS))
    hcat = jnp.concatenate(outs, axis=1) * ng_ref[...]
    o_ref[...] = (hcat * jax.nn.sigmoid(og_ref[...])).astype(o_ref.dtype)
    cout_ref[0] = c_s[...]
    nout_ref[0] = n_s[...]
    mout_ref[0] = m_s[...]


def mlstm(z, gates_T, gate_b, norm_g, C0, n0, m0, *, B, T, L):
    nchunk = T // L
    m0p = jnp.broadcast_to(m0[:, :, None], (B, N_HEADS, 128))
    row = lambda cb: pl.BlockSpec((L, 256), lambda b, c, cb=cb: (b * nchunk + c, cb))
    st = lambda *shape: pl.BlockSpec((1,) + shape, lambda b, c: (b,) + (0,) * len(shape))
    out, C, n, m = pl.pallas_call(
        functools.partial(_mlstm_kernel, L=L),
        out_shape=[jax.ShapeDtypeStruct((B * T, GROUP_WIDTH), _act_dtype(L)),
                   jax.ShapeDtypeStruct((B, N_HEADS, HEAD_DIM, HEAD_DIM), f32),
                   jax.ShapeDtypeStruct((B, N_HEADS, HEAD_DIM), f32),
                   jax.ShapeDtypeStruct((B, N_HEADS, 128), f32)],
        grid=(B, nchunk),
        in_specs=[row(CB_ML_Q), row(CB_ML_K), row(CB_ML_V), row(CB_ML_O),
                  pl.BlockSpec((L, 128), lambda b, c: (b * nchunk + c, CB128_SMALL)),
                  pl.BlockSpec((1, 8, L), lambda b, c: (b * nchunk + c, 0, 0)),
                  pl.BlockSpec((2, N_HEADS), lambda b, c: (0, 0)),
                  pl.BlockSpec((1, GROUP_WIDTH), lambda b, c: (0, 0)),
                  st(N_HEADS, HEAD_DIM, HEAD_DIM), st(N_HEADS, HEAD_DIM), st(N_HEADS, 128)],
        out_specs=[pl.BlockSpec((L, GROUP_WIDTH), lambda b, c: (b * nchunk + c, 0)),
                   st(N_HEADS, HEAD_DIM, HEAD_DIM), st(N_HEADS, HEAD_DIM), st(N_HEADS, 128)],
        scratch_shapes=[pltpu.VMEM((N_HEADS, HEAD_DIM, HEAD_DIM), f32), pltpu.VMEM((N_HEADS, HEAD_DIM), f32),
                        pltpu.VMEM((N_HEADS, 128), f32)],
        compiler_params=_cparams(("parallel", "arbitrary")),
        name="mlstm",
    )(z, z, z, z, z, gates_T, gate_b, norm_g.reshape(1, GROUP_WIDTH), C0, n0, m0p)
    return out, C, n, m[:, :, 0]


def _rglru_kernel(x_ref, gate_ref, cw_ref, cb_ref, wa_ref, ba_ref, wx_ref, bx_ref, lam_ref, h0_ref, buf0_ref,
                  y_ref, hout_ref, bufout_ref, xbuf, a_s, u_s, h_s, hs_s, *, tm):
    ti = pl.program_id(0)
    Bb = x_ref.shape[0]
    W = GROUP_WIDTH

    @pl.when(ti == 0)
    def _():
        xbuf[:, 0:8, :] = buf0_ref[...]
        h_s[...] = h0_ref[...]

    xbuf[:, 8:8 + tm, :] = x_ref[...]
    cw = cw_ref[...]
    xc = cb_ref[...] + cw[0:1, :] * xbuf[:, 5:5 + tm, :]
    for j in range(1, CONV_W):
        xc = xc + cw[j:j + 1, :] * xbuf[:, 5 + j:5 + j + tm, :]
    flat = xc.reshape(Bb * tm, W).astype(bf16)
    r = jax.nn.sigmoid(jnp.dot(flat, wa_ref[...], preferred_element_type=f32) + ba_ref[...])
    i = jax.nn.sigmoid(jnp.dot(flat, wx_ref[...], preferred_element_type=f32) + bx_ref[...])
    lam = lam_ref[...]
    softplus = jnp.maximum(-lam, 0.0) + jnp.log1p(jnp.exp(-jnp.abs(lam)))
    log_a = (-LRU_C * r) * softplus
    a = jnp.exp(log_a)
    u = jnp.sqrt(-jnp.tanh(log_a) * (a * a + 1.0)) * (i * xc.reshape(Bb * tm, W))
    a_s[...] = a.reshape(Bb, tm, W)
    u_s[...] = u.reshape(Bb, tm, W)

    def body(t, h):
        h = a_s[:, pl.ds(t, 1), :] * h + u_s[:, pl.ds(t, 1), :]
        hs_s[:, pl.ds(t, 1), :] = h
        return h

    h_last = lax.fori_loop(0, tm, body, h_s[...])
    h_s[...] = h_last
    y_ref[...] = (hs_s[...] * jax.nn.gelu(gate_ref[...])).astype(y_ref.dtype)
    hout_ref[...] = h_last
    bufout_ref[...] = xbuf[:, tm:tm + 8, :]
    xbuf[:, 0:8, :] = xbuf[:, tm:tm + 8, :]


def rglru(z3, conv_w, conv_b, wa_bd, ba, wx_bd, bx, lam, h0, buf0, *, tm):
    B, T, _ = z3.shape
    tm = min(tm, T)
    assert T % tm == 0 and tm % 8 == 0
    W = GROUP_WIDTH
    buf8 = jnp.concatenate([jnp.zeros((B, 8 - (CONV_W - 1), W), f32), buf0], 1)
    vec = lambda: pl.BlockSpec((1, W), lambda i: (0, 0))
    mat = lambda: pl.BlockSpec((W, W), lambda i: (0, 0))
    y, h, buf = pl.pallas_call(
        functools.partial(_rglru_kernel, tm=tm),
        out_shape=[jax.ShapeDtypeStruct((B, T, W), _act_dtype(tm)), jax.ShapeDtypeStruct((B, 1, W), f32),
                   jax.ShapeDtypeStruct((B, 8, W), f32)],
        grid=(T // tm,),
        in_specs=[pl.BlockSpec((B, tm, W), lambda i: (0, i, CB_LR_X)),
                  pl.BlockSpec((B, tm, W), lambda i: (0, i, CB_LR_G)),
                  pl.BlockSpec((CONV_W, W), lambda i: (0, 0)), vec(), mat(), vec(), mat(), vec(), vec(),
                  pl.BlockSpec((B, 1, W), lambda i: (0, 0, 0)), pl.BlockSpec((B, 8, W), lambda i: (0, 0, 0))],
        out_specs=[pl.BlockSpec((B, tm, W), lambda i: (0, i, 0)),
                   pl.BlockSpec((B, 1, W), lambda i: (0, 0, 0)), pl.BlockSpec((B, 8, W), lambda i: (0, 0, 0))],
        scratch_shapes=[pltpu.VMEM((B, tm + 8, W), f32), pltpu.VMEM((B, tm, W), f32), pltpu.VMEM((B, tm, W), f32),
                        pltpu.VMEM((B, 1, W), f32), pltpu.VMEM((B, tm, W), f32)],
        compiler_params=_cparams(("arbitrary",)),
        name="rglru",
    )(z3, z3, conv_w, conv_b.reshape(1, W), wa_bd, ba.reshape(1, W), wx_bd, bx.reshape(1, W),
      lam.reshape(1, W), h0.reshape(B, 1, W), buf8)
    return y, h[:, 0], buf[:, 8 - (CONV_W - 1):]


def _mem_attn_kernel(q_ref, k_ref, v_ref, o_ref):
    q = q_ref[0]
    lead = (0,) * (len(k_ref.shape) - 2)
    k = k_ref[lead]
    v = v_ref[lead]
    dh = MEM_HEAD_DIM
    for h in range(MEM_HEADS):
        cs = slice(dh * h, dh * (h + 1))
        s = _nt(q[:, cs].astype(bf16), k[:, cs].astype(bf16)) * (dh ** -0.5)
        e = jnp.exp(s - jnp.max(s, axis=-1, keepdims=True))
        p = e / jnp.sum(e, axis=-1, keepdims=True)
        o_ref[0, :, cs] = jnp.dot(p.astype(bf16), v[:, cs].astype(bf16),
                                  preferred_element_type=f32).astype(o_ref.dtype)


def mem_attention(q, k, v, *, layer=None, tq=256):
    B, T, D = q.shape
    tq = min(tq, T)
    M = k.shape[-2]
    if layer is None:
        kv_spec = pl.BlockSpec((1, M, D), lambda b, i: (b, 0, 0))
    else:
        kv_spec = pl.BlockSpec((1, 1, M, D), lambda b, i: (layer, b, 0, 0))
    return pl.pallas_call(
        _mem_attn_kernel,
        out_shape=jax.ShapeDtypeStruct((B, T, D), _act_dtype(tq)),
        grid=(B, T // tq),
        in_specs=[pl.BlockSpec((1, tq, D), lambda b, i: (b, i, 0)), kv_spec, kv_spec],
        out_specs=pl.BlockSpec((1, tq, D), lambda b, i: (b, i, 0)),
        compiler_params=_cparams(("parallel", "parallel")),
        name="mem_attention",
    )(q, k, v)


def _even_odd(cmp):
    B, nc, w = cmp.shape
    return jnp.swapaxes(cmp.reshape(B, nc // 2, 2, w), 1, 2).reshape(B, nc, w)


def _layer(x, lw, lam_init, tables, *, B, T, mem, ml_state, lr_state, paged, cfg):
    N = B * T
    z = matmul(x, lw['w_in'], g=lw['g_mix'], tm=512)
    z3 = z.reshape(B, T, D_PROJ_PAD)
    dfq, dfk, nsq, slc_new, win_new = rope_prep(z, tables, cfg['rope_tm'])
    df_v = z[:, 256 * CB_DF_V:256 * (CB_DF_V + 1)]
    cmp_new = z[:, 128 * CB128_CMP:128 * (CB128_CMP + 1)]
    small = z[:, 128 * CB128_SMALL:128 * CB128_SMALL + SMALL_GATES]
    r3 = lambda a: a.reshape(B, T, a.shape[-1])

    L = cfg['ml_chunk']
    gates_T = jnp.swapaxes(small.reshape(N // L, L, SMALL_GATES), 1, 2)
    o_ml, ml_C, ml_n, ml_m = mlstm(z, gates_T, lw['ml_gate_b'], lw['ml_norm_g'], *ml_state, B=B, T=T, L=L)

    if paged is None:
        o_df = diff_attention(r3(dfq), r3(dfk), z3, lw['df_lam'], lw['df_norm_g'], lam_init=lam_init,
                              tq=cfg['df_tq'], tk=cfg['df_tk'])
        comp = nsa_compress(cmp_new.reshape(N // CMP_BLOCK, CMP_BLOCK * 128), lw['nsa_pos_flat'],
                            lw['nsa_w1c'], lw['nsa_w2c']).reshape(B, T // CMP_BLOCK, 128)
        o_ns = nsa_attention(r3(nsq), z3, _even_odd(comp), r3(slc_new), r3(win_new),
                             tq=cfg['ns_tq'], tk=cfg['ns_tk'], tkw=cfg['ns_tkw'])
        win_state = r3(win_new)[:, T - min(WINDOW, T):]
    else:
        pt, l = paged['page_table'], paged['layer']
        n_past = pt.shape[1] * paged['df_k'].shape[3]
        assert (n_past + T) // CMP_BLOCK == n_past // CMP_BLOCK
        o_df = diff_decode(pt, r3(dfq), r3(dfk), r3(df_v), paged['df_k'], paged['df_v'], lw['df_lam'],
                           lw['df_norm_g'], layer=l, pos0=n_past, lam_init=lam_init)
        comp = nsa_compress_paged(pt, paged['nsa_cmp'], lw['nsa_pos_flat'], lw['nsa_w1c'], lw['nsa_w2c'], layer=l)
        o_ns = nsa_decode(pt, r3(nsq), z3, _even_odd(comp), r3(slc_new), paged['nsa_win'], r3(win_new),
                          paged['nsa_slc'], layer=l, pos0=n_past)
        win_cat = jnp.concatenate([paged['nsa_win'][l], jnp.swapaxes(r3(win_new), 1, 2)], 2)
        win_state = jnp.swapaxes(win_cat[:, :, win_cat.shape[2] - min(WINDOW, win_cat.shape[2]):], 1, 2)

    o_lr, lr_h, lr_buf = rglru(z3, lw['lru_conv_w'], lw['lru_conv_b'], lw['lru_wa_bd'],
                               lw['lru_ba'], lw['lru_wx_bd'], lw['lru_bx'], lw['lru_lambda'], *lr_state,
                               tm=cfg['lru_tm'])

    x = matmul([o_ml, o_df.reshape(N, 256), o_ns.reshape(N, 256), o_lr.reshape(N, 256)], lw['w_out'], res=x,
               tm=1024)
    q = matmul(x, lw['w_mq'], g=lw['g_mem_q'], tm=1024, out_dtype=_act_dtype(cfg['mem_tq']))
    if isinstance(mem, tuple):
        att = mem_attention(q.reshape(B, T, D_MODEL), mem[0], mem[1], tq=cfg['mem_tq'])
    else:
        att = mem_attention(q.reshape(B, T, D_MODEL), mem['k'], mem['v'], layer=mem['layer'], tq=cfg['mem_tq'])
    x = matmul(att.reshape(N, D_MODEL), lw['w_mo'], res=x, tm=1024)
    u = matmul(x, lw['w_up'], g=lw['g_mlp'], act='relu2', out_dtype=bf16, tm=512)
    x = matmul(u, lw['w_down'], res=x, tm=512)
    new = dict(df_k=dfk.reshape(B, T, N_HEADS, HEAD_DIM), df_v=df_v.reshape(B, T, N_HEADS, HEAD_DIM),
               nsa_cmp=cmp_new.reshape(B, T, 2, HEAD_DIM), nsa_slc=slc_new.reshape(B, T, 2, HEAD_DIM),
               nsa_win=win_state.reshape(B, -1, 2, HEAD_DIM), ml_C=ml_C, ml_n=ml_n, ml_m=ml_m,
               lru_h=lr_h, lru_conv=lr_buf)
    return x, new


def _block_diag(w):
    H, a, b = w.shape
    eye = jnp.eye(H, dtype=w.dtype)
    return jnp.einsum('hij,hg->higj', w, eye).reshape(H * a, H * b)


def kernel(x_prompt, x_sample, mem_prompt, cache_df_k, cache_df_v, cache_nsa_cmp, cache_nsa_slc, cache_nsa_win, state_ml_C, state_ml_n, state_ml_m, state_lru_h, state_lru_conv, cache_mem_k, cache_mem_v, page_table, g_mix, w_in, w_out, ml_gate_b, ml_norm_g, df_lam, df_norm_g, nsa_pos, nsa_w1, nsa_w2, lru_conv_w, lru_conv_b, lru_wa, lru_ba, lru_wx, lru_bx, lru_lambda, g_mem_q, g_mem_kv, w_mq, w_mk, w_mv, w_mo, g_mlp, w_up, w_down, g_final):
    Bp, Tp, _ = x_prompt.shape
    Bs, Ts, _ = x_sample.shape
    depth = w_in.shape[0]
    n_pool, page = cache_df_k.shape[1], cache_df_k.shape[2]
    n_past = page_table.shape[1] * page
    M = mem_prompt.shape[1]

    perm, n_real = _proj_perm()
    col_ok = (jnp.arange(D_PROJ_PAD) < n_real)
    w_in_p = jnp.where(col_ok[None, None, :], jnp.take(w_in, perm, axis=2), 0.0).astype(bf16)
    eye2 = jnp.eye(2, dtype=f32)
    w1 = nsa_w1.reshape(depth, 2, CMP_BLOCK, HEAD_DIM, CMP_HIDDEN)
    w1c = jnp.einsum('lstih,sg->ltsigh', w1, eye2).reshape(depth, CMP_BLOCK * 2 * HEAD_DIM, 2 * CMP_HIDDEN)
    w2c = jnp.einsum('lshd,sg->lshgd', nsa_w2, eye2).reshape(depth, 2 * CMP_HIDDEN, 2 * HEAD_DIM)
    cfg_p = dict(rope_tm=512, ml_chunk=512, df_tq=256, df_tk=512, ns_tq=256, ns_tk=512, ns_tkw=256,
                 lru_tm=256, mem_tq=256)
    cfg_s = dict(rope_tm=Bs * Ts, ml_chunk=Ts, lru_tm=Ts, mem_tq=Ts)
    tab_p = rope_tables(jnp.arange(Tp))
    tab_s = tuple(jnp.tile(t, (Bs, 1)) for t in rope_tables(n_past + jnp.arange(Ts)))

    tok_minor = lambda c: jnp.transpose(c, (0, 1, 3, 4, 2)).reshape(c.shape[0], c.shape[1], -1, c.shape[2])
    pool_df_k = tok_minor(cache_df_k)
    pool_df_v = tok_minor(cache_df_v)
    pool_cmp = tok_minor(cache_nsa_cmp)
    pool_slc = tok_minor(cache_nsa_slc)
    win_cache = tok_minor(cache_nsa_win)
    mem_k_cache = cache_mem_k.reshape(depth, Bs, M, D_MODEL)
    mem_v_cache = cache_mem_v.reshape(depth, Bs, M, D_MODEL)
    mem_rows = mem_prompt.reshape(Bp * M, D_MODEL)

    xp = x_prompt.reshape(Bp * Tp, D_MODEL)
    xs = x_sample.reshape(Bs * Ts, D_MODEL)
    names = ('df_k', 'df_v', 'nsa_cmp', 'nsa_slc', 'nsa_win', 'ml_C', 'ml_n', 'ml_m', 'lru_h', 'lru_conv')
    acc = {pre + n: [] for n in names for pre in ('p_', 's_')}
    acc['p_mem_k'] = []
    acc['p_mem_v'] = []
    for l in range(depth):
        lw = dict(g_mix=g_mix[l], w_in=w_in_p[l], w_out=w_out[l].astype(bf16), ml_gate_b=ml_gate_b[l],
                  ml_norm_g=ml_norm_g[l], df_lam=df_lam[l], df_norm_g=df_norm_g[l],
                  nsa_pos_flat=nsa_pos[l].reshape(1, CMP_BLOCK * 2 * HEAD_DIM), nsa_w1c=w1c[l].astype(bf16),
                  nsa_w2c=w2c[l].astype(bf16), lru_conv_w=lru_conv_w[l], lru_conv_b=lru_conv_b[l],
                  lru_wa_bd=_block_diag(lru_wa[l]).astype(bf16), lru_ba=lru_ba[l],
                  lru_wx_bd=_block_diag(lru_wx[l]).astype(bf16), lru_bx=lru_bx[l], lru_lambda=lru_lambda[l],
                  g_mem_q=g_mem_q[l], w_mq=w_mq[l].astype(bf16), w_mo=w_mo[l].astype(bf16), g_mlp=g_mlp[l],
                  w_up=w_up[l].astype(bf16), w_down=w_down[l].astype(bf16))
        lam_init = 0.8 - 0.6 * math.exp(-0.3 * l)
        mk_p = matmul(mem_rows, w_mk[l].astype(bf16), g=g_mem_kv[l])
        mv_p = matmul(mem_rows, w_mv[l].astype(bf16), g=g_mem_kv[l])
        ml0 = (jnp.zeros((Bp, N_HEADS, HEAD_DIM, HEAD_DIM), f32), jnp.zeros((Bp, N_HEADS, HEAD_DIM), f32),
               jnp.zeros((Bp, N_HEADS), f32))
        lr0 = (jnp.zeros((Bp, GROUP_WIDTH), f32), jnp.zeros((Bp, CONV_W - 1, GROUP_WIDTH), f32))
        xp, new_p = _layer(xp, lw, lam_init, tab_p, B=Bp, T=Tp,
                           mem=(mk_p.reshape(Bp, M, D_MODEL), mv_p.reshape(Bp, M, D_MODEL)),
                           ml_state=ml0, lr_state=lr0, paged=None, cfg=cfg_p)
        paged = dict(page_table=page_table, layer=l, df_k=pool_df_k, df_v=pool_df_v, nsa_cmp=pool_cmp,
                     nsa_slc=pool_slc, nsa_win=win_cache)
        xs, new_s = _layer(xs, lw, lam_init, tab_s, B=Bs, T=Ts,
                           mem=dict(k=mem_k_cache, v=mem_v_cache, layer=l),
                           ml_state=(state_ml_C[l], state_ml_n[l], state_ml_m[l]),
                           lr_state=(state_lru_h[l], state_lru_conv[l]), paged=paged, cfg=cfg_s)
        for pre, new in (('p_', new_p), ('s_', new_s)):
            for n in names:
                acc[pre + n].append(new[n])
        acc['p_mem_k'].append(mk_p.reshape(Bp, M, MEM_HEADS, MEM_HEAD_DIM))
        acc['p_mem_v'].append(mv_p.reshape(Bp, M, MEM_HEADS, MEM_HEAD_DIM))
    st = {k: jnp.stack(v) for k, v in acc.items()}
    y_prompt = rmsnorm_rows(xp, g_final).reshape(Bp, Tp, D_MODEL)
    y_sample = rmsnorm_rows(xs, g_final).reshape(Bs, Ts, D_MODEL)
    return (y_prompt, y_sample, st['p_df_k'], st['s_df_k'], st['p_df_v'], st['s_df_v'],
            st['p_nsa_cmp'], st['s_nsa_cmp'], st['p_nsa_slc'], st['s_nsa_slc'],
            st['p_nsa_win'], st['s_nsa_win'], st['p_ml_C'], st['s_ml_C'], st['p_ml_n'], st['s_ml_n'],
            st['p_ml_m'], st['s_ml_m'], st['p_lru_h'], st['s_lru_h'], st['p_lru_conv'], st['s_lru_conv'],
            st['p_mem_k'], st['p_mem_v'])
```

```python
import functools
import math

import jax
import jax.numpy as jnp
from jax import lax
from jax.experimental import pallas as pl
from jax.experimental.pallas import tpu as pltpu

f32 = jnp.float32
bf16 = jnp.bfloat16

D_MODEL = 1024
GROUP_WIDTH = 256
N_HEADS = 4
HEAD_DIM = 64
DQK = 32
ROPE_THETA = 10000.0
CMP_BLOCK = 32
CMP_HIDDEN = 128
SEL_BLOCK = 64
SEL_TOPK = 16
WINDOW = 512
CONV_W = 4
LRU_C = 8.0
MEM_HEADS = 4
MEM_HEAD_DIM = 256
EPS = 1e-6
NEG = -1e30
FORCE_SCORE = 1e9
TINY = 1e-30

VMEM_LIMIT_BYTES = 48 * 1024 * 1024
PAGES_PER_STEP = 16
PAGES_PER_STEP_SLC = 32
PAGES_PER_STEP_CMP = 64

_SRC_SPLITS = (
    ('ml_q', 256), ('ml_k', 256), ('ml_v', 256), ('ml_i', 4), ('ml_f', 4), ('ml_o', 256),
    ('df_q', 256), ('df_k', 256), ('df_v', 256),
    ('ns_q', 256), ('ns_kc', 64), ('ns_vc', 64), ('ns_ks', 64), ('ns_vs', 64),
    ('ns_kw', 64), ('ns_vw', 64), ('ns_g', 12), ('lr_x', 256), ('lr_g', 256),
)
_DST_ORDER = ('ml_q', 'ml_k', 'ml_v', 'ml_o', 'df_q', 'df_k', 'df_v', 'ns_q', 'lr_x', 'lr_g',
              'ns_kc', 'ns_vc', 'ns_ks', 'ns_vs', 'ns_kw', 'ns_vw', 'ml_i', 'ml_f', 'ns_g')
D_PROJ_PAD = 3072
CB_ML_Q, CB_ML_K, CB_ML_V, CB_ML_O, CB_DF_Q, CB_DF_K, CB_DF_V, CB_NS_Q, CB_LR_X, CB_LR_G = range(10)
CB128_CMP, CB128_SLC, CB128_WIN, CB128_SMALL = 20, 21, 22, 23
SMALL_GATES = 8


def _proj_perm():
    off = {}
    o = 0
    for name, w in _SRC_SPLITS:
        off[name] = (o, w)
        o += w
    idx = []
    for name in _DST_ORDER:
        s, w = off[name]
        idx.extend(range(s, s + w))
    n_real = len(idx)
    idx.extend([0] * (D_PROJ_PAD - n_real))
    return jnp.asarray(idx, jnp.int32), n_real


def _cparams(sem):
    return pltpu.CompilerParams(dimension_semantics=sem, vmem_limit_bytes=VMEM_LIMIT_BYTES)


def _act_dtype(rows):
    return bf16 if rows % 16 == 0 else f32


def _nt(a, b):
    return lax.dot_general(a, b, (((1,), (1,)), ((), ())), preferred_element_type=f32)


def _mm_kernel(*refs, n_parts, has_norm, has_res, act, tn):
    it = iter(refs)
    x_refs = [next(it) for _ in range(n_parts)]
    w_ref = next(it)
    g_ref = next(it) if has_norm else None
    r_ref = next(it) if has_res else None
    o_ref = next(it)
    h_ref = next(it)
    if has_norm:
        x = x_refs[0][...].astype(f32)
        y = x * lax.rsqrt(jnp.mean(x * x, axis=-1, keepdims=True) + EPS)
        h_ref[...] = (y * g_ref[...]).astype(bf16)
    elif n_parts > 1 or x_refs[0].dtype != bf16:
        off = 0
        for x_ref in x_refs:
            kp = x_ref.shape[-1]
            h_ref[:, off:off + kp] = x_ref[...].astype(bf16)
            off += kp
    else:
        h_ref = x_refs[0]
    N = o_ref.shape[-1]
    for c0 in range(0, N, tn):
        cs = slice(c0, min(c0 + tn, N))
        acc = jnp.dot(h_ref[...], w_ref[:, cs], preferred_element_type=f32)
        if act == 'relu2':
            acc = jnp.maximum(acc, 0.0)
            acc = acc * acc
        if has_res:
            acc = acc + r_ref[:, cs]
        o_ref[:, cs] = acc.astype(o_ref.dtype)


def matmul(x, w, *, g=None, res=None, act=None, out_dtype=f32, tm=512, tn=512):
    parts = list(x) if isinstance(x, (list, tuple)) else [x]
    M = parts[0].shape[0]
    K, N = w.shape
    assert sum(p.shape[1] for p in parts) == K
    tm = min(tm, M)
    assert M % tm == 0
    has_norm = g is not None
    has_res = res is not None
    assert not (has_norm and len(parts) > 1)
    in_specs = [pl.BlockSpec((tm, p.shape[1]), lambda i: (i, 0)) for p in parts]
    in_specs.append(pl.BlockSpec((K, N), lambda i: (0, 0)))
    args = parts + [w]
    if has_norm:
        in_specs.append(pl.BlockSpec((1, K), lambda i: (0, 0)))
        args.append(g.reshape(1, K).astype(f32))
    if has_res:
        in_specs.append(pl.BlockSpec((tm, N), lambda i: (i, 0)))
        args.append(res)
    return pl.pallas_call(
        functools.partial(_mm_kernel, n_parts=len(parts), has_norm=has_norm, has_res=has_res, act=act, tn=tn),
        out_shape=jax.ShapeDtypeStruct((M, N), out_dtype),
        grid=(M // tm,),
        in_specs=in_specs,
        out_specs=pl.BlockSpec((tm, N), lambda i: (i, 0)),
        scratch_shapes=[pltpu.VMEM((tm, K), bf16)],
        compiler_params=_cparams(("parallel",)),
        name="matmul",
    )(*args)


def _rmsnorm_kernel(x_ref, g_ref, o_ref):
    x = x_ref[...]
    y = x * lax.rsqrt(jnp.mean(x * x, axis=-1, keepdims=True) + EPS)
    o_ref[...] = y * g_ref[...]


def rmsnorm_rows(x, g, tm=1024):
    M, K = x.shape
    tm = min(tm, M)
    return pl.pallas_call(
        _rmsnorm_kernel,
        out_shape=jax.ShapeDtypeStruct((M, K), f32),
        grid=(M // tm,),
        in_specs=[pl.BlockSpec((tm, K), lambda i: (i, 0)), pl.BlockSpec((1, K), lambda i: (0, 0))],
        out_specs=pl.BlockSpec((tm, K), lambda i: (i, 0)),
        compiler_params=_cparams(("parallel",)),
        name="final_norm",
    )(x, g.reshape(1, K))


def _rotate(x, cos, sin, half):
    n = x.shape[-1]
    lane = lax.broadcasted_iota(jnp.int32, x.shape, 1)
    first = (lane & (2 * half - 1)) < half
    partner = jnp.where(first, pltpu.roll(x, n - half, 1), pltpu.roll(x, half, 1))
    return x * cos + partner * sin


def _rope_kernel(dq, dk, nq, sl, wn, ca, sa, cb, sb, cc, sc, odq, odk, onq, osl, own):
    odq[...] = _rotate(dq[...], ca[...], sa[...], DQK // 2)
    odk[...] = _rotate(dk[...], ca[...], sa[...], DQK // 2)
    onq[...] = _rotate(nq[...], cb[...], sb[...], HEAD_DIM // 2)
    osl[...] = _rotate(sl[...], cc[...], sc[...], HEAD_DIM // 2)
    own[...] = _rotate(wn[...], cc[...], sc[...], HEAD_DIM // 2)


def rope_tables(pos):
    posf = pos.astype(f32)[:, None]

    def tab(half, reps):
        inv = ROPE_THETA ** (-jnp.arange(half, dtype=f32) / half)
        ang = posf * inv[None, :]
        c = jnp.cos(ang)
        s = jnp.sin(ang)
        return jnp.tile(jnp.concatenate([c, c], 1), (1, reps)), jnp.tile(jnp.concatenate([-s, s], 1), (1, reps))

    ca, sa = tab(DQK // 2, GROUP_WIDTH // DQK)
    cb, sb = tab(HEAD_DIM // 2, N_HEADS)
    n = pos.shape[0]
    cc = jnp.concatenate([cb[:, :HEAD_DIM], jnp.ones((n, HEAD_DIM), f32)], 1)
    sc = jnp.concatenate([sb[:, :HEAD_DIM], jnp.zeros((n, HEAD_DIM), f32)], 1)
    return ca, sa, cb, sb, cc, sc


def rope_prep(z, tables, tm):
    N = z.shape[0]
    R = tables[0].shape[0]
    tm = min(tm, R)
    assert R % tm == 0 and N % tm == 0
    nr = R // tm
    zs = lambda cb: pl.BlockSpec((tm, 256), lambda i, cb=cb: (i, cb))
    zs128 = lambda cb: pl.BlockSpec((tm, 128), lambda i, cb=cb: (i, cb))
    t256 = pl.BlockSpec((tm, 256), lambda i: (i % nr, 0))
    t128 = pl.BlockSpec((tm, 128), lambda i: (i % nr, 0))
    o256 = pl.BlockSpec((tm, 256), lambda i: (i, 0))
    o128 = pl.BlockSpec((tm, 128), lambda i: (i, 0))
    return pl.pallas_call(
        _rope_kernel,
        out_shape=[jax.ShapeDtypeStruct((N, 256), f32)] * 3 + [jax.ShapeDtypeStruct((N, 128), f32)] * 2,
        grid=(N // tm,),
        in_specs=[zs(CB_DF_Q), zs(CB_DF_K), zs(CB_NS_Q), zs128(CB128_SLC), zs128(CB128_WIN),
                  t256, t256, t256, t256, t128, t128],
        out_specs=[o256, o256, o256, o128, o128],
        compiler_params=_cparams(("parallel",)),
        name="rope_prep",
    )(z, z, z, z, z, *tables)


ROW_CHUNK = 16
LANE_CHUNK = 512


def _softmax_tile(s, s_ref, pb_ref, m_ref, l_ref):
    rows, n = s.shape
    s_ref[0:rows, :] = s
    m_old = m_ref[...]
    m_new = jnp.maximum(m_old, jnp.max(s, axis=0, keepdims=True))
    alpha = jnp.exp(m_old - m_new)
    m_ref[...] = m_new
    for c0 in range(0, n, LANE_CHUNK):
        cs = slice(c0, c0 + LANE_CHUNK)
        mb = jnp.broadcast_to(m_new[:, cs], (ROW_CHUNK, LANE_CHUNK))
        part = jnp.zeros((8, LANE_CHUNK), f32)
        for r in range(0, rows, ROW_CHUNK):
            p = jnp.exp(s_ref[r:r + ROW_CHUNK, cs] - mb)
            pb_ref[r:r + ROW_CHUNK, cs] = p.astype(bf16)
            part = part + (p[0:8, :] + p[8:16, :])
        l_ref[:, cs] = alpha[:, cs] * l_ref[:, cs] + jnp.sum(part, axis=0, keepdims=True)
    return alpha


def _diff_lambda(lam_ref, lam_init):
    lq = lam_ref[...]
    return (jnp.exp(jnp.sum(lq[0:1] * lq[1:2], keepdims=True))
            - jnp.exp(jnp.sum(lq[2:3] * lq[3:4], keepdims=True)) + lam_init)


def _diff_kernel(q_ref, k_ref, v_ref, lam_ref, g_ref, o_ref, qp_ref, m_ref, l_ref, acc_ref, oT_ref,
                 s_ref, pb_ref, *, tq, tk, nk, lam_init):
    qi = pl.program_id(1)
    kj = pl.program_id(2)
    q_lo = qi * tq
    needed = (q_lo + tq - 1) // tk + 1
    nmap = 2 * N_HEADS

    @pl.when(kj == 0)
    def _():
        qT = jnp.transpose(q_ref[0]) * (DQK ** -0.5)
        row = lax.broadcasted_iota(jnp.int32, qT.shape, 0)
        for c in range(nmap):
            blk = jnp.where((row >= DQK * c) & (row < DQK * (c + 1)), qT, 0.0)
            qp_ref[:, c * tq:(c + 1) * tq] = blk.astype(bf16)
        m_ref[...] = jnp.full(m_ref.shape, NEG, f32)
        l_ref[...] = jnp.zeros(l_ref.shape, f32)
        acc_ref[...] = jnp.zeros(acc_ref.shape, f32)

    def step(masked):
        k = k_ref[0].astype(bf16)
        s = jnp.dot(k, qp_ref[...], preferred_element_type=f32)
        if masked:
            kpos = kj * tk + lax.broadcasted_iota(jnp.int32, s.shape, 0)
            qpos = q_lo + (lax.broadcasted_iota(jnp.int32, s.shape, 1) & (tq - 1))
            ok = kpos <= qpos
            s = jnp.where(ok, s, NEG)
        alpha = _softmax_tile(s, s_ref, pb_ref, m_ref, l_ref)
        vT = jnp.transpose(v_ref[0]).astype(bf16)
        for h in range(N_HEADS):
            rs = slice(HEAD_DIM * h, HEAD_DIM * (h + 1))
            cs = slice(2 * tq * h, 2 * tq * (h + 1))
            acc_ref[rs, :] = acc_ref[rs, :] * alpha[:, cs] + jnp.dot(
                vT[rs, :], pb_ref[:, cs], preferred_element_type=f32)

    active = kj < needed
    crosses = kj * tk + tk - 1 > q_lo

    @pl.when(active & crosses)
    def _():
        step(True)

    @pl.when(active & jnp.logical_not(crosses))
    def _():
        step(False)

    @pl.when(kj == nk - 1)
    def _():
        lam = _diff_lambda(lam_ref, lam_init)
        l = l_ref[...]
        for h in range(N_HEADS):
            rs = slice(HEAD_DIM * h, HEAD_DIM * (h + 1))
            c1 = slice(2 * h * tq, (2 * h + 1) * tq)
            c2 = slice((2 * h + 1) * tq, (2 * h + 2) * tq)
            o = acc_ref[rs, 0:tq] / l[:, c1] - lam * (acc_ref[rs, tq:2 * tq] / l[:, c2])
            y = o * lax.rsqrt(jnp.mean(o * o, axis=0, keepdims=True) + EPS)
            oT_ref[rs, :] = (y * g_ref[...]) * (1.0 - lam_init)
        o_ref[0] = jnp.transpose(oT_ref[...]).astype(o_ref.dtype)


def diff_attention(q, k, z3, lam_q, norm_g, *, lam_init, tq, tk):
    B, T, _ = q.shape
    assert T % tq == 0 and T % tk == 0 and (tq & (tq - 1)) == 0 and tq % 128 == 0 and tk % 128 == 0
    nq, nk = T // tq, T // tk
    last = lambda qi: (qi * tq + tq - 1) // tk
    kern = functools.partial(_diff_kernel, tq=tq, tk=tk, nk=nk, lam_init=lam_init)
    return pl.pallas_call(
        kern,
        out_shape=jax.ShapeDtypeStruct((B, T, GROUP_WIDTH), bf16),
        grid=(B, nq, nk),
        in_specs=[
            pl.BlockSpec((1, tq, GROUP_WIDTH), lambda b, qi, kj: (b, qi, 0)),
            pl.BlockSpec((1, tk, GROUP_WIDTH), lambda b, qi, kj: (b, jnp.minimum(kj, last(qi)), 0)),
            pl.BlockSpec((1, tk, GROUP_WIDTH), lambda b, qi, kj: (b, jnp.minimum(kj, last(qi)), CB_DF_V)),
            pl.BlockSpec((4, DQK), lambda b, qi, kj: (0, 0)),
            pl.BlockSpec((HEAD_DIM, 1), lambda b, qi, kj: (0, 0)),
        ],
        out_specs=pl.BlockSpec((1, tq, GROUP_WIDTH), lambda b, qi, kj: (b, qi, 0)),
        scratch_shapes=[pltpu.VMEM((GROUP_WIDTH, 8 * tq), bf16), pltpu.VMEM((1, 8 * tq), f32),
                        pltpu.VMEM((1, 8 * tq), f32), pltpu.VMEM((GROUP_WIDTH, 2 * tq), f32),
                        pltpu.VMEM((GROUP_WIDTH, tq), f32),
                        pltpu.VMEM((tk, 8 * tq), f32), pltpu.VMEM((tk, 8 * tq), bf16)],
        compiler_params=_cparams(("parallel", "parallel", "arbitrary")),
        name="diff_attention",
    )(q, k, z3, lam_q, norm_g.reshape(HEAD_DIM, 1))


def _diff_decode_kernel(pt_ref, q_ref, kn_ref, vn_ref, lam_ref, g_ref, *rest, G, n_steps, Ts, pos0, n_past,
                        lam_init):
    kpages = rest[:G]
    vpages = rest[G:2 * G]
    o_ref, qp_ref, m_ref, l_ref, acc_ref = rest[2 * G:]
    j = pl.program_id(1)
    nmap = 2 * N_HEADS
    R = nmap * Ts

    @pl.when(j == 0)
    def _():
        q = q_ref[0] * (DQK ** -0.5)
        col = lax.broadcasted_iota(jnp.int32, q.shape, 1)
        for c in range(nmap):
            qp_ref[c * Ts:(c + 1) * Ts, :] = jnp.where((col >= DQK * c) & (col < DQK * (c + 1)), q, 0.0)
        s = _nt(qp_ref[...].astype(bf16), kn_ref[0].astype(bf16))
        qpos = pos0 + (lax.broadcasted_iota(jnp.int32, s.shape, 0) & (Ts - 1))
        kpos = n_past + lax.broadcasted_iota(jnp.int32, s.shape, 1)
        ok = kpos <= qpos
        s = jnp.where(ok, s, NEG)
        m = jnp.max(s, axis=1, keepdims=True)
        p = jnp.where(ok, jnp.exp(s - m), 0.0)
        m_ref[...] = m
        l_ref[...] = jnp.sum(p, axis=1, keepdims=True)
        acc_ref[...] = jnp.dot(p.astype(bf16), vn_ref[0].astype(bf16), preferred_element_type=f32)

    kT = jnp.concatenate([kp[...] for kp in kpages], axis=1).astype(bf16)
    vT = jnp.concatenate([vp[...] for vp in vpages], axis=1).astype(bf16)
    s = jnp.dot(qp_ref[...].astype(bf16), kT, preferred_element_type=f32)
    m_old = m_ref[...]
    m_new = jnp.maximum(m_old, jnp.max(s, axis=1, keepdims=True))
    alpha = jnp.exp(m_old - m_new)
    p = jnp.exp(s - m_new)
    l_ref[...] = alpha * l_ref[...] + jnp.sum(p, axis=1, keepdims=True)
    m_ref[...] = m_new
    acc_ref[...] = alpha * acc_ref[...] + _nt(p.astype(bf16), vT)

    @pl.when(j == n_steps - 1)
    def _():
        lam = _diff_lambda(lam_ref, lam_init)
        o_all = acc_ref[...] / l_ref[...]
        ys = []
        for h in range(N_HEADS):
            cs = slice(HEAD_DIM * h, HEAD_DIM * (h + 1))
            o = o_all[2 * h * Ts:(2 * h + 1) * Ts, cs] - lam * o_all[(2 * h + 1) * Ts:(2 * h + 2) * Ts, cs]
            y = o * lax.rsqrt(jnp.mean(o * o, axis=1, keepdims=True) + EPS)
            ys.append((y * g_ref[...]) * (1.0 - lam_init))
        o_ref[0] = jnp.concatenate(ys, axis=1)


def _page_specs(layer, n_pages, G, block):
    def spec(i):
        return pl.BlockSpec((None, None) + block,
                            lambda b, j, pt, i=i: (layer, pt[b * n_pages + j * G + i]) + (0,) * len(block))
    return [spec(i) for i in range(G)]


def diff_decode(page_table, q, k_new, v_new, pool_k, pool_v, lam_q, norm_g, *, layer, pos0, lam_init):
    B, Ts, _ = q.shape
    n_pages = page_table.shape[1]
    page = pool_k.shape[3]
    G = min(PAGES_PER_STEP, n_pages)
    assert n_pages % G == 0 and (Ts & (Ts - 1)) == 0 and Ts % 8 == 0
    n_steps = n_pages // G
    R = 2 * N_HEADS * Ts
    row = lambda w: pl.BlockSpec((1, Ts, w), lambda b, j, pt: (b, 0, 0))
    kern = functools.partial(_diff_decode_kernel, G=G, n_steps=n_steps, Ts=Ts, pos0=pos0,
                             n_past=n_pages * page, lam_init=lam_init)
    return pl.pallas_call(
        kern,
        out_shape=jax.ShapeDtypeStruct((B, Ts, GROUP_WIDTH), f32),
        grid_spec=pltpu.PrefetchScalarGridSpec(
            num_scalar_prefetch=1, grid=(B, n_steps),
            in_specs=[row(256), row(256), row(256),
                      pl.BlockSpec((4, DQK), lambda b, j, pt: (0, 0)),
                      pl.BlockSpec((1, HEAD_DIM), lambda b, j, pt: (0, 0))]
            + _page_specs(layer, n_pages, G, (256, page)) + _page_specs(layer, n_pages, G, (256, page)),
            out_specs=row(256),
            scratch_shapes=[pltpu.VMEM((R, GROUP_WIDTH), f32), pltpu.VMEM((R, 1), f32),
                            pltpu.VMEM((R, 1), f32), pltpu.VMEM((R, GROUP_WIDTH), f32)]),
        compiler_params=_cparams(("parallel", "arbitrary")),
        name="diff_decode",
    )(page_table.reshape(-1), q, k_new, v_new, lam_q, norm_g.reshape(1, HEAD_DIM),
      *([pool_k] * G), *([pool_v] * G))


def _compress_rows(x, pos_ref, w1_ref, w2_ref):
    x = (x + pos_ref[...]).astype(bf16)
    hid = jax.nn.gelu(jnp.dot(x, w1_ref[...], preferred_element_type=f32))
    return jnp.dot(hid.astype(bf16), w2_ref[...], preferred_element_type=f32)


def _compress_kernel(x_ref, pos_ref, w1_ref, w2_ref, o_ref):
    o_ref[...] = _compress_rows(x_ref[...], pos_ref, w1_ref, w2_ref)


def nsa_compress(blocks, pos_flat, w1c, w2c, tm=256):
    R, K = blocks.shape
    tm = min(tm, R)
    assert R % tm == 0
    return pl.pallas_call(
        _compress_kernel,
        out_shape=jax.ShapeDtypeStruct((R, 2 * HEAD_DIM), f32),
        grid=(R // tm,),
        in_specs=[pl.BlockSpec((tm, K), lambda i: (i, 0)), pl.BlockSpec((1, K), lambda i: (0, 0)),
                  pl.BlockSpec((K, 2 * CMP_HIDDEN), lambda i: (0, 0)),
                  pl.BlockSpec((2 * CMP_HIDDEN, 2 * HEAD_DIM), lambda i: (0, 0))],
        out_specs=pl.BlockSpec((tm, 2 * HEAD_DIM), lambda i: (i, 0)),
        compiler_params=_cparams(("parallel",)),
        name="nsa_compress",
    )(blocks, pos_flat, w1c, w2c)


def _compress_paged_kernel(pt_ref, pos_ref, w1_ref, w2_ref, *rest, G, page):
    pages = rest[:G]
    o_ref, xs = rest[G:]
    w = 2 * HEAD_DIM
    for i in range(G):
        xs[page * i:page * (i + 1), :] = jnp.transpose(pages[i][...])
    nb = G * page // CMP_BLOCK
    acc = None
    for t in range(CMP_BLOCK):
        xt = xs[pl.ds(t, nb, stride=CMP_BLOCK), :] + pos_ref[:, w * t:w * (t + 1)]
        part = jnp.dot(xt.astype(bf16), w1_ref[w * t:w * (t + 1), :], preferred_element_type=f32)
        acc = part if acc is None else acc + part
    hid = jax.nn.gelu(acc)
    o_ref[0] = jnp.dot(hid.astype(bf16), w2_ref[...], preferred_element_type=f32)


def nsa_compress_paged(page_table, pool, pos_flat, w1c, w2c, *, layer):
    B, n_pages = page_table.shape
    page = pool.shape[3]
    K = CMP_BLOCK * 2 * HEAD_DIM
    rp = page // CMP_BLOCK
    G = min(PAGES_PER_STEP_CMP, n_pages)
    assert n_pages % G == 0 and page % CMP_BLOCK == 0 and (G * rp) % 8 == 0
    const = lambda shape: pl.BlockSpec(shape, lambda b, j, pt: (0, 0))
    return pl.pallas_call(
        functools.partial(_compress_paged_kernel, G=G, page=page),
        out_shape=jax.ShapeDtypeStruct((B, n_pages * rp, 2 * HEAD_DIM), f32),
        grid_spec=pltpu.PrefetchScalarGridSpec(
            num_scalar_prefetch=1, grid=(B, n_pages // G),
            in_specs=[const((1, K)), const((K, 2 * CMP_HIDDEN)), const((2 * CMP_HIDDEN, 2 * HEAD_DIM))]
            + _page_specs(layer, n_pages, G, (2 * HEAD_DIM, page)),
            out_specs=pl.BlockSpec((1, G * rp, 2 * HEAD_DIM), lambda b, j, pt: (b, j, 0)),
            scratch_shapes=[pltpu.VMEM((G * page, 2 * HEAD_DIM), f32)]),
        compiler_params=_cparams(("parallel", "arbitrary")),
        name="nsa_compress_paged",
    )(page_table.reshape(-1), pos_flat, w1c, w2c, *([pool] * G))


def _cmp_block_end(r, half):
    blk = jnp.where(r < half, 2 * r, 2 * (r - half) + 1)
    return (blk + 1) * CMP_BLOCK - 1


def _nsa_kernel(qr_in, qw_in, sm_in, cmp_ref, slc_ref, win_ref,
                o_ref, qr_ref, v_ref, sel_ref, ocmp_ref, ms_ref, ls_ref, accs_ref, mw_ref, lw_ref, accw_ref,
                oT_ref, s_ref, pb_ref, *, tq, tk, tkw, nk, nc, nsp, topk):
    qi = pl.program_id(1)
    kj = pl.program_id(2)
    q_lo = qi * tq
    H = N_HEADS
    d = HEAD_DIM
    scale = d ** -0.5
    needed = (q_lo + tq - 1) // tk + 1
    w_lo = jnp.maximum(q_lo - (WINDOW - 1), 0) // tkw
    w_hi = (q_lo + tq - 1) // tkw
    half = nc // 2

    def heads_on_lanes(xT):
        return jnp.concatenate([xT[d * h:d * (h + 1), :] for h in range(H)], axis=1)

    def qpos_row(n):
        return q_lo + (lax.broadcasted_iota(jnp.int32, (1, n), 1) & (tq - 1))

    @pl.when(kj == 0)
    def _():
        zeros = jnp.zeros((d, H * tq), f32)
        qrT = jnp.transpose(qr_in[0]) * scale
        qwT = jnp.transpose(qw_in[0]) * scale
        qr_ref[...] = jnp.concatenate([heads_on_lanes(qrT), zeros], 0).astype(bf16)
        qw = jnp.concatenate([heads_on_lanes(qwT), zeros], 0).astype(bf16)
        cmp = cmp_ref[0]
        s = jnp.dot(cmp.astype(bf16), qw, preferred_element_type=f32)
        c_end = _cmp_block_end(lax.broadcasted_iota(jnp.int32, (nc, 1), 0), half)
        c_ok = c_end <= qpos_row(H * tq)
        s = jnp.where(c_ok, s, NEG)
        p = jnp.where(c_ok, jnp.exp(s - jnp.max(s, axis=0, keepdims=True)), 0.0)
        p = p / jnp.maximum(jnp.sum(p, axis=0, keepdims=True), TINY)
        cmpT = jnp.transpose(cmp)
        ocmp_ref[...] = jnp.dot(cmpT[d:2 * d, :].astype(bf16), p.astype(bf16), preferred_element_type=f32)
        imp = p[:, 0:tq]
        for h in range(1, H):
            imp = imp + p[:, h * tq:(h + 1) * tq]
        imp = imp[:half, :] + imp[half:, :]
        if nsp > half:
            imp = jnp.concatenate([imp, jnp.zeros((nsp - half, tq), f32)], 0)
        sb = lax.broadcasted_iota(jnp.int32, (nsp, tq), 0)
        cur = qpos_row(tq) // SEL_BLOCK
        v = jnp.where((sb == cur) | (sb == 0), FORCE_SCORE, imp)
        v = jnp.where(sb > cur, NEG, v)
        v_ref[...] = v

        def rank(i, cnt):
            vi = v_ref[pl.ds(i, 1), :]
            ahead = (vi > v) | ((vi == v) & (i < sb))
            return cnt + jnp.where(ahead, 1.0, 0.0)

        cnt = lax.fori_loop(0, nsp, rank, jnp.zeros((nsp, tq), f32))
        sel_ref[...] = jnp.where(cnt < topk, 1.0, 0.0)
        for m_r, l_r, a_r in ((ms_ref, ls_ref, accs_ref), (mw_ref, lw_ref, accw_ref)):
            m_r[...] = jnp.full(m_r.shape, NEG, f32)
            l_r[...] = jnp.zeros(l_r.shape, f32)
            a_r[...] = jnp.zeros(a_r.shape, f32)

    def flash(kv_ref, ok, m_r, l_r, a_r):
        kv = kv_ref[0]
        s = jnp.dot(kv.astype(bf16), qr_ref[...], preferred_element_type=f32)
        s = s + jnp.concatenate([jnp.where(ok, 0.0, NEG)] * H, axis=1)
        alpha = _softmax_tile(s, s_ref, pb_ref, m_r, l_r)
        vT = jnp.transpose(kv)[d:2 * d, :].astype(bf16)
        a_r[...] = a_r[...] * alpha + jnp.dot(vT, pb_ref[0:kv.shape[0], :], preferred_element_type=f32)

    @pl.when(kj < needed)
    def _():
        kpos = kj * tk + lax.broadcasted_iota(jnp.int32, (tk, tq), 0)
        nb = tk // SEL_BLOCK
        rows = [jnp.broadcast_to(sel_ref[pl.ds(kj * nb + c, 1), :], (SEL_BLOCK, tq)) for c in range(nb)]
        chosen = jnp.concatenate(rows, axis=0) > 0.5
        ok = chosen & (kpos <= qpos_row(tq))
        flash(slc_ref, ok, ms_ref, ls_ref, accs_ref)

    @pl.when(kj <= w_hi - w_lo)
    def _():
        wpos = (w_lo + kj) * tkw + lax.broadcasted_iota(jnp.int32, (tkw, tq), 0)
        qp = qpos_row(tq)
        ok = (wpos <= qp) & (qp - wpos < WINDOW) & (wpos >= 0)
        flash(win_ref, ok, mw_ref, lw_ref, accw_ref)

    @pl.when(kj == nk - 1)
    def _():
        smT = jnp.transpose(sm_in[0])
        g = jax.nn.sigmoid(smT[SMALL_GATES:SMALL_GATES + 3 * H, :])
        o_slc = accs_ref[...] / ls_ref[...]
        o_win = accw_ref[...] / lw_ref[...]
        o_cmp = ocmp_ref[...]
        for h in range(H):
            cs = slice(h * tq, (h + 1) * tq)
            oT_ref[d * h:d * (h + 1), :] = (g[3 * h:3 * h + 1] * o_cmp[:, cs]
                                            + g[3 * h + 1:3 * h + 2] * o_slc[:, cs]
                                            + g[3 * h + 2:3 * h + 3] * o_win[:, cs])
        o_ref[0] = jnp.transpose(oT_ref[...]).astype(o_ref.dtype)


def nsa_attention(qr, z3, cmp, slc, win, *, tq, tk, tkw):
    B, T, _ = qr.shape
    nc = cmp.shape[1]
    assert T % tq == 0 and T % tk == 0 and T % tkw == 0 and tk % SEL_BLOCK == 0 and nc % 2 == 0
    assert tq % 128 == 0 and tk % 128 == 0 and tkw % 128 == 0 and nc % 8 == 0 and (tq & (tq - 1)) == 0
    nq, nk = T // tq, T // tk
    nsp = T // SEL_BLOCK
    assert nsp >= nc // 2 and nsp % 8 == 0
    last = lambda qi: (qi * tq + tq - 1) // tk
    w_lo = lambda qi: jnp.maximum(qi * tq - (WINDOW - 1), 0) // tkw
    w_hi = lambda qi: (qi * tq + tq - 1) // tkw
    for qi in range(nq):
        lo = max(qi * tq - (WINDOW - 1), 0) // tkw
        assert (qi * tq + tq - 1) // tkw - lo + 1 <= nk
    w_idx = lambda qi, kj: jnp.minimum(w_lo(qi) + kj, w_hi(qi))
    kern = functools.partial(_nsa_kernel, tq=tq, tk=tk, tkw=tkw, nk=nk, nc=nc, nsp=nsp,
                             topk=min(SEL_TOPK, nsp))
    stat = pltpu.VMEM((1, N_HEADS * tq), f32)
    acc = pltpu.VMEM((HEAD_DIM, N_HEADS * tq), f32)
    return pl.pallas_call(
        kern,
        out_shape=jax.ShapeDtypeStruct((B, T, GROUP_WIDTH), bf16),
        grid=(B, nq, nk),
        in_specs=[
            pl.BlockSpec((1, tq, GROUP_WIDTH), lambda b, qi, kj: (b, qi, 0)),
            pl.BlockSpec((1, tq, GROUP_WIDTH), lambda b, qi, kj: (b, qi, CB_NS_Q)),
            pl.BlockSpec((1, tq, 128), lambda b, qi, kj: (b, qi, CB128_SMALL)),
            pl.BlockSpec((1, nc, 128), lambda b, qi, kj: (b, 0, 0)),
            pl.BlockSpec((1, tk, 128), lambda b, qi, kj: (b, jnp.minimum(kj, last(qi)), 0)),
            pl.BlockSpec((1, tkw, 128), lambda b, qi, kj: (b, w_idx(qi, kj), 0)),
        ],
        out_specs=pl.BlockSpec((1, tq, GROUP_WIDTH), lambda b, qi, kj: (b, qi, 0)),
        scratch_shapes=[pltpu.VMEM((128, N_HEADS * tq), bf16), pltpu.VMEM((nsp, tq), f32),
                        pltpu.VMEM((nsp, tq), f32), acc, stat, stat, acc, stat, stat, acc,
                        pltpu.VMEM((GROUP_WIDTH, tq), f32),
                        pltpu.VMEM((max(tk, tkw), N_HEADS * tq), f32),
                        pltpu.VMEM((max(tk, tkw), N_HEADS * tq), bf16)],
        compiler_params=_cparams(("parallel", "parallel", "arbitrary")),
        name="nsa_attention",
    )(qr, z3, z3, cmp, slc, win)


def _nsa_decode_kernel(pt_ref, qr_in, qw_in, sm_in, cmp_ref, sn_ref, wb_ref, wn_ref, *rest,
                       G, n_steps, Ts, pos0, n_past, wpos0, nc, nsl, n_sel, topk, page):
    pages = rest[:G]
    o_ref, qr_ref, selst_ref, ocmp_ref, owin_ref, m_ref, l_ref, acc_ref = rest[G:]
    j = pl.program_id(1)
    H, d = N_HEADS, HEAD_DIM
    scale = d ** -0.5
    R = H * Ts
    half = nc // 2
    bps = G * page // SEL_BLOCK

    def rows_by_head(x):
        z = jnp.zeros((Ts, d), f32)
        return jnp.concatenate([jnp.concatenate([x[:, d * h:d * (h + 1)], z], axis=1) for h in range(H)], axis=0)

    def tile_heads(x):
        return jnp.concatenate([x] * H, axis=0)

    @pl.when(j == 0)
    def _():
        qr = rows_by_head(qr_in[0] * scale)
        qr_ref[...] = qr
        qrb = qr.astype(bf16)
        qwb = rows_by_head(qw_in[0] * scale).astype(bf16)
        qpos = pos0 + (lax.broadcasted_iota(jnp.int32, (R, 1), 0) & (Ts - 1))
        cmpb = cmp_ref[0].astype(bf16)
        s = _nt(qwb, cmpb)
        c_ok = _cmp_block_end(lax.broadcasted_iota(jnp.int32, (1, nc), 1), half) <= qpos
        s = jnp.where(c_ok, s, NEG)
        p = jnp.where(c_ok, jnp.exp(s - jnp.max(s, axis=1, keepdims=True)), 0.0)
        p = p / jnp.maximum(jnp.sum(p, axis=1, keepdims=True), TINY)
        ocmp_ref[...] = jnp.dot(p.astype(bf16), cmpb, preferred_element_type=f32)
        imp = p[0:Ts, :]
        for h in range(1, H):
            imp = imp + p[h * Ts:(h + 1) * Ts, :]
        imp = imp[:, :half] + imp[:, half:]
        imp = jnp.concatenate([imp, jnp.zeros((Ts, nsl - half), f32)], axis=1)
        lane = lax.broadcasted_iota(jnp.int32, (Ts, nsl), 1)
        cur = (pos0 + lax.broadcasted_iota(jnp.int32, (Ts, 1), 0)) // SEL_BLOCK
        v = jnp.where((lane == cur) | (lane == 0), FORCE_SCORE, imp)
        v = jnp.where(lane > cur, NEG, v)
        cnt = jnp.zeros((Ts, nsl), f32)
        for i in range(n_sel):
            vi = v[:, i:i + 1]
            cnt = cnt + jnp.where((vi > v) | ((vi == v) & (lane > i)), 1.0, 0.0)
        sel = jnp.where(cnt < topk, 1.0, 0.0)
        for st in range(n_steps):
            selst_ref[st] = sel[:, bps * st:bps * (st + 1)]
        blk_new = n_past // SEL_BLOCK
        sel_new = tile_heads(sel[:, blk_new:blk_new + 1]) > 0.5
        snb = sn_ref[0].astype(bf16)
        s = _nt(qrb, snb)
        kpos = n_past + lax.broadcasted_iota(jnp.int32, (1, Ts), 1)
        ok = sel_new & (kpos <= qpos)
        s = jnp.where(ok, s, NEG)
        m = jnp.max(s, axis=1, keepdims=True)
        p = jnp.where(ok, jnp.exp(s - m), 0.0)
        m_ref[...] = m
        l_ref[...] = jnp.sum(p, axis=1, keepdims=True)
        acc_ref[...] = jnp.dot(p.astype(bf16), snb, preferred_element_type=f32)
        wbT = wb_ref[...].astype(bf16)
        wnb = wn_ref[0].astype(bf16)
        lw = wbT.shape[1]
        s1 = jnp.dot(qrb, wbT, preferred_element_type=f32)
        s2 = _nt(qrb, wnb)
        wp1 = wpos0 + lax.broadcasted_iota(jnp.int32, (1, lw), 1)
        wp2 = wpos0 + lw + lax.broadcasted_iota(jnp.int32, (1, Ts), 1)
        ok1 = (wp1 <= qpos) & (qpos - wp1 < WINDOW) & (wp1 >= 0)
        ok2 = (wp2 <= qpos) & (qpos - wp2 < WINDOW) & (wp2 >= 0)
        s1 = jnp.where(ok1, s1, NEG)
        s2 = jnp.where(ok2, s2, NEG)
        mw = jnp.maximum(jnp.max(s1, axis=1, keepdims=True), jnp.max(s2, axis=1, keepdims=True))
        p1 = jnp.where(ok1, jnp.exp(s1 - mw), 0.0)
        p2 = jnp.where(ok2, jnp.exp(s2 - mw), 0.0)
        lsum = jnp.sum(p1, axis=1, keepdims=True) + jnp.sum(p2, axis=1, keepdims=True)
        owin_ref[...] = (_nt(p1.astype(bf16), wbT)
                         + jnp.dot(p2.astype(bf16), wnb, preferred_element_type=f32)) / lsum

    kvT = jnp.concatenate([pg[...] for pg in pages], axis=1).astype(bf16)
    n = kvT.shape[1]
    s = jnp.dot(qr_ref[...].astype(bf16), kvT, preferred_element_type=f32)
    expand = (lax.broadcasted_iota(jnp.int32, (bps, n), 1) // SEL_BLOCK
              == lax.broadcasted_iota(jnp.int32, (bps, n), 0))
    chosen = jnp.dot(selst_ref[j].astype(bf16), jnp.where(expand, 1.0, 0.0).astype(bf16),
                     preferred_element_type=f32)
    ok = tile_heads(chosen) > 0.5
    s = jnp.where(ok, s, NEG)
    m_old = m_ref[...]
    m_new = jnp.maximum(m_old, jnp.max(s, axis=1, keepdims=True))
    alpha = jnp.exp(m_old - m_new)
    p = jnp.where(ok, jnp.exp(s - m_new), 0.0)
    l_ref[...] = alpha * l_ref[...] + jnp.sum(p, axis=1, keepdims=True)
    m_ref[...] = m_new
    acc_ref[...] = alpha * acc_ref[...] + _nt(p.astype(bf16), kvT)

    @pl.when(j == n_steps - 1)
    def _():
        g = jax.nn.sigmoid(sm_in[0][:, SMALL_GATES:SMALL_GATES + 3 * H])
        o_slc = acc_ref[...] / l_ref[...]
        outs = []
        for h in range(H):
            rs = slice(h * Ts, (h + 1) * Ts)
            outs.append(g[:, 3 * h:3 * h + 1] * ocmp_ref[rs, d:2 * d]
                        + g[:, 3 * h + 1:3 * h + 2] * o_slc[rs, d:2 * d]
                        + g[:, 3 * h + 2:3 * h + 3] * owin_ref[rs, d:2 * d])
        o_ref[0] = jnp.concatenate(outs, axis=1)


def nsa_decode(page_table, qr, z3, cmp, slc_new, win_cache, win_new, pool_slc, *, layer, pos0):
    B, Ts, _ = qr.shape
    n_pages = page_table.shape[1]
    page = pool_slc.shape[3]
    n_past = n_pages * page
    lw = win_cache.shape[3]
    nc = cmp.shape[1]
    G = min(PAGES_PER_STEP_SLC, n_pages)
    n_steps = n_pages // G
    n_sel = -(-(n_past + Ts) // SEL_BLOCK)
    nsl = -(-n_sel // 128) * 128
    assert n_pages % G == 0 and (Ts & (Ts - 1)) == 0 and Ts % 8 == 0 and page % SEL_BLOCK == 0
    assert pos0 == n_past and n_past % SEL_BLOCK + Ts <= SEL_BLOCK and nc % 2 == 0 and nc // 2 <= nsl
    R = N_HEADS * Ts
    row = lambda w, cb=0: pl.BlockSpec((1, Ts, w), lambda b, j, pt, cb=cb: (b, 0, cb))
    kern = functools.partial(_nsa_decode_kernel, G=G, n_steps=n_steps, Ts=Ts, pos0=pos0, n_past=n_past,
                             wpos0=n_past - lw, nc=nc, nsl=nsl, n_sel=n_sel, topk=min(SEL_TOPK, n_sel),
                             page=page)
    buf = lambda: pltpu.VMEM((R, 128), f32)
    stat = lambda: pltpu.VMEM((R, 1), f32)
    return pl.pallas_call(
        kern,
        out_shape=jax.ShapeDtypeStruct((B, Ts, GROUP_WIDTH), f32),
        grid_spec=pltpu.PrefetchScalarGridSpec(
            num_scalar_prefetch=1, grid=(B, n_steps),
            in_specs=[row(256), row(256, CB_NS_Q), row(128, CB128_SMALL),
                      pl.BlockSpec((1, nc, 128), lambda b, j, pt: (b, 0, 0)),
                      row(128),
                      pl.BlockSpec((None, None, 128, lw), lambda b, j, pt: (layer, b, 0, 0)),
                      row(128)]
            + _page_specs(layer, n_pages, G, (128, page)),
            out_specs=row(256),
            scratch_shapes=[buf(), pltpu.VMEM((n_steps, Ts, G * page // SEL_BLOCK), f32), buf(), buf(),
                            stat(), stat(), buf()]),
        compiler_params=_cparams(("parallel", "arbitrary")),
        name="nsa_decode",
    )(page_table.reshape(-1), qr, z3, z3, cmp, slc_new, win_cache, win_new, *([pool_slc] * G))


def _mlstm_kernel(q_ref, k_ref, v_ref, og_ref, gc_ref, gr_ref, bias_ref, ng_ref, c0_ref, n0_ref, m0_ref,
                  o_ref, cout_ref, nout_ref, mout_ref, c_s, n_s, m_s, *, L):
    ci = pl.program_id(1)
    H, d = N_HEADS, HEAD_DIM

    @pl.when(ci == 0)
    def _():
        c_s[...] = c0_ref[0]
        n_s[...] = n0_ref[0]
        m_s[...] = m0_ref[0]

    q = q_ref[...]
    k = k_ref[...] * (d ** -0.5)
    v = v_ref[...]
    gc = gc_ref[...]
    gr = gr_ref[0]
    bias = bias_ref[...]
    t_i = lax.broadcasted_iota(jnp.int32, (L, L), 0)
    s_i = lax.broadcasted_iota(jnp.int32, (L, L), 1)
    causal = s_i <= t_i
    causal_T = t_i <= s_i
    outs = []
    for h in range(H):
        bi = bias[0:1, h:h + 1]
        bf_ = bias[1:2, h:h + 1]
        ig_c = gc[:, h:h + 1] + bi
        lf_c = jax.nn.log_sigmoid(gc[:, H + h:H + h + 1] + bf_)
        ig_r = gr[h:h + 1, :] + bi
        lf_r = jax.nn.log_sigmoid(gr[H + h:H + h + 1, :] + bf_)
        b_c = jnp.sum(jnp.where(causal, lf_r, 0.0), axis=1, keepdims=True)
        b_r = jnp.sum(jnp.where(causal_T, lf_c, 0.0), axis=0, keepdims=True)
        m_prev = m_s[h:h + 1, 0:1]
        D = jnp.where(causal, b_c - b_r + ig_r, -jnp.inf)
        m_t = jnp.maximum(b_c + m_prev, jnp.max(D, axis=1, keepdims=True))
        inter = jnp.exp(b_c + m_prev - m_t)
        qh = q[:, d * h:d * (h + 1)]
        kh = k[:, d * h:d * (h + 1)]
        vh = v[:, d * h:d * (h + 1)]
        qb = qh.astype(bf16)
        S = _nt(qb, kh.astype(bf16))
        Sw = jnp.exp(D - m_t) * S
        C = c_s[h]
        n_row = n_s[h:h + 1, :]
        num = (jnp.dot(Sw.astype(bf16), vh.astype(bf16), preferred_element_type=f32)
               + inter * jnp.dot(qb, C.astype(bf16), preferred_element_type=f32))
        qn = jnp.sum(qh * n_row, axis=1, keepdims=True)
        den = jnp.sum(Sw, axis=1, keepdims=True) + inter * qn
        hh = num / jnp.maximum(jnp.abs(den), jnp.exp(-m_t))
        m_new = m_t[L - 1:L, :]
        b_last = b_c[L - 1:L, :]
        wl = jnp.exp(b_last - b_c + ig_c - m_new)
        decay = jnp.exp(b_last + m_prev - m_new)
        kw = kh * wl
        c_s[h] = decay * C + lax.dot_general(kw.astype(bf16), vh.astype(bf16), (((0,), (0,)), ((), ())),
                                             preferred_element_type=f32)
        n_s[h:h + 1, :] = decay * n_row + jnp.sum(kw, axis=0, keepdims=True)
        m_s[h:h + 1, :] = jnp.broadcast_to(m_new, (1, 128))
        mu = jnp.mean(hh, axis=1, keepdims=True)
        var = jnp.mean(jnp.square(hh - mu), axis=1, keepdims=True)
        outs.append((hh - mu) * lax.rsqrt(var + EPS))
    hcat = jnp.concatenate(outs, axis=1) * ng_ref[...]
    o_ref[...] = (hcat * jax.nn.sigmoid(og_ref[...])).astype(o_ref.dtype)
    cout_ref[0] = c_s[...]
    nout_ref[0] = n_s[...]
    mout_ref[0] = m_s[...]


def mlstm(z, gates_T, gate_b, norm_g, C0, n0, m0, *, B, T, L):
    nchunk = T // L
    m0p = jnp.broadcast_to(m0[:, :, None], (B, N_HEADS, 128))
    row = lambda cb: pl.BlockSpec((L, 256), lambda b, c, cb=cb: (b * nchunk + c, cb))
    st = lambda *shape: pl.BlockSpec((1,) + shape, lambda b, c: (b,) + (0,) * len(shape))
    out, C, n, m = pl.pallas_call(
        functools.partial(_mlstm_kernel, L=L),
        out_shape=[jax.ShapeDtypeStruct((B * T, GROUP_WIDTH), _act_dtype(L)),
                   jax.ShapeDtypeStruct((B, N_HEADS, HEAD_DIM, HEAD_DIM), f32),
                   jax.ShapeDtypeStruct((B, N_HEADS, HEAD_DIM), f32),
                   jax.ShapeDtypeStruct((B, N_HEADS, 128), f32)],
        grid=(B, nchunk),
        in_specs=[row(CB_ML_Q), row(CB_ML_K), row(CB_ML_V), row(CB_ML_O),
                  pl.BlockSpec((L, 128), lambda b, c: (b * nchunk + c, CB128_SMALL)),
                  pl.BlockSpec((1, 8, L), lambda b, c: (b * nchunk + c, 0, 0)),
                  pl.BlockSpec((2, N_HEADS), lambda b, c: (0, 0)),
                  pl.BlockSpec((1, GROUP_WIDTH), lambda b, c: (0, 0)),
                  st(N_HEADS, HEAD_DIM, HEAD_DIM), st(N_HEADS, HEAD_DIM), st(N_HEADS, 128)],
        out_specs=[pl.BlockSpec((L, GROUP_WIDTH), lambda b, c: (b * nchunk + c, 0)),
                   st(N_HEADS, HEAD_DIM, HEAD_DIM), st(N_HEADS, HEAD_DIM), st(N_HEADS, 128)],
        scratch_shapes=[pltpu.VMEM((N_HEADS, HEAD_DIM, HEAD_DIM), f32), pltpu.VMEM((N_HEADS, HEAD_DIM), f32),
                        pltpu.VMEM((N_HEADS, 128), f32)],
        compiler_params=_cparams(("parallel", "arbitrary")),
        name="mlstm",
    )(z, z, z, z, z, gates_T, gate_b, norm_g.reshape(1, GROUP_WIDTH), C0, n0, m0p)
    return out, C, n, m[:, :, 0]


def _rglru_kernel(x_ref, gate_ref, cw_ref, cb_ref, wa_ref, ba_ref, wx_ref, bx_ref, lam_ref, h0_ref, buf0_ref,
                  y_ref, hout_ref, bufout_ref, xbuf, a_s, u_s, h_s, hs_s, *, tm):
    ti = pl.program_id(0)
    Bb = x_ref.shape[0]
    W = GROUP_WIDTH

    @pl.when(ti == 0)
    def _():
        xbuf[:, 0:8, :] = buf0_ref[...]
        h_s[...] = h0_ref[...]

    xbuf[:, 8:8 + tm, :] = x_ref[...]
    cw = cw_ref[...]
    xc = cb_ref[...] + cw[0:1, :] * xbuf[:, 5:5 + tm, :]
    for j in range(1, CONV_W):
        xc = xc + cw[j:j + 1, :] * xbuf[:, 5 + j:5 + j + tm, :]
    flat = xc.reshape(Bb * tm, W).astype(bf16)
    r = jax.nn.sigmoid(jnp.dot(flat, wa_ref[...], preferred_element_type=f32) + ba_ref[...])
    i = jax.nn.sigmoid(jnp.dot(flat, wx_ref[...], preferred_element_type=f32) + bx_ref[...])
    lam = lam_ref[...]
    softplus = jnp.maximum(-lam, 0.0) + jnp.log1p(jnp.exp(-jnp.abs(lam)))
    log_a = (-LRU_C * r) * softplus
    a = jnp.exp(log_a)
    u = jnp.sqrt(-jnp.tanh(log_a) * (a * a + 1.0)) * (i * xc.reshape(Bb * tm, W))
    a_s[...] = a.reshape(Bb, tm, W)
    u_s[...] = u.reshape(Bb, tm, W)

    def body(t, h):
        h = a_s[:, pl.ds(t, 1), :] * h + u_s[:, pl.ds(t, 1), :]
        hs_s[:, pl.ds(t, 1), :] = h
        return h

    h_last = lax.fori_loop(0, tm, body, h_s[...])
    h_s[...] = h_last
    y_ref[...] = (hs_s[...] * jax.nn.gelu(gate_ref[...])).astype(y_ref.dtype)
    hout_ref[...] = h_last
    bufout_ref[...] = xbuf[:, tm:tm + 8, :]
    xbuf[:, 0:8, :] = xbuf[:, tm:tm + 8, :]


def rglru(z3, conv_w, conv_b, wa_bd, ba, wx_bd, bx, lam, h0, buf0, *, tm):
    B, T, _ = z3.shape
    tm = min(tm, T)
    assert T % tm == 0 and tm % 8 == 0
    W = GROUP_WIDTH
    buf8 = jnp.concatenate([jnp.zeros((B, 8 - (CONV_W - 1), W), f32), buf0], 1)
    vec = lambda: pl.BlockSpec((1, W), lambda i: (0, 0))
    mat = lambda: pl.BlockSpec((W, W), lambda i: (0, 0))
    y, h, buf = pl.pallas_call(
        functools.partial(_rglru_kernel, tm=tm),
        out_shape=[jax.ShapeDtypeStruct((B, T, W), _act_dtype(tm)), jax.ShapeDtypeStruct((B, 1, W), f32),
                   jax.ShapeDtypeStruct((B, 8, W), f32)],
        grid=(T // tm,),
        in_specs=[pl.BlockSpec((B, tm, W), lambda i: (0, i, CB_LR_X)),
                  pl.BlockSpec((B, tm, W), lambda i: (0, i, CB_LR_G)),
                  pl.BlockSpec((CONV_W, W), lambda i: (0, 0)), vec(), mat(), vec(), mat(), vec(), vec(),
                  pl.BlockSpec((B, 1, W), lambda i: (0, 0, 0)), pl.BlockSpec((B, 8, W), lambda i: (0, 0, 0))],
        out_specs=[pl.BlockSpec((B, tm, W), lambda i: (0, i, 0)),
                   pl.BlockSpec((B, 1, W), lambda i: (0, 0, 0)), pl.BlockSpec((B, 8, W), lambda i: (0, 0, 0))],
        scratch_shapes=[pltpu.VMEM((B, tm + 8, W), f32), pltpu.VMEM((B, tm, W), f32), pltpu.VMEM((B, tm, W), f32),
                        pltpu.VMEM((B, 1, W), f32), pltpu.VMEM((B, tm, W), f32)],
        compiler_params=_cparams(("arbitrary",)),
        name="rglru",
    )(z3, z3, conv_w, conv_b.reshape(1, W), wa_bd, ba.reshape(1, W), wx_bd, bx.reshape(1, W),
      lam.reshape(1, W), h0.reshape(B, 1, W), buf8)
    return y, h[:, 0], buf[:, 8 - (CONV_W - 1):]


def _mem_attn_kernel(q_ref, k_ref, v_ref, o_ref):
    q = q_ref[0]
    lead = (0,) * (len(k_ref.shape) - 2)
    k = k_ref[lead]
    v = v_ref[lead]
    dh = MEM_HEAD_DIM
    for h in range(MEM_HEADS):
        cs = slice(dh * h, dh * (h + 1))
        s = _nt(q[:, cs].astype(bf16), k[:, cs].astype(bf16)) * (dh ** -0.5)
        e = jnp.exp(s - jnp.max(s, axis=-1, keepdims=True))
        p = e / jnp.sum(e, axis=-1, keepdims=True)
        o_ref[0, :, cs] = jnp.dot(p.astype(bf16), v[:, cs].astype(bf16),
                                  preferred_element_type=f32).astype(o_ref.dtype)


def mem_attention(q, k, v, *, layer=None, tq=256):
    B, T, D = q.shape
    tq = min(tq, T)
    M = k.shape[-2]
    if layer is None:
        kv_spec = pl.BlockSpec((1, M, D), lambda b, i: (b, 0, 0))
    else:
        kv_spec = pl.BlockSpec((1, 1, M, D), lambda b, i: (layer, b, 0, 0))
    return pl.pallas_call(
        _mem_attn_kernel,
        out_shape=jax.ShapeDtypeStruct((B, T, D), _act_dtype(tq)),
        grid=(B, T // tq),
        in_specs=[pl.BlockSpec((1, tq, D), lambda b, i: (b, i, 0)), kv_spec, kv_spec],
        out_specs=pl.BlockSpec((1, tq, D), lambda b, i: (b, i, 0)),
        compiler_params=_cparams(("parallel", "parallel")),
        name="mem_attention",
    )(q, k, v)


def _even_odd(cmp):
    B, nc, w = cmp.shape
    return jnp.swapaxes(cmp.reshape(B, nc // 2, 2, w), 1, 2).reshape(B, nc, w)


def _layer(x, lw, lam_init, tables, *, B, T, mem, ml_state, lr_state, paged, cfg):
    N = B * T
    z = matmul(x, lw['w_in'], g=lw['g_mix'], tm=512)
    z3 = z.reshape(B, T, D_PROJ_PAD)
    dfq, dfk, nsq, slc_new, win_new = rope_prep(z, tables, cfg['rope_tm'])
    df_v = z[:, 256 * CB_DF_V:256 * (CB_DF_V + 1)]
    cmp_new = z[:, 128 * CB128_CMP:128 * (CB128_CMP + 1)]
    small = z[:, 128 * CB128_SMALL:128 * CB128_SMALL + SMALL_GATES]
    r3 = lambda a: a.reshape(B, T, a.shape[-1])

    L = cfg['ml_chunk']
    gates_T = jnp.swapaxes(small.reshape(N // L, L, SMALL_GATES), 1, 2)
    o_ml, ml_C, ml_n, ml_m = mlstm(z, gates_T, lw['ml_gate_b'], lw['ml_norm_g'], *ml_state, B=B, T=T, L=L)

    if paged is None:
        o_df = diff_attention(r3(dfq), r3(dfk), z3, lw['df_lam'], lw['df_norm_g'], lam_init=lam_init,
                              tq=cfg['df_tq'], tk=cfg['df_tk'])
        comp = nsa_compress(cmp_new.reshape(N // CMP_BLOCK, CMP_BLOCK * 128), lw['nsa_pos_flat'],
                            lw['nsa_w1c'], lw['nsa_w2c']).reshape(B, T // CMP_BLOCK, 128)
        o_ns = nsa_attention(r3(nsq), z3, _even_odd(comp), r3(slc_new), r3(win_new),
                             tq=cfg['ns_tq'], tk=cfg['ns_tk'], tkw=cfg['ns_tkw'])
        win_state = r3(win_new)[:, T - min(WINDOW, T):]
    else:
        pt, l = paged['page_table'], paged['layer']
        n_past = pt.shape[1] * paged['df_k'].shape[3]
        assert (n_past + T) // CMP_BLOCK == n_past // CMP_BLOCK
        o_df = diff_decode(pt, r3(dfq), r3(dfk), r3(df_v), paged['df_k'], paged['df_v'], lw['df_lam'],
                           lw['df_norm_g'], layer=l, pos0=n_past, lam_init=lam_init)
        comp = nsa_compress_paged(pt, paged['nsa_cmp'], lw['nsa_pos_flat'], lw['nsa_w1c'], lw['nsa_w2c'], layer=l)
        o_ns = nsa_decode(pt, r3(nsq), z3, _even_odd(comp), r3(slc_new), paged['nsa_win'], r3(win_new),
                          paged['nsa_slc'], layer=l, pos0=n_past)
        win_cat = jnp.concatenate([paged['nsa_win'][l], jnp.swapaxes(r3(win_new), 1, 2)], 2)
        win_state = jnp.swapaxes(win_cat[:, :, win_cat.shape[2] - min(WINDOW, win_cat.shape[2]):], 1, 2)

    o_lr, lr_h, lr_buf = rglru(z3, lw['lru_conv_w'], lw['lru_conv_b'], lw['lru_wa_bd'],
                               lw['lru_ba'], lw['lru_wx_bd'], lw['lru_bx'], lw['lru_lambda'], *lr_state,
                               tm=cfg['lru_tm'])

    x = matmul([o_ml, o_df.reshape(N, 256), o_ns.reshape(N, 256), o_lr.reshape(N, 256)], lw['w_out'], res=x,
               tm=1024)
    q = matmul(x, lw['w_mq'], g=lw['g_mem_q'], tm=1024, out_dtype=_act_dtype(cfg['mem_tq']))
    if isinstance(mem, tuple):
        att = mem_attention(q.reshape(B, T, D_MODEL), mem[0], mem[1], tq=cfg['mem_tq'])
    else:
        att = mem_attention(q.reshape(B, T, D_MODEL), mem['k'], mem['v'], layer=mem['layer'], tq=cfg['mem_tq'])
    x = matmul(att.reshape(N, D_MODEL), lw['w_mo'], res=x, tm=1024)
    u = matmul(x, lw['w_up'], g=lw['g_mlp'], act='relu2', out_dtype=bf16, tm=512)
    x = matmul(u, lw['w_down'], res=x, tm=512)
    new = dict(df_k=dfk.reshape(B, T, N_HEADS, HEAD_DIM), df_v=df_v.reshape(B, T, N_HEADS, HEAD_DIM),
               nsa_cmp=cmp_new.reshape(B, T, 2, HEAD_DIM), nsa_slc=slc_new.reshape(B, T, 2, HEAD_DIM),
               nsa_win=win_state.reshape(B, -1, 2, HEAD_DIM), ml_C=ml_C, ml_n=ml_n, ml_m=ml_m,
               lru_h=lr_h, lru_conv=lr_buf)
    return x, new


def _block_diag(w):
    H, a, b = w.shape
    eye = jnp.eye(H, dtype=w.dtype)
    return jnp.einsum('hij,hg->higj', w, eye).reshape(H * a, H * b)


def kernel(x_prompt, x_sample, mem_prompt, cache_df_k, cache_df_v, cache_nsa_cmp, cache_nsa_slc, cache_nsa_win, state_ml_C, state_ml_n, state_ml_m, state_lru_h, state_lru_conv, cache_mem_k, cache_mem_v, page_table, g_mix, w_in, w_out, ml_gate_b, ml_norm_g, df_lam, df_norm_g, nsa_pos, nsa_w1, nsa_w2, lru_conv_w, lru_conv_b, lru_wa, lru_ba, lru_wx, lru_bx, lru_lambda, g_mem_q, g_mem_kv, w_mq, w_mk, w_mv, w_mo, g_mlp, w_up, w_down, g_final):
    Bp, Tp, _ = x_prompt.shape
    Bs, Ts, _ = x_sample.shape
    depth = w_in.shape[0]
    n_pool, page = cache_df_k.shape[1], cache_df_k.shape[2]
    n_past = page_table.shape[1] * page
    M = mem_prompt.shape[1]

    perm, n_real = _proj_perm()
    col_ok = (jnp.arange(D_PROJ_PAD) < n_real)
    w_in_p = jnp.where(col_ok[None, None, :], jnp.take(w_in, perm, axis=2), 0.0).astype(bf16)
    eye2 = jnp.eye(2, dtype=f32)
    w1 = nsa_w1.reshape(depth, 2, CMP_BLOCK, HEAD_DIM, CMP_HIDDEN)
    w1c = jnp.einsum('lstih,sg->ltsigh', w1, eye2).reshape(depth, CMP_BLOCK * 2 * HEAD_DIM, 2 * CMP_HIDDEN)
    w2c = jnp.einsum('lshd,sg->lshgd', nsa_w2, eye2).reshape(depth, 2 * CMP_HIDDEN, 2 * HEAD_DIM)
    cfg_p = dict(rope_tm=512, ml_chunk=512, df_tq=512, df_tk=512, ns_tq=512, ns_tk=512, ns_tkw=256,
                 lru_tm=256, mem_tq=256)
    cfg_s = dict(rope_tm=Bs * Ts, ml_chunk=Ts, lru_tm=Ts, mem_tq=Ts)
    tab_p = rope_tables(jnp.arange(Tp))
    tab_s = tuple(jnp.tile(t, (Bs, 1)) for t in rope_tables(n_past + jnp.arange(Ts)))

    tok_minor = lambda c: jnp.transpose(c, (0, 1, 3, 4, 2)).reshape(c.shape[0], c.shape[1], -1, c.shape[2])
    pool_df_k = tok_minor(cache_df_k)
    pool_df_v = tok_minor(cache_df_v)
    pool_cmp = tok_minor(cache_nsa_cmp)
    pool_slc = tok_minor(cache_nsa_slc)
    win_cache = tok_minor(cache_nsa_win)
    mem_k_cache = cache_mem_k.reshape(depth, Bs, M, D_MODEL)
    mem_v_cache = cache_mem_v.reshape(depth, Bs, M, D_MODEL)
    mem_rows = mem_prompt.reshape(Bp * M, D_MODEL)

    xp = x_prompt.reshape(Bp * Tp, D_MODEL)
    xs = x_sample.reshape(Bs * Ts, D_MODEL)
    names = ('df_k', 'df_v', 'nsa_cmp', 'nsa_slc', 'nsa_win', 'ml_C', 'ml_n', 'ml_m', 'lru_h', 'lru_conv')
    acc = {pre + n: [] for n in names for pre in ('p_', 's_')}
    acc['p_mem_k'] = []
    acc['p_mem_v'] = []
    for l in range(depth):
        lw = dict(g_mix=g_mix[l], w_in=w_in_p[l], w_out=w_out[l].astype(bf16), ml_gate_b=ml_gate_b[l],
                  ml_norm_g=ml_norm_g[l], df_lam=df_lam[l], df_norm_g=df_norm_g[l],
                  nsa_pos_flat=nsa_pos[l].reshape(1, CMP_BLOCK * 2 * HEAD_DIM), nsa_w1c=w1c[l].astype(bf16),
                  nsa_w2c=w2c[l].astype(bf16), lru_conv_w=lru_conv_w[l], lru_conv_b=lru_conv_b[l],
                  lru_wa_bd=_block_diag(lru_wa[l]).astype(bf16), lru_ba=lru_ba[l],
                  lru_wx_bd=_block_diag(lru_wx[l]).astype(bf16), lru_bx=lru_bx[l], lru_lambda=lru_lambda[l],
                  g_mem_q=g_mem_q[l], w_mq=w_mq[l].astype(bf16), w_mo=w_mo[l].astype(bf16), g_mlp=g_mlp[l],
                  w_up=w_up[l].astype(bf16), w_down=w_down[l].astype(bf16))
        lam_init = 0.8 - 0.6 * math.exp(-0.3 * l)
        mk_p = matmul(mem_rows, w_mk[l].astype(bf16), g=g_mem_kv[l])
        mv_p = matmul(mem_rows, w_mv[l].astype(bf16), g=g_mem_kv[l])
        ml0 = (jnp.zeros((Bp, N_HEADS, HEAD_DIM, HEAD_DIM), f32), jnp.zeros((Bp, N_HEADS, HEAD_DIM), f32),
               jnp.zeros((Bp, N_HEADS), f32))
        lr0 = (jnp.zeros((Bp, GROUP_WIDTH), f32), jnp.zeros((Bp, CONV_W - 1, GROUP_WIDTH), f32))
        xp, new_p = _layer(xp, lw, lam_init, tab_p, B=Bp, T=Tp,
                           mem=(mk_p.reshape(Bp, M, D_MODEL), mv_p.reshape(Bp, M, D_MODEL)),
                           ml_state=ml0, lr_state=lr0, paged=None, cfg=cfg_p)
        paged = dict(page_table=page_table, layer=l, df_k=pool_df_k, df_v=pool_df_v, nsa_cmp=pool_cmp,
                     nsa_slc=pool_slc, nsa_win=win_cache)
        xs, new_s = _layer(xs, lw, lam_init, tab_s, B=Bs, T=Ts,
                           mem=dict(k=mem_k_cache, v=mem_v_cache, layer=l),
                           ml_state=(state_ml_C[l], state_ml_n[l], state_ml_m[l]),
                           lr_state=(state_lru_h[l], state_lru_conv[l]), paged=paged, cfg=cfg_s)
        for pre, new in (('p_', new_p), ('s_', new_s)):
            for n in names:
                acc[pre + n].append(new[n])
        acc['p_mem_k'].append(mk_p.reshape(Bp, M, MEM_HEADS, MEM_HEAD_DIM))
        acc['p_mem_v'].append(mv_p.reshape(Bp, M, MEM_HEADS, MEM_HEAD_DIM))
    st = {k: jnp.stack(v) for k, v in acc.items()}
    y_prompt = rmsnorm_rows(xp, g_final).reshape(Bp, Tp, D_MODEL)
    y_sample = rmsnorm_rows(xs, g_final).reshape(Bs, Ts, D_MODEL)
    return (y_prompt, y_sample, st['p_df_k'], st['s_df_k'], st['p_df_v'], st['s_df_v'],
            st['p_nsa_cmp'], st['s_nsa_cmp'], st['p_nsa_slc'], st['s_nsa_slc'],
            st['p_nsa_win'], st['s_nsa_win'], st['p_ml_C'], st['s_ml_C'], st['p_ml_n'], st['s_ml_n'],
            st['p_ml_m'], st['s_ml_m'], st['p_lru_h'], st['s_lru_h'], st['p_lru_conv'], st['s_lru_conv'],
            st['p_mem_k'], st['p_mem_v'])
```

```python
import functools
import math

import jax
import jax.numpy as jnp
from jax import lax
from jax.experimental import pallas as pl
from jax.experimental.pallas import tpu as pltpu

f32 = jnp.float32
bf16 = jnp.bfloat16

D_MODEL = 1024
GROUP_WIDTH = 256
N_HEADS = 4
HEAD_DIM = 64
DQK = 32
ROPE_THETA = 10000.0
CMP_BLOCK = 32
CMP_HIDDEN = 128
SEL_BLOCK = 64
SEL_TOPK = 16
WINDOW = 512
CONV_W = 4
LRU_C = 8.0
MEM_HEADS = 4
MEM_HEAD_DIM = 256
EPS = 1e-6
NEG = -1e30
FORCE_SCORE = 1e9
TINY = 1e-30

VMEM_LIMIT_BYTES = 48 * 1024 * 1024
PAGES_PER_STEP = 32
PAGES_PER_STEP_SLC = 32
PAGES_PER_STEP_CMP = 64

_SRC_SPLITS = (
    ('ml_q', 256), ('ml_k', 256), ('ml_v', 256), ('ml_i', 4), ('ml_f', 4), ('ml_o', 256),
    ('df_q', 256), ('df_k', 256), ('df_v', 256),
    ('ns_q', 256), ('ns_kc', 64), ('ns_vc', 64), ('ns_ks', 64), ('ns_vs', 64),
    ('ns_kw', 64), ('ns_vw', 64), ('ns_g', 12), ('lr_x', 256), ('lr_g', 256),
)
_DST_ORDER = ('ml_q', 'ml_k', 'ml_v', 'ml_o', 'df_q', 'df_k', 'df_v', 'ns_q', 'lr_x', 'lr_g',
              'ns_kc', 'ns_vc', 'ns_ks', 'ns_vs', 'ns_kw', 'ns_vw', 'ml_i', 'ml_f', 'ns_g')
D_PROJ_PAD = 3072
CB_ML_Q, CB_ML_K, CB_ML_V, CB_ML_O, CB_DF_Q, CB_DF_K, CB_DF_V, CB_NS_Q, CB_LR_X, CB_LR_G = range(10)
CB128_CMP, CB128_SLC, CB128_WIN, CB128_SMALL = 20, 21, 22, 23
SMALL_GATES = 8


def _proj_perm():
    off = {}
    o = 0
    for name, w in _SRC_SPLITS:
        off[name] = (o, w)
        o += w
    idx = []
    for name in _DST_ORDER:
        s, w = off[name]
        idx.extend(range(s, s + w))
    n_real = len(idx)
    idx.extend([0] * (D_PROJ_PAD - n_real))
    return jnp.asarray(idx, jnp.int32), n_real


def _cparams(sem):
    return pltpu.CompilerParams(dimension_semantics=sem, vmem_limit_bytes=VMEM_LIMIT_BYTES)


def _act_dtype(rows):
    return bf16 if rows % 16 == 0 else f32


def _nt(a, b):
    return lax.dot_general(a, b, (((1,), (1,)), ((), ())), preferred_element_type=f32)


def _mm_kernel(*refs, n_parts, has_norm, has_res, act, tn):
    it = iter(refs)
    x_refs = [next(it) for _ in range(n_parts)]
    w_ref = next(it)
    g_ref = next(it) if has_norm else None
    r_ref = next(it) if has_res else None
    o_ref = next(it)
    h_ref = next(it)
    if has_norm:
        x = x_refs[0][...].astype(f32)
        y = x * lax.rsqrt(jnp.mean(x * x, axis=-1, keepdims=True) + EPS)
        h_ref[...] = (y * g_ref[...]).astype(bf16)
    elif n_parts > 1 or x_refs[0].dtype != bf16:
        off = 0
        for x_ref in x_refs:
            kp = x_ref.shape[-1]
            h_ref[:, off:off + kp] = x_ref[...].astype(bf16)
            off += kp
    else:
        h_ref = x_refs[0]
    N = o_ref.shape[-1]
    for c0 in range(0, N, tn):
        cs = slice(c0, min(c0 + tn, N))
        acc = jnp.dot(h_ref[...], w_ref[:, cs], preferred_element_type=f32)
        if act == 'relu2':
            acc = jnp.maximum(acc, 0.0)
            acc = acc * acc
        if has_res:
            acc = acc + r_ref[:, cs]
        o_ref[:, cs] = acc.astype(o_ref.dtype)


def matmul(x, w, *, g=None, res=None, act=None, out_dtype=f32, tm=512, tn=512):
    parts = list(x) if isinstance(x, (list, tuple)) else [x]
    M = parts[0].shape[0]
    K, N = w.shape
    assert sum(p.shape[1] for p in parts) == K
    tm = min(tm, M)
    assert M % tm == 0
    has_norm = g is not None
    has_res = res is not None
    assert not (has_norm and len(parts) > 1)
    in_specs = [pl.BlockSpec((tm, p.shape[1]), lambda i: (i, 0)) for p in parts]
    in_specs.append(pl.BlockSpec((K, N), lambda i: (0, 0)))
    args = parts + [w]
    if has_norm:
        in_specs.append(pl.BlockSpec((1, K), lambda i: (0, 0)))
        args.append(g.reshape(1, K).astype(f32))
    if has_res:
        in_specs.append(pl.BlockSpec((tm, N), lambda i: (i, 0)))
        args.append(res)
    return pl.pallas_call(
        functools.partial(_mm_kernel, n_parts=len(parts), has_norm=has_norm, has_res=has_res, act=act, tn=tn),
        out_shape=jax.ShapeDtypeStruct((M, N), out_dtype),
        grid=(M // tm,),
        in_specs=in_specs,
        out_specs=pl.BlockSpec((tm, N), lambda i: (i, 0)),
        scratch_shapes=[pltpu.VMEM((tm, K), bf16)],
        compiler_params=_cparams(("parallel",)),
        name="matmul",
    )(*args)


def _rmsnorm_kernel(x_ref, g_ref, o_ref):
    x = x_ref[...]
    y = x * lax.rsqrt(jnp.mean(x * x, axis=-1, keepdims=True) + EPS)
    o_ref[...] = y * g_ref[...]


def rmsnorm_rows(x, g, tm=1024):
    M, K = x.shape
    tm = min(tm, M)
    return pl.pallas_call(
        _rmsnorm_kernel,
        out_shape=jax.ShapeDtypeStruct((M, K), f32),
        grid=(M // tm,),
        in_specs=[pl.BlockSpec((tm, K), lambda i: (i, 0)), pl.BlockSpec((1, K), lambda i: (0, 0))],
        out_specs=pl.BlockSpec((tm, K), lambda i: (i, 0)),
        compiler_params=_cparams(("parallel",)),
        name="final_norm",
    )(x, g.reshape(1, K))


def _rotate(x, cos, sin, half):
    n = x.shape[-1]
    lane = lax.broadcasted_iota(jnp.int32, x.shape, 1)
    first = (lane & (2 * half - 1)) < half
    partner = jnp.where(first, pltpu.roll(x, n - half, 1), pltpu.roll(x, half, 1))
    return x * cos + partner * sin


def _rope_kernel(dq, dk, nq, sl, wn, ca, sa, cb, sb, cc, sc, odq, odk, onq, osl, own):
    odq[...] = _rotate(dq[...], ca[...], sa[...], DQK // 2)
    odk[...] = _rotate(dk[...], ca[...], sa[...], DQK // 2)
    onq[...] = _rotate(nq[...], cb[...], sb[...], HEAD_DIM // 2)
    osl[...] = _rotate(sl[...], cc[...], sc[...], HEAD_DIM // 2)
    own[...] = _rotate(wn[...], cc[...], sc[...], HEAD_DIM // 2)


def rope_tables(pos):
    posf = pos.astype(f32)[:, None]

    def tab(half, reps):
        inv = ROPE_THETA ** (-jnp.arange(half, dtype=f32) / half)
        ang = posf * inv[None, :]
        c = jnp.cos(ang)
        s = jnp.sin(ang)
        return jnp.tile(jnp.concatenate([c, c], 1), (1, reps)), jnp.tile(jnp.concatenate([-s, s], 1), (1, reps))

    ca, sa = tab(DQK // 2, GROUP_WIDTH // DQK)
    cb, sb = tab(HEAD_DIM // 2, N_HEADS)
    n = pos.shape[0]
    cc = jnp.concatenate([cb[:, :HEAD_DIM], jnp.ones((n, HEAD_DIM), f32)], 1)
    sc = jnp.concatenate([sb[:, :HEAD_DIM], jnp.zeros((n, HEAD_DIM), f32)], 1)
    return ca, sa, cb, sb, cc, sc


def rope_prep(z, tables, tm):
    N = z.shape[0]
    R = tables[0].shape[0]
    tm = min(tm, R)
    assert R % tm == 0 and N % tm == 0
    nr = R // tm
    zs = lambda cb: pl.BlockSpec((tm, 256), lambda i, cb=cb: (i, cb))
    zs128 = lambda cb: pl.BlockSpec((tm, 128), lambda i, cb=cb: (i, cb))
    t256 = pl.BlockSpec((tm, 256), lambda i: (i % nr, 0))
    t128 = pl.BlockSpec((tm, 128), lambda i: (i % nr, 0))
    o256 = pl.BlockSpec((tm, 256), lambda i: (i, 0))
    o128 = pl.BlockSpec((tm, 128), lambda i: (i, 0))
    return pl.pallas_call(
        _rope_kernel,
        out_shape=[jax.ShapeDtypeStruct((N, 256), f32)] * 3 + [jax.ShapeDtypeStruct((N, 128), f32)] * 2,
        grid=(N // tm,),
        in_specs=[zs(CB_DF_Q), zs(CB_DF_K), zs(CB_NS_Q), zs128(CB128_SLC), zs128(CB128_WIN),
                  t256, t256, t256, t256, t128, t128],
        out_specs=[o256, o256, o256, o128, o128],
        compiler_params=_cparams(("parallel",)),
        name="rope_prep",
    )(z, z, z, z, z, *tables)


ROW_CHUNK = 16
LANE_CHUNK = 512


ONES_ROWS = 8


def _softmax_tile(s, s_ref, pb_ref, m_ref):
    rows, n = s.shape
    s_ref[0:rows, :] = s
    m_old = m_ref[...]
    m_new = jnp.maximum(m_old, jnp.max(s, axis=0, keepdims=True))
    alpha = jnp.exp(m_old - m_new)
    m_ref[...] = m_new
    for c0 in range(0, n, LANE_CHUNK):
        cs = slice(c0, c0 + LANE_CHUNK)
        mb = jnp.broadcast_to(m_new[:, cs], (ROW_CHUNK, LANE_CHUNK))
        for r in range(0, rows, ROW_CHUNK):
            pb_ref[r:r + ROW_CHUNK, cs] = jnp.exp((s_ref[r:r + ROW_CHUNK, cs] - mb).astype(bf16))
    return alpha


def _with_ones(vT):
    return jnp.concatenate([vT, jnp.ones((ONES_ROWS, vT.shape[1]), bf16)], axis=0)


def _diff_lambda(lam_ref, lam_init):
    lq = lam_ref[...]
    return (jnp.exp(jnp.sum(lq[0:1] * lq[1:2], keepdims=True))
            - jnp.exp(jnp.sum(lq[2:3] * lq[3:4], keepdims=True)) + lam_init)


def _diff_kernel(q_ref, k_ref, v_ref, lam_ref, g_ref, o_ref, qp_ref, m_ref, l_ref, acc_ref, oT_ref,
                 s_ref, pb_ref, *, tq, tk, nk, lam_init):
    qi = pl.program_id(1)
    kj = pl.program_id(2)
    q_lo = qi * tq
    needed = (q_lo + tq - 1) // tk + 1
    nmap = 2 * N_HEADS

    @pl.when(kj == 0)
    def _():
        qT = jnp.transpose(q_ref[0]) * (DQK ** -0.5)
        row = lax.broadcasted_iota(jnp.int32, qT.shape, 0)
        for c in range(nmap):
            blk = jnp.where((row >= DQK * c) & (row < DQK * (c + 1)), qT, 0.0)
            qp_ref[:, c * tq:(c + 1) * tq] = blk.astype(bf16)
        m_ref[...] = jnp.full(m_ref.shape, NEG, f32)
        l_ref[...] = jnp.zeros(l_ref.shape, f32)
        acc_ref[...] = jnp.zeros(acc_ref.shape, f32)

    def step(masked):
        k = k_ref[0].astype(bf16)
        s = jnp.dot(k, qp_ref[...], preferred_element_type=f32)
        if masked:
            kpos = kj * tk + lax.broadcasted_iota(jnp.int32, s.shape, 0)
            qpos = q_lo + (lax.broadcasted_iota(jnp.int32, s.shape, 1) & (tq - 1))
            ok = kpos <= qpos
            s = jnp.where(ok, s, NEG)
        alpha = _softmax_tile(s, s_ref, pb_ref, m_ref)
        vT = jnp.transpose(v_ref[0]).astype(bf16)
        for h in range(N_HEADS):
            rs = slice(HEAD_DIM * h, HEAD_DIM * (h + 1))
            cs = slice(2 * tq * h, 2 * tq * (h + 1))
            pv = jnp.dot(_with_ones(vT[rs, :]), pb_ref[:, cs], preferred_element_type=f32)
            acc_ref[rs, :] = acc_ref[rs, :] * alpha[:, cs] + pv[0:HEAD_DIM, :]
            l_ref[:, cs] = l_ref[:, cs] * alpha[:, cs] + pv[HEAD_DIM:HEAD_DIM + 1, :]

    active = kj < needed
    crosses = kj * tk + tk - 1 > q_lo

    @pl.when(active & crosses)
    def _():
        step(True)

    @pl.when(active & jnp.logical_not(crosses))
    def _():
        step(False)

    @pl.when(kj == nk - 1)
    def _():
        lam = _diff_lambda(lam_ref, lam_init)
        l = l_ref[...]
        for h in range(N_HEADS):
            rs = slice(HEAD_DIM * h, HEAD_DIM * (h + 1))
            c1 = slice(2 * h * tq, (2 * h + 1) * tq)
            c2 = slice((2 * h + 1) * tq, (2 * h + 2) * tq)
            o = acc_ref[rs, 0:tq] / l[:, c1] - lam * (acc_ref[rs, tq:2 * tq] / l[:, c2])
            y = o * lax.rsqrt(jnp.mean(o * o, axis=0, keepdims=True) + EPS)
            oT_ref[rs, :] = (y * g_ref[...]) * (1.0 - lam_init)
        o_ref[0] = jnp.transpose(oT_ref[...]).astype(o_ref.dtype)


def diff_attention(q, k, z3, lam_q, norm_g, *, lam_init, tq, tk):
    B, T, _ = q.shape
    assert T % tq == 0 and T % tk == 0 and (tq & (tq - 1)) == 0 and tq % 128 == 0 and tk % 128 == 0
    nq, nk = T // tq, T // tk
    last = lambda qi: (qi * tq + tq - 1) // tk
    kern = functools.partial(_diff_kernel, tq=tq, tk=tk, nk=nk, lam_init=lam_init)
    return pl.pallas_call(
        kern,
        out_shape=jax.ShapeDtypeStruct((B, T, GROUP_WIDTH), bf16),
        grid=(B, nq, nk),
        in_specs=[
            pl.BlockSpec((1, tq, GROUP_WIDTH), lambda b, qi, kj: (b, qi, 0)),
            pl.BlockSpec((1, tk, GROUP_WIDTH), lambda b, qi, kj: (b, jnp.minimum(kj, last(qi)), 0)),
            pl.BlockSpec((1, tk, GROUP_WIDTH), lambda b, qi, kj: (b, jnp.minimum(kj, last(qi)), CB_DF_V)),
            pl.BlockSpec((4, DQK), lambda b, qi, kj: (0, 0)),
            pl.BlockSpec((HEAD_DIM, 1), lambda b, qi, kj: (0, 0)),
        ],
        out_specs=pl.BlockSpec((1, tq, GROUP_WIDTH), lambda b, qi, kj: (b, qi, 0)),
        scratch_shapes=[pltpu.VMEM((GROUP_WIDTH, 8 * tq), bf16), pltpu.VMEM((1, 8 * tq), f32),
                        pltpu.VMEM((1, 8 * tq), f32), pltpu.VMEM((GROUP_WIDTH, 2 * tq), f32),
                        pltpu.VMEM((GROUP_WIDTH, tq), f32),
                        pltpu.VMEM((tk, 8 * tq), f32), pltpu.VMEM((tk, 8 * tq), bf16)],
        compiler_params=_cparams(("parallel", "parallel", "arbitrary")),
        name="diff_attention",
    )(q, k, z3, lam_q, norm_g.reshape(HEAD_DIM, 1))


def _diff_decode_kernel(pt_ref, q_ref, kn_ref, vn_ref, lam_ref, g_ref, *rest, G, n_steps, Ts, pos0, n_past,
                        lam_init):
    kpages = rest[:G]
    vpages = rest[G:2 * G]
    o_ref, qp_ref, m_ref, l_ref, acc_ref = rest[2 * G:]
    j = pl.program_id(1)
    nmap = 2 * N_HEADS
    R = nmap * Ts

    @pl.when(j == 0)
    def _():
        q = q_ref[0] * (DQK ** -0.5)
        col = lax.broadcasted_iota(jnp.int32, q.shape, 1)
        for c in range(nmap):
            qp_ref[c * Ts:(c + 1) * Ts, :] = jnp.where((col >= DQK * c) & (col < DQK * (c + 1)), q, 0.0)
        s = _nt(qp_ref[...].astype(bf16), kn_ref[0].astype(bf16))
        qpos = pos0 + (lax.broadcasted_iota(jnp.int32, s.shape, 0) & (Ts - 1))
        kpos = n_past + lax.broadcasted_iota(jnp.int32, s.shape, 1)
        ok = kpos <= qpos
        s = jnp.where(ok, s, NEG)
        m = jnp.max(s, axis=1, keepdims=True)
        p = jnp.where(ok, jnp.exp(s - m), 0.0)
        m_ref[...] = m
        l_ref[...] = jnp.sum(p, axis=1, keepdims=True)
        acc_ref[...] = jnp.dot(p.astype(bf16), vn_ref[0].astype(bf16), preferred_element_type=f32)

    kT = jnp.concatenate([kp[...] for kp in kpages], axis=1).astype(bf16)
    vT = jnp.concatenate([vp[...] for vp in vpages], axis=1).astype(bf16)
    s = jnp.dot(qp_ref[...].astype(bf16), kT, preferred_element_type=f32)
    m_old = m_ref[...]
    m_new = jnp.maximum(m_old, jnp.max(s, axis=1, keepdims=True))
    alpha = jnp.exp(m_old - m_new)
    p = jnp.exp(s - m_new)
    l_ref[...] = alpha * l_ref[...] + jnp.sum(p, axis=1, keepdims=True)
    m_ref[...] = m_new
    acc_ref[...] = alpha * acc_ref[...] + _nt(p.astype(bf16), vT)

    @pl.when(j == n_steps - 1)
    def _():
        lam = _diff_lambda(lam_ref, lam_init)
        o_all = acc_ref[...] / l_ref[...]
        ys = []
        for h in range(N_HEADS):
            cs = slice(HEAD_DIM * h, HEAD_DIM * (h + 1))
            o = o_all[2 * h * Ts:(2 * h + 1) * Ts, cs] - lam * o_all[(2 * h + 1) * Ts:(2 * h + 2) * Ts, cs]
            y = o * lax.rsqrt(jnp.mean(o * o, axis=1, keepdims=True) + EPS)
            ys.append((y * g_ref[...]) * (1.0 - lam_init))
        o_ref[0] = jnp.concatenate(ys, axis=1)


def _page_specs(layer, n_pages, G, block):
    def spec(i):
        return pl.BlockSpec((None, None) + block,
                            lambda b, j, pt, i=i: (layer, pt[b * n_pages + j * G + i]) + (0,) * len(block))
    return [spec(i) for i in range(G)]


def diff_decode(page_table, q, k_new, v_new, pool_k, pool_v, lam_q, norm_g, *, layer, pos0, lam_init):
    B, Ts, _ = q.shape
    n_pages = page_table.shape[1]
    page = pool_k.shape[3]
    G = min(PAGES_PER_STEP, n_pages)
    assert n_pages % G == 0 and (Ts & (Ts - 1)) == 0 and Ts % 8 == 0
    n_steps = n_pages // G
    R = 2 * N_HEADS * Ts
    row = lambda w: pl.BlockSpec((1, Ts, w), lambda b, j, pt: (b, 0, 0))
    kern = functools.partial(_diff_decode_kernel, G=G, n_steps=n_steps, Ts=Ts, pos0=pos0,
                             n_past=n_pages * page, lam_init=lam_init)
    return pl.pallas_call(
        kern,
        out_shape=jax.ShapeDtypeStruct((B, Ts, GROUP_WIDTH), f32),
        grid_spec=pltpu.PrefetchScalarGridSpec(
            num_scalar_prefetch=1, grid=(B, n_steps),
            in_specs=[row(256), row(256), row(256),
                      pl.BlockSpec((4, DQK), lambda b, j, pt: (0, 0)),
                      pl.BlockSpec((1, HEAD_DIM), lambda b, j, pt: (0, 0))]
            + _page_specs(layer, n_pages, G, (256, page)) + _page_specs(layer, n_pages, G, (256, page)),
            out_specs=row(256),
            scratch_shapes=[pltpu.VMEM((R, GROUP_WIDTH), f32), pltpu.VMEM((R, 1), f32),
                            pltpu.VMEM((R, 1), f32), pltpu.VMEM((R, GROUP_WIDTH), f32)]),
        compiler_params=_cparams(("parallel", "arbitrary")),
        name="diff_decode",
    )(page_table.reshape(-1), q, k_new, v_new, lam_q, norm_g.reshape(1, HEAD_DIM),
      *([pool_k] * G), *([pool_v] * G))


def _compress_rows(x, pos_ref, w1_ref, w2_ref):
    x = (x + pos_ref[...]).astype(bf16)
    hid = jax.nn.gelu(jnp.dot(x, w1_ref[...], preferred_element_type=f32))
    return jnp.dot(hid.astype(bf16), w2_ref[...], preferred_element_type=f32)


def _compress_kernel(x_ref, pos_ref, w1_ref, w2_ref, o_ref):
    o_ref[...] = _compress_rows(x_ref[...], pos_ref, w1_ref, w2_ref)


def nsa_compress(blocks, pos_flat, w1c, w2c, tm=256):
    R, K = blocks.shape
    tm = min(tm, R)
    assert R % tm == 0
    return pl.pallas_call(
        _compress_kernel,
        out_shape=jax.ShapeDtypeStruct((R, 2 * HEAD_DIM), f32),
        grid=(R // tm,),
        in_specs=[pl.BlockSpec((tm, K), lambda i: (i, 0)), pl.BlockSpec((1, K), lambda i: (0, 0)),
                  pl.BlockSpec((K, 2 * CMP_HIDDEN), lambda i: (0, 0)),
                  pl.BlockSpec((2 * CMP_HIDDEN, 2 * HEAD_DIM), lambda i: (0, 0))],
        out_specs=pl.BlockSpec((tm, 2 * HEAD_DIM), lambda i: (i, 0)),
        compiler_params=_cparams(("parallel",)),
        name="nsa_compress",
    )(blocks, pos_flat, w1c, w2c)


def _compress_paged_kernel(pt_ref, pos_ref, w1_ref, w2_ref, *rest, G, page):
    pages = rest[:G]
    o_ref, xs = rest[G:]
    w = 2 * HEAD_DIM
    for i in range(G):
        xs[page * i:page * (i + 1), :] = jnp.transpose(pages[i][...])
    nb = G * page // CMP_BLOCK
    acc = None
    for t in range(CMP_BLOCK):
        xt = xs[pl.ds(t, nb, stride=CMP_BLOCK), :] + pos_ref[:, w * t:w * (t + 1)]
        part = jnp.dot(xt.astype(bf16), w1_ref[w * t:w * (t + 1), :], preferred_element_type=f32)
        acc = part if acc is None else acc + part
    hid = jax.nn.gelu(acc)
    o_ref[0] = jnp.dot(hid.astype(bf16), w2_ref[...], preferred_element_type=f32)


def nsa_compress_paged(page_table, pool, pos_flat, w1c, w2c, *, layer):
    B, n_pages = page_table.shape
    page = pool.shape[3]
    K = CMP_BLOCK * 2 * HEAD_DIM
    rp = page // CMP_BLOCK
    G = min(PAGES_PER_STEP_CMP, n_pages)
    assert n_pages % G == 0 and page % CMP_BLOCK == 0 and (G * rp) % 8 == 0
    const = lambda shape: pl.BlockSpec(shape, lambda b, j, pt: (0, 0))
    return pl.pallas_call(
        functools.partial(_compress_paged_kernel, G=G, page=page),
        out_shape=jax.ShapeDtypeStruct((B, n_pages * rp, 2 * HEAD_DIM), f32),
        grid_spec=pltpu.PrefetchScalarGridSpec(
            num_scalar_prefetch=1, grid=(B, n_pages // G),
            in_specs=[const((1, K)), const((K, 2 * CMP_HIDDEN)), const((2 * CMP_HIDDEN, 2 * HEAD_DIM))]
            + _page_specs(layer, n_pages, G, (2 * HEAD_DIM, page)),
            out_specs=pl.BlockSpec((1, G * rp, 2 * HEAD_DIM), lambda b, j, pt: (b, j, 0)),
            scratch_shapes=[pltpu.VMEM((G * page, 2 * HEAD_DIM), f32)]),
        compiler_params=_cparams(("parallel", "arbitrary")),
        name="nsa_compress_paged",
    )(page_table.reshape(-1), pos_flat, w1c, w2c, *([pool] * G))


def _cmp_block_end(r, half):
    blk = jnp.where(r < half, 2 * r, 2 * (r - half) + 1)
    return (blk + 1) * CMP_BLOCK - 1


def _nsa_kernel(qr_in, qw_in, sm_in, cmp_ref, slc_ref, win_ref,
                o_ref, qr_ref, v_ref, sel_ref, ocmp_ref, ms_ref, ls_ref, accs_ref, mw_ref, lw_ref, accw_ref,
                oT_ref, s_ref, pb_ref, *, tq, tk, tkw, nk, nc, nsp, topk):
    qi = pl.program_id(1)
    kj = pl.program_id(2)
    q_lo = qi * tq
    H = N_HEADS
    d = HEAD_DIM
    scale = d ** -0.5
    needed = (q_lo + tq - 1) // tk + 1
    w_lo = jnp.maximum(q_lo - (WINDOW - 1), 0) // tkw
    w_hi = (q_lo + tq - 1) // tkw
    half = nc // 2

    def heads_on_lanes(xT):
        return jnp.concatenate([xT[d * h:d * (h + 1), :] for h in range(H)], axis=1)

    def qpos_row(n):
        return q_lo + (lax.broadcasted_iota(jnp.int32, (1, n), 1) & (tq - 1))

    @pl.when(kj == 0)
    def _():
        zeros = jnp.zeros((d, H * tq), f32)
        qrT = jnp.transpose(qr_in[0]) * scale
        qwT = jnp.transpose(qw_in[0]) * scale
        qr_ref[...] = jnp.concatenate([heads_on_lanes(qrT), zeros], 0).astype(bf16)
        qw = jnp.concatenate([heads_on_lanes(qwT), zeros], 0).astype(bf16)
        cmp = cmp_ref[0]
        s = jnp.dot(cmp.astype(bf16), qw, preferred_element_type=f32)
        c_end = _cmp_block_end(lax.broadcasted_iota(jnp.int32, (nc, 1), 0), half)
        c_ok = c_end <= qpos_row(H * tq)
        s = jnp.where(c_ok, s, NEG)
        p = jnp.where(c_ok, jnp.exp(s - jnp.max(s, axis=0, keepdims=True)), 0.0)
        p = p / jnp.maximum(jnp.sum(p, axis=0, keepdims=True), TINY)
        cmpT = jnp.transpose(cmp)
        ocmp_ref[...] = jnp.dot(cmpT[d:2 * d, :].astype(bf16), p.astype(bf16), preferred_element_type=f32)
        imp = p[:, 0:tq]
        for h in range(1, H):
            imp = imp + p[:, h * tq:(h + 1) * tq]
        imp = imp[:half, :] + imp[half:, :]
        if nsp > half:
            imp = jnp.concatenate([imp, jnp.zeros((nsp - half, tq), f32)], 0)
        sb = lax.broadcasted_iota(jnp.int32, (nsp, tq), 0)
        cur = qpos_row(tq) // SEL_BLOCK
        v = jnp.where((sb == cur) | (sb == 0), FORCE_SCORE, imp)
        v = jnp.where(sb > cur, NEG, v)
        v_ref[...] = v

        def rank(i, cnt):
            vi = v_ref[pl.ds(i, 1), :]
            ahead = (vi > v) | ((vi == v) & (i < sb))
            return cnt + jnp.where(ahead, 1.0, 0.0)

        cnt = lax.fori_loop(0, nsp, rank, jnp.zeros((nsp, tq), f32))
        sel_ref[...] = jnp.where(cnt < topk, 1.0, 0.0)
        for m_r, l_r, a_r in ((ms_ref, ls_ref, accs_ref), (mw_ref, lw_ref, accw_ref)):
            m_r[...] = jnp.full(m_r.shape, NEG, f32)
            l_r[...] = jnp.zeros(l_r.shape, f32)
            a_r[...] = jnp.zeros(a_r.shape, f32)

    def flash(kv_ref, ok, m_r, l_r, a_r):
        kv = kv_ref[0]
        s = jnp.dot(kv.astype(bf16), qr_ref[...], preferred_element_type=f32)
        s = s + jnp.concatenate([jnp.where(ok, 0.0, NEG)] * H, axis=1)
        alpha = _softmax_tile(s, s_ref, pb_ref, m_r)
        vT = jnp.transpose(kv)[d:2 * d, :].astype(bf16)
        pv = jnp.dot(_with_ones(vT), pb_ref[0:kv.shape[0], :], preferred_element_type=f32)
        a_r[...] = a_r[...] * alpha + pv[0:d, :]
        l_r[...] = l_r[...] * alpha + pv[d:d + 1, :]

    @pl.when(kj < needed)
    def _():
        kpos = kj * tk + lax.broadcasted_iota(jnp.int32, (tk, tq), 0)
        nb = tk // SEL_BLOCK
        rows = [jnp.broadcast_to(sel_ref[pl.ds(kj * nb + c, 1), :], (SEL_BLOCK, tq)) for c in range(nb)]
        chosen = jnp.concatenate(rows, axis=0) > 0.5
        ok = chosen & (kpos <= qpos_row(tq))
        flash(slc_ref, ok, ms_ref, ls_ref, accs_ref)

    @pl.when(kj <= w_hi - w_lo)
    def _():
        wpos = (w_lo + kj) * tkw + lax.broadcasted_iota(jnp.int32, (tkw, tq), 0)
        qp = qpos_row(tq)
        ok = (wpos <= qp) & (qp - wpos < WINDOW) & (wpos >= 0)
        flash(win_ref, ok, mw_ref, lw_ref, accw_ref)

    @pl.when(kj == nk - 1)
    def _():
        smT = jnp.transpose(sm_in[0])
        g = jax.nn.sigmoid(smT[SMALL_GATES:SMALL_GATES + 3 * H, :])
        o_slc = accs_ref[...] / ls_ref[...]
        o_win = accw_ref[...] / lw_ref[...]
        o_cmp = ocmp_ref[...]
        for h in range(H):
            cs = slice(h * tq, (h + 1) * tq)
            oT_ref[d * h:d * (h + 1), :] = (g[3 * h:3 * h + 1] * o_cmp[:, cs]
                                            + g[3 * h + 1:3 * h + 2] * o_slc[:, cs]
                                            + g[3 * h + 2:3 * h + 3] * o_win[:, cs])
        o_ref[0] = jnp.transpose(oT_ref[...]).astype(o_ref.dtype)


def nsa_attention(qr, z3, cmp, slc, win, *, tq, tk, tkw):
    B, T, _ = qr.shape
    nc = cmp.shape[1]
    assert T % tq == 0 and T % tk == 0 and T % tkw == 0 and tk % SEL_BLOCK == 0 and nc % 2 == 0
    assert tq % 128 == 0 and tk % 128 == 0 and tkw % 128 == 0 and nc % 8 == 0 and (tq & (tq - 1)) == 0
    nq, nk = T // tq, T // tk
    nsp = T // SEL_BLOCK
    assert nsp >= nc // 2 and nsp % 8 == 0
    last = lambda qi: (qi * tq + tq - 1) // tk
    w_lo = lambda qi: jnp.maximum(qi * tq - (WINDOW - 1), 0) // tkw
    w_hi = lambda qi: (qi * tq + tq - 1) // tkw
    for qi in range(nq):
        lo = max(qi * tq - (WINDOW - 1), 0) // tkw
        assert (qi * tq + tq - 1) // tkw - lo + 1 <= nk
    w_idx = lambda qi, kj: jnp.minimum(w_lo(qi) + kj, w_hi(qi))
    kern = functools.partial(_nsa_kernel, tq=tq, tk=tk, tkw=tkw, nk=nk, nc=nc, nsp=nsp,
                             topk=min(SEL_TOPK, nsp))
    stat = pltpu.VMEM((1, N_HEADS * tq), f32)
    acc = pltpu.VMEM((HEAD_DIM, N_HEADS * tq), f32)
    return pl.pallas_call(
        kern,
        out_shape=jax.ShapeDtypeStruct((B, T, GROUP_WIDTH), bf16),
        grid=(B, nq, nk),
        in_specs=[
            pl.BlockSpec((1, tq, GROUP_WIDTH), lambda b, qi, kj: (b, qi, 0)),
            pl.BlockSpec((1, tq, GROUP_WIDTH), lambda b, qi, kj: (b, qi, CB_NS_Q)),
            pl.BlockSpec((1, tq, 128), lambda b, qi, kj: (b, qi, CB128_SMALL)),
            pl.BlockSpec((1, nc, 128), lambda b, qi, kj: (b, 0, 0)),
            pl.BlockSpec((1, tk, 128), lambda b, qi, kj: (b, jnp.minimum(kj, last(qi)), 0)),
            pl.BlockSpec((1, tkw, 128), lambda b, qi, kj: (b, w_idx(qi, kj), 0)),
        ],
        out_specs=pl.BlockSpec((1, tq, GROUP_WIDTH), lambda b, qi, kj: (b, qi, 0)),
        scratch_shapes=[pltpu.VMEM((128, N_HEADS * tq), bf16), pltpu.VMEM((nsp, tq), f32),
                        pltpu.VMEM((nsp, tq), f32), acc, stat, stat, acc, stat, stat, acc,
                        pltpu.VMEM((GROUP_WIDTH, tq), f32),
                        pltpu.VMEM((max(tk, tkw), N_HEADS * tq), f32),
                        pltpu.VMEM((max(tk, tkw), N_HEADS * tq), bf16)],
        compiler_params=_cparams(("parallel", "parallel", "arbitrary")),
        name="nsa_attention",
    )(qr, z3, z3, cmp, slc, win)


def _nsa_decode_kernel(pt_ref, qr_in, qw_in, sm_in, cmp_ref, sn_ref, wb_ref, wn_ref, *rest,
                       G, n_steps, Ts, pos0, n_past, wpos0, nc, nsl, n_sel, topk, page):
    pages = rest[:G]
    o_ref, qr_ref, selst_ref, ocmp_ref, owin_ref, m_ref, l_ref, acc_ref = rest[G:]
    j = pl.program_id(1)
    H, d = N_HEADS, HEAD_DIM
    scale = d ** -0.5
    R = H * Ts
    half = nc // 2
    bps = G * page // SEL_BLOCK

    def rows_by_head(x):
        z = jnp.zeros((Ts, d), f32)
        return jnp.concatenate([jnp.concatenate([x[:, d * h:d * (h + 1)], z], axis=1) for h in range(H)], axis=0)

    def tile_heads(x):
        return jnp.concatenate([x] * H, axis=0)

    @pl.when(j == 0)
    def _():
        qr = rows_by_head(qr_in[0] * scale)
        qr_ref[...] = qr
        qrb = qr.astype(bf16)
        qwb = rows_by_head(qw_in[0] * scale).astype(bf16)
        qpos = pos0 + (lax.broadcasted_iota(jnp.int32, (R, 1), 0) & (Ts - 1))
        cmpb = cmp_ref[0].astype(bf16)
        s = _nt(qwb, cmpb)
        c_ok = _cmp_block_end(lax.broadcasted_iota(jnp.int32, (1, nc), 1), half) <= qpos
        s = jnp.where(c_ok, s, NEG)
        p = jnp.where(c_ok, jnp.exp(s - jnp.max(s, axis=1, keepdims=True)), 0.0)
        p = p / jnp.maximum(jnp.sum(p, axis=1, keepdims=True), TINY)
        ocmp_ref[...] = jnp.dot(p.astype(bf16), cmpb, preferred_element_type=f32)
        imp = p[0:Ts, :]
        for h in range(1, H):
            imp = imp + p[h * Ts:(h + 1) * Ts, :]
        imp = imp[:, :half] + imp[:, half:]
        imp = jnp.concatenate([imp, jnp.zeros((Ts, nsl - half), f32)], axis=1)
        lane = lax.broadcasted_iota(jnp.int32, (Ts, nsl), 1)
        cur = (pos0 + lax.broadcasted_iota(jnp.int32, (Ts, 1), 0)) // SEL_BLOCK
        v = jnp.where((lane == cur) | (lane == 0), FORCE_SCORE, imp)
        v = jnp.where(lane > cur, NEG, v)
        cnt = jnp.zeros((Ts, nsl), f32)
        for i in range(n_sel):
            vi = v[:, i:i + 1]
            cnt = cnt + jnp.where((vi > v) | ((vi == v) & (lane > i)), 1.0, 0.0)
        sel = jnp.where(cnt < topk, 1.0, 0.0)
        for st in range(n_steps):
            selst_ref[st] = sel[:, bps * st:bps * (st + 1)]
        blk_new = n_past // SEL_BLOCK
        sel_new = tile_heads(sel[:, blk_new:blk_new + 1]) > 0.5
        snb = sn_ref[0].astype(bf16)
        s = _nt(qrb, snb)
        kpos = n_past + lax.broadcasted_iota(jnp.int32, (1, Ts), 1)
        ok = sel_new & (kpos <= qpos)
        s = jnp.where(ok, s, NEG)
        m = jnp.max(s, axis=1, keepdims=True)
        p = jnp.where(ok, jnp.exp(s - m), 0.0)
        m_ref[...] = m
        l_ref[...] = jnp.sum(p, axis=1, keepdims=True)
        acc_ref[...] = jnp.dot(p.astype(bf16), snb, preferred_element_type=f32)
        wbT = wb_ref[...].astype(bf16)
        wnb = wn_ref[0].astype(bf16)
        lw = wbT.shape[1]
        s1 = jnp.dot(qrb, wbT, preferred_element_type=f32)
        s2 = _nt(qrb, wnb)
        wp1 = wpos0 + lax.broadcasted_iota(jnp.int32, (1, lw), 1)
        wp2 = wpos0 + lw + lax.broadcasted_iota(jnp.int32, (1, Ts), 1)
        ok1 = (wp1 <= qpos) & (qpos - wp1 < WINDOW) & (wp1 >= 0)
        ok2 = (wp2 <= qpos) & (qpos - wp2 < WINDOW) & (wp2 >= 0)
        s1 = jnp.where(ok1, s1, NEG)
        s2 = jnp.where(ok2, s2, NEG)
        mw = jnp.maximum(jnp.max(s1, axis=1, keepdims=True), jnp.max(s2, axis=1, keepdims=True))
        p1 = jnp.where(ok1, jnp.exp(s1 - mw), 0.0)
        p2 = jnp.where(ok2, jnp.exp(s2 - mw), 0.0)
        lsum = jnp.sum(p1, axis=1, keepdims=True) + jnp.sum(p2, axis=1, keepdims=True)
        owin_ref[...] = (_nt(p1.astype(bf16), wbT)
                         + jnp.dot(p2.astype(bf16), wnb, preferred_element_type=f32)) / lsum

    kvT = jnp.concatenate([pg[...] for pg in pages], axis=1).astype(bf16)
    n = kvT.shape[1]
    s = jnp.dot(qr_ref[...].astype(bf16), kvT, preferred_element_type=f32)
    expand = (lax.broadcasted_iota(jnp.int32, (bps, n), 1) // SEL_BLOCK
              == lax.broadcasted_iota(jnp.int32, (bps, n), 0))
    chosen = jnp.dot(selst_ref[j].astype(bf16), jnp.where(expand, 1.0, 0.0).astype(bf16),
                     preferred_element_type=f32)
    ok = tile_heads(chosen) > 0.5
    s = jnp.where(ok, s, NEG)
    m_old = m_ref[...]
    m_new = jnp.maximum(m_old, jnp.max(s, axis=1, keepdims=True))
    alpha = jnp.exp(m_old - m_new)
    p = jnp.where(ok, jnp.exp(s - m_new), 0.0)
    l_ref[...] = alpha * l_ref[...] + jnp.sum(p, axis=1, keepdims=True)
    m_ref[...] = m_new
    acc_ref[...] = alpha * acc_ref[...] + _nt(p.astype(bf16), kvT)

    @pl.when(j == n_steps - 1)
    def _():
        g = jax.nn.sigmoid(sm_in[0][:, SMALL_GATES:SMALL_GATES + 3 * H])
        o_slc = acc_ref[...] / l_ref[...]
        outs = []
        for h in range(H):
            rs = slice(h * Ts, (h + 1) * Ts)
            outs.append(g[:, 3 * h:3 * h + 1] * ocmp_ref[rs, d:2 * d]
                        + g[:, 3 * h + 1:3 * h + 2] * o_slc[rs, d:2 * d]
                        + g[:, 3 * h + 2:3 * h + 3] * owin_ref[rs, d:2 * d])
        o_ref[0] = jnp.concatenate(outs, axis=1)


def nsa_decode(page_table, qr, z3, cmp, slc_new, win_cache, win_new, pool_slc, *, layer, pos0):
    B, Ts, _ = qr.shape
    n_pages = page_table.shape[1]
    page = pool_slc.shape[3]
    n_past = n_pages * page
    lw = win_cache.shape[3]
    nc = cmp.shape[1]
    G = min(PAGES_PER_STEP_SLC, n_pages)
    n_steps = n_pages // G
    n_sel = -(-(n_past + Ts) // SEL_BLOCK)
    nsl = -(-n_sel // 128) * 128
    assert n_pages % G == 0 and (Ts & (Ts - 1)) == 0 and Ts % 8 == 0 and page % SEL_BLOCK == 0
    assert pos0 == n_past and n_past % SEL_BLOCK + Ts <= SEL_BLOCK and nc % 2 == 0 and nc // 2 <= nsl
    R = N_HEADS * Ts
    row = lambda w, cb=0: pl.BlockSpec((1, Ts, w), lambda b, j, pt, cb=cb: (b, 0, cb))
    kern = functools.partial(_nsa_decode_kernel, G=G, n_steps=n_steps, Ts=Ts, pos0=pos0, n_past=n_past,
                             wpos0=n_past - lw, nc=nc, nsl=nsl, n_sel=n_sel, topk=min(SEL_TOPK, n_sel),
                             page=page)
    buf = lambda: pltpu.VMEM((R, 128), f32)
    stat = lambda: pltpu.VMEM((R, 1), f32)
    return pl.pallas_call(
        kern,
        out_shape=jax.ShapeDtypeStruct((B, Ts, GROUP_WIDTH), f32),
        grid_spec=pltpu.PrefetchScalarGridSpec(
            num_scalar_prefetch=1, grid=(B, n_steps),
            in_specs=[row(256), row(256, CB_NS_Q), row(128, CB128_SMALL),
                      pl.BlockSpec((1, nc, 128), lambda b, j, pt: (b, 0, 0)),
                      row(128),
                      pl.BlockSpec((None, None, 128, lw), lambda b, j, pt: (layer, b, 0, 0)),
                      row(128)]
            + _page_specs(layer, n_pages, G, (128, page)),
            out_specs=row(256),
            scratch_shapes=[buf(), pltpu.VMEM((n_steps, Ts, G * page // SEL_BLOCK), f32), buf(), buf(),
                            stat(), stat(), buf()]),
        compiler_params=_cparams(("parallel", "arbitrary")),
        name="nsa_decode",
    )(page_table.reshape(-1), qr, z3, z3, cmp, slc_new, win_cache, win_new, *([pool_slc] * G))


def _mlstm_kernel(q_ref, k_ref, v_ref, og_ref, gc_ref, gr_ref, bias_ref, ng_ref, c0_ref, n0_ref, m0_ref,
                  o_ref, cout_ref, nout_ref, mout_ref, c_s, n_s, m_s, *, L):
    ci = pl.program_id(1)
    H, d = N_HEADS, HEAD_DIM

    @pl.when(ci == 0)
    def _():
        c_s[...] = c0_ref[0]
        n_s[...] = n0_ref[0]
        m_s[...] = m0_ref[0]

    q = q_ref[...]
    k = k_ref[...] * (d ** -0.5)
    v = v_ref[...]
    gc = gc_ref[...]
    gr = gr_ref[0]
    bias = bias_ref[...]
    t_i = lax.broadcasted_iota(jnp.int32, (L, L), 0)
    s_i = lax.broadcasted_iota(jnp.int32, (L, L), 1)
    causal = s_i <= t_i
    causal_T = t_i <= s_i
    outs = []
    for h in range(H):
        bi = bias[0:1, h:h + 1]
        bf_ = bias[1:2, h:h + 1]
        ig_c = gc[:, h:h + 1] + bi
        lf_c = jax.nn.log_sigmoid(gc[:, H + h:H + h + 1] + bf_)
        ig_r = gr[h:h + 1, :] + bi
        lf_r = jax.nn.log_sigmoid(gr[H + h:H + h + 1, :] + bf_)
        b_c = jnp.sum(jnp.where(causal, lf_r, 0.0), axis=1, keepdims=True)
        b_r = jnp.sum(jnp.where(causal_T, lf_c, 0.0), axis=0, keepdims=True)
        m_prev = m_s[h:h + 1, 0:1]
        D = jnp.where(causal, b_c - b_r + ig_r, -jnp.inf)
        m_t = jnp.maximum(b_c + m_prev, jnp.max(D, axis=1, keepdims=True))
        inter = jnp.exp(b_c + m_prev - m_t)
        qh = q[:, d * h:d * (h + 1)]
        kh = k[:, d * h:d * (h + 1)]
        vh = v[:, d * h:d * (h + 1)]
        qb = qh.astype(bf16)
        S = _nt(qb, kh.astype(bf16))
        Sw = jnp.exp(D - m_t) * S
        C = c_s[h]
        n_row = n_s[h:h + 1, :]
        num = (jnp.dot(Sw.astype(bf16), vh.astype(bf16), preferred_element_type=f32)
               + inter * jnp.dot(qb, C.astype(bf16), preferred_element_type=f32))
        qn = jnp.sum(qh * n_row, axis=1, keepdims=True)
        den = jnp.sum(Sw, axis=1, keepdims=True) + inter * qn
        hh = num / jnp.maximum(jnp.abs(den), jnp.exp(-m_t))
        m_new = m_t[L - 1:L, :]
        b_last = b_c[L - 1:L, :]
        wl = jnp.exp(b_last - b_c + ig_c - m_new)
        decay = jnp.exp(b_last + m_prev - m_new)
        kw = kh * wl
        c_s[h] = decay * C + lax.dot_general(kw.astype(bf16), vh.astype(bf16), (((0,), (0,)), ((), ())),
                                             preferred_element_type=f32)
        n_s[h:h + 1, :] = decay * n_row + jnp.sum(kw, axis=0, keepdims=True)
        m_s[h:h + 1, :] = jnp.broadcast_to(m_new, (1, 128))
        mu = jnp.mean(hh, axis=1, keepdims=True)
        var = jnp.mean(jnp.square(hh - mu), axis=1, keepdims=True)
        outs.append((hh - mu) * lax.rsqrt(var + EPS))
    hcat = jnp.concatenate(outs, axis=1) * ng_ref[...]
    o_ref[...] = (hcat * jax.nn.sigmoid(og_ref[...])).astype(o_ref.dtype)
    cout_ref[0] = c_s[...]
    nout_ref[0] = n_s[...]
    mout_ref[0] = m_s[...]


def mlstm(z, gates_T, gate_b, norm_g, C0, n0, m0, *, B, T, L):
    nchunk = T // L
    m0p = jnp.broadcast_to(m0[:, :, None], (B, N_HEADS, 128))
    row = lambda cb: pl.BlockSpec((L, 256), lambda b, c, cb=cb: (b * nchunk + c, cb))
    st = lambda *shape: pl.BlockSpec((1,) + shape, lambda b, c: (b,) + (0,) * len(shape))
    out, C, n, m = pl.pallas_call(
        functools.partial(_mlstm_kernel, L=L),
        out_shape=[jax.ShapeDtypeStruct((B * T, GROUP_WIDTH), _act_dtype(L)),
                   jax.ShapeDtypeStruct((B, N_HEADS, HEAD_DIM, HEAD_DIM), f32),
                   jax.ShapeDtypeStruct((B, N_HEADS, HEAD_DIM), f32),
                   jax.ShapeDtypeStruct((B, N_HEADS, 128), f32)],
        grid=(B, nchunk),
        in_specs=[row(CB_ML_Q), row(CB_ML_K), row(CB_ML_V), row(CB_ML_O),
                  pl.BlockSpec((L, 128), lambda b, c: (b * nchunk + c, CB128_SMALL)),
                  pl.BlockSpec((1, 8, L), lambda b, c: (b * nchunk + c, 0, 0)),
                  pl.BlockSpec((2, N_HEADS), lambda b, c: (0, 0)),
                  pl.BlockSpec((1, GROUP_WIDTH), lambda b, c: (0, 0)),
                  st(N_HEADS, HEAD_DIM, HEAD_DIM), st(N_HEADS, HEAD_DIM), st(N_HEADS, 128)],
        out_specs=[pl.BlockSpec((L, GROUP_WIDTH), lambda b, c: (b * nchunk + c, 0)),
                   st(N_HEADS, HEAD_DIM, HEAD_DIM), st(N_HEADS, HEAD_DIM), st(N_HEADS, 128)],
        scratch_shapes=[pltpu.VMEM((N_HEADS, HEAD_DIM, HEAD_DIM), f32), pltpu.VMEM((N_HEADS, HEAD_DIM), f32),
                        pltpu.VMEM((N_HEADS, 128), f32)],
        compiler_params=_cparams(("parallel", "arbitrary")),
        name="mlstm",
    )(z, z, z, z, z, gates_T, gate_b, norm_g.reshape(1, GROUP_WIDTH), C0, n0, m0p)
    return out, C, n, m[:, :, 0]


def _rglru_kernel(x_ref, gate_ref, cw_ref, cb_ref, wa_ref, ba_ref, wx_ref, bx_ref, lam_ref, h0_ref, buf0_ref,
                  y_ref, hout_ref, bufout_ref, xbuf, a_s, u_s, h_s, hs_s, *, tm):
    ti = pl.program_id(0)
    Bb = x_ref.shape[0]
    W = GROUP_WIDTH

    @pl.when(ti == 0)
    def _():
        xbuf[:, 0:8, :] = buf0_ref[...]
        h_s[...] = h0_ref[...]

    xbuf[:, 8:8 + tm, :] = x_ref[...]
    cw = cw_ref[...]
    xc = cb_ref[...] + cw[0:1, :] * xbuf[:, 5:5 + tm, :]
    for j in range(1, CONV_W):
        xc = xc + cw[j:j + 1, :] * xbuf[:, 5 + j:5 + j + tm, :]
    flat = xc.reshape(Bb * tm, W).astype(bf16)
    r = jax.nn.sigmoid(jnp.dot(flat, wa_ref[...], preferred_element_type=f32) + ba_ref[...])
    i = jax.nn.sigmoid(jnp.dot(flat, wx_ref[...], preferred_element_type=f32) + bx_ref[...])
    lam = lam_ref[...]
    softplus = jnp.maximum(-lam, 0.0) + jnp.log1p(jnp.exp(-jnp.abs(lam)))
    log_a = (-LRU_C * r) * softplus
    a = jnp.exp(log_a)
    u = jnp.sqrt(-jnp.tanh(log_a) * (a * a + 1.0)) * (i * xc.reshape(Bb * tm, W))
    a_s[...] = a.reshape(Bb, tm, W)
    u_s[...] = u.reshape(Bb, tm, W)

    def body(t, h):
        h = a_s[:, pl.ds(t, 1), :] * h + u_s[:, pl.ds(t, 1), :]
        hs_s[:, pl.ds(t, 1), :] = h
        return h

    h_last = lax.fori_loop(0, tm, body, h_s[...])
    h_s[...] = h_last
    y_ref[...] = (hs_s[...] * jax.nn.gelu(gate_ref[...])).astype(y_ref.dtype)
    hout_ref[...] = h_last
    bufout_ref[...] = xbuf[:, tm:tm + 8, :]
    xbuf[:, 0:8, :] = xbuf[:, tm:tm + 8, :]


def rglru(z3, conv_w, conv_b, wa_bd, ba, wx_bd, bx, lam, h0, buf0, *, tm):
    B, T, _ = z3.shape
    tm = min(tm, T)
    assert T % tm == 0 and tm % 8 == 0
    W = GROUP_WIDTH
    buf8 = jnp.concatenate([jnp.zeros((B, 8 - (CONV_W - 1), W), f32), buf0], 1)
    vec = lambda: pl.BlockSpec((1, W), lambda i: (0, 0))
    mat = lambda: pl.BlockSpec((W, W), lambda i: (0, 0))
    y, h, buf = pl.pallas_call(
        functools.partial(_rglru_kernel, tm=tm),
        out_shape=[jax.ShapeDtypeStruct((B, T, W), _act_dtype(tm)), jax.ShapeDtypeStruct((B, 1, W), f32),
                   jax.ShapeDtypeStruct((B, 8, W), f32)],
        grid=(T // tm,),
        in_specs=[pl.BlockSpec((B, tm, W), lambda i: (0, i, CB_LR_X)),
                  pl.BlockSpec((B, tm, W), lambda i: (0, i, CB_LR_G)),
                  pl.BlockSpec((CONV_W, W), lambda i: (0, 0)), vec(), mat(), vec(), mat(), vec(), vec(),
                  pl.BlockSpec((B, 1, W), lambda i: (0, 0, 0)), pl.BlockSpec((B, 8, W), lambda i: (0, 0, 0))],
        out_specs=[pl.BlockSpec((B, tm, W), lambda i: (0, i, 0)),
                   pl.BlockSpec((B, 1, W), lambda i: (0, 0, 0)), pl.BlockSpec((B, 8, W), lambda i: (0, 0, 0))],
        scratch_shapes=[pltpu.VMEM((B, tm + 8, W), f32), pltpu.VMEM((B, tm, W), f32), pltpu.VMEM((B, tm, W), f32),
                        pltpu.VMEM((B, 1, W), f32), pltpu.VMEM((B, tm, W), f32)],
        compiler_params=_cparams(("arbitrary",)),
        name="rglru",
    )(z3, z3, conv_w, conv_b.reshape(1, W), wa_bd, ba.reshape(1, W), wx_bd, bx.reshape(1, W),
      lam.reshape(1, W), h0.reshape(B, 1, W), buf8)
    return y, h[:, 0], buf[:, 8 - (CONV_W - 1):]


def _mem_attn_kernel(q_ref, k_ref, v_ref, o_ref):
    q = q_ref[0]
    lead = (0,) * (len(k_ref.shape) - 2)
    k = k_ref[lead]
    v = v_ref[lead]
    dh = MEM_HEAD_DIM
    for h in range(MEM_HEADS):
        cs = slice(dh * h, dh * (h + 1))
        s = _nt(q[:, cs].astype(bf16), k[:, cs].astype(bf16)) * (dh ** -0.5)
        e = jnp.exp(s - jnp.max(s, axis=-1, keepdims=True))
        p = e / jnp.sum(e, axis=-1, keepdims=True)
        o_ref[0, :, cs] = jnp.dot(p.astype(bf16), v[:, cs].astype(bf16),
                                  preferred_element_type=f32).astype(o_ref.dtype)


def mem_attention(q, k, v, *, layer=None, tq=256):
    B, T, D = q.shape
    tq = min(tq, T)
    M = k.shape[-2]
    if layer is None:
        kv_spec = pl.BlockSpec((1, M, D), lambda b, i: (b, 0, 0))
    else:
        kv_spec = pl.BlockSpec((1, 1, M, D), lambda b, i: (layer, b, 0, 0))
    return pl.pallas_call(
        _mem_attn_kernel,
        out_shape=jax.ShapeDtypeStruct((B, T, D), _act_dtype(tq)),
        grid=(B, T // tq),
        in_specs=[pl.BlockSpec((1, tq, D), lambda b, i: (b, i, 0)), kv_spec, kv_spec],
        out_specs=pl.BlockSpec((1, tq, D), lambda b, i: (b, i, 0)),
        compiler_params=_cparams(("parallel", "parallel")),
        name="mem_attention",
    )(q, k, v)


def _even_odd(cmp):
    B, nc, w = cmp.shape
    return jnp.swapaxes(cmp.reshape(B, nc // 2, 2, w), 1, 2).reshape(B, nc, w)


def _layer(x, lw, lam_init, tables, *, B, T, mem, ml_state, lr_state, paged, cfg):
    N = B * T
    z = matmul(x, lw['w_in'], g=lw['g_mix'], tm=512)
    z3 = z.reshape(B, T, D_PROJ_PAD)
    dfq, dfk, nsq, slc_new, win_new = rope_prep(z, tables, cfg['rope_tm'])
    df_v = z[:, 256 * CB_DF_V:256 * (CB_DF_V + 1)]
    cmp_new = z[:, 128 * CB128_CMP:128 * (CB128_CMP + 1)]
    small = z[:, 128 * CB128_SMALL:128 * CB128_SMALL + SMALL_GATES]
    r3 = lambda a: a.reshape(B, T, a.shape[-1])

    L = cfg['ml_chunk']
    gates_T = jnp.swapaxes(small.reshape(N // L, L, SMALL_GATES), 1, 2)
    o_ml, ml_C, ml_n, ml_m = mlstm(z, gates_T, lw['ml_gate_b'], lw['ml_norm_g'], *ml_state, B=B, T=T, L=L)

    if paged is None:
        o_df = diff_attention(r3(dfq), r3(dfk), z3, lw['df_lam'], lw['df_norm_g'], lam_init=lam_init,
                              tq=cfg['df_tq'], tk=cfg['df_tk'])
        comp = nsa_compress(cmp_new.reshape(N // CMP_BLOCK, CMP_BLOCK * 128), lw['nsa_pos_flat'],
                            lw['nsa_w1c'], lw['nsa_w2c']).reshape(B, T // CMP_BLOCK, 128)
        o_ns = nsa_attention(r3(nsq), z3, _even_odd(comp), r3(slc_new), r3(win_new),
                             tq=cfg['ns_tq'], tk=cfg['ns_tk'], tkw=cfg['ns_tkw'])
        win_state = r3(win_new)[:, T - min(WINDOW, T):]
    else:
        pt, l = paged['page_table'], paged['layer']
        n_past = pt.shape[1] * paged['df_k'].shape[3]
        assert (n_past + T) // CMP_BLOCK == n_past // CMP_BLOCK
        o_df = diff_decode(pt, r3(dfq), r3(dfk), r3(df_v), paged['df_k'], paged['df_v'], lw['df_lam'],
                           lw['df_norm_g'], layer=l, pos0=n_past, lam_init=lam_init)
        comp = nsa_compress_paged(pt, paged['nsa_cmp'], lw['nsa_pos_flat'], lw['nsa_w1c'], lw['nsa_w2c'], layer=l)
        o_ns = nsa_decode(pt, r3(nsq), z3, _even_odd(comp), r3(slc_new), paged['nsa_win'], r3(win_new),
                          paged['nsa_slc'], layer=l, pos0=n_past)
        win_cat = jnp.concatenate([paged['nsa_win'][l], jnp.swapaxes(r3(win_new), 1, 2)], 2)
        win_state = jnp.swapaxes(win_cat[:, :, win_cat.shape[2] - min(WINDOW, win_cat.shape[2]):], 1, 2)

    o_lr, lr_h, lr_buf = rglru(z3, lw['lru_conv_w'], lw['lru_conv_b'], lw['lru_wa_bd'],
                               lw['lru_ba'], lw['lru_wx_bd'], lw['lru_bx'], lw['lru_lambda'], *lr_state,
                               tm=cfg['lru_tm'])

    x = matmul([o_ml, o_df.reshape(N, 256), o_ns.reshape(N, 256), o_lr.reshape(N, 256)], lw['w_out'], res=x,
               tm=1024)
    q = matmul(x, lw['w_mq'], g=lw['g_mem_q'], tm=1024, out_dtype=_act_dtype(cfg['mem_tq']))
    if isinstance(mem, tuple):
        att = mem_attention(q.reshape(B, T, D_MODEL), mem[0], mem[1], tq=cfg['mem_tq'])
    else:
        att = mem_attention(q.reshape(B, T, D_MODEL), mem['k'], mem['v'], layer=mem['layer'], tq=cfg['mem_tq'])
    x = matmul(att.reshape(N, D_MODEL), lw['w_mo'], res=x, tm=1024)
    u = matmul(x, lw['w_up'], g=lw['g_mlp'], act='relu2', out_dtype=bf16, tm=512)
    x = matmul(u, lw['w_down'], res=x, tm=512)
    new = dict(df_k=dfk.reshape(B, T, N_HEADS, HEAD_DIM), df_v=df_v.reshape(B, T, N_HEADS, HEAD_DIM),
               nsa_cmp=cmp_new.reshape(B, T, 2, HEAD_DIM), nsa_slc=slc_new.reshape(B, T, 2, HEAD_DIM),
               nsa_win=win_state.reshape(B, -1, 2, HEAD_DIM), ml_C=ml_C, ml_n=ml_n, ml_m=ml_m,
               lru_h=lr_h, lru_conv=lr_buf)
    return x, new


def _block_diag(w):
    H, a, b = w.shape
    eye = jnp.eye(H, dtype=w.dtype)
    return jnp.einsum('hij,hg->higj', w, eye).reshape(H * a, H * b)


def kernel(x_prompt, x_sample, mem_prompt, cache_df_k, cache_df_v, cache_nsa_cmp, cache_nsa_slc, cache_nsa_win, state_ml_C, state_ml_n, state_ml_m, state_lru_h, state_lru_conv, cache_mem_k, cache_mem_v, page_table, g_mix, w_in, w_out, ml_gate_b, ml_norm_g, df_lam, df_norm_g, nsa_pos, nsa_w1, nsa_w2, lru_conv_w, lru_conv_b, lru_wa, lru_ba, lru_wx, lru_bx, lru_lambda, g_mem_q, g_mem_kv, w_mq, w_mk, w_mv, w_mo, g_mlp, w_up, w_down, g_final):
    Bp, Tp, _ = x_prompt.shape
    Bs, Ts, _ = x_sample.shape
    depth = w_in.shape[0]
    n_pool, page = cache_df_k.shape[1], cache_df_k.shape[2]
    n_past = page_table.shape[1] * page
    M = mem_prompt.shape[1]

    perm, n_real = _proj_perm()
    col_ok = (jnp.arange(D_PROJ_PAD) < n_real)
    w_in_p = jnp.where(col_ok[None, None, :], jnp.take(w_in, perm, axis=2), 0.0).astype(bf16)
    eye2 = jnp.eye(2, dtype=f32)
    w1 = nsa_w1.reshape(depth, 2, CMP_BLOCK, HEAD_DIM, CMP_HIDDEN)
    w1c = jnp.einsum('lstih,sg->ltsigh', w1, eye2).reshape(depth, CMP_BLOCK * 2 * HEAD_DIM, 2 * CMP_HIDDEN)
    w2c = jnp.einsum('lshd,sg->lshgd', nsa_w2, eye2).reshape(depth, 2 * CMP_HIDDEN, 2 * HEAD_DIM)
    cfg_p = dict(rope_tm=512, ml_chunk=512, df_tq=512, df_tk=512, ns_tq=512, ns_tk=512, ns_tkw=256,
                 lru_tm=256, mem_tq=256)
    cfg_s = dict(rope_tm=Bs * Ts, ml_chunk=Ts, lru_tm=Ts, mem_tq=Ts)
    tab_p = rope_tables(jnp.arange(Tp))
    tab_s = tuple(jnp.tile(t, (Bs, 1)) for t in rope_tables(n_past + jnp.arange(Ts)))

    tok_minor = lambda c: jnp.transpose(c, (0, 1, 3, 4, 2)).reshape(c.shape[0], c.shape[1], -1, c.shape[2])
    pool_df_k = tok_minor(cache_df_k)
    pool_df_v = tok_minor(cache_df_v)
    pool_cmp = tok_minor(cache_nsa_cmp)
    pool_slc = tok_minor(cache_nsa_slc)
    win_cache = tok_minor(cache_nsa_win)
    mem_k_cache = cache_mem_k.reshape(depth, Bs, M, D_MODEL)
    mem_v_cache = cache_mem_v.reshape(depth, Bs, M, D_MODEL)
    mem_rows = mem_prompt.reshape(Bp * M, D_MODEL)

    xp = x_prompt.reshape(Bp * Tp, D_MODEL)
    xs = x_sample.reshape(Bs * Ts, D_MODEL)
    names = ('df_k', 'df_v', 'nsa_cmp', 'nsa_slc', 'nsa_win', 'ml_C', 'ml_n', 'ml_m', 'lru_h', 'lru_conv')
    acc = {pre + n: [] for n in names for pre in ('p_', 's_')}
    acc['p_mem_k'] = []
    acc['p_mem_v'] = []
    for l in range(depth):
        lw = dict(g_mix=g_mix[l], w_in=w_in_p[l], w_out=w_out[l].astype(bf16), ml_gate_b=ml_gate_b[l],
                  ml_norm_g=ml_norm_g[l], df_lam=df_lam[l], df_norm_g=df_norm_g[l],
                  nsa_pos_flat=nsa_pos[l].reshape(1, CMP_BLOCK * 2 * HEAD_DIM), nsa_w1c=w1c[l].astype(bf16),
                  nsa_w2c=w2c[l].astype(bf16), lru_conv_w=lru_conv_w[l], lru_conv_b=lru_conv_b[l],
                  lru_wa_bd=_block_diag(lru_wa[l]).astype(bf16), lru_ba=lru_ba[l],
                  lru_wx_bd=_block_diag(lru_wx[l]).astype(bf16), lru_bx=lru_bx[l], lru_lambda=lru_lambda[l],
                  g_mem_q=g_mem_q[l], w_mq=w_mq[l].astype(bf16), w_mo=w_mo[l].astype(bf16), g_mlp=g_mlp[l],
                  w_up=w_up[l].astype(bf16), w_down=w_down[l].astype(bf16))
        lam_init = 0.8 - 0.6 * math.exp(-0.3 * l)
        mk_p = matmul(mem_rows, w_mk[l].astype(bf16), g=g_mem_kv[l])
        mv_p = matmul(mem_rows, w_mv[l].astype(bf16), g=g_mem_kv[l])
        ml0 = (jnp.zeros((Bp, N_HEADS, HEAD_DIM, HEAD_DIM), f32), jnp.zeros((Bp, N_HEADS, HEAD_DIM), f32),
               jnp.zeros((Bp, N_HEADS), f32))
        lr0 = (jnp.zeros((Bp, GROUP_WIDTH), f32), jnp.zeros((Bp, CONV_W - 1, GROUP_WIDTH), f32))
        xp, new_p = _layer(xp, lw, lam_init, tab_p, B=Bp, T=Tp,
                           mem=(mk_p.reshape(Bp, M, D_MODEL), mv_p.reshape(Bp, M, D_MODEL)),
                           ml_state=ml0, lr_state=lr0, paged=None, cfg=cfg_p)
        paged = dict(page_table=page_table, layer=l, df_k=pool_df_k, df_v=pool_df_v, nsa_cmp=pool_cmp,
                     nsa_slc=pool_slc, nsa_win=win_cache)
        xs, new_s = _layer(xs, lw, lam_init, tab_s, B=Bs, T=Ts,
                           mem=dict(k=mem_k_cache, v=mem_v_cache, layer=l),
                           ml_state=(state_ml_C[l], state_ml_n[l], state_ml_m[l]),
                           lr_state=(state_lru_h[l], state_lru_conv[l]), paged=paged, cfg=cfg_s)
        for pre, new in (('p_', new_p), ('s_', new_s)):
            for n in names:
                acc[pre + n].append(new[n])
        acc['p_mem_k'].append(mk_p.reshape(Bp, M, MEM_HEADS, MEM_HEAD_DIM))
        acc['p_mem_v'].append(mv_p.reshape(Bp, M, MEM_HEADS, MEM_HEAD_DIM))
    st = {k: jnp.stack(v) for k, v in acc.items()}
    y_prompt = rmsnorm_rows(xp, g_final).reshape(Bp, Tp, D_MODEL)
    y_sample = rmsnorm_rows(xs, g_final).reshape(Bs, Ts, D_MODEL)
    return (y_prompt, y_sample, st['p_df_k'], st['s_df_k'], st['p_df_v'], st['s_df_v'],
            st['p_nsa_cmp'], st['s_nsa_cmp'], st['p_nsa_slc'], st['s_nsa_slc'],
            st['p_nsa_win'], st['s_nsa_win'], st['p_ml_C'], st['s_ml_C'], st['p_ml_n'], st['s_ml_n'],
            st['p_ml_m'], st['s_ml_m'], st['p_lru_h'], st['s_lru_h'], st['p_lru_conv'], st['s_lru_conv'],
            st['p_mem_k'], st['p_mem_v'])
```

```python
import functools
import math

import jax
import jax.numpy as jnp
from jax import lax
from jax.experimental import pallas as pl
from jax.experimental.pallas import tpu as pltpu

f32 = jnp.float32
bf16 = jnp.bfloat16

D_MODEL = 1024
GROUP_WIDTH = 256
N_HEADS = 4
HEAD_DIM = 64
DQK = 32
ROPE_THETA = 10000.0
CMP_BLOCK = 32
CMP_HIDDEN = 128
SEL_BLOCK = 64
SEL_TOPK = 16
WINDOW = 512
CONV_W = 4
LRU_C = 8.0
MEM_HEADS = 4
MEM_HEAD_DIM = 256
EPS = 1e-6
NEG = -1e30
FORCE_SCORE = 1e9
TINY = 1e-30

VMEM_LIMIT_BYTES = 48 * 1024 * 1024
PAGES_PER_STEP = 32
PAGES_PER_STEP_SLC = 32
PAGES_PER_STEP_CMP = 64

_SRC_SPLITS = (
    ('ml_q', 256), ('ml_k', 256), ('ml_v', 256), ('ml_i', 4), ('ml_f', 4), ('ml_o', 256),
    ('df_q', 256), ('df_k', 256), ('df_v', 256),
    ('ns_q', 256), ('ns_kc', 64), ('ns_vc', 64), ('ns_ks', 64), ('ns_vs', 64),
    ('ns_kw', 64), ('ns_vw', 64), ('ns_g', 12), ('lr_x', 256), ('lr_g', 256),
)
_DST_ORDER = ('ml_q', 'ml_k', 'ml_v', 'ml_o', 'df_q', 'df_k', 'df_v', 'ns_q', 'lr_x', 'lr_g',
              'ns_kc', 'ns_vc', 'ns_ks', 'ns_vs', 'ns_kw', 'ns_vw', 'ml_i', 'ml_f', 'ns_g')
D_PROJ_PAD = 3072
CB_ML_Q, CB_ML_K, CB_ML_V, CB_ML_O, CB_DF_Q, CB_DF_K, CB_DF_V, CB_NS_Q, CB_LR_X, CB_LR_G = range(10)
CB128_CMP, CB128_SLC, CB128_WIN, CB128_SMALL = 20, 21, 22, 23
SMALL_GATES = 8


def _proj_perm():
    off = {}
    o = 0
    for name, w in _SRC_SPLITS:
        off[name] = (o, w)
        o += w
    idx = []
    for name in _DST_ORDER:
        s, w = off[name]
        idx.extend(range(s, s + w))
    n_real = len(idx)
    idx.extend([0] * (D_PROJ_PAD - n_real))
    return jnp.asarray(idx, jnp.int32), n_real


def _cparams(sem):
    return pltpu.CompilerParams(dimension_semantics=sem, vmem_limit_bytes=VMEM_LIMIT_BYTES)


def _act_dtype(rows):
    return bf16 if rows % 16 == 0 else f32


def _nt(a, b):
    return lax.dot_general(a, b, (((1,), (1,)), ((), ())), preferred_element_type=f32)


def _mm_kernel(*refs, n_parts, has_norm, has_res, act, tn):
    it = iter(refs)
    x_refs = [next(it) for _ in range(n_parts)]
    w_ref = next(it)
    g_ref = next(it) if has_norm else None
    r_ref = next(it) if has_res else None
    o_ref = next(it)
    h_ref = next(it)
    if has_norm:
        x = x_refs[0][...].astype(f32)
        y = x * lax.rsqrt(jnp.mean(x * x, axis=-1, keepdims=True) + EPS)
        h_ref[...] = (y * g_ref[...]).astype(bf16)
    elif n_parts > 1 or x_refs[0].dtype != bf16:
        off = 0
        for x_ref in x_refs:
            kp = x_ref.shape[-1]
            h_ref[:, off:off + kp] = x_ref[...].astype(bf16)
            off += kp
    else:
        h_ref = x_refs[0]
    N = o_ref.shape[-1]
    for c0 in range(0, N, tn):
        cs = slice(c0, min(c0 + tn, N))
        acc = jnp.dot(h_ref[...], w_ref[:, cs], preferred_element_type=f32)
        if act == 'relu2':
            acc = jnp.maximum(acc, 0.0)
            acc = acc * acc
        if has_res:
            acc = acc + r_ref[:, cs]
        o_ref[:, cs] = acc.astype(o_ref.dtype)


def matmul(x, w, *, g=None, res=None, act=None, out_dtype=f32, tm=512, tn=512):
    parts = list(x) if isinstance(x, (list, tuple)) else [x]
    M = parts[0].shape[0]
    K, N = w.shape
    assert sum(p.shape[1] for p in parts) == K
    tm = min(tm, M)
    assert M % tm == 0
    has_norm = g is not None
    has_res = res is not None
    assert not (has_norm and len(parts) > 1)
    in_specs = [pl.BlockSpec((tm, p.shape[1]), lambda i: (i, 0)) for p in parts]
    in_specs.append(pl.BlockSpec((K, N), lambda i: (0, 0)))
    args = parts + [w]
    if has_norm:
        in_specs.append(pl.BlockSpec((1, K), lambda i: (0, 0)))
        args.append(g.reshape(1, K).astype(f32))
    if has_res:
        in_specs.append(pl.BlockSpec((tm, N), lambda i: (i, 0)))
        args.append(res)
    return pl.pallas_call(
        functools.partial(_mm_kernel, n_parts=len(parts), has_norm=has_norm, has_res=has_res, act=act, tn=tn),
        out_shape=jax.ShapeDtypeStruct((M, N), out_dtype),
        grid=(M // tm,),
        in_specs=in_specs,
        out_specs=pl.BlockSpec((tm, N), lambda i: (i, 0)),
        scratch_shapes=[pltpu.VMEM((tm, K), bf16)],
        compiler_params=_cparams(("parallel",)),
        name="matmul",
    )(*args)


def _rmsnorm_kernel(x_ref, g_ref, o_ref):
    x = x_ref[...]
    y = x * lax.rsqrt(jnp.mean(x * x, axis=-1, keepdims=True) + EPS)
    o_ref[...] = y * g_ref[...]


def rmsnorm_rows(x, g, tm=1024):
    M, K = x.shape
    tm = min(tm, M)
    return pl.pallas_call(
        _rmsnorm_kernel,
        out_shape=jax.ShapeDtypeStruct((M, K), f32),
        grid=(M // tm,),
        in_specs=[pl.BlockSpec((tm, K), lambda i: (i, 0)), pl.BlockSpec((1, K), lambda i: (0, 0))],
        out_specs=pl.BlockSpec((tm, K), lambda i: (i, 0)),
        compiler_params=_cparams(("parallel",)),
        name="final_norm",
    )(x, g.reshape(1, K))


def _rotate(x, cos, sin, half):
    n = x.shape[-1]
    lane = lax.broadcasted_iota(jnp.int32, x.shape, 1)
    first = (lane & (2 * half - 1)) < half
    partner = jnp.where(first, pltpu.roll(x, n - half, 1), pltpu.roll(x, half, 1))
    return x * cos + partner * sin


def _rope_kernel(dq, dk, nq, sl, wn, ca, sa, cb, sb, cc, sc, odq, odk, onq, osl, own):
    odq[...] = _rotate(dq[...], ca[...], sa[...], DQK // 2)
    odk[...] = _rotate(dk[...], ca[...], sa[...], DQK // 2)
    onq[...] = _rotate(nq[...], cb[...], sb[...], HEAD_DIM // 2)
    osl[...] = _rotate(sl[...], cc[...], sc[...], HEAD_DIM // 2)
    own[...] = _rotate(wn[...], cc[...], sc[...], HEAD_DIM // 2)


def rope_tables(pos):
    posf = pos.astype(f32)[:, None]

    def tab(half, reps):
        inv = ROPE_THETA ** (-jnp.arange(half, dtype=f32) / half)
        ang = posf * inv[None, :]
        c = jnp.cos(ang)
        s = jnp.sin(ang)
        return jnp.tile(jnp.concatenate([c, c], 1), (1, reps)), jnp.tile(jnp.concatenate([-s, s], 1), (1, reps))

    ca, sa = tab(DQK // 2, GROUP_WIDTH // DQK)
    cb, sb = tab(HEAD_DIM // 2, N_HEADS)
    n = pos.shape[0]
    cc = jnp.concatenate([cb[:, :HEAD_DIM], jnp.ones((n, HEAD_DIM), f32)], 1)
    sc = jnp.concatenate([sb[:, :HEAD_DIM], jnp.zeros((n, HEAD_DIM), f32)], 1)
    return ca, sa, cb, sb, cc, sc


def rope_prep(z, tables, tm):
    N = z.shape[0]
    R = tables[0].shape[0]
    tm = min(tm, R)
    assert R % tm == 0 and N % tm == 0
    nr = R // tm
    zs = lambda cb: pl.BlockSpec((tm, 256), lambda i, cb=cb: (i, cb))
    zs128 = lambda cb: pl.BlockSpec((tm, 128), lambda i, cb=cb: (i, cb))
    t256 = pl.BlockSpec((tm, 256), lambda i: (i % nr, 0))
    t128 = pl.BlockSpec((tm, 128), lambda i: (i % nr, 0))
    o256 = pl.BlockSpec((tm, 256), lambda i: (i, 0))
    o128 = pl.BlockSpec((tm, 128), lambda i: (i, 0))
    return pl.pallas_call(
        _rope_kernel,
        out_shape=[jax.ShapeDtypeStruct((N, 256), f32)] * 3 + [jax.ShapeDtypeStruct((N, 128), f32)] * 2,
        grid=(N // tm,),
        in_specs=[zs(CB_DF_Q), zs(CB_DF_K), zs(CB_NS_Q), zs128(CB128_SLC), zs128(CB128_WIN),
                  t256, t256, t256, t256, t128, t128],
        out_specs=[o256, o256, o256, o128, o128],
        compiler_params=_cparams(("parallel",)),
        name="rope_prep",
    )(z, z, z, z, z, *tables)


ROW_CHUNK = 16
LANE_CHUNK = 512


def _softmax_tile(s, s_ref, pb_ref, m_ref, l_ref):
    rows, n = s.shape
    s_ref[0:rows, :] = s
    m_old = m_ref[...]
    m_new = jnp.maximum(m_old, jnp.max(s, axis=0, keepdims=True))
    alpha = jnp.exp(m_old - m_new)
    m_ref[...] = m_new
    for c0 in range(0, n, LANE_CHUNK):
        cs = slice(c0, c0 + LANE_CHUNK)
        mb = jnp.broadcast_to(m_new[:, cs], (ROW_CHUNK, LANE_CHUNK))
        part = jnp.zeros((8, LANE_CHUNK), f32)
        for r in range(0, rows, ROW_CHUNK):
            p = jnp.exp(s_ref[r:r + ROW_CHUNK, cs] - mb)
            pb_ref[r:r + ROW_CHUNK, cs] = p.astype(bf16)
            part = part + (p[0:8, :] + p[8:16, :])
        l_ref[:, cs] = alpha[:, cs] * l_ref[:, cs] + jnp.sum(part, axis=0, keepdims=True)
    return alpha


def _diff_lambda(lam_ref, lam_init):
    lq = lam_ref[...]
    return (jnp.exp(jnp.sum(lq[0:1] * lq[1:2], keepdims=True))
            - jnp.exp(jnp.sum(lq[2:3] * lq[3:4], keepdims=True)) + lam_init)


def _diff_kernel(qt_ref, kt_ref, q_ref, k_ref, v_ref, lam_ref, g_ref, o_ref, qp_ref, m_ref, l_ref, acc_ref,
                 oT_ref, s_ref, pb_ref, *, tq, tk, lam_init):
    qi = qt_ref[pl.program_id(1)]
    kj = kt_ref[pl.program_id(1)]
    q_lo = qi * tq
    last = (q_lo + tq - 1) // tk
    nmap = 2 * N_HEADS

    @pl.when(kj == 0)
    def _():
        qT = jnp.transpose(q_ref[0]) * (DQK ** -0.5)
        row = lax.broadcasted_iota(jnp.int32, qT.shape, 0)
        for c in range(nmap):
            blk = jnp.where((row >= DQK * c) & (row < DQK * (c + 1)), qT, 0.0)
            qp_ref[:, c * tq:(c + 1) * tq] = blk.astype(bf16)
        m_ref[...] = jnp.full(m_ref.shape, NEG, f32)
        l_ref[...] = jnp.zeros(l_ref.shape, f32)
        acc_ref[...] = jnp.zeros(acc_ref.shape, f32)

    def step(masked):
        k = k_ref[0].astype(bf16)
        s = jnp.dot(k, qp_ref[...], preferred_element_type=f32)
        if masked:
            kpos = kj * tk + lax.broadcasted_iota(jnp.int32, s.shape, 0)
            qpos = q_lo + (lax.broadcasted_iota(jnp.int32, s.shape, 1) & (tq - 1))
            ok = kpos <= qpos
            s = jnp.where(ok, s, NEG)
        alpha = _softmax_tile(s, s_ref, pb_ref, m_ref, l_ref)
        vT = jnp.transpose(v_ref[0]).astype(bf16)
        for h in range(N_HEADS):
            rs = slice(HEAD_DIM * h, HEAD_DIM * (h + 1))
            cs = slice(2 * tq * h, 2 * tq * (h + 1))
            acc_ref[rs, :] = acc_ref[rs, :] * alpha[:, cs] + jnp.dot(
                vT[rs, :], pb_ref[:, cs], preferred_element_type=f32)

    crosses = kj * tk + tk - 1 > q_lo

    @pl.when(crosses)
    def _():
        step(True)

    @pl.when(jnp.logical_not(crosses))
    def _():
        step(False)

    @pl.when(kj == last)
    def _():
        lam = _diff_lambda(lam_ref, lam_init)
        l = l_ref[...]
        for h in range(N_HEADS):
            rs = slice(HEAD_DIM * h, HEAD_DIM * (h + 1))
            c1 = slice(2 * h * tq, (2 * h + 1) * tq)
            c2 = slice((2 * h + 1) * tq, (2 * h + 2) * tq)
            o = acc_ref[rs, 0:tq] / l[:, c1] - lam * (acc_ref[rs, tq:2 * tq] / l[:, c2])
            y = o * lax.rsqrt(jnp.mean(o * o, axis=0, keepdims=True) + EPS)
            oT_ref[rs, :] = (y * g_ref[...]) * (1.0 - lam_init)
        o_ref[0] = jnp.transpose(oT_ref[...]).astype(o_ref.dtype)


def diff_attention(q, k, z3, lam_q, norm_g, *, lam_init, tq, tk):
    B, T, _ = q.shape
    assert T % tq == 0 and T % tk == 0 and (tq & (tq - 1)) == 0 and tq % 128 == 0 and tk % 128 == 0
    nq = T // tq
    pairs = [(qi, kj) for qi in range(nq) for kj in range((qi * tq + tq - 1) // tk + 1)]
    qt = jnp.asarray([p[0] for p in pairs], jnp.int32)
    kt = jnp.asarray([p[1] for p in pairs], jnp.int32)
    kern = functools.partial(_diff_kernel, tq=tq, tk=tk, lam_init=lam_init)
    return pl.pallas_call(
        kern,
        out_shape=jax.ShapeDtypeStruct((B, T, GROUP_WIDTH), bf16),
        grid_spec=pltpu.PrefetchScalarGridSpec(
            num_scalar_prefetch=2, grid=(B, len(pairs)),
            in_specs=[
                pl.BlockSpec((1, tq, GROUP_WIDTH), lambda b, t, qt, kt: (b, qt[t], 0)),
                pl.BlockSpec((1, tk, GROUP_WIDTH), lambda b, t, qt, kt: (b, kt[t], 0)),
                pl.BlockSpec((1, tk, GROUP_WIDTH), lambda b, t, qt, kt: (b, kt[t], CB_DF_V)),
                pl.BlockSpec((4, DQK), lambda b, t, qt, kt: (0, 0)),
                pl.BlockSpec((HEAD_DIM, 1), lambda b, t, qt, kt: (0, 0)),
            ],
            out_specs=pl.BlockSpec((1, tq, GROUP_WIDTH), lambda b, t, qt, kt: (b, qt[t], 0)),
            scratch_shapes=[pltpu.VMEM((GROUP_WIDTH, 8 * tq), bf16), pltpu.VMEM((1, 8 * tq), f32),
                            pltpu.VMEM((1, 8 * tq), f32), pltpu.VMEM((GROUP_WIDTH, 2 * tq), f32),
                            pltpu.VMEM((GROUP_WIDTH, tq), f32),
                            pltpu.VMEM((tk, 8 * tq), f32), pltpu.VMEM((tk, 8 * tq), bf16)]),
        compiler_params=_cparams(("parallel", "arbitrary")),
        name="diff_attention",
    )(qt, kt, q, k, z3, lam_q, norm_g.reshape(HEAD_DIM, 1))


def _diff_decode_kernel(pt_ref, q_ref, kn_ref, vn_ref, lam_ref, g_ref, *rest, G, n_steps, Ts, pos0, n_past,
                        lam_init):
    kpages = rest[:G]
    vpages = rest[G:2 * G]
    o_ref, qp_ref, m_ref, l_ref, acc_ref = rest[2 * G:]
    j = pl.program_id(1)
    nmap = 2 * N_HEADS
    R = nmap * Ts

    @pl.when(j == 0)
    def _():
        q = q_ref[0] * (DQK ** -0.5)
        col = lax.broadcasted_iota(jnp.int32, q.shape, 1)
        for c in range(nmap):
            qp_ref[c * Ts:(c + 1) * Ts, :] = jnp.where((col >= DQK * c) & (col < DQK * (c + 1)), q, 0.0)
        s = _nt(qp_ref[...].astype(bf16), kn_ref[0].astype(bf16))
        qpos = pos0 + (lax.broadcasted_iota(jnp.int32, s.shape, 0) & (Ts - 1))
        kpos = n_past + lax.broadcasted_iota(jnp.int32, s.shape, 1)
        ok = kpos <= qpos
        s = jnp.where(ok, s, NEG)
        m = jnp.max(s, axis=1, keepdims=True)
        p = jnp.where(ok, jnp.exp(s - m), 0.0)
        m_ref[...] = m
        l_ref[...] = jnp.sum(p, axis=1, keepdims=True)
        acc_ref[...] = jnp.dot(p.astype(bf16), vn_ref[0].astype(bf16), preferred_element_type=f32)

    kT = jnp.concatenate([kp[...] for kp in kpages], axis=1).astype(bf16)
    vT = jnp.concatenate([vp[...] for vp in vpages], axis=1).astype(bf16)
    s = jnp.dot(qp_ref[...].astype(bf16), kT, preferred_element_type=f32)
    m_old = m_ref[...]
    m_new = jnp.maximum(m_old, jnp.max(s, axis=1, keepdims=True))
    alpha = jnp.exp(m_old - m_new)
    p = jnp.exp(s - m_new)
    l_ref[...] = alpha * l_ref[...] + jnp.sum(p, axis=1, keepdims=True)
    m_ref[...] = m_new
    acc_ref[...] = alpha * acc_ref[...] + _nt(p.astype(bf16), vT)

    @pl.when(j == n_steps - 1)
    def _():
        lam = _diff_lambda(lam_ref, lam_init)
        o_all = acc_ref[...] / l_ref[...]
        ys = []
        for h in range(N_HEADS):
            cs = slice(HEAD_DIM * h, HEAD_DIM * (h + 1))
            o = o_all[2 * h * Ts:(2 * h + 1) * Ts, cs] - lam * o_all[(2 * h + 1) * Ts:(2 * h + 2) * Ts, cs]
            y = o * lax.rsqrt(jnp.mean(o * o, axis=1, keepdims=True) + EPS)
            ys.append((y * g_ref[...]) * (1.0 - lam_init))
        o_ref[0] = jnp.concatenate(ys, axis=1)


def _page_specs(layer, n_pages, G, block):
    def spec(i):
        return pl.BlockSpec((None, None) + block,
                            lambda b, j, pt, i=i: (layer, pt[b * n_pages + j * G + i]) + (0,) * len(block))
    return [spec(i) for i in range(G)]


def diff_decode(page_table, q, k_new, v_new, pool_k, pool_v, lam_q, norm_g, *, layer, pos0, lam_init):
    B, Ts, _ = q.shape
    n_pages = page_table.shape[1]
    page = pool_k.shape[3]
    G = min(PAGES_PER_STEP, n_pages)
    assert n_pages % G == 0 and (Ts & (Ts - 1)) == 0 and Ts % 8 == 0
    n_steps = n_pages // G
    R = 2 * N_HEADS * Ts
    row = lambda w: pl.BlockSpec((1, Ts, w), lambda b, j, pt: (b, 0, 0))
    kern = functools.partial(_diff_decode_kernel, G=G, n_steps=n_steps, Ts=Ts, pos0=pos0,
                             n_past=n_pages * page, lam_init=lam_init)
    return pl.pallas_call(
        kern,
        out_shape=jax.ShapeDtypeStruct((B, Ts, GROUP_WIDTH), f32),
        grid_spec=pltpu.PrefetchScalarGridSpec(
            num_scalar_prefetch=1, grid=(B, n_steps),
            in_specs=[row(256), row(256), row(256),
                      pl.BlockSpec((4, DQK), lambda b, j, pt: (0, 0)),
                      pl.BlockSpec((1, HEAD_DIM), lambda b, j, pt: (0, 0))]
            + _page_specs(layer, n_pages, G, (256, page)) + _page_specs(layer, n_pages, G, (256, page)),
            out_specs=row(256),
            scratch_shapes=[pltpu.VMEM((R, GROUP_WIDTH), f32), pltpu.VMEM((R, 1), f32),
                            pltpu.VMEM((R, 1), f32), pltpu.VMEM((R, GROUP_WIDTH), f32)]),
        compiler_params=_cparams(("parallel", "arbitrary")),
        name="diff_decode",
    )(page_table.reshape(-1), q, k_new, v_new, lam_q, norm_g.reshape(1, HEAD_DIM),
      *([pool_k] * G), *([pool_v] * G))


def _compress_rows(x, pos_ref, w1_ref, w2_ref):
    x = (x + pos_ref[...]).astype(bf16)
    hid = jax.nn.gelu(jnp.dot(x, w1_ref[...], preferred_element_type=f32))
    return jnp.dot(hid.astype(bf16), w2_ref[...], preferred_element_type=f32)


def _compress_kernel(x_ref, pos_ref, w1_ref, w2_ref, o_ref):
    o_ref[...] = _compress_rows(x_ref[...], pos_ref, w1_ref, w2_ref)


def nsa_compress(blocks, pos_flat, w1c, w2c, tm=256):
    R, K = blocks.shape
    tm = min(tm, R)
    assert R % tm == 0
    return pl.pallas_call(
        _compress_kernel,
        out_shape=jax.ShapeDtypeStruct((R, 2 * HEAD_DIM), f32),
        grid=(R // tm,),
        in_specs=[pl.BlockSpec((tm, K), lambda i: (i, 0)), pl.BlockSpec((1, K), lambda i: (0, 0)),
                  pl.BlockSpec((K, 2 * CMP_HIDDEN), lambda i: (0, 0)),
                  pl.BlockSpec((2 * CMP_HIDDEN, 2 * HEAD_DIM), lambda i: (0, 0))],
        out_specs=pl.BlockSpec((tm, 2 * HEAD_DIM), lambda i: (i, 0)),
        compiler_params=_cparams(("parallel",)),
        name="nsa_compress",
    )(blocks, pos_flat, w1c, w2c)


def _compress_paged_kernel(pt_ref, pos_ref, w1_ref, w2_ref, *rest, G, page):
    pages = rest[:G]
    o_ref, xs = rest[G:]
    w = 2 * HEAD_DIM
    for i in range(G):
        xs[page * i:page * (i + 1), :] = jnp.transpose(pages[i][...])
    nb = G * page // CMP_BLOCK
    acc = None
    for t in range(CMP_BLOCK):
        xt = xs[pl.ds(t, nb, stride=CMP_BLOCK), :] + pos_ref[:, w * t:w * (t + 1)]
        part = jnp.dot(xt.astype(bf16), w1_ref[w * t:w * (t + 1), :], preferred_element_type=f32)
        acc = part if acc is None else acc + part
    hid = jax.nn.gelu(acc)
    o_ref[0] = jnp.dot(hid.astype(bf16), w2_ref[...], preferred_element_type=f32)


def nsa_compress_paged(page_table, pool, pos_flat, w1c, w2c, *, layer):
    B, n_pages = page_table.shape
    page = pool.shape[3]
    K = CMP_BLOCK * 2 * HEAD_DIM
    rp = page // CMP_BLOCK
    G = min(PAGES_PER_STEP_CMP, n_pages)
    assert n_pages % G == 0 and page % CMP_BLOCK == 0 and (G * rp) % 8 == 0
    const = lambda shape: pl.BlockSpec(shape, lambda b, j, pt: (0, 0))
    return pl.pallas_call(
        functools.partial(_compress_paged_kernel, G=G, page=page),
        out_shape=jax.ShapeDtypeStruct((B, n_pages * rp, 2 * HEAD_DIM), f32),
        grid_spec=pltpu.PrefetchScalarGridSpec(
            num_scalar_prefetch=1, grid=(B, n_pages // G),
            in_specs=[const((1, K)), const((K, 2 * CMP_HIDDEN)), const((2 * CMP_HIDDEN, 2 * HEAD_DIM))]
            + _page_specs(layer, n_pages, G, (2 * HEAD_DIM, page)),
            out_specs=pl.BlockSpec((1, G * rp, 2 * HEAD_DIM), lambda b, j, pt: (b, j, 0)),
            scratch_shapes=[pltpu.VMEM((G * page, 2 * HEAD_DIM), f32)]),
        compiler_params=_cparams(("parallel", "arbitrary")),
        name="nsa_compress_paged",
    )(page_table.reshape(-1), pos_flat, w1c, w2c, *([pool] * G))


def _cmp_block_end(r, half):
    blk = jnp.where(r < half, 2 * r, 2 * (r - half) + 1)
    return (blk + 1) * CMP_BLOCK - 1


def _nsa_kernel(qt_ref, kt_ref, qr_in, qw_in, sm_in, cmp_ref, slc_ref, win_ref,
                o_ref, qr_ref, v_ref, sel_ref, ocmp_ref, ms_ref, ls_ref, accs_ref, mw_ref, lw_ref, accw_ref,
                oT_ref, s_ref, pb_ref, *, tq, tk, tkw, nc, nsp, topk):
    qi = qt_ref[pl.program_id(1)]
    kj = kt_ref[pl.program_id(1)]
    q_lo = qi * tq
    H = N_HEADS
    d = HEAD_DIM
    scale = d ** -0.5
    needed = (q_lo + tq - 1) // tk + 1
    w_lo = jnp.maximum(q_lo - (WINDOW - 1), 0) // tkw
    w_hi = (q_lo + tq - 1) // tkw
    n_steps = jnp.maximum(needed, w_hi - w_lo + 1)
    half = nc // 2

    def heads_on_lanes(xT):
        return jnp.concatenate([xT[d * h:d * (h + 1), :] for h in range(H)], axis=1)

    def qpos_row(n):
        return q_lo + (lax.broadcasted_iota(jnp.int32, (1, n), 1) & (tq - 1))

    @pl.when(kj == 0)
    def _():
        zeros = jnp.zeros((d, H * tq), f32)
        qrT = jnp.transpose(qr_in[0]) * scale
        qwT = jnp.transpose(qw_in[0]) * scale
        qr_ref[...] = jnp.concatenate([heads_on_lanes(qrT), zeros], 0).astype(bf16)
        qw = jnp.concatenate([heads_on_lanes(qwT), zeros], 0).astype(bf16)
        cmp = cmp_ref[0]
        s = jnp.dot(cmp.astype(bf16), qw, preferred_element_type=f32)
        c_end = _cmp_block_end(lax.broadcasted_iota(jnp.int32, (nc, 1), 0), half)
        c_ok = c_end <= qpos_row(H * tq)
        s = jnp.where(c_ok, s, NEG)
        p = jnp.where(c_ok, jnp.exp(s - jnp.max(s, axis=0, keepdims=True)), 0.0)
        p = p / jnp.maximum(jnp.sum(p, axis=0, keepdims=True), TINY)
        cmpT = jnp.transpose(cmp)
        ocmp_ref[...] = jnp.dot(cmpT[d:2 * d, :].astype(bf16), p.astype(bf16), preferred_element_type=f32)
        imp = p[:, 0:tq]
        for h in range(1, H):
            imp = imp + p[:, h * tq:(h + 1) * tq]
        imp = imp[:half, :] + imp[half:, :]
        if nsp > half:
            imp = jnp.concatenate([imp, jnp.zeros((nsp - half, tq), f32)], 0)
        sb = lax.broadcasted_iota(jnp.int32, (nsp, tq), 0)
        cur = qpos_row(tq) // SEL_BLOCK
        v = jnp.where((sb == cur) | (sb == 0), FORCE_SCORE, imp)
        v = jnp.where(sb > cur, NEG, v)
        v_ref[...] = v

        def rank(i, cnt):
            vi = v_ref[pl.ds(i, 1), :]
            ahead = (vi > v) | ((vi == v) & (i < sb))
            return cnt + jnp.where(ahead, 1.0, 0.0)

        cnt = lax.fori_loop(0, nsp, rank, jnp.zeros((nsp, tq), f32))
        sel_ref[...] = jnp.where(cnt < topk, 1.0, 0.0)
        for m_r, l_r, a_r in ((ms_ref, ls_ref, accs_ref), (mw_ref, lw_ref, accw_ref)):
            m_r[...] = jnp.full(m_r.shape, NEG, f32)
            l_r[...] = jnp.zeros(l_r.shape, f32)
            a_r[...] = jnp.zeros(a_r.shape, f32)

    def flash(kv_ref, ok, m_r, l_r, a_r):
        kv = kv_ref[0]
        s = jnp.dot(kv.astype(bf16), qr_ref[...], preferred_element_type=f32)
        s = s + jnp.concatenate([jnp.where(ok, 0.0, NEG)] * H, axis=1)
        alpha = _softmax_tile(s, s_ref, pb_ref, m_r, l_r)
        vT = jnp.transpose(kv)[d:2 * d, :].astype(bf16)
        a_r[...] = a_r[...] * alpha + jnp.dot(vT, pb_ref[0:kv.shape[0], :], preferred_element_type=f32)

    @pl.when(kj < needed)
    def _():
        kpos = kj * tk + lax.broadcasted_iota(jnp.int32, (tk, tq), 0)
        nb = tk // SEL_BLOCK
        rows = [jnp.broadcast_to(sel_ref[pl.ds(kj * nb + c, 1), :], (SEL_BLOCK, tq)) for c in range(nb)]
        chosen = jnp.concatenate(rows, axis=0) > 0.5
        ok = chosen & (kpos <= qpos_row(tq))
        flash(slc_ref, ok, ms_ref, ls_ref, accs_ref)

    @pl.when(kj <= w_hi - w_lo)
    def _():
        wpos = (w_lo + kj) * tkw + lax.broadcasted_iota(jnp.int32, (tkw, tq), 0)
        qp = qpos_row(tq)
        ok = (wpos <= qp) & (qp - wpos < WINDOW) & (wpos >= 0)
        flash(win_ref, ok, mw_ref, lw_ref, accw_ref)

    @pl.when(kj == n_steps - 1)
    def _():
        smT = jnp.transpose(sm_in[0])
        g = jax.nn.sigmoid(smT[SMALL_GATES:SMALL_GATES + 3 * H, :])
        o_slc = accs_ref[...] / ls_ref[...]
        o_win = accw_ref[...] / lw_ref[...]
        o_cmp = ocmp_ref[...]
        for h in range(H):
            cs = slice(h * tq, (h + 1) * tq)
            oT_ref[d * h:d * (h + 1), :] = (g[3 * h:3 * h + 1] * o_cmp[:, cs]
                                            + g[3 * h + 1:3 * h + 2] * o_slc[:, cs]
                                            + g[3 * h + 2:3 * h + 3] * o_win[:, cs])
        o_ref[0] = jnp.transpose(oT_ref[...]).astype(o_ref.dtype)


def nsa_attention(qr, z3, cmp, slc, win, *, tq, tk, tkw):
    B, T, _ = qr.shape
    nc = cmp.shape[1]
    assert T % tq == 0 and T % tk == 0 and T % tkw == 0 and tk % SEL_BLOCK == 0 and nc % 2 == 0
    assert tq % 128 == 0 and tk % 128 == 0 and tkw % 128 == 0 and nc % 8 == 0 and (tq & (tq - 1)) == 0
    nq = T // tq
    nsp = T // SEL_BLOCK
    assert nsp >= nc // 2 and nsp % 8 == 0
    last = lambda qi: (qi * tq + tq - 1) // tk
    w_lo = lambda qi: jnp.maximum(qi * tq - (WINDOW - 1), 0) // tkw
    w_hi = lambda qi: (qi * tq + tq - 1) // tkw
    pairs = []
    for qi in range(nq):
        n_win = (qi * tq + tq - 1) // tkw - max(qi * tq - (WINDOW - 1), 0) // tkw + 1
        pairs += [(qi, kj) for kj in range(max((qi * tq + tq - 1) // tk + 1, n_win))]
    qt = jnp.asarray([p[0] for p in pairs], jnp.int32)
    kt = jnp.asarray([p[1] for p in pairs], jnp.int32)
    kern = functools.partial(_nsa_kernel, tq=tq, tk=tk, tkw=tkw, nc=nc, nsp=nsp, topk=min(SEL_TOPK, nsp))
    stat = pltpu.VMEM((1, N_HEADS * tq), f32)
    acc = pltpu.VMEM((HEAD_DIM, N_HEADS * tq), f32)
    return pl.pallas_call(
        kern,
        out_shape=jax.ShapeDtypeStruct((B, T, GROUP_WIDTH), bf16),
        grid_spec=pltpu.PrefetchScalarGridSpec(
            num_scalar_prefetch=2, grid=(B, len(pairs)),
            in_specs=[
                pl.BlockSpec((1, tq, GROUP_WIDTH), lambda b, t, qt, kt: (b, qt[t], 0)),
                pl.BlockSpec((1, tq, GROUP_WIDTH), lambda b, t, qt, kt: (b, qt[t], CB_NS_Q)),
                pl.BlockSpec((1, tq, 128), lambda b, t, qt, kt: (b, qt[t], CB128_SMALL)),
                pl.BlockSpec((1, nc, 128), lambda b, t, qt, kt: (b, 0, 0)),
                pl.BlockSpec((1, tk, 128), lambda b, t, qt, kt: (b, jnp.minimum(kt[t], last(qt[t])), 0)),
                pl.BlockSpec((1, tkw, 128),
                             lambda b, t, qt, kt: (b, jnp.minimum(w_lo(qt[t]) + kt[t], w_hi(qt[t])), 0)),
            ],
            out_specs=pl.BlockSpec((1, tq, GROUP_WIDTH), lambda b, t, qt, kt: (b, qt[t], 0)),
            scratch_shapes=[pltpu.VMEM((128, N_HEADS * tq), bf16), pltpu.VMEM((nsp, tq), f32),
                            pltpu.VMEM((nsp, tq), f32), acc, stat, stat, acc, stat, stat, acc,
                            pltpu.VMEM((GROUP_WIDTH, tq), f32),
                            pltpu.VMEM((max(tk, tkw), N_HEADS * tq), f32),
                            pltpu.VMEM((max(tk, tkw), N_HEADS * tq), bf16)]),
        compiler_params=_cparams(("parallel", "arbitrary")),
        name="nsa_attention",
    )(qt, kt, qr, z3, z3, cmp, slc, win)


def _nsa_decode_kernel(pt_ref, qr_in, qw_in, sm_in, cmp_ref, sn_ref, wb_ref, wn_ref, *rest,
                       G, n_steps, Ts, pos0, n_past, wpos0, nc, nsl, n_sel, topk, page):
    pages = rest[:G]
    o_ref, qr_ref, selst_ref, ocmp_ref, owin_ref, m_ref, l_ref, acc_ref = rest[G:]
    j = pl.program_id(1)
    H, d = N_HEADS, HEAD_DIM
    scale = d ** -0.5
    R = H * Ts
    half = nc // 2
    bps = G * page // SEL_BLOCK

    def rows_by_head(x):
        z = jnp.zeros((Ts, d), f32)
        return jnp.concatenate([jnp.concatenate([x[:, d * h:d * (h + 1)], z], axis=1) for h in range(H)], axis=0)

    def tile_heads(x):
        return jnp.concatenate([x] * H, axis=0)

    @pl.when(j == 0)
    def _():
        qr = rows_by_head(qr_in[0] * scale)
        qr_ref[...] = qr
        qrb = qr.astype(bf16)
        qwb = rows_by_head(qw_in[0] * scale).astype(bf16)
        qpos = pos0 + (lax.broadcasted_iota(jnp.int32, (R, 1), 0) & (Ts - 1))
        cmpb = cmp_ref[0].astype(bf16)
        s = _nt(qwb, cmpb)
        c_ok = _cmp_block_end(lax.broadcasted_iota(jnp.int32, (1, nc), 1), half) <= qpos
        s = jnp.where(c_ok, s, NEG)
        p = jnp.where(c_ok, jnp.exp(s - jnp.max(s, axis=1, keepdims=True)), 0.0)
        p = p / jnp.maximum(jnp.sum(p, axis=1, keepdims=True), TINY)
        ocmp_ref[...] = jnp.dot(p.astype(bf16), cmpb, preferred_element_type=f32)
        imp = p[0:Ts, :]
        for h in range(1, H):
            imp = imp + p[h * Ts:(h + 1) * Ts, :]
        imp = imp[:, :half] + imp[:, half:]
        imp = jnp.concatenate([imp, jnp.zeros((Ts, nsl - half), f32)], axis=1)
        lane = lax.broadcasted_iota(jnp.int32, (Ts, nsl), 1)
        cur = (pos0 + lax.broadcasted_iota(jnp.int32, (Ts, 1), 0)) // SEL_BLOCK
        v = jnp.where((lane == cur) | (lane == 0), FORCE_SCORE, imp)
        v = jnp.where(lane > cur, NEG, v)
        cnt = jnp.zeros((Ts, nsl), f32)
        for i in range(n_sel):
            vi = v[:, i:i + 1]
            cnt = cnt + jnp.where((vi > v) | ((vi == v) & (lane > i)), 1.0, 0.0)
        sel = jnp.where(cnt < topk, 1.0, 0.0)
        for st in range(n_steps):
            selst_ref[st] = sel[:, bps * st:bps * (st + 1)]
        blk_new = n_past // SEL_BLOCK
        sel_new = tile_heads(sel[:, blk_new:blk_new + 1]) > 0.5
        snb = sn_ref[0].astype(bf16)
        s = _nt(qrb, snb)
        kpos = n_past + lax.broadcasted_iota(jnp.int32, (1, Ts), 1)
        ok = sel_new & (kpos <= qpos)
        s = jnp.where(ok, s, NEG)
        m = jnp.max(s, axis=1, keepdims=True)
        p = jnp.where(ok, jnp.exp(s - m), 0.0)
        m_ref[...] = m
        l_ref[...] = jnp.sum(p, axis=1, keepdims=True)
        acc_ref[...] = jnp.dot(p.astype(bf16), snb, preferred_element_type=f32)
        wbT = wb_ref[...].astype(bf16)
        wnb = wn_ref[0].astype(bf16)
        lw = wbT.shape[1]
        s1 = jnp.dot(qrb, wbT, preferred_element_type=f32)
        s2 = _nt(qrb, wnb)
        wp1 = wpos0 + lax.broadcasted_iota(jnp.int32, (1, lw), 1)
        wp2 = wpos0 + lw + lax.broadcasted_iota(jnp.int32, (1, Ts), 1)
        ok1 = (wp1 <= qpos) & (qpos - wp1 < WINDOW) & (wp1 >= 0)
        ok2 = (wp2 <= qpos) & (qpos - wp2 < WINDOW) & (wp2 >= 0)
        s1 = jnp.where(ok1, s1, NEG)
        s2 = jnp.where(ok2, s2, NEG)
        mw = jnp.maximum(jnp.max(s1, axis=1, keepdims=True), jnp.max(s2, axis=1, keepdims=True))
        p1 = jnp.where(ok1, jnp.exp(s1 - mw), 0.0)
        p2 = jnp.where(ok2, jnp.exp(s2 - mw), 0.0)
        lsum = jnp.sum(p1, axis=1, keepdims=True) + jnp.sum(p2, axis=1, keepdims=True)
        owin_ref[...] = (_nt(p1.astype(bf16), wbT)
                         + jnp.dot(p2.astype(bf16), wnb, preferred_element_type=f32)) / lsum

    kvT = jnp.concatenate([pg[...] for pg in pages], axis=1).astype(bf16)
    n = kvT.shape[1]
    s = jnp.dot(qr_ref[...].astype(bf16), kvT, preferred_element_type=f32)
    expand = (lax.broadcasted_iota(jnp.int32, (bps, n), 1) // SEL_BLOCK
              == lax.broadcasted_iota(jnp.int32, (bps, n), 0))
    chosen = jnp.dot(selst_ref[j].astype(bf16), jnp.where(expand, 1.0, 0.0).astype(bf16),
                     preferred_element_type=f32)
    ok = tile_heads(chosen) > 0.5
    s = jnp.where(ok, s, NEG)
    m_old = m_ref[...]
    m_new = jnp.maximum(m_old, jnp.max(s, axis=1, keepdims=True))
    alpha = jnp.exp(m_old - m_new)
    p = jnp.where(ok, jnp.exp(s - m_new), 0.0)
    l_ref[...] = alpha * l_ref[...] + jnp.sum(p, axis=1, keepdims=True)
    m_ref[...] = m_new
    acc_ref[...] = alpha * acc_ref[...] + _nt(p.astype(bf16), kvT)

    @pl.when(j == n_steps - 1)
    def _():
        g = jax.nn.sigmoid(sm_in[0][:, SMALL_GATES:SMALL_GATES + 3 * H])
        o_slc = acc_ref[...] / l_ref[...]
        outs = []
        for h in range(H):
            rs = slice(h * Ts, (h + 1) * Ts)
            outs.append(g[:, 3 * h:3 * h + 1] * ocmp_ref[rs, d:2 * d]
                        + g[:, 3 * h + 1:3 * h + 2] * o_slc[rs, d:2 * d]
                        + g[:, 3 * h + 2:3 * h + 3] * owin_ref[rs, d:2 * d])
        o_ref[0] = jnp.concatenate(outs, axis=1)


def nsa_decode(page_table, qr, z3, cmp, slc_new, win_cache, win_new, pool_slc, *, layer, pos0):
    B, Ts, _ = qr.shape
    n_pages = page_table.shape[1]
    page = pool_slc.shape[3]
    n_past = n_pages * page
    lw = win_cache.shape[3]
    nc = cmp.shape[1]
    G = min(PAGES_PER_STEP_SLC, n_pages)
    n_steps = n_pages // G
    n_sel = -(-(n_past + Ts) // SEL_BLOCK)
    nsl = -(-n_sel // 128) * 128
    assert n_pages % G == 0 and (Ts & (Ts - 1)) == 0 and Ts % 8 == 0 and page % SEL_BLOCK == 0
    assert pos0 == n_past and n_past % SEL_BLOCK + Ts <= SEL_BLOCK and nc % 2 == 0 and nc // 2 <= nsl
    R = N_HEADS * Ts
    row = lambda w, cb=0: pl.BlockSpec((1, Ts, w), lambda b, j, pt, cb=cb: (b, 0, cb))
    kern = functools.partial(_nsa_decode_kernel, G=G, n_steps=n_steps, Ts=Ts, pos0=pos0, n_past=n_past,
                             wpos0=n_past - lw, nc=nc, nsl=nsl, n_sel=n_sel, topk=min(SEL_TOPK, n_sel),
                             page=page)
    buf = lambda: pltpu.VMEM((R, 128), f32)
    stat = lambda: pltpu.VMEM((R, 1), f32)
    return pl.pallas_call(
        kern,
        out_shape=jax.ShapeDtypeStruct((B, Ts, GROUP_WIDTH), f32),
        grid_spec=pltpu.PrefetchScalarGridSpec(
            num_scalar_prefetch=1, grid=(B, n_steps),
            in_specs=[row(256), row(256, CB_NS_Q), row(128, CB128_SMALL),
                      pl.BlockSpec((1, nc, 128), lambda b, j, pt: (b, 0, 0)),
                      row(128),
                      pl.BlockSpec((None, None, 128, lw), lambda b, j, pt: (layer, b, 0, 0)),
                      row(128)]
            + _page_specs(layer, n_pages, G, (128, page)),
            out_specs=row(256),
            scratch_shapes=[buf(), pltpu.VMEM((n_steps, Ts, G * page // SEL_BLOCK), f32), buf(), buf(),
                            stat(), stat(), buf()]),
        compiler_params=_cparams(("parallel", "arbitrary")),
        name="nsa_decode",
    )(page_table.reshape(-1), qr, z3, z3, cmp, slc_new, win_cache, win_new, *([pool_slc] * G))


def _mlstm_kernel(q_ref, k_ref, v_ref, og_ref, gc_ref, gr_ref, bias_ref, ng_ref, c0_ref, n0_ref, m0_ref,
                  o_ref, cout_ref, nout_ref, mout_ref, c_s, n_s, m_s, *, L):
    ci = pl.program_id(1)
    H, d = N_HEADS, HEAD_DIM

    @pl.when(ci == 0)
    def _():
        c_s[...] = c0_ref[0]
        n_s[...] = n0_ref[0]
        m_s[...] = m0_ref[0]

    q = q_ref[...]
    k = k_ref[...] * (d ** -0.5)
    v = v_ref[...]
    gc = gc_ref[...]
    gr = gr_ref[0]
    bias = bias_ref[...]
    t_i = lax.broadcasted_iota(jnp.int32, (L, L), 0)
    s_i = lax.broadcasted_iota(jnp.int32, (L, L), 1)
    causal = s_i <= t_i
    causal_T = t_i <= s_i
    outs = []
    for h in range(H):
        bi = bias[0:1, h:h + 1]
        bf_ = bias[1:2, h:h + 1]
        ig_c = gc[:, h:h + 1] + bi
        lf_c = jax.nn.log_sigmoid(gc[:, H + h:H + h + 1] + bf_)
        ig_r = gr[h:h + 1, :] + bi
        lf_r = jax.nn.log_sigmoid(gr[H + h:H + h + 1, :] + bf_)
        b_c = jnp.sum(jnp.where(causal, lf_r, 0.0), axis=1, keepdims=True)
        b_r = jnp.sum(jnp.where(causal_T, lf_c, 0.0), axis=0, keepdims=True)
        m_prev = m_s[h:h + 1, 0:1]
        D = jnp.where(causal, b_c - b_r + ig_r, -jnp.inf)
        m_t = jnp.maximum(b_c + m_prev, jnp.max(D, axis=1, keepdims=True))
        inter = jnp.exp(b_c + m_prev - m_t)
        qh = q[:, d * h:d * (h + 1)]
        kh = k[:, d * h:d * (h + 1)]
        vh = v[:, d * h:d * (h + 1)]
        qb = qh.astype(bf16)
        S = _nt(qb, kh.astype(bf16))
        Sw = jnp.exp(D - m_t) * S
        C = c_s[h]
        n_row = n_s[h:h + 1, :]
        num = (jnp.dot(Sw.astype(bf16), vh.astype(bf16), preferred_element_type=f32)
               + inter * jnp.dot(qb, C.astype(bf16), preferred_element_type=f32))
        qn = jnp.sum(qh * n_row, axis=1, keepdims=True)
        den = jnp.sum(Sw, axis=1, keepdims=True) + inter * qn
        hh = num / jnp.maximum(jnp.abs(den), jnp.exp(-m_t))
        m_new = m_t[L - 1:L, :]
        b_last = b_c[L - 1:L, :]
        wl = jnp.exp(b_last - b_c + ig_c - m_new)
        decay = jnp.exp(b_last + m_prev - m_new)
        kw = kh * wl
        c_s[h] = decay * C + lax.dot_general(kw.astype(bf16), vh.astype(bf16), (((0,), (0,)), ((), ())),
                                             preferred_element_type=f32)
        n_s[h:h + 1, :] = decay * n_row + jnp.sum(kw, axis=0, keepdims=True)
        m_s[h:h + 1, :] = jnp.broadcast_to(m_new, (1, 128))
        mu = jnp.mean(hh, axis=1, keepdims=True)
        var = jnp.mean(jnp.square(hh - mu), axis=1, keepdims=True)
        outs.append((hh - mu) * lax.rsqrt(var + EPS))
    hcat = jnp.concatenate(outs, axis=1) * ng_ref[...]
    o_ref[...] = (hcat * jax.nn.sigmoid(og_ref[...])).astype(o_ref.dtype)
    cout_ref[0] = c_s[...]
    nout_ref[0] = n_s[...]
    mout_ref[0] = m_s[...]


def mlstm(z, gates_T, gate_b, norm_g, C0, n0, m0, *, B, T, L):
    nchunk = T // L
    m0p = jnp.broadcast_to(m0[:, :, None], (B, N_HEADS, 128))
    row = lambda cb: pl.BlockSpec((L, 256), lambda b, c, cb=cb: (b * nchunk + c, cb))
    st = lambda *shape: pl.BlockSpec((1,) + shape, lambda b, c: (b,) + (0,) * len(shape))
    out, C, n, m = pl.pallas_call(
        functools.partial(_mlstm_kernel, L=L),
        out_shape=[jax.ShapeDtypeStruct((B * T, GROUP_WIDTH), _act_dtype(L)),
                   jax.ShapeDtypeStruct((B, N_HEADS, HEAD_DIM, HEAD_DIM), f32),
                   jax.ShapeDtypeStruct((B, N_HEADS, HEAD_DIM), f32),
                   jax.ShapeDtypeStruct((B, N_HEADS, 128), f32)],
        grid=(B, nchunk),
        in_specs=[row(CB_ML_Q), row(CB_ML_K), row(CB_ML_V), row(CB_ML_O),
                  pl.BlockSpec((L, 128), lambda b, c: (b * nchunk + c, CB128_SMALL)),
                  pl.BlockSpec((1, 8, L), lambda b, c: (b * nchunk + c, 0, 0)),
                  pl.BlockSpec((2, N_HEADS), lambda b, c: (0, 0)),
                  pl.BlockSpec((1, GROUP_WIDTH), lambda b, c: (0, 0)),
                  st(N_HEADS, HEAD_DIM, HEAD_DIM), st(N_HEADS, HEAD_DIM), st(N_HEADS, 128)],
        out_specs=[pl.BlockSpec((L, GROUP_WIDTH), lambda b, c: (b * nchunk + c, 0)),
                   st(N_HEADS, HEAD_DIM, HEAD_DIM), st(N_HEADS, HEAD_DIM), st(N_HEADS, 128)],
        scratch_shapes=[pltpu.VMEM((N_HEADS, HEAD_DIM, HEAD_DIM), f32), pltpu.VMEM((N_HEADS, HEAD_DIM), f32),
                        pltpu.VMEM((N_HEADS, 128), f32)],
        compiler_params=_cparams(("parallel", "arbitrary")),
        name="mlstm",
    )(z, z, z, z, z, gates_T, gate_b, norm_g.reshape(1, GROUP_WIDTH), C0, n0, m0p)
    return out, C, n, m[:, :, 0]


def _rglru_kernel(x_ref, gate_ref, cw_ref, cb_ref, wa_ref, ba_ref, wx_ref, bx_ref, lam_ref, h0_ref, buf0_ref,
                  y_ref, hout_ref, bufout_ref, xbuf, a_s, u_s, h_s, hs_s, *, tm):
    ti = pl.program_id(0)
    Bb = x_ref.shape[0]
    W = GROUP_WIDTH

    @pl.when(ti == 0)
    def _():
        xbuf[:, 0:8, :] = buf0_ref[...]
        h_s[...] = h0_ref[...]

    xbuf[:, 8:8 + tm, :] = x_ref[...]
    cw = cw_ref[...]
    xc = cb_ref[...] + cw[0:1, :] * xbuf[:, 5:5 + tm, :]
    for j in range(1, CONV_W):
        xc = xc + cw[j:j + 1, :] * xbuf[:, 5 + j:5 + j + tm, :]
    flat = xc.reshape(Bb * tm, W).astype(bf16)
    r = jax.nn.sigmoid(jnp.dot(flat, wa_ref[...], preferred_element_type=f32) + ba_ref[...])
    i = jax.nn.sigmoid(jnp.dot(flat, wx_ref[...], preferred_element_type=f32) + bx_ref[...])
    lam = lam_ref[...]
    softplus = jnp.maximum(-lam, 0.0) + jnp.log1p(jnp.exp(-jnp.abs(lam)))
    log_a = (-LRU_C * r) * softplus
    a = jnp.exp(log_a)
    u = jnp.sqrt(-jnp.tanh(log_a) * (a * a + 1.0)) * (i * xc.reshape(Bb * tm, W))
    a_s[...] = a.reshape(Bb, tm, W)
    u_s[...] = u.reshape(Bb, tm, W)

    def body(t, h):
        h = a_s[:, pl.ds(t, 1), :] * h + u_s[:, pl.ds(t, 1), :]
        hs_s[:, pl.ds(t, 1), :] = h
        return h

    h_last = lax.fori_loop(0, tm, body, h_s[...])
    h_s[...] = h_last
    y_ref[...] = (hs_s[...] * jax.nn.gelu(gate_ref[...])).astype(y_ref.dtype)
    hout_ref[...] = h_last
    bufout_ref[...] = xbuf[:, tm:tm + 8, :]
    xbuf[:, 0:8, :] = xbuf[:, tm:tm + 8, :]


def rglru(z3, conv_w, conv_b, wa_bd, ba, wx_bd, bx, lam, h0, buf0, *, tm):
    B, T, _ = z3.shape
    tm = min(tm, T)
    assert T % tm == 0 and tm % 8 == 0
    W = GROUP_WIDTH
    buf8 = jnp.concatenate([jnp.zeros((B, 8 - (CONV_W - 1), W), f32), buf0], 1)
    vec = lambda: pl.BlockSpec((1, W), lambda i: (0, 0))
    mat = lambda: pl.BlockSpec((W, W), lambda i: (0, 0))
    y, h, buf = pl.pallas_call(
        functools.partial(_rglru_kernel, tm=tm),
        out_shape=[jax.ShapeDtypeStruct((B, T, W), _act_dtype(tm)), jax.ShapeDtypeStruct((B, 1, W), f32),
                   jax.ShapeDtypeStruct((B, 8, W), f32)],
        grid=(T // tm,),
        in_specs=[pl.BlockSpec((B, tm, W), lambda i: (0, i, CB_LR_X)),
                  pl.BlockSpec((B, tm, W), lambda i: (0, i, CB_LR_G)),
                  pl.BlockSpec((CONV_W, W), lambda i: (0, 0)), vec(), mat(), vec(), mat(), vec(), vec(),
                  pl.BlockSpec((B, 1, W), lambda i: (0, 0, 0)), pl.BlockSpec((B, 8, W), lambda i: (0, 0, 0))],
        out_specs=[pl.BlockSpec((B, tm, W), lambda i: (0, i, 0)),
                   pl.BlockSpec((B, 1, W), lambda i: (0, 0, 0)), pl.BlockSpec((B, 8, W), lambda i: (0, 0, 0))],
        scratch_shapes=[pltpu.VMEM((B, tm + 8, W), f32), pltpu.VMEM((B, tm, W), f32), pltpu.VMEM((B, tm, W), f32),
                        pltpu.VMEM((B, 1, W), f32), pltpu.VMEM((B, tm, W), f32)],
        compiler_params=_cparams(("arbitrary",)),
        name="rglru",
    )(z3, z3, conv_w, conv_b.reshape(1, W), wa_bd, ba.reshape(1, W), wx_bd, bx.reshape(1, W),
      lam.reshape(1, W), h0.reshape(B, 1, W), buf8)
    return y, h[:, 0], buf[:, 8 - (CONV_W - 1):]


def _mem_attn_kernel(q_ref, k_ref, v_ref, o_ref):
    q = q_ref[0]
    lead = (0,) * (len(k_ref.shape) - 2)
    k = k_ref[lead]
    v = v_ref[lead]
    dh = MEM_HEAD_DIM
    for h in range(MEM_HEADS):
        cs = slice(dh * h, dh * (h + 1))
        s = _nt(q[:, cs].astype(bf16), k[:, cs].astype(bf16)) * (dh ** -0.5)
        e = jnp.exp(s - jnp.max(s, axis=-1, keepdims=True))
        p = e / jnp.sum(e, axis=-1, keepdims=True)
        o_ref[0, :, cs] = jnp.dot(p.astype(bf16), v[:, cs].astype(bf16),
                                  preferred_element_type=f32).astype(o_ref.dtype)


def mem_attention(q, k, v, *, layer=None, tq=256):
    B, T, D = q.shape
    tq = min(tq, T)
    M = k.shape[-2]
    if layer is None:
        kv_spec = pl.BlockSpec((1, M, D), lambda b, i: (b, 0, 0))
    else:
        kv_spec = pl.BlockSpec((1, 1, M, D), lambda b, i: (layer, b, 0, 0))
    return pl.pallas_call(
        _mem_attn_kernel,
        out_shape=jax.ShapeDtypeStruct((B, T, D), _act_dtype(tq)),
        grid=(B, T // tq),
        in_specs=[pl.BlockSpec((1, tq, D), lambda b, i: (b, i, 0)), kv_spec, kv_spec],
        out_specs=pl.BlockSpec((1, tq, D), lambda b, i: (b, i, 0)),
        compiler_params=_cparams(("parallel", "parallel")),
        name="mem_attention",
    )(q, k, v)


def _even_odd(cmp):
    B, nc, w = cmp.shape
    return jnp.swapaxes(cmp.reshape(B, nc // 2, 2, w), 1, 2).reshape(B, nc, w)


def _layer(x, lw, lam_init, tables, *, B, T, mem, ml_state, lr_state, paged, cfg):
    N = B * T
    z = matmul(x, lw['w_in'], g=lw['g_mix'], tm=512)
    z3 = z.reshape(B, T, D_PROJ_PAD)
    dfq, dfk, nsq, slc_new, win_new = rope_prep(z, tables, cfg['rope_tm'])
    df_v = z[:, 256 * CB_DF_V:256 * (CB_DF_V + 1)]
    cmp_new = z[:, 128 * CB128_CMP:128 * (CB128_CMP + 1)]
    small = z[:, 128 * CB128_SMALL:128 * CB128_SMALL + SMALL_GATES]
    r3 = lambda a: a.reshape(B, T, a.shape[-1])

    L = cfg['ml_chunk']
    gates_T = jnp.swapaxes(small.reshape(N // L, L, SMALL_GATES), 1, 2)
    o_ml, ml_C, ml_n, ml_m = mlstm(z, gates_T, lw['ml_gate_b'], lw['ml_norm_g'], *ml_state, B=B, T=T, L=L)

    if paged is None:
        o_df = diff_attention(r3(dfq), r3(dfk), z3, lw['df_lam'], lw['df_norm_g'], lam_init=lam_init,
                              tq=cfg['df_tq'], tk=cfg['df_tk'])
        comp = nsa_compress(cmp_new.reshape(N // CMP_BLOCK, CMP_BLOCK * 128), lw['nsa_pos_flat'],
                            lw['nsa_w1c'], lw['nsa_w2c']).reshape(B, T // CMP_BLOCK, 128)
        o_ns = nsa_attention(r3(nsq), z3, _even_odd(comp), r3(slc_new), r3(win_new),
                             tq=cfg['ns_tq'], tk=cfg['ns_tk'], tkw=cfg['ns_tkw'])
        win_state = r3(win_new)[:, T - min(WINDOW, T):]
    else:
        pt, l = paged['page_table'], paged['layer']
        n_past = pt.shape[1] * paged['df_k'].shape[3]
        assert (n_past + T) // CMP_BLOCK == n_past // CMP_BLOCK
        o_df = diff_decode(pt, r3(dfq), r3(dfk), r3(df_v), paged['df_k'], paged['df_v'], lw['df_lam'],
                           lw['df_norm_g'], layer=l, pos0=n_past, lam_init=lam_init)
        comp = nsa_compress_paged(pt, paged['nsa_cmp'], lw['nsa_pos_flat'], lw['nsa_w1c'], lw['nsa_w2c'], layer=l)
        o_ns = nsa_decode(pt, r3(nsq), z3, _even_odd(comp), r3(slc_new), paged['nsa_win'], r3(win_new),
                          paged['nsa_slc'], layer=l, pos0=n_past)
        win_cat = jnp.concatenate([paged['nsa_win'][l], jnp.swapaxes(r3(win_new), 1, 2)], 2)
        win_state = jnp.swapaxes(win_cat[:, :, win_cat.shape[2] - min(WINDOW, win_cat.shape[2]):], 1, 2)

    o_lr, lr_h, lr_buf = rglru(z3, lw['lru_conv_w'], lw['lru_conv_b'], lw['lru_wa_bd'],
                               lw['lru_ba'], lw['lru_wx_bd'], lw['lru_bx'], lw['lru_lambda'], *lr_state,
                               tm=cfg['lru_tm'])

    x = matmul([o_ml, o_df.reshape(N, 256), o_ns.reshape(N, 256), o_lr.reshape(N, 256)], lw['w_out'], res=x,
               tm=1024)
    q = matmul(x, lw['w_mq'], g=lw['g_mem_q'], tm=1024, out_dtype=_act_dtype(cfg['mem_tq']))
    if isinstance(mem, tuple):
        att = mem_attention(q.reshape(B, T, D_MODEL), mem[0], mem[1], tq=cfg['mem_tq'])
    else:
        att = mem_attention(q.reshape(B, T, D_MODEL), mem['k'], mem['v'], layer=mem['layer'], tq=cfg['mem_tq'])
    x = matmul(att.reshape(N, D_MODEL), lw['w_mo'], res=x, tm=1024)
    u = matmul(x, lw['w_up'], g=lw['g_mlp'], act='relu2', out_dtype=bf16, tm=512)
    x = matmul(u, lw['w_down'], res=x, tm=512)
    new = dict(df_k=dfk.reshape(B, T, N_HEADS, HEAD_DIM), df_v=df_v.reshape(B, T, N_HEADS, HEAD_DIM),
               nsa_cmp=cmp_new.reshape(B, T, 2, HEAD_DIM), nsa_slc=slc_new.reshape(B, T, 2, HEAD_DIM),
               nsa_win=win_state.reshape(B, -1, 2, HEAD_DIM), ml_C=ml_C, ml_n=ml_n, ml_m=ml_m,
               lru_h=lr_h, lru_conv=lr_buf)
    return x, new


def _block_diag(w):
    H, a, b = w.shape
    eye = jnp.eye(H, dtype=w.dtype)
    return jnp.einsum('hij,hg->higj', w, eye).reshape(H * a, H * b)


def kernel(x_prompt, x_sample, mem_prompt, cache_df_k, cache_df_v, cache_nsa_cmp, cache_nsa_slc, cache_nsa_win, state_ml_C, state_ml_n, state_ml_m, state_lru_h, state_lru_conv, cache_mem_k, cache_mem_v, page_table, g_mix, w_in, w_out, ml_gate_b, ml_norm_g, df_lam, df_norm_g, nsa_pos, nsa_w1, nsa_w2, lru_conv_w, lru_conv_b, lru_wa, lru_ba, lru_wx, lru_bx, lru_lambda, g_mem_q, g_mem_kv, w_mq, w_mk, w_mv, w_mo, g_mlp, w_up, w_down, g_final):
    Bp, Tp, _ = x_prompt.shape
    Bs, Ts, _ = x_sample.shape
    depth = w_in.shape[0]
    n_pool, page = cache_df_k.shape[1], cache_df_k.shape[2]
    n_past = page_table.shape[1] * page
    M = mem_prompt.shape[1]

    perm, n_real = _proj_perm()
    col_ok = (jnp.arange(D_PROJ_PAD) < n_real)
    w_in_p = jnp.where(col_ok[None, None, :], jnp.take(w_in, perm, axis=2), 0.0).astype(bf16)
    eye2 = jnp.eye(2, dtype=f32)
    w1 = nsa_w1.reshape(depth, 2, CMP_BLOCK, HEAD_DIM, CMP_HIDDEN)
    w1c = jnp.einsum('lstih,sg->ltsigh', w1, eye2).reshape(depth, CMP_BLOCK * 2 * HEAD_DIM, 2 * CMP_HIDDEN)
    w2c = jnp.einsum('lshd,sg->lshgd', nsa_w2, eye2).reshape(depth, 2 * CMP_HIDDEN, 2 * HEAD_DIM)
    cfg_p = dict(rope_tm=512, ml_chunk=512, df_tq=512, df_tk=512, ns_tq=512, ns_tk=512, ns_tkw=256,
                 lru_tm=256, mem_tq=256)
    cfg_s = dict(rope_tm=Bs * Ts, ml_chunk=Ts, lru_tm=Ts, mem_tq=Ts)
    tab_p = rope_tables(jnp.arange(Tp))
    tab_s = tuple(jnp.tile(t, (Bs, 1)) for t in rope_tables(n_past + jnp.arange(Ts)))

    tok_minor = lambda c: jnp.transpose(c, (0, 1, 3, 4, 2)).reshape(c.shape[0], c.shape[1], -1, c.shape[2])
    pool_df_k = tok_minor(cache_df_k)
    pool_df_v = tok_minor(cache_df_v)
    pool_cmp = tok_minor(cache_nsa_cmp)
    pool_slc = tok_minor(cache_nsa_slc)
    win_cache = tok_minor(cache_nsa_win)
    mem_k_cache = cache_mem_k.reshape(depth, Bs, M, D_MODEL)
    mem_v_cache = cache_mem_v.reshape(depth, Bs, M, D_MODEL)
    mem_rows = mem_prompt.reshape(Bp * M, D_MODEL)

    xp = x_prompt.reshape(Bp * Tp, D_MODEL)
    xs = x_sample.reshape(Bs * Ts, D_MODEL)
    names = ('df_k', 'df_v', 'nsa_cmp', 'nsa_slc', 'nsa_win', 'ml_C', 'ml_n', 'ml_m', 'lru_h', 'lru_conv')
    acc = {pre + n: [] for n in names for pre in ('p_', 's_')}
    acc['p_mem_k'] = []
    acc['p_mem_v'] = []
    for l in range(depth):
        lw = dict(g_mix=g_mix[l], w_in=w_in_p[l], w_out=w_out[l].astype(bf16), ml_gate_b=ml_gate_b[l],
                  ml_norm_g=ml_norm_g[l], df_lam=df_lam[l], df_norm_g=df_norm_g[l],
                  nsa_pos_flat=nsa_pos[l].reshape(1, CMP_BLOCK * 2 * HEAD_DIM), nsa_w1c=w1c[l].astype(bf16),
                  nsa_w2c=w2c[l].astype(bf16), lru_conv_w=lru_conv_w[l], lru_conv_b=lru_conv_b[l],
                  lru_wa_bd=_block_diag(lru_wa[l]).astype(bf16), lru_ba=lru_ba[l],
                  lru_wx_bd=_block_diag(lru_wx[l]).astype(bf16), lru_bx=lru_bx[l], lru_lambda=lru_lambda[l],
                  g_mem_q=g_mem_q[l], w_mq=w_mq[l].astype(bf16), w_mo=w_mo[l].astype(bf16), g_mlp=g_mlp[l],
                  w_up=w_up[l].astype(bf16), w_down=w_down[l].astype(bf16))
        lam_init = 0.8 - 0.6 * math.exp(-0.3 * l)
        mk_p = matmul(mem_rows, w_mk[l].astype(bf16), g=g_mem_kv[l])
        mv_p = matmul(mem_rows, w_mv[l].astype(bf16), g=g_mem_kv[l])
        ml0 = (jnp.zeros((Bp, N_HEADS, HEAD_DIM, HEAD_DIM), f32), jnp.zeros((Bp, N_HEADS, HEAD_DIM), f32),
               jnp.zeros((Bp, N_HEADS), f32))
        lr0 = (jnp.zeros((Bp, GROUP_WIDTH), f32), jnp.zeros((Bp, CONV_W - 1, GROUP_WIDTH), f32))
        xp, new_p = _layer(xp, lw, lam_init, tab_p, B=Bp, T=Tp,
                           mem=(mk_p.reshape(Bp, M, D_MODEL), mv_p.reshape(Bp, M, D_MODEL)),
                           ml_state=ml0, lr_state=lr0, paged=None, cfg=cfg_p)
        paged = dict(page_table=page_table, layer=l, df_k=pool_df_k, df_v=pool_df_v, nsa_cmp=pool_cmp,
                     nsa_slc=pool_slc, nsa_win=win_cache)
        xs, new_s = _layer(xs, lw, lam_init, tab_s, B=Bs, T=Ts,
                           mem=dict(k=mem_k_cache, v=mem_v_cache, layer=l),
                           ml_state=(state_ml_C[l], state_ml_n[l], state_ml_m[l]),
                           lr_state=(state_lru_h[l], state_lru_conv[l]), paged=paged, cfg=cfg_s)
        for pre, new in (('p_', new_p), ('s_', new_s)):
            for n in names:
                acc[pre + n].append(new[n])
        acc['p_mem_k'].append(mk_p.reshape(Bp, M, MEM_HEADS, MEM_HEAD_DIM))
        acc['p_mem_v'].append(mv_p.reshape(Bp, M, MEM_HEADS, MEM_HEAD_DIM))
    st = {k: jnp.stack(v) for k, v in acc.items()}
    y_prompt = rmsnorm_rows(xp, g_final).reshape(Bp, Tp, D_MODEL)
    y_sample = rmsnorm_rows(xs, g_final).reshape(Bs, Ts, D_MODEL)
    return (y_prompt, y_sample, st['p_df_k'], st['s_df_k'], st['p_df_v'], st['s_df_v'],
            st['p_nsa_cmp'], st['s_nsa_cmp'], st['p_nsa_slc'], st['s_nsa_slc'],
            st['p_nsa_win'], st['s_nsa_win'], st['p_ml_C'], st['s_ml_C'], st['p_ml_n'], st['s_ml_n'],
            st['p_ml_m'], st['s_ml_m'], st['p_lru_h'], st['s_lru_h'], st['p_lru_conv'], st['s_lru_conv'],
            st['p_mem_k'], st['p_mem_v'])
```

```python
import functools
import math

import jax
import jax.numpy as jnp
from jax import lax
from jax.experimental import pallas as pl
from jax.experimental.pallas import tpu as pltpu

f32 = jnp.float32
bf16 = jnp.bfloat16

D_MODEL = 1024
GROUP_WIDTH = 256
N_HEADS = 4
HEAD_DIM = 64
DQK = 32
ROPE_THETA = 10000.0
CMP_BLOCK = 32
CMP_HIDDEN = 128
SEL_BLOCK = 64
SEL_TOPK = 16
WINDOW = 512
CONV_W = 4
LRU_C = 8.0
MEM_HEADS = 4
MEM_HEAD_DIM = 256
EPS = 1e-6
NEG = -1e30
FORCE_SCORE = 1e9
TINY = 1e-30

VMEM_LIMIT_BYTES = 48 * 1024 * 1024
PAGES_PER_STEP = 32
PAGES_PER_STEP_SLC = 64
PAGES_PER_STEP_CMP = 64

_SRC_SPLITS = (
    ('ml_q', 256), ('ml_k', 256), ('ml_v', 256), ('ml_i', 4), ('ml_f', 4), ('ml_o', 256),
    ('df_q', 256), ('df_k', 256), ('df_v', 256),
    ('ns_q', 256), ('ns_kc', 64), ('ns_vc', 64), ('ns_ks', 64), ('ns_vs', 64),
    ('ns_kw', 64), ('ns_vw', 64), ('ns_g', 12), ('lr_x', 256), ('lr_g', 256),
)
_DST_ORDER = ('ml_q', 'ml_k', 'ml_v', 'ml_o', 'df_q', 'df_k', 'df_v', 'ns_q', 'lr_x', 'lr_g',
              'ns_kc', 'ns_vc', 'ns_ks', 'ns_vs', 'ns_kw', 'ns_vw', 'ml_i', 'ml_f', 'ns_g')
D_PROJ_PAD = 3072
CB_ML_Q, CB_ML_K, CB_ML_V, CB_ML_O, CB_DF_Q, CB_DF_K, CB_DF_V, CB_NS_Q, CB_LR_X, CB_LR_G = range(10)
CB128_CMP, CB128_SLC, CB128_WIN, CB128_SMALL = 20, 21, 22, 23
SMALL_GATES = 8


def _proj_perm():
    off = {}
    o = 0
    for name, w in _SRC_SPLITS:
        off[name] = (o, w)
        o += w
    idx = []
    for name in _DST_ORDER:
        s, w = off[name]
        idx.extend(range(s, s + w))
    n_real = len(idx)
    idx.extend([0] * (D_PROJ_PAD - n_real))
    return jnp.asarray(idx, jnp.int32), n_real


def _cparams(sem):
    return pltpu.CompilerParams(dimension_semantics=sem, vmem_limit_bytes=VMEM_LIMIT_BYTES)


def _act_dtype(rows):
    return bf16 if rows % 16 == 0 else f32


def _nt(a, b):
    return lax.dot_general(a, b, (((1,), (1,)), ((), ())), preferred_element_type=f32)


def _mm_kernel(*refs, n_parts, has_norm, has_res, act, tn):
    it = iter(refs)
    x_refs = [next(it) for _ in range(n_parts)]
    w_ref = next(it)
    g_ref = next(it) if has_norm else None
    r_ref = next(it) if has_res else None
    o_ref = next(it)
    h_ref = next(it)
    if has_norm:
        x = x_refs[0][...].astype(f32)
        y = x * lax.rsqrt(jnp.mean(x * x, axis=-1, keepdims=True) + EPS)
        h_ref[...] = (y * g_ref[...]).astype(bf16)
    elif n_parts > 1 or x_refs[0].dtype != bf16:
        off = 0
        for x_ref in x_refs:
            kp = x_ref.shape[-1]
            h_ref[:, off:off + kp] = x_ref[...].astype(bf16)
            off += kp
    else:
        h_ref = x_refs[0]
    N = o_ref.shape[-1]
    for c0 in range(0, N, tn):
        cs = slice(c0, min(c0 + tn, N))
        acc = jnp.dot(h_ref[...], w_ref[:, cs], preferred_element_type=f32)
        if act == 'relu2':
            acc = jnp.maximum(acc, 0.0)
            acc = acc * acc
        if has_res:
            acc = acc + r_ref[:, cs]
        o_ref[:, cs] = acc.astype(o_ref.dtype)


def matmul(x, w, *, g=None, res=None, act=None, out_dtype=f32, tm=512, tn=512):
    parts = list(x) if isinstance(x, (list, tuple)) else [x]
    M = parts[0].shape[0]
    K, N = w.shape
    assert sum(p.shape[1] for p in parts) == K
    tm = min(tm, M)
    assert M % tm == 0
    has_norm = g is not None
    has_res = res is not None
    assert not (has_norm and len(parts) > 1)
    in_specs = [pl.BlockSpec((tm, p.shape[1]), lambda i: (i, 0)) for p in parts]
    in_specs.append(pl.BlockSpec((K, N), lambda i: (0, 0)))
    args = parts + [w]
    if has_norm:
        in_specs.append(pl.BlockSpec((1, K), lambda i: (0, 0)))
        args.append(g.reshape(1, K).astype(f32))
    if has_res:
        in_specs.append(pl.BlockSpec((tm, N), lambda i: (i, 0)))
        args.append(res)
    return pl.pallas_call(
        functools.partial(_mm_kernel, n_parts=len(parts), has_norm=has_norm, has_res=has_res, act=act, tn=tn),
        out_shape=jax.ShapeDtypeStruct((M, N), out_dtype),
        grid=(M // tm,),
        in_specs=in_specs,
        out_specs=pl.BlockSpec((tm, N), lambda i: (i, 0)),
        scratch_shapes=[pltpu.VMEM((tm, K), bf16)],
        compiler_params=_cparams(("parallel",)),
        name="matmul",
    )(*args)


def _rmsnorm_kernel(x_ref, g_ref, o_ref):
    x = x_ref[...]
    y = x * lax.rsqrt(jnp.mean(x * x, axis=-1, keepdims=True) + EPS)
    o_ref[...] = y * g_ref[...]


def rmsnorm_rows(x, g, tm=1024):
    M, K = x.shape
    tm = min(tm, M)
    return pl.pallas_call(
        _rmsnorm_kernel,
        out_shape=jax.ShapeDtypeStruct((M, K), f32),
        grid=(M // tm,),
        in_specs=[pl.BlockSpec((tm, K), lambda i: (i, 0)), pl.BlockSpec((1, K), lambda i: (0, 0))],
        out_specs=pl.BlockSpec((tm, K), lambda i: (i, 0)),
        compiler_params=_cparams(("parallel",)),
        name="final_norm",
    )(x, g.reshape(1, K))


def _rotate(x, cos, sin, half):
    n = x.shape[-1]
    lane = lax.broadcasted_iota(jnp.int32, x.shape, 1)
    first = (lane & (2 * half - 1)) < half
    partner = jnp.where(first, pltpu.roll(x, n - half, 1), pltpu.roll(x, half, 1))
    return x * cos + partner * sin


def _rope_kernel(dq, dk, nq, sl, wn, ca, sa, cb, sb, cc, sc, odq, odk, onq, osl, own):
    odq[...] = _rotate(dq[...], ca[...], sa[...], DQK // 2)
    odk[...] = _rotate(dk[...], ca[...], sa[...], DQK // 2)
    onq[...] = _rotate(nq[...], cb[...], sb[...], HEAD_DIM // 2)
    osl[...] = _rotate(sl[...], cc[...], sc[...], HEAD_DIM // 2)
    own[...] = _rotate(wn[...], cc[...], sc[...], HEAD_DIM // 2)


def rope_tables(pos):
    posf = pos.astype(f32)[:, None]

    def tab(half, reps):
        inv = ROPE_THETA ** (-jnp.arange(half, dtype=f32) / half)
        ang = posf * inv[None, :]
        c = jnp.cos(ang)
        s = jnp.sin(ang)
        return jnp.tile(jnp.concatenate([c, c], 1), (1, reps)), jnp.tile(jnp.concatenate([-s, s], 1), (1, reps))

    ca, sa = tab(DQK // 2, GROUP_WIDTH // DQK)
    cb, sb = tab(HEAD_DIM // 2, N_HEADS)
    n = pos.shape[0]
    cc = jnp.concatenate([cb[:, :HEAD_DIM], jnp.ones((n, HEAD_DIM), f32)], 1)
    sc = jnp.concatenate([sb[:, :HEAD_DIM], jnp.zeros((n, HEAD_DIM), f32)], 1)
    return ca, sa, cb, sb, cc, sc


def rope_prep(z, tables, tm):
    N = z.shape[0]
    R = tables[0].shape[0]
    tm = min(tm, R)
    assert R % tm == 0 and N % tm == 0
    nr = R // tm
    zs = lambda cb: pl.BlockSpec((tm, 256), lambda i, cb=cb: (i, cb))
    zs128 = lambda cb: pl.BlockSpec((tm, 128), lambda i, cb=cb: (i, cb))
    t256 = pl.BlockSpec((tm, 256), lambda i: (i % nr, 0))
    t128 = pl.BlockSpec((tm, 128), lambda i: (i % nr, 0))
    o256 = pl.BlockSpec((tm, 256), lambda i: (i, 0))
    o128 = pl.BlockSpec((tm, 128), lambda i: (i, 0))
    return pl.pallas_call(
        _rope_kernel,
        out_shape=[jax.ShapeDtypeStruct((N, 256), f32)] * 3 + [jax.ShapeDtypeStruct((N, 128), f32)] * 2,
        grid=(N // tm,),
        in_specs=[zs(CB_DF_Q), zs(CB_DF_K), zs(CB_NS_Q), zs128(CB128_SLC), zs128(CB128_WIN),
                  t256, t256, t256, t256, t128, t128],
        out_specs=[o256, o256, o256, o128, o128],
        compiler_params=_cparams(("parallel",)),
        name="rope_prep",
    )(z, z, z, z, z, *tables)


ROW_CHUNK = 16
LANE_CHUNK = 512


def _softmax_tile(s, s_ref, pb_ref, m_ref, l_ref):
    rows, n = s.shape
    s_ref[0:rows, :] = s
    m_old = m_ref[...]
    m_new = jnp.maximum(m_old, jnp.max(s, axis=0, keepdims=True))
    alpha = jnp.exp(m_old - m_new)
    m_ref[...] = m_new
    for c0 in range(0, n, LANE_CHUNK):
        cs = slice(c0, c0 + LANE_CHUNK)
        mb = jnp.broadcast_to(m_new[:, cs], (ROW_CHUNK, LANE_CHUNK))
        part = jnp.zeros((8, LANE_CHUNK), f32)
        for r in range(0, rows, ROW_CHUNK):
            p = jnp.exp(s_ref[r:r + ROW_CHUNK, cs] - mb)
            pb_ref[r:r + ROW_CHUNK, cs] = p.astype(bf16)
            part = part + (p[0:8, :] + p[8:16, :])
        l_ref[:, cs] = alpha[:, cs] * l_ref[:, cs] + jnp.sum(part, axis=0, keepdims=True)
    return alpha


def _diff_lambda(lam_ref, lam_init):
    lq = lam_ref[...]
    return (jnp.exp(jnp.sum(lq[0:1] * lq[1:2], keepdims=True))
            - jnp.exp(jnp.sum(lq[2:3] * lq[3:4], keepdims=True)) + lam_init)


def _diff_kernel(qt_ref, kt_ref, q_ref, k_ref, v_ref, lam_ref, g_ref, o_ref, qp_ref, m_ref, l_ref, acc_ref,
                 oT_ref, s_ref, pb_ref, *, tq, tk, lam_init):
    qi = qt_ref[pl.program_id(1)]
    kj = kt_ref[pl.program_id(1)]
    q_lo = qi * tq
    last = (q_lo + tq - 1) // tk
    nmap = 2 * N_HEADS

    @pl.when(kj == 0)
    def _():
        qT = jnp.transpose(q_ref[0]) * (DQK ** -0.5)
        row = lax.broadcasted_iota(jnp.int32, qT.shape, 0)
        for c in range(nmap):
            blk = jnp.where((row >= DQK * c) & (row < DQK * (c + 1)), qT, 0.0)
            qp_ref[:, c * tq:(c + 1) * tq] = blk.astype(bf16)
        m_ref[...] = jnp.full(m_ref.shape, NEG, f32)
        l_ref[...] = jnp.zeros(l_ref.shape, f32)
        acc_ref[...] = jnp.zeros(acc_ref.shape, f32)

    def step(masked):
        k = k_ref[0].astype(bf16)
        s = jnp.dot(k, qp_ref[...], preferred_element_type=f32)
        if masked:
            kpos = kj * tk + lax.broadcasted_iota(jnp.int32, s.shape, 0)
            qpos = q_lo + (lax.broadcasted_iota(jnp.int32, s.shape, 1) & (tq - 1))
            ok = kpos <= qpos
            s = jnp.where(ok, s, NEG)
        alpha = _softmax_tile(s, s_ref, pb_ref, m_ref, l_ref)
        vT = jnp.transpose(v_ref[0]).astype(bf16)
        for h in range(N_HEADS):
            rs = slice(HEAD_DIM * h, HEAD_DIM * (h + 1))
            cs = slice(2 * tq * h, 2 * tq * (h + 1))
            acc_ref[rs, :] = acc_ref[rs, :] * alpha[:, cs] + jnp.dot(
                vT[rs, :], pb_ref[:, cs], preferred_element_type=f32)

    crosses = kj * tk + tk - 1 > q_lo

    @pl.when(crosses)
    def _():
        step(True)

    @pl.when(jnp.logical_not(crosses))
    def _():
        step(False)

    @pl.when(kj == last)
    def _():
        lam = _diff_lambda(lam_ref, lam_init)
        l = l_ref[...]
        for h in range(N_HEADS):
            rs = slice(HEAD_DIM * h, HEAD_DIM * (h + 1))
            c1 = slice(2 * h * tq, (2 * h + 1) * tq)
            c2 = slice((2 * h + 1) * tq, (2 * h + 2) * tq)
            o = acc_ref[rs, 0:tq] / l[:, c1] - lam * (acc_ref[rs, tq:2 * tq] / l[:, c2])
            y = o * lax.rsqrt(jnp.mean(o * o, axis=0, keepdims=True) + EPS)
            oT_ref[rs, :] = (y * g_ref[...]) * (1.0 - lam_init)
        o_ref[0] = jnp.transpose(oT_ref[...]).astype(o_ref.dtype)


def diff_attention(q, k, z3, lam_q, norm_g, *, lam_init, tq, tk):
    B, T, _ = q.shape
    assert T % tq == 0 and T % tk == 0 and (tq & (tq - 1)) == 0 and tq % 128 == 0 and tk % 128 == 0
    nq = T // tq
    pairs = [(qi, kj) for qi in range(nq) for kj in range((qi * tq + tq - 1) // tk + 1)]
    qt = jnp.asarray([p[0] for p in pairs], jnp.int32)
    kt = jnp.asarray([p[1] for p in pairs], jnp.int32)
    kern = functools.partial(_diff_kernel, tq=tq, tk=tk, lam_init=lam_init)
    return pl.pallas_call(
        kern,
        out_shape=jax.ShapeDtypeStruct((B, T, GROUP_WIDTH), bf16),
        grid_spec=pltpu.PrefetchScalarGridSpec(
            num_scalar_prefetch=2, grid=(B, len(pairs)),
            in_specs=[
                pl.BlockSpec((1, tq, GROUP_WIDTH), lambda b, t, qt, kt: (b, qt[t], 0)),
                pl.BlockSpec((1, tk, GROUP_WIDTH), lambda b, t, qt, kt: (b, kt[t], 0)),
                pl.BlockSpec((1, tk, GROUP_WIDTH), lambda b, t, qt, kt: (b, kt[t], CB_DF_V)),
                pl.BlockSpec((4, DQK), lambda b, t, qt, kt: (0, 0)),
                pl.BlockSpec((HEAD_DIM, 1), lambda b, t, qt, kt: (0, 0)),
            ],
            out_specs=pl.BlockSpec((1, tq, GROUP_WIDTH), lambda b, t, qt, kt: (b, qt[t], 0)),
            scratch_shapes=[pltpu.VMEM((GROUP_WIDTH, 8 * tq), bf16), pltpu.VMEM((1, 8 * tq), f32),
                            pltpu.VMEM((1, 8 * tq), f32), pltpu.VMEM((GROUP_WIDTH, 2 * tq), f32),
                            pltpu.VMEM((GROUP_WIDTH, tq), f32),
                            pltpu.VMEM((tk, 8 * tq), f32), pltpu.VMEM((tk, 8 * tq), bf16)]),
        compiler_params=_cparams(("parallel", "arbitrary")),
        name="diff_attention",
    )(qt, kt, q, k, z3, lam_q, norm_g.reshape(HEAD_DIM, 1))


def _diff_decode_kernel(pt_ref, q_ref, kn_ref, vn_ref, lam_ref, g_ref, *rest, G, n_steps, Ts, pos0, n_past,
                        lam_init):
    kpages = rest[:G]
    vpages = rest[G:2 * G]
    o_ref, qp_ref, m_ref, l_ref, acc_ref = rest[2 * G:]
    j = pl.program_id(1)
    nmap = 2 * N_HEADS
    R = nmap * Ts

    @pl.when(j == 0)
    def _():
        q = q_ref[0] * (DQK ** -0.5)
        col = lax.broadcasted_iota(jnp.int32, q.shape, 1)
        for c in range(nmap):
            qp_ref[c * Ts:(c + 1) * Ts, :] = jnp.where((col >= DQK * c) & (col < DQK * (c + 1)), q, 0.0)
        s = _nt(qp_ref[...].astype(bf16), kn_ref[0].astype(bf16))
        qpos = pos0 + (lax.broadcasted_iota(jnp.int32, s.shape, 0) & (Ts - 1))
        kpos = n_past + lax.broadcasted_iota(jnp.int32, s.shape, 1)
        ok = kpos <= qpos
        s = jnp.where(ok, s, NEG)
        m = jnp.max(s, axis=1, keepdims=True)
        p = jnp.where(ok, jnp.exp(s - m), 0.0)
        m_ref[...] = m
        l_ref[...] = jnp.sum(p, axis=1, keepdims=True)
        acc_ref[...] = jnp.dot(p.astype(bf16), vn_ref[0].astype(bf16), preferred_element_type=f32)

    kT = jnp.concatenate([kp[...] for kp in kpages], axis=1).astype(bf16)
    vT = jnp.concatenate([vp[...] for vp in vpages], axis=1).astype(bf16)
    s = jnp.dot(qp_ref[...].astype(bf16), kT, preferred_element_type=f32)
    m_old = m_ref[...]
    m_new = jnp.maximum(m_old, jnp.max(s, axis=1, keepdims=True))
    alpha = jnp.exp(m_old - m_new)
    p = jnp.exp(s - m_new)
    l_ref[...] = alpha * l_ref[...] + jnp.sum(p, axis=1, keepdims=True)
    m_ref[...] = m_new
    acc_ref[...] = alpha * acc_ref[...] + _nt(p.astype(bf16), vT)

    @pl.when(j == n_steps - 1)
    def _():
        lam = _diff_lambda(lam_ref, lam_init)
        o_all = acc_ref[...] / l_ref[...]
        ys = []
        for h in range(N_HEADS):
            cs = slice(HEAD_DIM * h, HEAD_DIM * (h + 1))
            o = o_all[2 * h * Ts:(2 * h + 1) * Ts, cs] - lam * o_all[(2 * h + 1) * Ts:(2 * h + 2) * Ts, cs]
            y = o * lax.rsqrt(jnp.mean(o * o, axis=1, keepdims=True) + EPS)
            ys.append((y * g_ref[...]) * (1.0 - lam_init))
        o_ref[0] = jnp.concatenate(ys, axis=1)


def _page_specs(layer, n_pages, G, block):
    def spec(i):
        return pl.BlockSpec((None, None) + block,
                            lambda b, j, pt, i=i: (layer, pt[b * n_pages + j * G + i]) + (0,) * len(block))
    return [spec(i) for i in range(G)]


def diff_decode(page_table, q, k_new, v_new, pool_k, pool_v, lam_q, norm_g, *, layer, pos0, lam_init):
    B, Ts, _ = q.shape
    n_pages = page_table.shape[1]
    page = pool_k.shape[3]
    G = min(PAGES_PER_STEP, n_pages)
    assert n_pages % G == 0 and (Ts & (Ts - 1)) == 0 and Ts % 8 == 0
    n_steps = n_pages // G
    R = 2 * N_HEADS * Ts
    row = lambda w: pl.BlockSpec((1, Ts, w), lambda b, j, pt: (b, 0, 0))
    kern = functools.partial(_diff_decode_kernel, G=G, n_steps=n_steps, Ts=Ts, pos0=pos0,
                             n_past=n_pages * page, lam_init=lam_init)
    return pl.pallas_call(
        kern,
        out_shape=jax.ShapeDtypeStruct((B, Ts, GROUP_WIDTH), f32),
        grid_spec=pltpu.PrefetchScalarGridSpec(
            num_scalar_prefetch=1, grid=(B, n_steps),
            in_specs=[row(256), row(256), row(256),
                      pl.BlockSpec((4, DQK), lambda b, j, pt: (0, 0)),
                      pl.BlockSpec((1, HEAD_DIM), lambda b, j, pt: (0, 0))]
            + _page_specs(layer, n_pages, G, (256, page)) + _page_specs(layer, n_pages, G, (256, page)),
            out_specs=row(256),
            scratch_shapes=[pltpu.VMEM((R, GROUP_WIDTH), f32), pltpu.VMEM((R, 1), f32),
                            pltpu.VMEM((R, 1), f32), pltpu.VMEM((R, GROUP_WIDTH), f32)]),
        compiler_params=_cparams(("parallel", "arbitrary")),
        name="diff_decode",
    )(page_table.reshape(-1), q, k_new, v_new, lam_q, norm_g.reshape(1, HEAD_DIM),
      *([pool_k] * G), *([pool_v] * G))


def _compress_rows(x, pos_ref, w1_ref, w2_ref):
    x = (x + pos_ref[...]).astype(bf16)
    hid = jax.nn.gelu(jnp.dot(x, w1_ref[...], preferred_element_type=f32))
    return jnp.dot(hid.astype(bf16), w2_ref[...], preferred_element_type=f32)


def _compress_kernel(x_ref, pos_ref, w1_ref, w2_ref, o_ref):
    o_ref[...] = _compress_rows(x_ref[...], pos_ref, w1_ref, w2_ref)


def nsa_compress(blocks, pos_flat, w1c, w2c, tm=256):
    R, K = blocks.shape
    tm = min(tm, R)
    assert R % tm == 0
    return pl.pallas_call(
        _compress_kernel,
        out_shape=jax.ShapeDtypeStruct((R, 2 * HEAD_DIM), f32),
        grid=(R // tm,),
        in_specs=[pl.BlockSpec((tm, K), lambda i: (i, 0)), pl.BlockSpec((1, K), lambda i: (0, 0)),
                  pl.BlockSpec((K, 2 * CMP_HIDDEN), lambda i: (0, 0)),
                  pl.BlockSpec((2 * CMP_HIDDEN, 2 * HEAD_DIM), lambda i: (0, 0))],
        out_specs=pl.BlockSpec((tm, 2 * HEAD_DIM), lambda i: (i, 0)),
        compiler_params=_cparams(("parallel",)),
        name="nsa_compress",
    )(blocks, pos_flat, w1c, w2c)


def _compress_paged_kernel(pt_ref, pos_ref, w1_ref, w2_ref, *rest, G, page):
    pages = rest[:G]
    o_ref, xs = rest[G:]
    w = 2 * HEAD_DIM
    for i in range(G):
        xs[page * i:page * (i + 1), :] = jnp.transpose(pages[i][...])
    nb = G * page // CMP_BLOCK
    acc = None
    for t in range(CMP_BLOCK):
        xt = xs[pl.ds(t, nb, stride=CMP_BLOCK), :] + pos_ref[:, w * t:w * (t + 1)]
        part = jnp.dot(xt.astype(bf16), w1_ref[w * t:w * (t + 1), :], preferred_element_type=f32)
        acc = part if acc is None else acc + part
    hid = jax.nn.gelu(acc)
    o_ref[0] = jnp.dot(hid.astype(bf16), w2_ref[...], preferred_element_type=f32)


def nsa_compress_paged(page_table, pool, pos_flat, w1c, w2c, *, layer):
    B, n_pages = page_table.shape
    page = pool.shape[3]
    K = CMP_BLOCK * 2 * HEAD_DIM
    rp = page // CMP_BLOCK
    G = min(PAGES_PER_STEP_CMP, n_pages)
    assert n_pages % G == 0 and page % CMP_BLOCK == 0 and (G * rp) % 8 == 0
    const = lambda shape: pl.BlockSpec(shape, lambda b, j, pt: (0, 0))
    return pl.pallas_call(
        functools.partial(_compress_paged_kernel, G=G, page=page),
        out_shape=jax.ShapeDtypeStruct((B, n_pages * rp, 2 * HEAD_DIM), f32),
        grid_spec=pltpu.PrefetchScalarGridSpec(
            num_scalar_prefetch=1, grid=(B, n_pages // G),
            in_specs=[const((1, K)), const((K, 2 * CMP_HIDDEN)), const((2 * CMP_HIDDEN, 2 * HEAD_DIM))]
            + _page_specs(layer, n_pages, G, (2 * HEAD_DIM, page)),
            out_specs=pl.BlockSpec((1, G * rp, 2 * HEAD_DIM), lambda b, j, pt: (b, j, 0)),
            scratch_shapes=[pltpu.VMEM((G * page, 2 * HEAD_DIM), f32)]),
        compiler_params=_cparams(("parallel", "arbitrary")),
        name="nsa_compress_paged",
    )(page_table.reshape(-1), pos_flat, w1c, w2c, *([pool] * G))


def _cmp_block_end(r, half):
    blk = jnp.where(r < half, 2 * r, 2 * (r - half) + 1)
    return (blk + 1) * CMP_BLOCK - 1


def _nsa_kernel(qt_ref, kt_ref, qr_in, qw_in, sm_in, cmp_ref, slc_ref, win_ref,
                o_ref, qr_ref, v_ref, sel_ref, ocmp_ref, ms_ref, ls_ref, accs_ref, mw_ref, lw_ref, accw_ref,
                oT_ref, s_ref, pb_ref, *, tq, tk, tkw, nc, nsp, topk):
    qi = qt_ref[pl.program_id(1)]
    kj = kt_ref[pl.program_id(1)]
    q_lo = qi * tq
    H = N_HEADS
    d = HEAD_DIM
    scale = d ** -0.5
    needed = (q_lo + tq - 1) // tk + 1
    w_lo = jnp.maximum(q_lo - (WINDOW - 1), 0) // tkw
    w_hi = (q_lo + tq - 1) // tkw
    n_steps = jnp.maximum(needed, w_hi - w_lo + 1)
    half = nc // 2

    def heads_on_lanes(xT):
        return jnp.concatenate([xT[d * h:d * (h + 1), :] for h in range(H)], axis=1)

    def qpos_row(n):
        return q_lo + (lax.broadcasted_iota(jnp.int32, (1, n), 1) & (tq - 1))

    @pl.when(kj == 0)
    def _():
        zeros = jnp.zeros((d, H * tq), f32)
        qrT = jnp.transpose(qr_in[0]) * scale
        qwT = jnp.transpose(qw_in[0]) * scale
        qr_ref[...] = jnp.concatenate([heads_on_lanes(qrT), zeros], 0).astype(bf16)
        qw = jnp.concatenate([heads_on_lanes(qwT), zeros], 0).astype(bf16)
        cmp = cmp_ref[0]
        s = jnp.dot(cmp.astype(bf16), qw, preferred_element_type=f32)
        c_end = _cmp_block_end(lax.broadcasted_iota(jnp.int32, (nc, 1), 0), half)
        c_ok = c_end <= qpos_row(H * tq)
        s = jnp.where(c_ok, s, NEG)
        p = jnp.where(c_ok, jnp.exp(s - jnp.max(s, axis=0, keepdims=True)), 0.0)
        p = p / jnp.maximum(jnp.sum(p, axis=0, keepdims=True), TINY)
        cmpT = jnp.transpose(cmp)
        ocmp_ref[...] = jnp.dot(cmpT[d:2 * d, :].astype(bf16), p.astype(bf16), preferred_element_type=f32)
        imp = p[:, 0:tq]
        for h in range(1, H):
            imp = imp + p[:, h * tq:(h + 1) * tq]
        imp = imp[:half, :] + imp[half:, :]
        if nsp > half:
            imp = jnp.concatenate([imp, jnp.zeros((nsp - half, tq), f32)], 0)
        sb = lax.broadcasted_iota(jnp.int32, (nsp, tq), 0)
        cur = qpos_row(tq) // SEL_BLOCK
        v = jnp.where((sb == cur) | (sb == 0), FORCE_SCORE, imp)
        v = jnp.where(sb > cur, NEG, v)
        v_ref[...] = v

        def rank(i, cnt):
            vi = v_ref[pl.ds(i, 1), :]
            ahead = (vi > v) | ((vi == v) & (i < sb))
            return cnt + jnp.where(ahead, 1.0, 0.0)

        cnt = lax.fori_loop(0, nsp, rank, jnp.zeros((nsp, tq), f32))
        sel_ref[...] = jnp.where(cnt < topk, 1.0, 0.0)
        for m_r, l_r, a_r in ((ms_ref, ls_ref, accs_ref), (mw_ref, lw_ref, accw_ref)):
            m_r[...] = jnp.full(m_r.shape, NEG, f32)
            l_r[...] = jnp.zeros(l_r.shape, f32)
            a_r[...] = jnp.zeros(a_r.shape, f32)

    def flash(kv_ref, ok, m_r, l_r, a_r):
        kv = kv_ref[0]
        s = jnp.dot(kv.astype(bf16), qr_ref[...], preferred_element_type=f32)
        s = s + jnp.concatenate([jnp.where(ok, 0.0, NEG)] * H, axis=1)
        alpha = _softmax_tile(s, s_ref, pb_ref, m_r, l_r)
        vT = jnp.transpose(kv)[d:2 * d, :].astype(bf16)
        a_r[...] = a_r[...] * alpha + jnp.dot(vT, pb_ref[0:kv.shape[0], :], preferred_element_type=f32)

    @pl.when(kj < needed)
    def _():
        kpos = kj * tk + lax.broadcasted_iota(jnp.int32, (tk, tq), 0)
        nb = tk // SEL_BLOCK
        rows = [jnp.broadcast_to(sel_ref[pl.ds(kj * nb + c, 1), :], (SEL_BLOCK, tq)) for c in range(nb)]
        chosen = jnp.concatenate(rows, axis=0) > 0.5
        ok = chosen & (kpos <= qpos_row(tq))
        flash(slc_ref, ok, ms_ref, ls_ref, accs_ref)

    @pl.when(kj <= w_hi - w_lo)
    def _():
        wpos = (w_lo + kj) * tkw + lax.broadcasted_iota(jnp.int32, (tkw, tq), 0)
        qp = qpos_row(tq)
        ok = (wpos <= qp) & (qp - wpos < WINDOW) & (wpos >= 0)
        flash(win_ref, ok, mw_ref, lw_ref, accw_ref)

    @pl.when(kj == n_steps - 1)
    def _():
        smT = jnp.transpose(sm_in[0])
        g = jax.nn.sigmoid(smT[SMALL_GATES:SMALL_GATES + 3 * H, :])
        o_slc = accs_ref[...] / ls_ref[...]
        o_win = accw_ref[...] / lw_ref[...]
        o_cmp = ocmp_ref[...]
        for h in range(H):
            cs = slice(h * tq, (h + 1) * tq)
            oT_ref[d * h:d * (h + 1), :] = (g[3 * h:3 * h + 1] * o_cmp[:, cs]
                                            + g[3 * h + 1:3 * h + 2] * o_slc[:, cs]
                                            + g[3 * h + 2:3 * h + 3] * o_win[:, cs])
        o_ref[0] = jnp.transpose(oT_ref[...]).astype(o_ref.dtype)


def nsa_attention(qr, z3, cmp, slc, win, *, tq, tk, tkw):
    B, T, _ = qr.shape
    nc = cmp.shape[1]
    assert T % tq == 0 and T % tk == 0 and T % tkw == 0 and tk % SEL_BLOCK == 0 and nc % 2 == 0
    assert tq % 128 == 0 and tk % 128 == 0 and tkw % 128 == 0 and nc % 8 == 0 and (tq & (tq - 1)) == 0
    nq = T // tq
    nsp = T // SEL_BLOCK
    assert nsp >= nc // 2 and nsp % 8 == 0
    last = lambda qi: (qi * tq + tq - 1) // tk
    w_lo = lambda qi: jnp.maximum(qi * tq - (WINDOW - 1), 0) // tkw
    w_hi = lambda qi: (qi * tq + tq - 1) // tkw
    pairs = []
    for qi in range(nq):
        n_win = (qi * tq + tq - 1) // tkw - max(qi * tq - (WINDOW - 1), 0) // tkw + 1
        pairs += [(qi, kj) for kj in range(max((qi * tq + tq - 1) // tk + 1, n_win))]
    qt = jnp.asarray([p[0] for p in pairs], jnp.int32)
    kt = jnp.asarray([p[1] for p in pairs], jnp.int32)
    kern = functools.partial(_nsa_kernel, tq=tq, tk=tk, tkw=tkw, nc=nc, nsp=nsp, topk=min(SEL_TOPK, nsp))
    stat = pltpu.VMEM((1, N_HEADS * tq), f32)
    acc = pltpu.VMEM((HEAD_DIM, N_HEADS * tq), f32)
    return pl.pallas_call(
        kern,
        out_shape=jax.ShapeDtypeStruct((B, T, GROUP_WIDTH), bf16),
        grid_spec=pltpu.PrefetchScalarGridSpec(
            num_scalar_prefetch=2, grid=(B, len(pairs)),
            in_specs=[
                pl.BlockSpec((1, tq, GROUP_WIDTH), lambda b, t, qt, kt: (b, qt[t], 0)),
                pl.BlockSpec((1, tq, GROUP_WIDTH), lambda b, t, qt, kt: (b, qt[t], CB_NS_Q)),
                pl.BlockSpec((1, tq, 128), lambda b, t, qt, kt: (b, qt[t], CB128_SMALL)),
                pl.BlockSpec((1, nc, 128), lambda b, t, qt, kt: (b, 0, 0)),
                pl.BlockSpec((1, tk, 128), lambda b, t, qt, kt: (b, jnp.minimum(kt[t], last(qt[t])), 0)),
                pl.BlockSpec((1, tkw, 128),
                             lambda b, t, qt, kt: (b, jnp.minimum(w_lo(qt[t]) + kt[t], w_hi(qt[t])), 0)),
            ],
            out_specs=pl.BlockSpec((1, tq, GROUP_WIDTH), lambda b, t, qt, kt: (b, qt[t], 0)),
            scratch_shapes=[pltpu.VMEM((128, N_HEADS * tq), bf16), pltpu.VMEM((nsp, tq), f32),
                            pltpu.VMEM((nsp, tq), f32), acc, stat, stat, acc, stat, stat, acc,
                            pltpu.VMEM((GROUP_WIDTH, tq), f32),
                            pltpu.VMEM((max(tk, tkw), N_HEADS * tq), f32),
                            pltpu.VMEM((max(tk, tkw), N_HEADS * tq), bf16)]),
        compiler_params=_cparams(("parallel", "arbitrary")),
        name="nsa_attention",
    )(qt, kt, qr, z3, z3, cmp, slc, win)


def _nsa_decode_kernel(pt_ref, qr_in, qw_in, sm_in, cmp_ref, sn_ref, wb_ref, wn_ref, *rest,
                       G, n_steps, Ts, pos0, n_past, wpos0, nc, nsl, n_sel, topk, page):
    pages = rest[:G]
    o_ref, qr_ref, selst_ref, ocmp_ref, owin_ref, m_ref, l_ref, acc_ref = rest[G:]
    j = pl.program_id(1)
    H, d = N_HEADS, HEAD_DIM
    scale = d ** -0.5
    R = H * Ts
    half = nc // 2
    bps = G * page // SEL_BLOCK

    def rows_by_head(x):
        z = jnp.zeros((Ts, d), f32)
        return jnp.concatenate([jnp.concatenate([x[:, d * h:d * (h + 1)], z], axis=1) for h in range(H)], axis=0)

    def tile_heads(x):
        return jnp.concatenate([x] * H, axis=0)

    @pl.when(j == 0)
    def _():
        qr = rows_by_head(qr_in[0] * scale)
        qr_ref[...] = qr
        qrb = qr.astype(bf16)
        qwb = rows_by_head(qw_in[0] * scale).astype(bf16)
        qpos = pos0 + (lax.broadcasted_iota(jnp.int32, (R, 1), 0) & (Ts - 1))
        cmpb = cmp_ref[0].astype(bf16)
        s = _nt(qwb, cmpb)
        c_ok = _cmp_block_end(lax.broadcasted_iota(jnp.int32, (1, nc), 1), half) <= qpos
        s = jnp.where(c_ok, s, NEG)
        p = jnp.where(c_ok, jnp.exp(s - jnp.max(s, axis=1, keepdims=True)), 0.0)
        p = p / jnp.maximum(jnp.sum(p, axis=1, keepdims=True), TINY)
        ocmp_ref[...] = jnp.dot(p.astype(bf16), cmpb, preferred_element_type=f32)
        imp = p[0:Ts, :]
        for h in range(1, H):
            imp = imp + p[h * Ts:(h + 1) * Ts, :]
        imp = imp[:, :half] + imp[:, half:]
        imp = jnp.concatenate([imp, jnp.zeros((Ts, nsl - half), f32)], axis=1)
        lane = lax.broadcasted_iota(jnp.int32, (Ts, nsl), 1)
        cur = (pos0 + lax.broadcasted_iota(jnp.int32, (Ts, 1), 0)) // SEL_BLOCK
        v = jnp.where((lane == cur) | (lane == 0), FORCE_SCORE, imp)
        v = jnp.where(lane > cur, NEG, v)
        cnt = jnp.zeros((Ts, nsl), f32)
        for i in range(n_sel):
            vi = v[:, i:i + 1]
            cnt = cnt + jnp.where((vi > v) | ((vi == v) & (lane > i)), 1.0, 0.0)
        sel = jnp.where(cnt < topk, 1.0, 0.0)
        for st in range(n_steps):
            selst_ref[st] = sel[:, bps * st:bps * (st + 1)]
        blk_new = n_past // SEL_BLOCK
        sel_new = tile_heads(sel[:, blk_new:blk_new + 1]) > 0.5
        snb = sn_ref[0].astype(bf16)
        s = _nt(qrb, snb)
        kpos = n_past + lax.broadcasted_iota(jnp.int32, (1, Ts), 1)
        ok = sel_new & (kpos <= qpos)
        s = jnp.where(ok, s, NEG)
        m = jnp.max(s, axis=1, keepdims=True)
        p = jnp.where(ok, jnp.exp(s - m), 0.0)
        m_ref[...] = m
        l_ref[...] = jnp.sum(p, axis=1, keepdims=True)
        acc_ref[...] = jnp.dot(p.astype(bf16), snb, preferred_element_type=f32)
        wbT = wb_ref[...].astype(bf16)
        wnb = wn_ref[0].astype(bf16)
        lw = wbT.shape[1]
        s1 = jnp.dot(qrb, wbT, preferred_element_type=f32)
        s2 = _nt(qrb, wnb)
        wp1 = wpos0 + lax.broadcasted_iota(jnp.int32, (1, lw), 1)
        wp2 = wpos0 + lw + lax.broadcasted_iota(jnp.int32, (1, Ts), 1)
        ok1 = (wp1 <= qpos) & (qpos - wp1 < WINDOW) & (wp1 >= 0)
        ok2 = (wp2 <= qpos) & (qpos - wp2 < WINDOW) & (wp2 >= 0)
        s1 = jnp.where(ok1, s1, NEG)
        s2 = jnp.where(ok2, s2, NEG)
        mw = jnp.maximum(jnp.max(s1, axis=1, keepdims=True), jnp.max(s2, axis=1, keepdims=True))
        p1 = jnp.where(ok1, jnp.exp(s1 - mw), 0.0)
        p2 = jnp.where(ok2, jnp.exp(s2 - mw), 0.0)
        lsum = jnp.sum(p1, axis=1, keepdims=True) + jnp.sum(p2, axis=1, keepdims=True)
        owin_ref[...] = (_nt(p1.astype(bf16), wbT)
                         + jnp.dot(p2.astype(bf16), wnb, preferred_element_type=f32)) / lsum

    kvT = jnp.concatenate([pg[...] for pg in pages], axis=1).astype(bf16)
    n = kvT.shape[1]
    s = jnp.dot(qr_ref[...].astype(bf16), kvT, preferred_element_type=f32)
    expand = (lax.broadcasted_iota(jnp.int32, (bps, n), 1) // SEL_BLOCK
              == lax.broadcasted_iota(jnp.int32, (bps, n), 0))
    chosen = jnp.dot(selst_ref[j].astype(bf16), jnp.where(expand, 1.0, 0.0).astype(bf16),
                     preferred_element_type=f32)
    ok = tile_heads(chosen) > 0.5
    s = jnp.where(ok, s, NEG)
    m_old = m_ref[...]
    m_new = jnp.maximum(m_old, jnp.max(s, axis=1, keepdims=True))
    alpha = jnp.exp(m_old - m_new)
    p = jnp.where(ok, jnp.exp(s - m_new), 0.0)
    l_ref[...] = alpha * l_ref[...] + jnp.sum(p, axis=1, keepdims=True)
    m_ref[...] = m_new
    acc_ref[...] = alpha * acc_ref[...] + _nt(p.astype(bf16), kvT)

    @pl.when(j == n_steps - 1)
    def _():
        g = jax.nn.sigmoid(sm_in[0][:, SMALL_GATES:SMALL_GATES + 3 * H])
        o_slc = acc_ref[...] / l_ref[...]
        outs = []
        for h in range(H):
            rs = slice(h * Ts, (h + 1) * Ts)
            outs.append(g[:, 3 * h:3 * h + 1] * ocmp_ref[rs, d:2 * d]
                        + g[:, 3 * h + 1:3 * h + 2] * o_slc[rs, d:2 * d]
                        + g[:, 3 * h + 2:3 * h + 3] * owin_ref[rs, d:2 * d])
        o_ref[0] = jnp.concatenate(outs, axis=1)


def nsa_decode(page_table, qr, z3, cmp, slc_new, win_cache, win_new, pool_slc, *, layer, pos0):
    B, Ts, _ = qr.shape
    n_pages = page_table.shape[1]
    page = pool_slc.shape[3]
    n_past = n_pages * page
    lw = win_cache.shape[3]
    nc = cmp.shape[1]
    G = min(PAGES_PER_STEP_SLC, n_pages)
    n_steps = n_pages // G
    n_sel = -(-(n_past + Ts) // SEL_BLOCK)
    nsl = -(-n_sel // 128) * 128
    assert n_pages % G == 0 and (Ts & (Ts - 1)) == 0 and Ts % 8 == 0 and page % SEL_BLOCK == 0
    assert pos0 == n_past and n_past % SEL_BLOCK + Ts <= SEL_BLOCK and nc % 2 == 0 and nc // 2 <= nsl
    R = N_HEADS * Ts
    row = lambda w, cb=0: pl.BlockSpec((1, Ts, w), lambda b, j, pt, cb=cb: (b, 0, cb))
    kern = functools.partial(_nsa_decode_kernel, G=G, n_steps=n_steps, Ts=Ts, pos0=pos0, n_past=n_past,
                             wpos0=n_past - lw, nc=nc, nsl=nsl, n_sel=n_sel, topk=min(SEL_TOPK, n_sel),
                             page=page)
    buf = lambda: pltpu.VMEM((R, 128), f32)
    stat = lambda: pltpu.VMEM((R, 1), f32)
    return pl.pallas_call(
        kern,
        out_shape=jax.ShapeDtypeStruct((B, Ts, GROUP_WIDTH), f32),
        grid_spec=pltpu.PrefetchScalarGridSpec(
            num_scalar_prefetch=1, grid=(B, n_steps),
            in_specs=[row(256), row(256, CB_NS_Q), row(128, CB128_SMALL),
                      pl.BlockSpec((1, nc, 128), lambda b, j, pt: (b, 0, 0)),
                      row(128),
                      pl.BlockSpec((None, None, 128, lw), lambda b, j, pt: (layer, b, 0, 0)),
                      row(128)]
            + _page_specs(layer, n_pages, G, (128, page)),
            out_specs=row(256),
            scratch_shapes=[buf(), pltpu.VMEM((n_steps, Ts, G * page // SEL_BLOCK), f32), buf(), buf(),
                            stat(), stat(), buf()]),
        compiler_params=_cparams(("parallel", "arbitrary")),
        name="nsa_decode",
    )(page_table.reshape(-1), qr, z3, z3, cmp, slc_new, win_cache, win_new, *([pool_slc] * G))


def _mlstm_kernel(q_ref, k_ref, v_ref, og_ref, gc_ref, gr_ref, bias_ref, ng_ref, c0_ref, n0_ref, m0_ref,
                  o_ref, cout_ref, nout_ref, mout_ref, c_s, n_s, m_s, *, L):
    ci = pl.program_id(1)
    H, d = N_HEADS, HEAD_DIM

    @pl.when(ci == 0)
    def _():
        c_s[...] = c0_ref[0]
        n_s[...] = n0_ref[0]
        m_s[...] = m0_ref[0]

    q = q_ref[...]
    k = k_ref[...] * (d ** -0.5)
    v = v_ref[...]
    gc = gc_ref[...]
    gr = gr_ref[0]
    bias = bias_ref[...]
    t_i = lax.broadcasted_iota(jnp.int32, (L, L), 0)
    s_i = lax.broadcasted_iota(jnp.int32, (L, L), 1)
    causal = s_i <= t_i
    causal_T = t_i <= s_i
    outs = []
    for h in range(H):
        bi = bias[0:1, h:h + 1]
        bf_ = bias[1:2, h:h + 1]
        ig_c = gc[:, h:h + 1] + bi
        lf_c = jax.nn.log_sigmoid(gc[:, H + h:H + h + 1] + bf_)
        ig_r = gr[h:h + 1, :] + bi
        lf_r = jax.nn.log_sigmoid(gr[H + h:H + h + 1, :] + bf_)
        b_c = jnp.sum(jnp.where(causal, lf_r, 0.0), axis=1, keepdims=True)
        b_r = jnp.sum(jnp.where(causal_T, lf_c, 0.0), axis=0, keepdims=True)
        m_prev = m_s[h:h + 1, 0:1]
        D = jnp.where(causal, b_c - b_r + ig_r, -jnp.inf)
        m_t = jnp.maximum(b_c + m_prev, jnp.max(D, axis=1, keepdims=True))
        inter = jnp.exp(b_c + m_prev - m_t)
        qh = q[:, d * h:d * (h + 1)]
        kh = k[:, d * h:d * (h + 1)]
        vh = v[:, d * h:d * (h + 1)]
        qb = qh.astype(bf16)
        S = _nt(qb, kh.astype(bf16))
        Sw = jnp.exp(D - m_t) * S
        C = c_s[h]
        n_row = n_s[h:h + 1, :]
        num = (jnp.dot(Sw.astype(bf16), vh.astype(bf16), preferred_element_type=f32)
               + inter * jnp.dot(qb, C.astype(bf16), preferred_element_type=f32))
        qn = jnp.sum(qh * n_row, axis=1, keepdims=True)
        den = jnp.sum(Sw, axis=1, keepdims=True) + inter * qn
        hh = num / jnp.maximum(jnp.abs(den), jnp.exp(-m_t))
        m_new = m_t[L - 1:L, :]
        b_last = b_c[L - 1:L, :]
        wl = jnp.exp(b_last - b_c + ig_c - m_new)
        decay = jnp.exp(b_last + m_prev - m_new)
        kw = kh * wl
        c_s[h] = decay * C + lax.dot_general(kw.astype(bf16), vh.astype(bf16), (((0,), (0,)), ((), ())),
                                             preferred_element_type=f32)
        n_s[h:h + 1, :] = decay * n_row + jnp.sum(kw, axis=0, keepdims=True)
        m_s[h:h + 1, :] = jnp.broadcast_to(m_new, (1, 128))
        mu = jnp.mean(hh, axis=1, keepdims=True)
        var = jnp.mean(jnp.square(hh - mu), axis=1, keepdims=True)
        outs.append((hh - mu) * lax.rsqrt(var + EPS))
    hcat = jnp.concatenate(outs, axis=1) * ng_ref[...]
    o_ref[...] = (hcat * jax.nn.sigmoid(og_ref[...])).astype(o_ref.dtype)
    cout_ref[0] = c_s[...]
    nout_ref[0] = n_s[...]
    mout_ref[0] = m_s[...]


def mlstm(z, gates_T, gate_b, norm_g, C0, n0, m0, *, B, T, L):
    nchunk = T // L
    m0p = jnp.broadcast_to(m0[:, :, None], (B, N_HEADS, 128))
    row = lambda cb: pl.BlockSpec((L, 256), lambda b, c, cb=cb: (b * nchunk + c, cb))
    st = lambda *shape: pl.BlockSpec((1,) + shape, lambda b, c: (b,) + (0,) * len(shape))
    out, C, n, m = pl.pallas_call(
        functools.partial(_mlstm_kernel, L=L),
        out_shape=[jax.ShapeDtypeStruct((B * T, GROUP_WIDTH), _act_dtype(L)),
                   jax.ShapeDtypeStruct((B, N_HEADS, HEAD_DIM, HEAD_DIM), f32),
                   jax.ShapeDtypeStruct((B, N_HEADS, HEAD_DIM), f32),
                   jax.ShapeDtypeStruct((B, N_HEADS, 128), f32)],
        grid=(B, nchunk),
        in_specs=[row(CB_ML_Q), row(CB_ML_K), row(CB_ML_V), row(CB_ML_O),
                  pl.BlockSpec((L, 128), lambda b, c: (b * nchunk + c, CB128_SMALL)),
                  pl.BlockSpec((1, 8, L), lambda b, c: (b * nchunk + c, 0, 0)),
                  pl.BlockSpec((2, N_HEADS), lambda b, c: (0, 0)),
                  pl.BlockSpec((1, GROUP_WIDTH), lambda b, c: (0, 0)),
                  st(N_HEADS, HEAD_DIM, HEAD_DIM), st(N_HEADS, HEAD_DIM), st(N_HEADS, 128)],
        out_specs=[pl.BlockSpec((L, GROUP_WIDTH), lambda b, c: (b * nchunk + c, 0)),
                   st(N_HEADS, HEAD_DIM, HEAD_DIM), st(N_HEADS, HEAD_DIM), st(N_HEADS, 128)],
        scratch_shapes=[pltpu.VMEM((N_HEADS, HEAD_DIM, HEAD_DIM), f32), pltpu.VMEM((N_HEADS, HEAD_DIM), f32),
                        pltpu.VMEM((N_HEADS, 128), f32)],
        compiler_params=_cparams(("parallel", "arbitrary")),
        name="mlstm",
    )(z, z, z, z, z, gates_T, gate_b, norm_g.reshape(1, GROUP_WIDTH), C0, n0, m0p)
    return out, C, n, m[:, :, 0]


def _rglru_kernel(x_ref, gate_ref, cw_ref, cb_ref, wa_ref, ba_ref, wx_ref, bx_ref, lam_ref, h0_ref, buf0_ref,
                  y_ref, hout_ref, bufout_ref, xbuf, a_s, u_s, h_s, hs_s, *, tm):
    ti = pl.program_id(0)
    Bb = x_ref.shape[0]
    W = GROUP_WIDTH

    @pl.when(ti == 0)
    def _():
        xbuf[:, 0:8, :] = buf0_ref[...]
        h_s[...] = h0_ref[...]

    xbuf[:, 8:8 + tm, :] = x_ref[...]
    cw = cw_ref[...]
    xc = cb_ref[...] + cw[0:1, :] * xbuf[:, 5:5 + tm, :]
    for j in range(1, CONV_W):
        xc = xc + cw[j:j + 1, :] * xbuf[:, 5 + j:5 + j + tm, :]
    flat = xc.reshape(Bb * tm, W).astype(bf16)
    r = jax.nn.sigmoid(jnp.dot(flat, wa_ref[...], preferred_element_type=f32) + ba_ref[...])
    i = jax.nn.sigmoid(jnp.dot(flat, wx_ref[...], preferred_element_type=f32) + bx_ref[...])
    lam = lam_ref[...]
    softplus = jnp.maximum(-lam, 0.0) + jnp.log1p(jnp.exp(-jnp.abs(lam)))
    log_a = (-LRU_C * r) * softplus
    a = jnp.exp(log_a)
    u = jnp.sqrt(-jnp.tanh(log_a) * (a * a + 1.0)) * (i * xc.reshape(Bb * tm, W))
    a_s[...] = a.reshape(Bb, tm, W)
    u_s[...] = u.reshape(Bb, tm, W)

    def body(t, h):
        h = a_s[:, pl.ds(t, 1), :] * h + u_s[:, pl.ds(t, 1), :]
        hs_s[:, pl.ds(t, 1), :] = h
        return h

    h_last = lax.fori_loop(0, tm, body, h_s[...], unroll=8)
    h_s[...] = h_last
    y_ref[...] = (hs_s[...] * jax.nn.gelu(gate_ref[...])).astype(y_ref.dtype)
    hout_ref[...] = h_last
    bufout_ref[...] = xbuf[:, tm:tm + 8, :]
    xbuf[:, 0:8, :] = xbuf[:, tm:tm + 8, :]


def rglru(z3, conv_w, conv_b, wa_bd, ba, wx_bd, bx, lam, h0, buf0, *, tm):
    B, T, _ = z3.shape
    tm = min(tm, T)
    assert T % tm == 0 and tm % 8 == 0
    W = GROUP_WIDTH
    buf8 = jnp.concatenate([jnp.zeros((B, 8 - (CONV_W - 1), W), f32), buf0], 1)
    vec = lambda: pl.BlockSpec((1, W), lambda i: (0, 0))
    mat = lambda: pl.BlockSpec((W, W), lambda i: (0, 0))
    y, h, buf = pl.pallas_call(
        functools.partial(_rglru_kernel, tm=tm),
        out_shape=[jax.ShapeDtypeStruct((B, T, W), _act_dtype(tm)), jax.ShapeDtypeStruct((B, 1, W), f32),
                   jax.ShapeDtypeStruct((B, 8, W), f32)],
        grid=(T // tm,),
        in_specs=[pl.BlockSpec((B, tm, W), lambda i: (0, i, CB_LR_X)),
                  pl.BlockSpec((B, tm, W), lambda i: (0, i, CB_LR_G)),
                  pl.BlockSpec((CONV_W, W), lambda i: (0, 0)), vec(), mat(), vec(), mat(), vec(), vec(),
                  pl.BlockSpec((B, 1, W), lambda i: (0, 0, 0)), pl.BlockSpec((B, 8, W), lambda i: (0, 0, 0))],
        out_specs=[pl.BlockSpec((B, tm, W), lambda i: (0, i, 0)),
                   pl.BlockSpec((B, 1, W), lambda i: (0, 0, 0)), pl.BlockSpec((B, 8, W), lambda i: (0, 0, 0))],
        scratch_shapes=[pltpu.VMEM((B, tm + 8, W), f32), pltpu.VMEM((B, tm, W), f32), pltpu.VMEM((B, tm, W), f32),
                        pltpu.VMEM((B, 1, W), f32), pltpu.VMEM((B, tm, W), f32)],
        compiler_params=_cparams(("arbitrary",)),
        name="rglru",
    )(z3, z3, conv_w, conv_b.reshape(1, W), wa_bd, ba.reshape(1, W), wx_bd, bx.reshape(1, W),
      lam.reshape(1, W), h0.reshape(B, 1, W), buf8)
    return y, h[:, 0], buf[:, 8 - (CONV_W - 1):]


def _mem_attn_kernel(q_ref, k_ref, v_ref, o_ref):
    q = q_ref[0]
    lead = (0,) * (len(k_ref.shape) - 2)
    k = k_ref[lead]
    v = v_ref[lead]
    dh = MEM_HEAD_DIM
    for h in range(MEM_HEADS):
        cs = slice(dh * h, dh * (h + 1))
        s = _nt(q[:, cs].astype(bf16), k[:, cs].astype(bf16)) * (dh ** -0.5)
        e = jnp.exp(s - jnp.max(s, axis=-1, keepdims=True))
        p = e / jnp.sum(e, axis=-1, keepdims=True)
        o_ref[0, :, cs] = jnp.dot(p.astype(bf16), v[:, cs].astype(bf16),
                                  preferred_element_type=f32).astype(o_ref.dtype)


def mem_attention(q, k, v, *, layer=None, tq=256):
    B, T, D = q.shape
    tq = min(tq, T)
    M = k.shape[-2]
    if layer is None:
        kv_spec = pl.BlockSpec((1, M, D), lambda b, i: (b, 0, 0))
    else:
        kv_spec = pl.BlockSpec((1, 1, M, D), lambda b, i: (layer, b, 0, 0))
    return pl.pallas_call(
        _mem_attn_kernel,
        out_shape=jax.ShapeDtypeStruct((B, T, D), _act_dtype(tq)),
        grid=(B, T // tq),
        in_specs=[pl.BlockSpec((1, tq, D), lambda b, i: (b, i, 0)), kv_spec, kv_spec],
        out_specs=pl.BlockSpec((1, tq, D), lambda b, i: (b, i, 0)),
        compiler_params=_cparams(("parallel", "parallel")),
        name="mem_attention",
    )(q, k, v)


def _even_odd(cmp):
    B, nc, w = cmp.shape
    return jnp.swapaxes(cmp.reshape(B, nc // 2, 2, w), 1, 2).reshape(B, nc, w)


def _layer(x, lw, lam_init, tables, *, B, T, mem, ml_state, lr_state, paged, cfg):
    N = B * T
    z = matmul(x, lw['w_in'], g=lw['g_mix'], tm=512)
    z3 = z.reshape(B, T, D_PROJ_PAD)
    dfq, dfk, nsq, slc_new, win_new = rope_prep(z, tables, cfg['rope_tm'])
    df_v = z[:, 256 * CB_DF_V:256 * (CB_DF_V + 1)]
    cmp_new = z[:, 128 * CB128_CMP:128 * (CB128_CMP + 1)]
    small = z[:, 128 * CB128_SMALL:128 * CB128_SMALL + SMALL_GATES]
    r3 = lambda a: a.reshape(B, T, a.shape[-1])

    L = cfg['ml_chunk']
    gates_T = jnp.swapaxes(small.reshape(N // L, L, SMALL_GATES), 1, 2)
    o_ml, ml_C, ml_n, ml_m = mlstm(z, gates_T, lw['ml_gate_b'], lw['ml_norm_g'], *ml_state, B=B, T=T, L=L)

    if paged is None:
        o_df = diff_attention(r3(dfq), r3(dfk), z3, lw['df_lam'], lw['df_norm_g'], lam_init=lam_init,
                              tq=cfg['df_tq'], tk=cfg['df_tk'])
        comp = nsa_compress(cmp_new.reshape(N // CMP_BLOCK, CMP_BLOCK * 128), lw['nsa_pos_flat'],
                            lw['nsa_w1c'], lw['nsa_w2c']).reshape(B, T // CMP_BLOCK, 128)
        o_ns = nsa_attention(r3(nsq), z3, _even_odd(comp), r3(slc_new), r3(win_new),
                             tq=cfg['ns_tq'], tk=cfg['ns_tk'], tkw=cfg['ns_tkw'])
        win_state = r3(win_new)[:, T - min(WINDOW, T):]
    else:
        pt, l = paged['page_table'], paged['layer']
        n_past = pt.shape[1] * paged['df_k'].shape[3]
        assert (n_past + T) // CMP_BLOCK == n_past // CMP_BLOCK
        o_df = diff_decode(pt, r3(dfq), r3(dfk), r3(df_v), paged['df_k'], paged['df_v'], lw['df_lam'],
                           lw['df_norm_g'], layer=l, pos0=n_past, lam_init=lam_init)
        comp = nsa_compress_paged(pt, paged['nsa_cmp'], lw['nsa_pos_flat'], lw['nsa_w1c'], lw['nsa_w2c'], layer=l)
        o_ns = nsa_decode(pt, r3(nsq), z3, _even_odd(comp), r3(slc_new), paged['nsa_win'], r3(win_new),
                          paged['nsa_slc'], layer=l, pos0=n_past)
        win_cat = jnp.concatenate([paged['nsa_win'][l], jnp.swapaxes(r3(win_new), 1, 2)], 2)
        win_state = jnp.swapaxes(win_cat[:, :, win_cat.shape[2] - min(WINDOW, win_cat.shape[2]):], 1, 2)

    o_lr, lr_h, lr_buf = rglru(z3, lw['lru_conv_w'], lw['lru_conv_b'], lw['lru_wa_bd'],
                               lw['lru_ba'], lw['lru_wx_bd'], lw['lru_bx'], lw['lru_lambda'], *lr_state,
                               tm=cfg['lru_tm'])

    x = matmul([o_ml, o_df.reshape(N, 256), o_ns.reshape(N, 256), o_lr.reshape(N, 256)], lw['w_out'], res=x,
               tm=1024)
    q = matmul(x, lw['w_mq'], g=lw['g_mem_q'], tm=1024, out_dtype=_act_dtype(cfg['mem_tq']))
    if isinstance(mem, tuple):
        att = mem_attention(q.reshape(B, T, D_MODEL), mem[0], mem[1], tq=cfg['mem_tq'])
    else:
        att = mem_attention(q.reshape(B, T, D_MODEL), mem['k'], mem['v'], layer=mem['layer'], tq=cfg['mem_tq'])
    x = matmul(att.reshape(N, D_MODEL), lw['w_mo'], res=x, tm=1024)
    u = matmul(x, lw['w_up'], g=lw['g_mlp'], act='relu2', out_dtype=bf16, tm=512)
    x = matmul(u, lw['w_down'], res=x, tm=512)
    new = dict(df_k=dfk.reshape(B, T, N_HEADS, HEAD_DIM), df_v=df_v.reshape(B, T, N_HEADS, HEAD_DIM),
               nsa_cmp=cmp_new.reshape(B, T, 2, HEAD_DIM), nsa_slc=slc_new.reshape(B, T, 2, HEAD_DIM),
               nsa_win=win_state.reshape(B, -1, 2, HEAD_DIM), ml_C=ml_C, ml_n=ml_n, ml_m=ml_m,
               lru_h=lr_h, lru_conv=lr_buf)
    return x, new


def _block_diag(w):
    H, a, b = w.shape
    eye = jnp.eye(H, dtype=w.dtype)
    return jnp.einsum('hij,hg->higj', w, eye).reshape(H * a, H * b)


def kernel(x_prompt, x_sample, mem_prompt, cache_df_k, cache_df_v, cache_nsa_cmp, cache_nsa_slc, cache_nsa_win, state_ml_C, state_ml_n, state_ml_m, state_lru_h, state_lru_conv, cache_mem_k, cache_mem_v, page_table, g_mix, w_in, w_out, ml_gate_b, ml_norm_g, df_lam, df_norm_g, nsa_pos, nsa_w1, nsa_w2, lru_conv_w, lru_conv_b, lru_wa, lru_ba, lru_wx, lru_bx, lru_lambda, g_mem_q, g_mem_kv, w_mq, w_mk, w_mv, w_mo, g_mlp, w_up, w_down, g_final):
    Bp, Tp, _ = x_prompt.shape
    Bs, Ts, _ = x_sample.shape
    depth = w_in.shape[0]
    n_pool, page = cache_df_k.shape[1], cache_df_k.shape[2]
    n_past = page_table.shape[1] * page
    M = mem_prompt.shape[1]

    perm, n_real = _proj_perm()
    col_ok = (jnp.arange(D_PROJ_PAD) < n_real)
    w_in_p = jnp.where(col_ok[None, None, :], jnp.take(w_in, perm, axis=2), 0.0).astype(bf16)
    eye2 = jnp.eye(2, dtype=f32)
    w1 = nsa_w1.reshape(depth, 2, CMP_BLOCK, HEAD_DIM, CMP_HIDDEN)
    w1c = jnp.einsum('lstih,sg->ltsigh', w1, eye2).reshape(depth, CMP_BLOCK * 2 * HEAD_DIM, 2 * CMP_HIDDEN)
    w2c = jnp.einsum('lshd,sg->lshgd', nsa_w2, eye2).reshape(depth, 2 * CMP_HIDDEN, 2 * HEAD_DIM)
    cfg_p = dict(rope_tm=512, ml_chunk=512, df_tq=512, df_tk=512, ns_tq=512, ns_tk=512, ns_tkw=256,
                 lru_tm=256, mem_tq=512)
    cfg_s = dict(rope_tm=Bs * Ts, ml_chunk=Ts, lru_tm=Ts, mem_tq=Ts)
    tab_p = rope_tables(jnp.arange(Tp))
    tab_s = tuple(jnp.tile(t, (Bs, 1)) for t in rope_tables(n_past + jnp.arange(Ts)))

    tok_minor = lambda c: jnp.transpose(c, (0, 1, 3, 4, 2)).reshape(c.shape[0], c.shape[1], -1, c.shape[2])
    pool_df_k = tok_minor(cache_df_k)
    pool_df_v = tok_minor(cache_df_v)
    pool_cmp = tok_minor(cache_nsa_cmp)
    pool_slc = tok_minor(cache_nsa_slc)
    win_cache = tok_minor(cache_nsa_win)
    mem_k_cache = cache_mem_k.reshape(depth, Bs, M, D_MODEL)
    mem_v_cache = cache_mem_v.reshape(depth, Bs, M, D_MODEL)
    mem_rows = mem_prompt.reshape(Bp * M, D_MODEL)

    xp = x_prompt.reshape(Bp * Tp, D_MODEL)
    xs = x_sample.reshape(Bs * Ts, D_MODEL)
    names = ('df_k', 'df_v', 'nsa_cmp', 'nsa_slc', 'nsa_win', 'ml_C', 'ml_n', 'ml_m', 'lru_h', 'lru_conv')
    acc = {pre + n: [] for n in names for pre in ('p_', 's_')}
    acc['p_mem_k'] = []
    acc['p_mem_v'] = []
    for l in range(depth):
        lw = dict(g_mix=g_mix[l], w_in=w_in_p[l], w_out=w_out[l].astype(bf16), ml_gate_b=ml_gate_b[l],
                  ml_norm_g=ml_norm_g[l], df_lam=df_lam[l], df_norm_g=df_norm_g[l],
                  nsa_pos_flat=nsa_pos[l].reshape(1, CMP_BLOCK * 2 * HEAD_DIM), nsa_w1c=w1c[l].astype(bf16),
                  nsa_w2c=w2c[l].astype(bf16), lru_conv_w=lru_conv_w[l], lru_conv_b=lru_conv_b[l],
                  lru_wa_bd=_block_diag(lru_wa[l]).astype(bf16), lru_ba=lru_ba[l],
                  lru_wx_bd=_block_diag(lru_wx[l]).astype(bf16), lru_bx=lru_bx[l], lru_lambda=lru_lambda[l],
                  g_mem_q=g_mem_q[l], w_mq=w_mq[l].astype(bf16), w_mo=w_mo[l].astype(bf16), g_mlp=g_mlp[l],
                  w_up=w_up[l].astype(bf16), w_down=w_down[l].astype(bf16))
        lam_init = 0.8 - 0.6 * math.exp(-0.3 * l)
        mk_p = matmul(mem_rows, w_mk[l].astype(bf16), g=g_mem_kv[l])
        mv_p = matmul(mem_rows, w_mv[l].astype(bf16), g=g_mem_kv[l])
        ml0 = (jnp.zeros((Bp, N_HEADS, HEAD_DIM, HEAD_DIM), f32), jnp.zeros((Bp, N_HEADS, HEAD_DIM), f32),
               jnp.zeros((Bp, N_HEADS), f32))
        lr0 = (jnp.zeros((Bp, GROUP_WIDTH), f32), jnp.zeros((Bp, CONV_W - 1, GROUP_WIDTH), f32))
        xp, new_p = _layer(xp, lw, lam_init, tab_p, B=Bp, T=Tp,
                           mem=(mk_p.reshape(Bp, M, D_MODEL), mv_p.reshape(Bp, M, D_MODEL)),
                           ml_state=ml0, lr_state=lr0, paged=None, cfg=cfg_p)
        paged = dict(page_table=page_table, layer=l, df_k=pool_df_k, df_v=pool_df_v, nsa_cmp=pool_cmp,
                     nsa_slc=pool_slc, nsa_win=win_cache)
        xs, new_s = _layer(xs, lw, lam_init, tab_s, B=Bs, T=Ts,
                           mem=dict(k=mem_k_cache, v=mem_v_cache, layer=l),
                           ml_state=(state_ml_C[l], state_ml_n[l], state_ml_m[l]),
                           lr_state=(state_lru_h[l], state_lru_conv[l]), paged=paged, cfg=cfg_s)
        for pre, new in (('p_', new_p), ('s_', new_s)):
            for n in names:
                acc[pre + n].append(new[n])
        acc['p_mem_k'].append(mk_p.reshape(Bp, M, MEM_HEADS, MEM_HEAD_DIM))
        acc['p_mem_v'].append(mv_p.reshape(Bp, M, MEM_HEADS, MEM_HEAD_DIM))
    st = {k: jnp.stack(v) for k, v in acc.items()}
    y_prompt = rmsnorm_rows(xp, g_final).reshape(Bp, Tp, D_MODEL)
    y_sample = rmsnorm_rows(xs, g_final).reshape(Bs, Ts, D_MODEL)
    return (y_prompt, y_sample, st['p_df_k'], st['s_df_k'], st['p_df_v'], st['s_df_v'],
            st['p_nsa_cmp'], st['s_nsa_cmp'], st['p_nsa_slc'], st['s_nsa_slc'],
            st['p_nsa_win'], st['s_nsa_win'], st['p_ml_C'], st['s_ml_C'], st['p_ml_n'], st['s_ml_n'],
            st['p_ml_m'], st['s_ml_m'], st['p_lru_h'], st['s_lru_h'], st['p_lru_conv'], st['s_lru_conv'],
            st['p_mem_k'], st['p_mem_v'])
```

```python
import functools
import math

import jax
import jax.numpy as jnp
from jax import lax
from jax.experimental import pallas as pl
from jax.experimental.pallas import tpu as pltpu

f32 = jnp.float32
bf16 = jnp.bfloat16

D_MODEL = 1024
GROUP_WIDTH = 256
N_HEADS = 4
HEAD_DIM = 64
DQK = 32
ROPE_THETA = 10000.0
CMP_BLOCK = 32
CMP_HIDDEN = 128
SEL_BLOCK = 64
SEL_TOPK = 16
WINDOW = 512
CONV_W = 4
LRU_C = 8.0
MEM_HEADS = 4
MEM_HEAD_DIM = 256
EPS = 1e-6
NEG = -1e30
FORCE_SCORE = 1e9
TINY = 1e-30

VMEM_LIMIT_BYTES = 48 * 1024 * 1024
PAGES_PER_STEP = 32
PAGE_GROUP = 8
PAGES_PER_STEP_SLC = 64
PAGES_PER_STEP_CMP = 64

_SRC_SPLITS = (
    ('ml_q', 256), ('ml_k', 256), ('ml_v', 256), ('ml_i', 4), ('ml_f', 4), ('ml_o', 256),
    ('df_q', 256), ('df_k', 256), ('df_v', 256),
    ('ns_q', 256), ('ns_kc', 64), ('ns_vc', 64), ('ns_ks', 64), ('ns_vs', 64),
    ('ns_kw', 64), ('ns_vw', 64), ('ns_g', 12), ('lr_x', 256), ('lr_g', 256),
)
_DST_ORDER = ('ml_q', 'ml_k', 'ml_v', 'ml_o', 'df_q', 'df_k', 'df_v', 'ns_q', 'lr_x', 'lr_g',
              'ns_kc', 'ns_vc', 'ns_ks', 'ns_vs', 'ns_kw', 'ns_vw', 'ml_i', 'ml_f', 'ns_g')
D_PROJ_PAD = 3072
CB_ML_Q, CB_ML_K, CB_ML_V, CB_ML_O, CB_DF_Q, CB_DF_K, CB_DF_V, CB_NS_Q, CB_LR_X, CB_LR_G = range(10)
CB128_CMP, CB128_SLC, CB128_WIN, CB128_SMALL = 20, 21, 22, 23
SMALL_GATES = 8


def _proj_perm():
    off = {}
    o = 0
    for name, w in _SRC_SPLITS:
        off[name] = (o, w)
        o += w
    idx = []
    for name in _DST_ORDER:
        s, w = off[name]
        idx.extend(range(s, s + w))
    n_real = len(idx)
    idx.extend([0] * (D_PROJ_PAD - n_real))
    return jnp.asarray(idx, jnp.int32), n_real


def _cparams(sem):
    return pltpu.CompilerParams(dimension_semantics=sem, vmem_limit_bytes=VMEM_LIMIT_BYTES)


def _act_dtype(rows):
    return bf16 if rows % 16 == 0 else f32


def _nt(a, b):
    return lax.dot_general(a, b, (((1,), (1,)), ((), ())), preferred_element_type=f32)


def _mm_kernel(*refs, n_parts, has_norm, has_res, act, tn):
    it = iter(refs)
    x_refs = [next(it) for _ in range(n_parts)]
    w_ref = next(it)
    g_ref = next(it) if has_norm else None
    r_ref = next(it) if has_res else None
    o_ref = next(it)
    h_ref = next(it)
    if has_norm:
        x = x_refs[0][...].astype(f32)
        y = x * lax.rsqrt(jnp.mean(x * x, axis=-1, keepdims=True) + EPS)
        h_ref[...] = (y * g_ref[...]).astype(bf16)
    elif n_parts > 1 or x_refs[0].dtype != bf16:
        off = 0
        for x_ref in x_refs:
            kp = x_ref.shape[-1]
            h_ref[:, off:off + kp] = x_ref[...].astype(bf16)
            off += kp
    else:
        h_ref = x_refs[0]
    N = o_ref.shape[-1]
    for c0 in range(0, N, tn):
        cs = slice(c0, min(c0 + tn, N))
        acc = jnp.dot(h_ref[...], w_ref[:, cs], preferred_element_type=f32)
        if act == 'relu2':
            acc = jnp.maximum(acc, 0.0)
            acc = acc * acc
        if has_res:
            acc = acc + r_ref[:, cs]
        o_ref[:, cs] = acc.astype(o_ref.dtype)


def matmul(x, w, *, g=None, res=None, act=None, out_dtype=f32, tm=512, tn=512):
    parts = list(x) if isinstance(x, (list, tuple)) else [x]
    M = parts[0].shape[0]
    K, N = w.shape
    assert sum(p.shape[1] for p in parts) == K
    tm = min(tm, M)
    assert M % tm == 0
    has_norm = g is not None
    has_res = res is not None
    assert not (has_norm and len(parts) > 1)
    in_specs = [pl.BlockSpec((tm, p.shape[1]), lambda i: (i, 0)) for p in parts]
    in_specs.append(pl.BlockSpec((K, N), lambda i: (0, 0)))
    args = parts + [w]
    if has_norm:
        in_specs.append(pl.BlockSpec((1, K), lambda i: (0, 0)))
        args.append(g.reshape(1, K).astype(f32))
    if has_res:
        in_specs.append(pl.BlockSpec((tm, N), lambda i: (i, 0)))
        args.append(res)
    return pl.pallas_call(
        functools.partial(_mm_kernel, n_parts=len(parts), has_norm=has_norm, has_res=has_res, act=act, tn=tn),
        out_shape=jax.ShapeDtypeStruct((M, N), out_dtype),
        grid=(M // tm,),
        in_specs=in_specs,
        out_specs=pl.BlockSpec((tm, N), lambda i: (i, 0)),
        scratch_shapes=[pltpu.VMEM((tm, K), bf16)],
        compiler_params=_cparams(("parallel",)),
        name="matmul",
    )(*args)


def _rmsnorm_kernel(x_ref, g_ref, o_ref):
    x = x_ref[...]
    y = x * lax.rsqrt(jnp.mean(x * x, axis=-1, keepdims=True) + EPS)
    o_ref[...] = y * g_ref[...]


def rmsnorm_rows(x, g, tm=1024):
    M, K = x.shape
    tm = min(tm, M)
    return pl.pallas_call(
        _rmsnorm_kernel,
        out_shape=jax.ShapeDtypeStruct((M, K), f32),
        grid=(M // tm,),
        in_specs=[pl.BlockSpec((tm, K), lambda i: (i, 0)), pl.BlockSpec((1, K), lambda i: (0, 0))],
        out_specs=pl.BlockSpec((tm, K), lambda i: (i, 0)),
        compiler_params=_cparams(("parallel",)),
        name="final_norm",
    )(x, g.reshape(1, K))


def _rotate(x, cos, sin, half):
    n = x.shape[-1]
    lane = lax.broadcasted_iota(jnp.int32, x.shape, 1)
    first = (lane & (2 * half - 1)) < half
    partner = jnp.where(first, pltpu.roll(x, n - half, 1), pltpu.roll(x, half, 1))
    return x * cos + partner * sin


def _rope_kernel(dq, dk, nq, sl, wn, ca, sa, cb, sb, cc, sc, odq, odk, onq, osl, own):
    odq[...] = _rotate(dq[...], ca[...], sa[...], DQK // 2)
    odk[...] = _rotate(dk[...], ca[...], sa[...], DQK // 2)
    onq[...] = _rotate(nq[...], cb[...], sb[...], HEAD_DIM // 2)
    osl[...] = _rotate(sl[...], cc[...], sc[...], HEAD_DIM // 2)
    own[...] = _rotate(wn[...], cc[...], sc[...], HEAD_DIM // 2)


def rope_tables(pos):
    posf = pos.astype(f32)[:, None]

    def tab(half, reps):
        inv = ROPE_THETA ** (-jnp.arange(half, dtype=f32) / half)
        ang = posf * inv[None, :]
        c = jnp.cos(ang)
        s = jnp.sin(ang)
        return jnp.tile(jnp.concatenate([c, c], 1), (1, reps)), jnp.tile(jnp.concatenate([-s, s], 1), (1, reps))

    ca, sa = tab(DQK // 2, GROUP_WIDTH // DQK)
    cb, sb = tab(HEAD_DIM // 2, N_HEADS)
    n = pos.shape[0]
    cc = jnp.concatenate([cb[:, :HEAD_DIM], jnp.ones((n, HEAD_DIM), f32)], 1)
    sc = jnp.concatenate([sb[:, :HEAD_DIM], jnp.zeros((n, HEAD_DIM), f32)], 1)
    return ca, sa, cb, sb, cc, sc


def rope_prep(z, tables, tm):
    N = z.shape[0]
    R = tables[0].shape[0]
    tm = min(tm, R)
    assert R % tm == 0 and N % tm == 0
    nr = R // tm
    zs = lambda cb: pl.BlockSpec((tm, 256), lambda i, cb=cb: (i, cb))
    zs128 = lambda cb: pl.BlockSpec((tm, 128), lambda i, cb=cb: (i, cb))
    t256 = pl.BlockSpec((tm, 256), lambda i: (i % nr, 0))
    t128 = pl.BlockSpec((tm, 128), lambda i: (i % nr, 0))
    o256 = pl.BlockSpec((tm, 256), lambda i: (i, 0))
    o128 = pl.BlockSpec((tm, 128), lambda i: (i, 0))
    return pl.pallas_call(
        _rope_kernel,
        out_shape=[jax.ShapeDtypeStruct((N, 256), f32)] * 3 + [jax.ShapeDtypeStruct((N, 128), f32)] * 2,
        grid=(N // tm,),
        in_specs=[zs(CB_DF_Q), zs(CB_DF_K), zs(CB_NS_Q), zs128(CB128_SLC), zs128(CB128_WIN),
                  t256, t256, t256, t256, t128, t128],
        out_specs=[o256, o256, o256, o128, o128],
        compiler_params=_cparams(("parallel",)),
        name="rope_prep",
    )(z, z, z, z, z, *tables)


ROW_CHUNK = 16
LANE_CHUNK = 512


def _softmax_tile(s, s_ref, pb_ref, m_ref, l_ref):
    rows, n = s.shape
    s_ref[0:rows, :] = s
    m_old = m_ref[...]
    m_new = jnp.maximum(m_old, jnp.max(s, axis=0, keepdims=True))
    alpha = jnp.exp(m_old - m_new)
    m_ref[...] = m_new
    for c0 in range(0, n, LANE_CHUNK):
        cs = slice(c0, c0 + LANE_CHUNK)
        mb = jnp.broadcast_to(m_new[:, cs], (ROW_CHUNK, LANE_CHUNK))
        part = jnp.zeros((8, LANE_CHUNK), f32)
        for r in range(0, rows, ROW_CHUNK):
            p = jnp.exp(s_ref[r:r + ROW_CHUNK, cs] - mb)
            pb_ref[r:r + ROW_CHUNK, cs] = p.astype(bf16)
            part = part + (p[0:8, :] + p[8:16, :])
        l_ref[:, cs] = alpha[:, cs] * l_ref[:, cs] + jnp.sum(part, axis=0, keepdims=True)
    return alpha


def _diff_lambda(lam_ref, lam_init):
    lq = lam_ref[...]
    return (jnp.exp(jnp.sum(lq[0:1] * lq[1:2], keepdims=True))
            - jnp.exp(jnp.sum(lq[2:3] * lq[3:4], keepdims=True)) + lam_init)


def _diff_kernel(qt_ref, kt_ref, q_ref, k_ref, v_ref, lam_ref, g_ref, o_ref, qp_ref, m_ref, l_ref, acc_ref,
                 oT_ref, s_ref, pb_ref, *, tq, tk, lam_init):
    qi = qt_ref[pl.program_id(1)]
    kj = kt_ref[pl.program_id(1)]
    q_lo = qi * tq
    last = (q_lo + tq - 1) // tk
    nmap = 2 * N_HEADS

    @pl.when(kj == 0)
    def _():
        qT = jnp.transpose(q_ref[0]) * (DQK ** -0.5)
        row = lax.broadcasted_iota(jnp.int32, qT.shape, 0)
        for c in range(nmap):
            blk = jnp.where((row >= DQK * c) & (row < DQK * (c + 1)), qT, 0.0)
            qp_ref[:, c * tq:(c + 1) * tq] = blk.astype(bf16)
        m_ref[...] = jnp.full(m_ref.shape, NEG, f32)
        l_ref[...] = jnp.zeros(l_ref.shape, f32)
        acc_ref[...] = jnp.zeros(acc_ref.shape, f32)

    def step(masked):
        k = k_ref[0].astype(bf16)
        s = jnp.dot(k, qp_ref[...], preferred_element_type=f32)
        if masked:
            kpos = kj * tk + lax.broadcasted_iota(jnp.int32, s.shape, 0)
            qpos = q_lo + (lax.broadcasted_iota(jnp.int32, s.shape, 1) & (tq - 1))
            ok = kpos <= qpos
            s = jnp.where(ok, s, NEG)
        alpha = _softmax_tile(s, s_ref, pb_ref, m_ref, l_ref)
        vT = jnp.transpose(v_ref[0]).astype(bf16)
        for h in range(N_HEADS):
            rs = slice(HEAD_DIM * h, HEAD_DIM * (h + 1))
            cs = slice(2 * tq * h, 2 * tq * (h + 1))
            acc_ref[rs, :] = acc_ref[rs, :] * alpha[:, cs] + jnp.dot(
                vT[rs, :], pb_ref[:, cs], preferred_element_type=f32)

    crosses = kj * tk + tk - 1 > q_lo

    @pl.when(crosses)
    def _():
        step(True)

    @pl.when(jnp.logical_not(crosses))
    def _():
        step(False)

    @pl.when(kj == last)
    def _():
        lam = _diff_lambda(lam_ref, lam_init)
        l = l_ref[...]
        for h in range(N_HEADS):
            rs = slice(HEAD_DIM * h, HEAD_DIM * (h + 1))
            c1 = slice(2 * h * tq, (2 * h + 1) * tq)
            c2 = slice((2 * h + 1) * tq, (2 * h + 2) * tq)
            o = acc_ref[rs, 0:tq] / l[:, c1] - lam * (acc_ref[rs, tq:2 * tq] / l[:, c2])
            y = o * lax.rsqrt(jnp.mean(o * o, axis=0, keepdims=True) + EPS)
            oT_ref[rs, :] = (y * g_ref[...]) * (1.0 - lam_init)
        o_ref[0] = jnp.transpose(oT_ref[...]).astype(o_ref.dtype)


def diff_attention(q, k, z3, lam_q, norm_g, *, lam_init, tq, tk):
    B, T, _ = q.shape
    assert T % tq == 0 and T % tk == 0 and (tq & (tq - 1)) == 0 and tq % 128 == 0 and tk % 128 == 0
    nq = T // tq
    pairs = [(qi, kj) for qi in range(nq) for kj in range((qi * tq + tq - 1) // tk + 1)]
    qt = jnp.asarray([p[0] for p in pairs], jnp.int32)
    kt = jnp.asarray([p[1] for p in pairs], jnp.int32)
    kern = functools.partial(_diff_kernel, tq=tq, tk=tk, lam_init=lam_init)
    return pl.pallas_call(
        kern,
        out_shape=jax.ShapeDtypeStruct((B, T, GROUP_WIDTH), bf16),
        grid_spec=pltpu.PrefetchScalarGridSpec(
            num_scalar_prefetch=2, grid=(B, len(pairs)),
            in_specs=[
                pl.BlockSpec((1, tq, GROUP_WIDTH), lambda b, t, qt, kt: (b, qt[t], 0)),
                pl.BlockSpec((1, tk, GROUP_WIDTH), lambda b, t, qt, kt: (b, kt[t], 0)),
                pl.BlockSpec((1, tk, GROUP_WIDTH), lambda b, t, qt, kt: (b, kt[t], CB_DF_V)),
                pl.BlockSpec((4, DQK), lambda b, t, qt, kt: (0, 0)),
                pl.BlockSpec((HEAD_DIM, 1), lambda b, t, qt, kt: (0, 0)),
            ],
            out_specs=pl.BlockSpec((1, tq, GROUP_WIDTH), lambda b, t, qt, kt: (b, qt[t], 0)),
            scratch_shapes=[pltpu.VMEM((GROUP_WIDTH, 8 * tq), bf16), pltpu.VMEM((1, 8 * tq), f32),
                            pltpu.VMEM((1, 8 * tq), f32), pltpu.VMEM((GROUP_WIDTH, 2 * tq), f32),
                            pltpu.VMEM((GROUP_WIDTH, tq), f32),
                            pltpu.VMEM((tk, 8 * tq), f32), pltpu.VMEM((tk, 8 * tq), bf16)]),
        compiler_params=_cparams(("parallel", "arbitrary")),
        name="diff_attention",
    )(qt, kt, q, k, z3, lam_q, norm_g.reshape(HEAD_DIM, 1))


def _diff_decode_kernel(pt_ref, q_ref, kn_ref, vn_ref, lam_ref, g_ref, *rest, G, n_steps, Ts, pos0, n_past,
                        lam_init):
    kpages = rest[:G]
    vpages = rest[G:2 * G]
    o_ref, qp_ref, m_ref, l_ref, acc_ref = rest[2 * G:]
    j = pl.program_id(1)
    nmap = 2 * N_HEADS
    R = nmap * Ts

    @pl.when(j == 0)
    def _():
        q = q_ref[0] * (DQK ** -0.5)
        col = lax.broadcasted_iota(jnp.int32, q.shape, 1)
        for c in range(nmap):
            qp_ref[c * Ts:(c + 1) * Ts, :] = jnp.where((col >= DQK * c) & (col < DQK * (c + 1)), q, 0.0)
        s = _nt(qp_ref[...].astype(bf16), kn_ref[0].astype(bf16))
        qpos = pos0 + (lax.broadcasted_iota(jnp.int32, s.shape, 0) & (Ts - 1))
        kpos = n_past + lax.broadcasted_iota(jnp.int32, s.shape, 1)
        ok = kpos <= qpos
        s = jnp.where(ok, s, NEG)
        m = jnp.max(s, axis=1, keepdims=True)
        p = jnp.where(ok, jnp.exp(s - m), 0.0)
        m_ref[...] = m
        l_ref[...] = jnp.sum(p, axis=1, keepdims=True)
        acc_ref[...] = jnp.dot(p.astype(bf16), vn_ref[0].astype(bf16), preferred_element_type=f32)

    qpb = qp_ref[...].astype(bf16)
    groups = [range(g0, min(g0 + PAGE_GROUP, G)) for g0 in range(0, G, PAGE_GROUP)]
    s = jnp.concatenate(
        [jnp.dot(qpb, jnp.concatenate([kpages[i][...] for i in grp], axis=1).astype(bf16),
                 preferred_element_type=f32) for grp in groups], axis=1)
    m_old = m_ref[...]
    m_new = jnp.maximum(m_old, jnp.max(s, axis=1, keepdims=True))
    alpha = jnp.exp(m_old - m_new)
    p = jnp.exp(s - m_new)
    l_ref[...] = alpha * l_ref[...] + jnp.sum(p, axis=1, keepdims=True)
    m_ref[...] = m_new
    pb = p.astype(bf16)
    page = kpages[0].shape[1]
    acc = alpha * acc_ref[...]
    for grp in groups:
        vT = jnp.concatenate([vpages[i][...] for i in grp], axis=1).astype(bf16)
        acc = acc + _nt(pb[:, grp[0] * page:(grp[-1] + 1) * page], vT)
    acc_ref[...] = acc

    @pl.when(j == n_steps - 1)
    def _():
        lam = _diff_lambda(lam_ref, lam_init)
        o_all = acc_ref[...] / l_ref[...]
        ys = []
        for h in range(N_HEADS):
            cs = slice(HEAD_DIM * h, HEAD_DIM * (h + 1))
            o = o_all[2 * h * Ts:(2 * h + 1) * Ts, cs] - lam * o_all[(2 * h + 1) * Ts:(2 * h + 2) * Ts, cs]
            y = o * lax.rsqrt(jnp.mean(o * o, axis=1, keepdims=True) + EPS)
            ys.append((y * g_ref[...]) * (1.0 - lam_init))
        o_ref[0] = jnp.concatenate(ys, axis=1)


def _page_specs(layer, n_pages, G, block):
    def spec(i):
        return pl.BlockSpec((None, None) + block,
                            lambda b, j, pt, i=i: (layer, pt[b * n_pages + j * G + i]) + (0,) * len(block))
    return [spec(i) for i in range(G)]


def diff_decode(page_table, q, k_new, v_new, pool_k, pool_v, lam_q, norm_g, *, layer, pos0, lam_init):
    B, Ts, _ = q.shape
    n_pages = page_table.shape[1]
    page = pool_k.shape[3]
    G = min(PAGES_PER_STEP, n_pages)
    assert n_pages % G == 0 and (Ts & (Ts - 1)) == 0 and Ts % 8 == 0
    n_steps = n_pages // G
    R = 2 * N_HEADS * Ts
    row = lambda w: pl.BlockSpec((1, Ts, w), lambda b, j, pt: (b, 0, 0))
    kern = functools.partial(_diff_decode_kernel, G=G, n_steps=n_steps, Ts=Ts, pos0=pos0,
                             n_past=n_pages * page, lam_init=lam_init)
    return pl.pallas_call(
        kern,
        out_shape=jax.ShapeDtypeStruct((B, Ts, GROUP_WIDTH), f32),
        grid_spec=pltpu.PrefetchScalarGridSpec(
            num_scalar_prefetch=1, grid=(B, n_steps),
            in_specs=[row(256), row(256), row(256),
                      pl.BlockSpec((4, DQK), lambda b, j, pt: (0, 0)),
                      pl.BlockSpec((1, HEAD_DIM), lambda b, j, pt: (0, 0))]
            + _page_specs(layer, n_pages, G, (256, page)) + _page_specs(layer, n_pages, G, (256, page)),
            out_specs=row(256),
            scratch_shapes=[pltpu.VMEM((R, GROUP_WIDTH), f32), pltpu.VMEM((R, 1), f32),
                            pltpu.VMEM((R, 1), f32), pltpu.VMEM((R, GROUP_WIDTH), f32)]),
        compiler_params=_cparams(("parallel", "arbitrary")),
        name="diff_decode",
    )(page_table.reshape(-1), q, k_new, v_new, lam_q, norm_g.reshape(1, HEAD_DIM),
      *([pool_k] * G), *([pool_v] * G))


def _compress_rows(x, pos_ref, w1_ref, w2_ref):
    x = (x + pos_ref[...]).astype(bf16)
    hid = jax.nn.gelu(jnp.dot(x, w1_ref[...], preferred_element_type=f32))
    return jnp.dot(hid.astype(bf16), w2_ref[...], preferred_element_type=f32)


def _compress_kernel(x_ref, pos_ref, w1_ref, w2_ref, o_ref):
    o_ref[...] = _compress_rows(x_ref[...], pos_ref, w1_ref, w2_ref)


def nsa_compress(blocks, pos_flat, w1c, w2c, tm=256):
    R, K = blocks.shape
    tm = min(tm, R)
    assert R % tm == 0
    return pl.pallas_call(
        _compress_kernel,
        out_shape=jax.ShapeDtypeStruct((R, 2 * HEAD_DIM), f32),
        grid=(R // tm,),
        in_specs=[pl.BlockSpec((tm, K), lambda i: (i, 0)), pl.BlockSpec((1, K), lambda i: (0, 0)),
                  pl.BlockSpec((K, 2 * CMP_HIDDEN), lambda i: (0, 0)),
                  pl.BlockSpec((2 * CMP_HIDDEN, 2 * HEAD_DIM), lambda i: (0, 0))],
        out_specs=pl.BlockSpec((tm, 2 * HEAD_DIM), lambda i: (i, 0)),
        compiler_params=_cparams(("parallel",)),
        name="nsa_compress",
    )(blocks, pos_flat, w1c, w2c)


def _compress_paged_kernel(pt_ref, pos_ref, w1_ref, w2_ref, *rest, G, page):
    pages = rest[:G]
    o_ref, xs = rest[G:]
    w = 2 * HEAD_DIM
    for i in range(G):
        xs[page * i:page * (i + 1), :] = jnp.transpose(pages[i][...])
    nb = G * page // CMP_BLOCK
    acc = None
    for t in range(CMP_BLOCK):
        xt = xs[pl.ds(t, nb, stride=CMP_BLOCK), :] + pos_ref[:, w * t:w * (t + 1)]
        part = jnp.dot(xt.astype(bf16), w1_ref[w * t:w * (t + 1), :], preferred_element_type=f32)
        acc = part if acc is None else acc + part
    hid = jax.nn.gelu(acc)
    o_ref[0] = jnp.dot(hid.astype(bf16), w2_ref[...], preferred_element_type=f32)


def nsa_compress_paged(page_table, pool, pos_flat, w1c, w2c, *, layer):
    B, n_pages = page_table.shape
    page = pool.shape[3]
    K = CMP_BLOCK * 2 * HEAD_DIM
    rp = page // CMP_BLOCK
    G = min(PAGES_PER_STEP_CMP, n_pages)
    assert n_pages % G == 0 and page % CMP_BLOCK == 0 and (G * rp) % 8 == 0
    const = lambda shape: pl.BlockSpec(shape, lambda b, j, pt: (0, 0))
    return pl.pallas_call(
        functools.partial(_compress_paged_kernel, G=G, page=page),
        out_shape=jax.ShapeDtypeStruct((B, n_pages * rp, 2 * HEAD_DIM), f32),
        grid_spec=pltpu.PrefetchScalarGridSpec(
            num_scalar_prefetch=1, grid=(B, n_pages // G),
            in_specs=[const((1, K)), const((K, 2 * CMP_HIDDEN)), const((2 * CMP_HIDDEN, 2 * HEAD_DIM))]
            + _page_specs(layer, n_pages, G, (2 * HEAD_DIM, page)),
            out_specs=pl.BlockSpec((1, G * rp, 2 * HEAD_DIM), lambda b, j, pt: (b, j, 0)),
            scratch_shapes=[pltpu.VMEM((G * page, 2 * HEAD_DIM), f32)]),
        compiler_params=_cparams(("parallel", "arbitrary")),
        name="nsa_compress_paged",
    )(page_table.reshape(-1), pos_flat, w1c, w2c, *([pool] * G))


def _cmp_block_end(r, half):
    blk = jnp.where(r < half, 2 * r, 2 * (r - half) + 1)
    return (blk + 1) * CMP_BLOCK - 1


def _nsa_kernel(qt_ref, kt_ref, qr_in, qw_in, sm_in, cmp_ref, slc_ref, win_ref,
                o_ref, qr_ref, v_ref, sel_ref, ocmp_ref, ms_ref, ls_ref, accs_ref, mw_ref, lw_ref, accw_ref,
                oT_ref, s_ref, pb_ref, *, tq, tk, tkw, nc, nsp, topk):
    qi = qt_ref[pl.program_id(1)]
    kj = kt_ref[pl.program_id(1)]
    q_lo = qi * tq
    H = N_HEADS
    d = HEAD_DIM
    scale = d ** -0.5
    needed = (q_lo + tq - 1) // tk + 1
    w_lo = jnp.maximum(q_lo - (WINDOW - 1), 0) // tkw
    w_hi = (q_lo + tq - 1) // tkw
    n_steps = jnp.maximum(needed, w_hi - w_lo + 1)
    half = nc // 2

    def heads_on_lanes(xT):
        return jnp.concatenate([xT[d * h:d * (h + 1), :] for h in range(H)], axis=1)

    def qpos_row(n):
        return q_lo + (lax.broadcasted_iota(jnp.int32, (1, n), 1) & (tq - 1))

    @pl.when(kj == 0)
    def _():
        zeros = jnp.zeros((d, H * tq), f32)
        qrT = jnp.transpose(qr_in[0]) * scale
        qwT = jnp.transpose(qw_in[0]) * scale
        qr_ref[...] = jnp.concatenate([heads_on_lanes(qrT), zeros], 0).astype(bf16)
        qw = jnp.concatenate([heads_on_lanes(qwT), zeros], 0).astype(bf16)
        cmp = cmp_ref[0]
        s = jnp.dot(cmp.astype(bf16), qw, preferred_element_type=f32)
        c_end = _cmp_block_end(lax.broadcasted_iota(jnp.int32, (nc, 1), 0), half)
        c_ok = c_end <= qpos_row(H * tq)
        s = jnp.where(c_ok, s, NEG)
        p = jnp.where(c_ok, jnp.exp(s - jnp.max(s, axis=0, keepdims=True)), 0.0)
        p = p / jnp.maximum(jnp.sum(p, axis=0, keepdims=True), TINY)
        cmpT = jnp.transpose(cmp)
        ocmp_ref[...] = jnp.dot(cmpT[d:2 * d, :].astype(bf16), p.astype(bf16), preferred_element_type=f32)
        imp = p[:, 0:tq]
        for h in range(1, H):
            imp = imp + p[:, h * tq:(h + 1) * tq]
        imp = imp[:half, :] + imp[half:, :]
        if nsp > half:
            imp = jnp.concatenate([imp, jnp.zeros((nsp - half, tq), f32)], 0)
        sb = lax.broadcasted_iota(jnp.int32, (nsp, tq), 0)
        cur = qpos_row(tq) // SEL_BLOCK
        v = jnp.where((sb == cur) | (sb == 0), FORCE_SCORE, imp)
        v = jnp.where(sb > cur, NEG, v)
        v_ref[...] = v

        def rank(i, cnt):
            vi = v_ref[pl.ds(i, 1), :]
            ahead = (vi > v) | ((vi == v) & (i < sb))
            return cnt + jnp.where(ahead, 1.0, 0.0)

        cnt = lax.fori_loop(0, nsp, rank, jnp.zeros((nsp, tq), f32))
        sel_ref[...] = jnp.where(cnt < topk, 1.0, 0.0)
        for m_r, l_r, a_r in ((ms_ref, ls_ref, accs_ref), (mw_ref, lw_ref, accw_ref)):
            m_r[...] = jnp.full(m_r.shape, NEG, f32)
            l_r[...] = jnp.zeros(l_r.shape, f32)
            a_r[...] = jnp.zeros(a_r.shape, f32)

    def flash(kv_ref, ok, m_r, l_r, a_r):
        kv = kv_ref[0]
        s = jnp.dot(kv.astype(bf16), qr_ref[...], preferred_element_type=f32)
        s = s + jnp.concatenate([jnp.where(ok, 0.0, NEG)] * H, axis=1)
        alpha = _softmax_tile(s, s_ref, pb_ref, m_r, l_r)
        vT = jnp.transpose(kv)[d:2 * d, :].astype(bf16)
        a_r[...] = a_r[...] * alpha + jnp.dot(vT, pb_ref[0:kv.shape[0], :], preferred_element_type=f32)

    @pl.when(kj < needed)
    def _():
        kpos = kj * tk + lax.broadcasted_iota(jnp.int32, (tk, tq), 0)
        nb = tk // SEL_BLOCK
        rows = [jnp.broadcast_to(sel_ref[pl.ds(kj * nb + c, 1), :], (SEL_BLOCK, tq)) for c in range(nb)]
        chosen = jnp.concatenate(rows, axis=0) > 0.5
        ok = chosen & (kpos <= qpos_row(tq))
        flash(slc_ref, ok, ms_ref, ls_ref, accs_ref)

    @pl.when(kj <= w_hi - w_lo)
    def _():
        wpos = (w_lo + kj) * tkw + lax.broadcasted_iota(jnp.int32, (tkw, tq), 0)
        qp = qpos_row(tq)
        ok = (wpos <= qp) & (qp - wpos < WINDOW) & (wpos >= 0)
        flash(win_ref, ok, mw_ref, lw_ref, accw_ref)

    @pl.when(kj == n_steps - 1)
    def _():
        smT = jnp.transpose(sm_in[0])
        g = jax.nn.sigmoid(smT[SMALL_GATES:SMALL_GATES + 3 * H, :])
        o_slc = accs_ref[...] / ls_ref[...]
        o_win = accw_ref[...] / lw_ref[...]
        o_cmp = ocmp_ref[...]
        for h in range(H):
            cs = slice(h * tq, (h + 1) * tq)
            oT_ref[d * h:d * (h + 1), :] = (g[3 * h:3 * h + 1] * o_cmp[:, cs]
                                            + g[3 * h + 1:3 * h + 2] * o_slc[:, cs]
                                            + g[3 * h + 2:3 * h + 3] * o_win[:, cs])
        o_ref[0] = jnp.transpose(oT_ref[...]).astype(o_ref.dtype)


def nsa_attention(qr, z3, cmp, slc, win, *, tq, tk, tkw):
    B, T, _ = qr.shape
    nc = cmp.shape[1]
    assert T % tq == 0 and T % tk == 0 and T % tkw == 0 and tk % SEL_BLOCK == 0 and nc % 2 == 0
    assert tq % 128 == 0 and tk % 128 == 0 and tkw % 128 == 0 and nc % 8 == 0 and (tq & (tq - 1)) == 0
    nq = T // tq
    nsp = T // SEL_BLOCK
    assert nsp >= nc // 2 and nsp % 8 == 0
    last = lambda qi: (qi * tq + tq - 1) // tk
    w_lo = lambda qi: jnp.maximum(qi * tq - (WINDOW - 1), 0) // tkw
    w_hi = lambda qi: (qi * tq + tq - 1) // tkw
    pairs = []
    for qi in range(nq):
        n_win = (qi * tq + tq - 1) // tkw - max(qi * tq - (WINDOW - 1), 0) // tkw + 1
        pairs += [(qi, kj) for kj in range(max((qi * tq + tq - 1) // tk + 1, n_win))]
    qt = jnp.asarray([p[0] for p in pairs], jnp.int32)
    kt = jnp.asarray([p[1] for p in pairs], jnp.int32)
    kern = functools.partial(_nsa_kernel, tq=tq, tk=tk, tkw=tkw, nc=nc, nsp=nsp, topk=min(SEL_TOPK, nsp))
    stat = pltpu.VMEM((1, N_HEADS * tq), f32)
    acc = pltpu.VMEM((HEAD_DIM, N_HEADS * tq), f32)
    return pl.pallas_call(
        kern,
        out_shape=jax.ShapeDtypeStruct((B, T, GROUP_WIDTH), bf16),
        grid_spec=pltpu.PrefetchScalarGridSpec(
            num_scalar_prefetch=2, grid=(B, len(pairs)),
            in_specs=[
                pl.BlockSpec((1, tq, GROUP_WIDTH), lambda b, t, qt, kt: (b, qt[t], 0)),
                pl.BlockSpec((1, tq, GROUP_WIDTH), lambda b, t, qt, kt: (b, qt[t], CB_NS_Q)),
                pl.BlockSpec((1, tq, 128), lambda b, t, qt, kt: (b, qt[t], CB128_SMALL)),
                pl.BlockSpec((1, nc, 128), lambda b, t, qt, kt: (b, 0, 0)),
                pl.BlockSpec((1, tk, 128), lambda b, t, qt, kt: (b, jnp.minimum(kt[t], last(qt[t])), 0)),
                pl.BlockSpec((1, tkw, 128),
                             lambda b, t, qt, kt: (b, jnp.minimum(w_lo(qt[t]) + kt[t], w_hi(qt[t])), 0)),
            ],
            out_specs=pl.BlockSpec((1, tq, GROUP_WIDTH), lambda b, t, qt, kt: (b, qt[t], 0)),
            scratch_shapes=[pltpu.VMEM((128, N_HEADS * tq), bf16), pltpu.VMEM((nsp, tq), f32),
                            pltpu.VMEM((nsp, tq), f32), acc, stat, stat, acc, stat, stat, acc,
                            pltpu.VMEM((GROUP_WIDTH, tq), f32),
                            pltpu.VMEM((max(tk, tkw), N_HEADS * tq), f32),
                            pltpu.VMEM((max(tk, tkw), N_HEADS * tq), bf16)]),
        compiler_params=_cparams(("parallel", "arbitrary")),
        name="nsa_attention",
    )(qt, kt, qr, z3, z3, cmp, slc, win)


def _nsa_decode_kernel(pt_ref, qr_in, qw_in, sm_in, cmp_ref, sn_ref, wb_ref, wn_ref, *rest,
                       G, n_steps, Ts, pos0, n_past, wpos0, nc, nsl, n_sel, topk, page):
    pages = rest[:G]
    o_ref, qr_ref, selst_ref, ocmp_ref, owin_ref, m_ref, l_ref, acc_ref = rest[G:]
    j = pl.program_id(1)
    H, d = N_HEADS, HEAD_DIM
    scale = d ** -0.5
    R = H * Ts
    half = nc // 2
    bps = G * page // SEL_BLOCK

    def rows_by_head(x):
        z = jnp.zeros((Ts, d), f32)
        return jnp.concatenate([jnp.concatenate([x[:, d * h:d * (h + 1)], z], axis=1) for h in range(H)], axis=0)

    def tile_heads(x):
        return jnp.concatenate([x] * H, axis=0)

    @pl.when(j == 0)
    def _():
        qr = rows_by_head(qr_in[0] * scale)
        qr_ref[...] = qr
        qrb = qr.astype(bf16)
        qwb = rows_by_head(qw_in[0] * scale).astype(bf16)
        qpos = pos0 + (lax.broadcasted_iota(jnp.int32, (R, 1), 0) & (Ts - 1))
        cmpb = cmp_ref[0].astype(bf16)
        s = _nt(qwb, cmpb)
        c_ok = _cmp_block_end(lax.broadcasted_iota(jnp.int32, (1, nc), 1), half) <= qpos
        s = jnp.where(c_ok, s, NEG)
        p = jnp.where(c_ok, jnp.exp(s - jnp.max(s, axis=1, keepdims=True)), 0.0)
        p = p / jnp.maximum(jnp.sum(p, axis=1, keepdims=True), TINY)
        ocmp_ref[...] = jnp.dot(p.astype(bf16), cmpb, preferred_element_type=f32)
        imp = p[0:Ts, :]
        for h in range(1, H):
            imp = imp + p[h * Ts:(h + 1) * Ts, :]
        imp = imp[:, :half] + imp[:, half:]
        imp = jnp.concatenate([imp, jnp.zeros((Ts, nsl - half), f32)], axis=1)
        lane = lax.broadcasted_iota(jnp.int32, (Ts, nsl), 1)
        cur = (pos0 + lax.broadcasted_iota(jnp.int32, (Ts, 1), 0)) // SEL_BLOCK
        v = jnp.where((lane == cur) | (lane == 0), FORCE_SCORE, imp)
        v = jnp.where(lane > cur, NEG, v)
        cnt = jnp.zeros((Ts, nsl), f32)
        for i in range(n_sel):
            vi = v[:, i:i + 1]
            cnt = cnt + jnp.where((vi > v) | ((vi == v) & (lane > i)), 1.0, 0.0)
        sel = jnp.where(cnt < topk, 1.0, 0.0)
        for st in range(n_steps):
            selst_ref[st] = sel[:, bps * st:bps * (st + 1)]
        blk_new = n_past // SEL_BLOCK
        sel_new = tile_heads(sel[:, blk_new:blk_new + 1]) > 0.5
        snb = sn_ref[0].astype(bf16)
        s = _nt(qrb, snb)
        kpos = n_past + lax.broadcasted_iota(jnp.int32, (1, Ts), 1)
        ok = sel_new & (kpos <= qpos)
        s = jnp.where(ok, s, NEG)
        m = jnp.max(s, axis=1, keepdims=True)
        p = jnp.where(ok, jnp.exp(s - m), 0.0)
        m_ref[...] = m
        l_ref[...] = jnp.sum(p, axis=1, keepdims=True)
        acc_ref[...] = jnp.dot(p.astype(bf16), snb, preferred_element_type=f32)
        wbT = wb_ref[...].astype(bf16)
        wnb = wn_ref[0].astype(bf16)
        lw = wbT.shape[1]
        s1 = jnp.dot(qrb, wbT, preferred_element_type=f32)
        s2 = _nt(qrb, wnb)
        wp1 = wpos0 + lax.broadcasted_iota(jnp.int32, (1, lw), 1)
        wp2 = wpos0 + lw + lax.broadcasted_iota(jnp.int32, (1, Ts), 1)
        ok1 = (wp1 <= qpos) & (qpos - wp1 < WINDOW) & (wp1 >= 0)
        ok2 = (wp2 <= qpos) & (qpos - wp2 < WINDOW) & (wp2 >= 0)
        s1 = jnp.where(ok1, s1, NEG)
        s2 = jnp.where(ok2, s2, NEG)
        mw = jnp.maximum(jnp.max(s1, axis=1, keepdims=True), jnp.max(s2, axis=1, keepdims=True))
        p1 = jnp.where(ok1, jnp.exp(s1 - mw), 0.0)
        p2 = jnp.where(ok2, jnp.exp(s2 - mw), 0.0)
        lsum = jnp.sum(p1, axis=1, keepdims=True) + jnp.sum(p2, axis=1, keepdims=True)
        owin_ref[...] = (_nt(p1.astype(bf16), wbT)
                         + jnp.dot(p2.astype(bf16), wnb, preferred_element_type=f32)) / lsum

    kvT = jnp.concatenate([pg[...] for pg in pages], axis=1).astype(bf16)
    n = kvT.shape[1]
    s = jnp.dot(qr_ref[...].astype(bf16), kvT, preferred_element_type=f32)
    expand = (lax.broadcasted_iota(jnp.int32, (bps, n), 1) // SEL_BLOCK
              == lax.broadcasted_iota(jnp.int32, (bps, n), 0))
    chosen = jnp.dot(selst_ref[j].astype(bf16), jnp.where(expand, 1.0, 0.0).astype(bf16),
                     preferred_element_type=f32)
    ok = tile_heads(chosen) > 0.5
    s = jnp.where(ok, s, NEG)
    m_old = m_ref[...]
    m_new = jnp.maximum(m_old, jnp.max(s, axis=1, keepdims=True))
    alpha = jnp.exp(m_old - m_new)
    p = jnp.where(ok, jnp.exp(s - m_new), 0.0)
    l_ref[...] = alpha * l_ref[...] + jnp.sum(p, axis=1, keepdims=True)
    m_ref[...] = m_new
    acc_ref[...] = alpha * acc_ref[...] + _nt(p.astype(bf16), kvT)

    @pl.when(j == n_steps - 1)
    def _():
        g = jax.nn.sigmoid(sm_in[0][:, SMALL_GATES:SMALL_GATES + 3 * H])
        o_slc = acc_ref[...] / l_ref[...]
        outs = []
        for h in range(H):
            rs = slice(h * Ts, (h + 1) * Ts)
            outs.append(g[:, 3 * h:3 * h + 1] * ocmp_ref[rs, d:2 * d]
                        + g[:, 3 * h + 1:3 * h + 2] * o_slc[rs, d:2 * d]
                        + g[:, 3 * h + 2:3 * h + 3] * owin_ref[rs, d:2 * d])
        o_ref[0] = jnp.concatenate(outs, axis=1)


def nsa_decode(page_table, qr, z3, cmp, slc_new, win_cache, win_new, pool_slc, *, layer, pos0):
    B, Ts, _ = qr.shape
    n_pages = page_table.shape[1]
    page = pool_slc.shape[3]
    n_past = n_pages * page
    lw = win_cache.shape[3]
    nc = cmp.shape[1]
    G = min(PAGES_PER_STEP_SLC, n_pages)
    n_steps = n_pages // G
    n_sel = -(-(n_past + Ts) // SEL_BLOCK)
    nsl = -(-n_sel // 128) * 128
    assert n_pages % G == 0 and (Ts & (Ts - 1)) == 0 and Ts % 8 == 0 and page % SEL_BLOCK == 0
    assert pos0 == n_past and n_past % SEL_BLOCK + Ts <= SEL_BLOCK and nc % 2 == 0 and nc // 2 <= nsl
    R = N_HEADS * Ts
    row = lambda w, cb=0: pl.BlockSpec((1, Ts, w), lambda b, j, pt, cb=cb: (b, 0, cb))
    kern = functools.partial(_nsa_decode_kernel, G=G, n_steps=n_steps, Ts=Ts, pos0=pos0, n_past=n_past,
                             wpos0=n_past - lw, nc=nc, nsl=nsl, n_sel=n_sel, topk=min(SEL_TOPK, n_sel),
                             page=page)
    buf = lambda: pltpu.VMEM((R, 128), f32)
    stat = lambda: pltpu.VMEM((R, 1), f32)
    return pl.pallas_call(
        kern,
        out_shape=jax.ShapeDtypeStruct((B, Ts, GROUP_WIDTH), f32),
        grid_spec=pltpu.PrefetchScalarGridSpec(
            num_scalar_prefetch=1, grid=(B, n_steps),
            in_specs=[row(256), row(256, CB_NS_Q), row(128, CB128_SMALL),
                      pl.BlockSpec((1, nc, 128), lambda b, j, pt: (b, 0, 0)),
                      row(128),
                      pl.BlockSpec((None, None, 128, lw), lambda b, j, pt: (layer, b, 0, 0)),
                      row(128)]
            + _page_specs(layer, n_pages, G, (128, page)),
            out_specs=row(256),
            scratch_shapes=[buf(), pltpu.VMEM((n_steps, Ts, G * page // SEL_BLOCK), f32), buf(), buf(),
                            stat(), stat(), buf()]),
        compiler_params=_cparams(("parallel", "arbitrary")),
        name="nsa_decode",
    )(page_table.reshape(-1), qr, z3, z3, cmp, slc_new, win_cache, win_new, *([pool_slc] * G))


def _mlstm_kernel(q_ref, k_ref, v_ref, og_ref, gc_ref, gr_ref, bias_ref, ng_ref, c0_ref, n0_ref, m0_ref,
                  o_ref, cout_ref, nout_ref, mout_ref, c_s, n_s, m_s, *, L):
    ci = pl.program_id(1)
    H, d = N_HEADS, HEAD_DIM

    @pl.when(ci == 0)
    def _():
        c_s[...] = c0_ref[0]
        n_s[...] = n0_ref[0]
        m_s[...] = m0_ref[0]

    q = q_ref[...]
    k = k_ref[...] * (d ** -0.5)
    v = v_ref[...]
    gc = gc_ref[...]
    gr = gr_ref[0]
    bias = bias_ref[...]
    t_i = lax.broadcasted_iota(jnp.int32, (L, L), 0)
    s_i = lax.broadcasted_iota(jnp.int32, (L, L), 1)
    causal = s_i <= t_i
    causal_T = t_i <= s_i
    outs = []
    for h in range(H):
        bi = bias[0:1, h:h + 1]
        bf_ = bias[1:2, h:h + 1]
        ig_c = gc[:, h:h + 1] + bi
        lf_c = jax.nn.log_sigmoid(gc[:, H + h:H + h + 1] + bf_)
        ig_r = gr[h:h + 1, :] + bi
        lf_r = jax.nn.log_sigmoid(gr[H + h:H + h + 1, :] + bf_)
        b_c = jnp.sum(jnp.where(causal, lf_r, 0.0), axis=1, keepdims=True)
        b_r = jnp.sum(jnp.where(causal_T, lf_c, 0.0), axis=0, keepdims=True)
        m_prev = m_s[h:h + 1, 0:1]
        D = jnp.where(causal, b_c - b_r + ig_r, -jnp.inf)
        m_t = jnp.maximum(b_c + m_prev, jnp.max(D, axis=1, keepdims=True))
        inter = jnp.exp(b_c + m_prev - m_t)
        qh = q[:, d * h:d * (h + 1)]
        kh = k[:, d * h:d * (h + 1)]
        vh = v[:, d * h:d * (h + 1)]
        qb = qh.astype(bf16)
        S = _nt(qb, kh.astype(bf16))
        Sw = jnp.exp(D - m_t) * S
        C = c_s[h]
        n_row = n_s[h:h + 1, :]
        num = (jnp.dot(Sw.astype(bf16), vh.astype(bf16), preferred_element_type=f32)
               + inter * jnp.dot(qb, C.astype(bf16), preferred_element_type=f32))
        qn = jnp.sum(qh * n_row, axis=1, keepdims=True)
        den = jnp.sum(Sw, axis=1, keepdims=True) + inter * qn
        hh = num / jnp.maximum(jnp.abs(den), jnp.exp(-m_t))
        m_new = m_t[L - 1:L, :]
        b_last = b_c[L - 1:L, :]
        wl = jnp.exp(b_last - b_c + ig_c - m_new)
        decay = jnp.exp(b_last + m_prev - m_new)
        kw = kh * wl
        c_s[h] = decay * C + lax.dot_general(kw.astype(bf16), vh.astype(bf16), (((0,), (0,)), ((), ())),
                                             preferred_element_type=f32)
        n_s[h:h + 1, :] = decay * n_row + jnp.sum(kw, axis=0, keepdims=True)
        m_s[h:h + 1, :] = jnp.broadcast_to(m_new, (1, 128))
        mu = jnp.mean(hh, axis=1, keepdims=True)
        var = jnp.mean(jnp.square(hh - mu), axis=1, keepdims=True)
        outs.append((hh - mu) * lax.rsqrt(var + EPS))
    hcat = jnp.concatenate(outs, axis=1) * ng_ref[...]
    o_ref[...] = (hcat * jax.nn.sigmoid(og_ref[...])).astype(o_ref.dtype)
    cout_ref[0] = c_s[...]
    nout_ref[0] = n_s[...]
    mout_ref[0] = m_s[...]


def mlstm(z, gates_T, gate_b, norm_g, C0, n0, m0, *, B, T, L):
    nchunk = T // L
    m0p = jnp.broadcast_to(m0[:, :, None], (B, N_HEADS, 128))
    row = lambda cb: pl.BlockSpec((L, 256), lambda b, c, cb=cb: (b * nchunk + c, cb))
    st = lambda *shape: pl.BlockSpec((1,) + shape, lambda b, c: (b,) + (0,) * len(shape))
    out, C, n, m = pl.pallas_call(
        functools.partial(_mlstm_kernel, L=L),
        out_shape=[jax.ShapeDtypeStruct((B * T, GROUP_WIDTH), _act_dtype(L)),
                   jax.ShapeDtypeStruct((B, N_HEADS, HEAD_DIM, HEAD_DIM), f32),
                   jax.ShapeDtypeStruct((B, N_HEADS, HEAD_DIM), f32),
                   jax.ShapeDtypeStruct((B, N_HEADS, 128), f32)],
        grid=(B, nchunk),
        in_specs=[row(CB_ML_Q), row(CB_ML_K), row(CB_ML_V), row(CB_ML_O),
                  pl.BlockSpec((L, 128), lambda b, c: (b * nchunk + c, CB128_SMALL)),
                  pl.BlockSpec((1, 8, L), lambda b, c: (b * nchunk + c, 0, 0)),
                  pl.BlockSpec((2, N_HEADS), lambda b, c: (0, 0)),
                  pl.BlockSpec((1, GROUP_WIDTH), lambda b, c: (0, 0)),
                  st(N_HEADS, HEAD_DIM, HEAD_DIM), st(N_HEADS, HEAD_DIM), st(N_HEADS, 128)],
        out_specs=[pl.BlockSpec((L, GROUP_WIDTH), lambda b, c: (b * nchunk + c, 0)),
                   st(N_HEADS, HEAD_DIM, HEAD_DIM), st(N_HEADS, HEAD_DIM), st(N_HEADS, 128)],
        scratch_shapes=[pltpu.VMEM((N_HEADS, HEAD_DIM, HEAD_DIM), f32), pltpu.VMEM((N_HEADS, HEAD_DIM), f32),
                        pltpu.VMEM((N_HEADS, 128), f32)],
        compiler_params=_cparams(("parallel", "arbitrary")),
        name="mlstm",
    )(z, z, z, z, z, gates_T, gate_b, norm_g.reshape(1, GROUP_WIDTH), C0, n0, m0p)
    return out, C, n, m[:, :, 0]


def _rglru_kernel(x_ref, gate_ref, cw_ref, cb_ref, wa_ref, ba_ref, wx_ref, bx_ref, lam_ref, h0_ref, buf0_ref,
                  y_ref, hout_ref, bufout_ref, xbuf, a_s, u_s, h_s, hs_s, *, tm):
    ti = pl.program_id(0)
    Bb = x_ref.shape[0]
    W = GROUP_WIDTH

    @pl.when(ti == 0)
    def _():
        xbuf[:, 0:8, :] = buf0_ref[...]
        h_s[...] = h0_ref[...]

    xbuf[:, 8:8 + tm, :] = x_ref[...]
    cw = cw_ref[...]
    xc = cb_ref[...] + cw[0:1, :] * xbuf[:, 5:5 + tm, :]
    for j in range(1, CONV_W):
        xc = xc + cw[j:j + 1, :] * xbuf[:, 5 + j:5 + j + tm, :]
    flat = xc.reshape(Bb * tm, W).astype(bf16)
    r = jax.nn.sigmoid(jnp.dot(flat, wa_ref[...], preferred_element_type=f32) + ba_ref[...])
    i = jax.nn.sigmoid(jnp.dot(flat, wx_ref[...], preferred_element_type=f32) + bx_ref[...])
    lam = lam_ref[...]
    softplus = jnp.maximum(-lam, 0.0) + jnp.log1p(jnp.exp(-jnp.abs(lam)))
    log_a = (-LRU_C * r) * softplus
    a = jnp.exp(log_a)
    u = jnp.sqrt(-jnp.tanh(log_a) * (a * a + 1.0)) * (i * xc.reshape(Bb * tm, W))
    a_s[...] = a.reshape(Bb, tm, W)
    u_s[...] = u.reshape(Bb, tm, W)

    def body(t, h):
        h = a_s[:, pl.ds(t, 1), :] * h + u_s[:, pl.ds(t, 1), :]
        hs_s[:, pl.ds(t, 1), :] = h
        return h

    h_last = lax.fori_loop(0, tm, body, h_s[...], unroll=8)
    h_s[...] = h_last
    y_ref[...] = (hs_s[...] * jax.nn.gelu(gate_ref[...])).astype(y_ref.dtype)
    hout_ref[...] = h_last
    bufout_ref[...] = xbuf[:, tm:tm + 8, :]
    xbuf[:, 0:8, :] = xbuf[:, tm:tm + 8, :]


def rglru(z3, conv_w, conv_b, wa_bd, ba, wx_bd, bx, lam, h0, buf0, *, tm):
    B, T, _ = z3.shape
    tm = min(tm, T)
    assert T % tm == 0 and tm % 8 == 0
    W = GROUP_WIDTH
    buf8 = jnp.concatenate([jnp.zeros((B, 8 - (CONV_W - 1), W), f32), buf0], 1)
    vec = lambda: pl.BlockSpec((1, W), lambda i: (0, 0))
    mat = lambda: pl.BlockSpec((W, W), lambda i: (0, 0))
    y, h, buf = pl.pallas_call(
        functools.partial(_rglru_kernel, tm=tm),
        out_shape=[jax.ShapeDtypeStruct((B, T, W), _act_dtype(tm)), jax.ShapeDtypeStruct((B, 1, W), f32),
                   jax.ShapeDtypeStruct((B, 8, W), f32)],
        grid=(T // tm,),
        in_specs=[pl.BlockSpec((B, tm, W), lambda i: (0, i, CB_LR_X)),
                  pl.BlockSpec((B, tm, W), lambda i: (0, i, CB_LR_G)),
                  pl.BlockSpec((CONV_W, W), lambda i: (0, 0)), vec(), mat(), vec(), mat(), vec(), vec(),
                  pl.BlockSpec((B, 1, W), lambda i: (0, 0, 0)), pl.BlockSpec((B, 8, W), lambda i: (0, 0, 0))],
        out_specs=[pl.BlockSpec((B, tm, W), lambda i: (0, i, 0)),
                   pl.BlockSpec((B, 1, W), lambda i: (0, 0, 0)), pl.BlockSpec((B, 8, W), lambda i: (0, 0, 0))],
        scratch_shapes=[pltpu.VMEM((B, tm + 8, W), f32), pltpu.VMEM((B, tm, W), f32), pltpu.VMEM((B, tm, W), f32),
                        pltpu.VMEM((B, 1, W), f32), pltpu.VMEM((B, tm, W), f32)],
        compiler_params=_cparams(("arbitrary",)),
        name="rglru",
    )(z3, z3, conv_w, conv_b.reshape(1, W), wa_bd, ba.reshape(1, W), wx_bd, bx.reshape(1, W),
      lam.reshape(1, W), h0.reshape(B, 1, W), buf8)
    return y, h[:, 0], buf[:, 8 - (CONV_W - 1):]


def _mem_attn_kernel(q_ref, k_ref, v_ref, o_ref):
    q = q_ref[0]
    lead = (0,) * (len(k_ref.shape) - 2)
    k = k_ref[lead]
    v = v_ref[lead]
    dh = MEM_HEAD_DIM
    for h in range(MEM_HEADS):
        cs = slice(dh * h, dh * (h + 1))
        s = _nt(q[:, cs].astype(bf16), k[:, cs].astype(bf16)) * (dh ** -0.5)
        e = jnp.exp(s - jnp.max(s, axis=-1, keepdims=True))
        p = e / jnp.sum(e, axis=-1, keepdims=True)
        o_ref[0, :, cs] = jnp.dot(p.astype(bf16), v[:, cs].astype(bf16),
                                  preferred_element_type=f32).astype(o_ref.dtype)


def mem_attention(q, k, v, *, layer=None, tq=256):
    B, T, D = q.shape
    tq = min(tq, T)
    M = k.shape[-2]
    if layer is None:
        kv_spec = pl.BlockSpec((1, M, D), lambda b, i: (b, 0, 0))
    else:
        kv_spec = pl.BlockSpec((1, 1, M, D), lambda b, i: (layer, b, 0, 0))
    return pl.pallas_call(
        _mem_attn_kernel,
        out_shape=jax.ShapeDtypeStruct((B, T, D), _act_dtype(tq)),
        grid=(B, T // tq),
        in_specs=[pl.BlockSpec((1, tq, D), lambda b, i: (b, i, 0)), kv_spec, kv_spec],
        out_specs=pl.BlockSpec((1, tq, D), lambda b, i: (b, i, 0)),
        compiler_params=_cparams(("parallel", "parallel")),
        name="mem_attention",
    )(q, k, v)


def _even_odd(cmp):
    B, nc, w = cmp.shape
    return jnp.swapaxes(cmp.reshape(B, nc // 2, 2, w), 1, 2).reshape(B, nc, w)


def _layer(x, lw, lam_init, tables, *, B, T, mem, ml_state, lr_state, paged, cfg):
    N = B * T
    z = matmul(x, lw['w_in'], g=lw['g_mix'], tm=512)
    z3 = z.reshape(B, T, D_PROJ_PAD)
    dfq, dfk, nsq, slc_new, win_new = rope_prep(z, tables, cfg['rope_tm'])
    df_v = z[:, 256 * CB_DF_V:256 * (CB_DF_V + 1)]
    cmp_new = z[:, 128 * CB128_CMP:128 * (CB128_CMP + 1)]
    small = z[:, 128 * CB128_SMALL:128 * CB128_SMALL + SMALL_GATES]
    r3 = lambda a: a.reshape(B, T, a.shape[-1])

    L = cfg['ml_chunk']
    gates_T = jnp.swapaxes(small.reshape(N // L, L, SMALL_GATES), 1, 2)
    o_ml, ml_C, ml_n, ml_m = mlstm(z, gates_T, lw['ml_gate_b'], lw['ml_norm_g'], *ml_state, B=B, T=T, L=L)

    if paged is None:
        o_df = diff_attention(r3(dfq), r3(dfk), z3, lw['df_lam'], lw['df_norm_g'], lam_init=lam_init,
                              tq=cfg['df_tq'], tk=cfg['df_tk'])
        comp = nsa_compress(cmp_new.reshape(N // CMP_BLOCK, CMP_BLOCK * 128), lw['nsa_pos_flat'],
                            lw['nsa_w1c'], lw['nsa_w2c']).reshape(B, T // CMP_BLOCK, 128)
        o_ns = nsa_attention(r3(nsq), z3, _even_odd(comp), r3(slc_new), r3(win_new),
                             tq=cfg['ns_tq'], tk=cfg['ns_tk'], tkw=cfg['ns_tkw'])
        win_state = r3(win_new)[:, T - min(WINDOW, T):]
    else:
        pt, l = paged['page_table'], paged['layer']
        n_past = pt.shape[1] * paged['df_k'].shape[3]
        assert (n_past + T) // CMP_BLOCK == n_past // CMP_BLOCK
        o_df = diff_decode(pt, r3(dfq), r3(dfk), r3(df_v), paged['df_k'], paged['df_v'], lw['df_lam'],
                           lw['df_norm_g'], layer=l, pos0=n_past, lam_init=lam_init)
        comp = nsa_compress_paged(pt, paged['nsa_cmp'], lw['nsa_pos_flat'], lw['nsa_w1c'], lw['nsa_w2c'], layer=l)
        o_ns = nsa_decode(pt, r3(nsq), z3, _even_odd(comp), r3(slc_new), paged['nsa_win'], r3(win_new),
                          paged['nsa_slc'], layer=l, pos0=n_past)
        win_cat = jnp.concatenate([paged['nsa_win'][l], jnp.swapaxes(r3(win_new), 1, 2)], 2)
        win_state = jnp.swapaxes(win_cat[:, :, win_cat.shape[2] - min(WINDOW, win_cat.shape[2]):], 1, 2)

    o_lr, lr_h, lr_buf = rglru(z3, lw['lru_conv_w'], lw['lru_conv_b'], lw['lru_wa_bd'],
                               lw['lru_ba'], lw['lru_wx_bd'], lw['lru_bx'], lw['lru_lambda'], *lr_state,
                               tm=cfg['lru_tm'])

    x = matmul([o_ml, o_df.reshape(N, 256), o_ns.reshape(N, 256), o_lr.reshape(N, 256)], lw['w_out'], res=x,
               tm=1024)
    q = matmul(x, lw['w_mq'], g=lw['g_mem_q'], tm=1024, out_dtype=_act_dtype(cfg['mem_tq']))
    if isinstance(mem, tuple):
        att = mem_attention(q.reshape(B, T, D_MODEL), mem[0], mem[1], tq=cfg['mem_tq'])
    else:
        att = mem_attention(q.reshape(B, T, D_MODEL), mem['k'], mem['v'], layer=mem['layer'], tq=cfg['mem_tq'])
    x = matmul(att.reshape(N, D_MODEL), lw['w_mo'], res=x, tm=1024)
    u = matmul(x, lw['w_up'], g=lw['g_mlp'], act='relu2', out_dtype=bf16, tm=512)
    x = matmul(u, lw['w_down'], res=x, tm=512)
    new = dict(df_k=dfk.reshape(B, T, N_HEADS, HEAD_DIM), df_v=df_v.reshape(B, T, N_HEADS, HEAD_DIM),
               nsa_cmp=cmp_new.reshape(B, T, 2, HEAD_DIM), nsa_slc=slc_new.reshape(B, T, 2, HEAD_DIM),
               nsa_win=win_state.reshape(B, -1, 2, HEAD_DIM), ml_C=ml_C, ml_n=ml_n, ml_m=ml_m,
               lru_h=lr_h, lru_conv=lr_buf)
    return x, new


def _block_diag(w):
    H, a, b = w.shape
    eye = jnp.eye(H, dtype=w.dtype)
    return jnp.einsum('hij,hg->higj', w, eye).reshape(H * a, H * b)


def kernel(x_prompt, x_sample, mem_prompt, cache_df_k, cache_df_v, cache_nsa_cmp, cache_nsa_slc, cache_nsa_win, state_ml_C, state_ml_n, state_ml_m, state_lru_h, state_lru_conv, cache_mem_k, cache_mem_v, page_table, g_mix, w_in, w_out, ml_gate_b, ml_norm_g, df_lam, df_norm_g, nsa_pos, nsa_w1, nsa_w2, lru_conv_w, lru_conv_b, lru_wa, lru_ba, lru_wx, lru_bx, lru_lambda, g_mem_q, g_mem_kv, w_mq, w_mk, w_mv, w_mo, g_mlp, w_up, w_down, g_final):
    Bp, Tp, _ = x_prompt.shape
    Bs, Ts, _ = x_sample.shape
    depth = w_in.shape[0]
    n_pool, page = cache_df_k.shape[1], cache_df_k.shape[2]
    n_past = page_table.shape[1] * page
    M = mem_prompt.shape[1]

    perm, n_real = _proj_perm()
    col_ok = (jnp.arange(D_PROJ_PAD) < n_real)
    w_in_p = jnp.where(col_ok[None, None, :], jnp.take(w_in, perm, axis=2), 0.0).astype(bf16)
    eye2 = jnp.eye(2, dtype=f32)
    w1 = nsa_w1.reshape(depth, 2, CMP_BLOCK, HEAD_DIM, CMP_HIDDEN)
    w1c = jnp.einsum('lstih,sg->ltsigh', w1, eye2).reshape(depth, CMP_BLOCK * 2 * HEAD_DIM, 2 * CMP_HIDDEN)
    w2c = jnp.einsum('lshd,sg->lshgd', nsa_w2, eye2).reshape(depth, 2 * CMP_HIDDEN, 2 * HEAD_DIM)
    cfg_p = dict(rope_tm=512, ml_chunk=512, df_tq=512, df_tk=512, ns_tq=512, ns_tk=512, ns_tkw=256,
                 lru_tm=256, mem_tq=512)
    cfg_s = dict(rope_tm=Bs * Ts, ml_chunk=Ts, lru_tm=Ts, mem_tq=Ts)
    tab_p = rope_tables(jnp.arange(Tp))
    tab_s = tuple(jnp.tile(t, (Bs, 1)) for t in rope_tables(n_past + jnp.arange(Ts)))

    tok_minor = lambda c: jnp.transpose(c, (0, 1, 3, 4, 2)).reshape(c.shape[0], c.shape[1], -1, c.shape[2])
    pool_df_k = tok_minor(cache_df_k)
    pool_df_v = tok_minor(cache_df_v)
    pool_cmp = tok_minor(cache_nsa_cmp)
    pool_slc = tok_minor(cache_nsa_slc)
    win_cache = tok_minor(cache_nsa_win)
    mem_k_cache = cache_mem_k.reshape(depth, Bs, M, D_MODEL)
    mem_v_cache = cache_mem_v.reshape(depth, Bs, M, D_MODEL)
    mem_rows = mem_prompt.reshape(Bp * M, D_MODEL)

    xp = x_prompt.reshape(Bp * Tp, D_MODEL)
    xs = x_sample.reshape(Bs * Ts, D_MODEL)
    names = ('df_k', 'df_v', 'nsa_cmp', 'nsa_slc', 'nsa_win', 'ml_C', 'ml_n', 'ml_m', 'lru_h', 'lru_conv')
    acc = {pre + n: [] for n in names for pre in ('p_', 's_')}
    acc['p_mem_k'] = []
    acc['p_mem_v'] = []
    for l in range(depth):
        lw = dict(g_mix=g_mix[l], w_in=w_in_p[l], w_out=w_out[l].astype(bf16), ml_gate_b=ml_gate_b[l],
                  ml_norm_g=ml_norm_g[l], df_lam=df_lam[l], df_norm_g=df_norm_g[l],
                  nsa_pos_flat=nsa_pos[l].reshape(1, CMP_BLOCK * 2 * HEAD_DIM), nsa_w1c=w1c[l].astype(bf16),
                  nsa_w2c=w2c[l].astype(bf16), lru_conv_w=lru_conv_w[l], lru_conv_b=lru_conv_b[l],
                  lru_wa_bd=_block_diag(lru_wa[l]).astype(bf16), lru_ba=lru_ba[l],
                  lru_wx_bd=_block_diag(lru_wx[l]).astype(bf16), lru_bx=lru_bx[l], lru_lambda=lru_lambda[l],
                  g_mem_q=g_mem_q[l], w_mq=w_mq[l].astype(bf16), w_mo=w_mo[l].astype(bf16), g_mlp=g_mlp[l],
                  w_up=w_up[l].astype(bf16), w_down=w_down[l].astype(bf16))
        lam_init = 0.8 - 0.6 * math.exp(-0.3 * l)
        mk_p = matmul(mem_rows, w_mk[l].astype(bf16), g=g_mem_kv[l])
        mv_p = matmul(mem_rows, w_mv[l].astype(bf16), g=g_mem_kv[l])
        ml0 = (jnp.zeros((Bp, N_HEADS, HEAD_DIM, HEAD_DIM), f32), jnp.zeros((Bp, N_HEADS, HEAD_DIM), f32),
               jnp.zeros((Bp, N_HEADS), f32))
        lr0 = (jnp.zeros((Bp, GROUP_WIDTH), f32), jnp.zeros((Bp, CONV_W - 1, GROUP_WIDTH), f32))
        xp, new_p = _layer(xp, lw, lam_init, tab_p, B=Bp, T=Tp,
                           mem=(mk_p.reshape(Bp, M, D_MODEL), mv_p.reshape(Bp, M, D_MODEL)),
                           ml_state=ml0, lr_state=lr0, paged=None, cfg=cfg_p)
        paged = dict(page_table=page_table, layer=l, df_k=pool_df_k, df_v=pool_df_v, nsa_cmp=pool_cmp,
                     nsa_slc=pool_slc, nsa_win=win_cache)
        xs, new_s = _layer(xs, lw, lam_init, tab_s, B=Bs, T=Ts,
                           mem=dict(k=mem_k_cache, v=mem_v_cache, layer=l),
                           ml_state=(state_ml_C[l], state_ml_n[l], state_ml_m[l]),
                           lr_state=(state_lru_h[l], state_lru_conv[l]), paged=paged, cfg=cfg_s)
        for pre, new in (('p_', new_p), ('s_', new_s)):
            for n in names:
                acc[pre + n].append(new[n])
        acc['p_mem_k'].append(mk_p.reshape(Bp, M, MEM_HEADS, MEM_HEAD_DIM))
        acc['p_mem_v'].append(mv_p.reshape(Bp, M, MEM_HEADS, MEM_HEAD_DIM))
    st = {k: jnp.stack(v) for k, v in acc.items()}
    y_prompt = rmsnorm_rows(xp, g_final).reshape(Bp, Tp, D_MODEL)
    y_sample = rmsnorm_rows(xs, g_final).reshape(Bs, Ts, D_MODEL)
    return (y_prompt, y_sample, st['p_df_k'], st['s_df_k'], st['p_df_v'], st['s_df_v'],
            st['p_nsa_cmp'], st['s_nsa_cmp'], st['p_nsa_slc'], st['s_nsa_slc'],
            st['p_nsa_win'], st['s_nsa_win'], st['p_ml_C'], st['s_ml_C'], st['p_ml_n'], st['s_ml_n'],
            st['p_ml_m'], st['s_ml_m'], st['p_lru_h'], st['s_lru_h'], st['p_lru_conv'], st['s_lru_conv'],
            st['p_mem_k'], st['p_mem_v'])
```
